```python
import jax, jax.numpy as jnp
from jax import lax
import numpy as np

D_MODEL = 2048
BATCH = 4
SEQ = 2048
DEPTH = 4

GRID_W = 64
CTX_LEN = 256
EPS = 1e-6
ROPE_BASE = 10000.0
NEG_INF = -1e30
F32 = jnp.float32

BRANCH_WIDTH = D_MODEL // 2
N_BRANCH = 3

RET_HEAD_DIM = 128
RET_HEADS = BRANCH_WIDTH // RET_HEAD_DIM
RET_CHUNK = 128
RET_GN_EPS = 1e-5

SWA_HEAD_DIM = 128
SWA_Q_HEADS = BRANCH_WIDTH // SWA_HEAD_DIM
SWA_KV_HEADS = SWA_Q_HEADS // 4
SWA_WINDOW = 128
SWA_BLOCK = 128

RWKV_HEAD_DIM = 64
RWKV_HEADS = BRANCH_WIDTH // RWKV_HEAD_DIM
RWKV_W_RANK = 64
RWKV_A_RANK = 64
RWKV_G_RANK = 128
RWKV_GN_EPS = 64e-5

D_FF = 4 * D_MODEL

RET_IN = 4 * BRANCH_WIDTH
SWA_IN = (SWA_Q_HEADS + 2 * SWA_KV_HEADS) * SWA_HEAD_DIM
RWKV_IN = 3 * BRANCH_WIDTH + 2 * (RWKV_W_RANK + RWKV_A_RANK) + RWKV_G_RANK
GATE_IN = N_BRANCH * D_MODEL
N_IN = RET_IN + SWA_IN + RWKV_IN + GATE_IN
IN_SPLITS = (RET_IN, SWA_IN, RWKV_IN, GATE_IN)
RWKV_SPLITS = (BRANCH_WIDTH, BRANCH_WIDTH, BRANCH_WIDTH, 2 * RWKV_W_RANK, 2 * RWKV_A_RANK, RWKV_G_RANK)

kernel_name = 'hybrid_retention_swa_rwkv7_prefix_dit'


def _split(t, sizes):
    offsets = np.cumsum(np.asarray(sizes))[:-1].tolist()
    return jnp.split(t, offsets, axis=-1)


def _flip(t, rev, axis):
    return jnp.flip(t, axis=axis) if rev else t


def rms_norm(x, g):
    x32 = x.astype(F32)
    y = x32 * lax.rsqrt(jnp.mean(jnp.square(x32), axis=-1, keepdims=True) + EPS)
    return y.astype(x.dtype) * g


def modulate(x, shift, scale):
    return x * (1 + scale) + shift


def sq_relu_mlp(x, w1, w2):
    return jnp.square(jax.nn.relu(x @ w1)) @ w2


def axial_rope_tables(row, col):
    n_freq = SWA_HEAD_DIM // 4
    inv_freq = ROPE_BASE ** (-jnp.arange(n_freq, dtype=F32) / n_freq)
    ang_r = row.astype(F32)[:, None] * inv_freq
    ang_c = col.astype(F32)[:, None] * inv_freq
    return (jnp.cos(ang_r), jnp.sin(ang_r), jnp.cos(ang_c), jnp.sin(ang_c))


def _rotate(x, cos, sin):
    cos = cos[None, :, None, :].astype(x.dtype)
    sin = sin[None, :, None, :].astype(x.dtype)
    x1, x2 = jnp.split(x, 2, axis=-1)
    return jnp.concatenate([x1 * cos - x2 * sin, x2 * cos + x1 * sin], axis=-1)


def apply_axial_rope(x, rope):
    cos_r, sin_r, cos_c, sin_c = rope
    x_row, x_col = jnp.split(x, 2, axis=-1)
    return jnp.concatenate([_rotate(x_row, cos_r, sin_r), _rotate(x_col, cos_c, sin_c)], axis=-1)


def retention_chunkwise(q, k, v, log_gamma, s0):
    b, h, t, _ = q.shape
    dv = v.shape[-1]
    n = t // RET_CHUNK
    q, k, v = (z.astype(F32).reshape(b, h, n, RET_CHUNK, z.shape[-1]) for z in (q, k, v))
    idx = jnp.arange(RET_CHUNK, dtype=F32)
    lg = log_gamma[:, None]
    diff = idx[:, None] - idx[None, :]
    intra = jnp.where(diff >= 0, jnp.exp(lg[:, :, None] * jnp.maximum(diff, 0.0)), 0.0)
    scores = jnp.einsum('bhncd,bhnsd->bhncs', q, k) * intra[None, :, None]
    y_intra = jnp.einsum('bhncs,bhnsv->bhncv', scores, v)
    q_decay = jnp.exp(lg * (idx + 1.0))
    k_decay = jnp.exp(lg * (RET_CHUNK - 1.0 - idx))
    chunk_kv = jnp.einsum('bhncd,hc,bhncv->nbhdv', k, k_decay, v)
    chunk_decay = jnp.exp(log_gamma * RET_CHUNK)[None, :, None, None]

    def step(s, kv):
        return s * chunk_decay + kv, s

    s_final, s_prev = lax.scan(step, s0, chunk_kv)
    y_cross = jnp.einsum('bhncd,hc,nbhdv->bhncv', q, q_decay, s_prev)
    return s_final, (y_intra + y_cross).reshape(b, h, t, dv)


def _ret_output(y, g):
    mean = jnp.mean(y, axis=-1, keepdims=True)
    var = jnp.mean(jnp.square(y - mean), axis=-1, keepdims=True)
    y = (y - mean) * lax.rsqrt(var + RET_GN_EPS)
    b, h, t, dv = y.shape
    y = y.transpose(0, 2, 1, 3).reshape(b, t, h * dv).astype(g.dtype)
    return jax.nn.silu(g) * y


def retention_mixer(p, pc, decay_logit):
    log_gamma = jax.nn.log_sigmoid(decay_logit.astype(F32))

    def heads(t):
        b, t_len, _ = t.shape
        return t.reshape(b, t_len, RET_HEADS, RET_HEAD_DIM).transpose(0, 2, 1, 3)

    def qkvg(t):
        q, k, v, g = jnp.split(t, 4, axis=-1)
        return heads(q), heads(k) * (RET_HEAD_DIM ** -0.5), heads(v), g

    q, k, v, g = qkvg(p)
    qc, kc, vc, gc = qkvg(pc)
    outs, outs_c = [], []
    for direction in range(2):
        rev = direction == 1
        s0 = jnp.zeros(qc.shape[:2] + (RET_HEAD_DIM, RET_HEAD_DIM), F32)
        s_ctx, y_c = retention_chunkwise(_flip(qc, rev, 2), _flip(kc, rev, 2), _flip(vc, rev, 2), log_gamma[direction], s0)
        _, y_l = retention_chunkwise(_flip(q, rev, 2), _flip(k, rev, 2), _flip(v, rev, 2), log_gamma[direction], s_ctx)
        outs.append(_flip(y_l, rev, 2))
        outs_c.append(_flip(y_c, rev, 2))
    return _ret_output(outs[0] + outs[1], g), _ret_output(outs_c[0] + outs_c[1], gc)


def banded_attention_with_ctx(q, k, v, kc, vc, sink):
    b, t, hq, d = q.shape
    g = hq // SWA_KV_HEADS
    nb = t // SWA_BLOCK
    blk = SWA_BLOCK
    tc = kc.shape[1]
    scale = d ** -0.5
    qb = q.reshape(b, nb, blk, SWA_KV_HEADS, g, d)

    def band(z):
        zp = jnp.pad(z, ((0, 0), (blk, blk), (0, 0), (0, 0))).reshape(b, nb + 2, blk, SWA_KV_HEADS, d)
        return jnp.concatenate([zp[:, :-2], zp[:, 1:-1], zp[:, 2:]], axis=2)

    kw, vw = band(k), band(v)
    s_win = jnp.einsum('bnqhgd,bnkhd->bnhgqk', qb, kw).astype(F32) * scale
    blocks = jnp.arange(nb)[:, None, None]
    qpos = blocks * blk + jnp.arange(blk)[None, :, None]
    kpos = (blocks - 1) * blk + jnp.arange(3 * blk)[None, None, :]
    valid = (jnp.abs(qpos - kpos) <= SWA_WINDOW) & (kpos >= 0) & (kpos < t)
    s_win = jnp.where(valid[None, :, None, None], s_win, NEG_INF)
    s_ctx = jnp.einsum('bnqhgd,bchd->bnhgqc', qb, kc).astype(F32) * scale
    sink_col = jnp.broadcast_to(sink.astype(F32).reshape(SWA_KV_HEADS, g)[None, None, :, :, None, None], s_win.shape[:-1] + (1,))
    probs = jax.nn.softmax(jnp.concatenate([s_win, s_ctx, sink_col], axis=-1), axis=-1)
    p_win = probs[..., :3 * blk].astype(v.dtype)
    p_ctx = probs[..., 3 * blk:3 * blk + tc].astype(v.dtype)
    out = jnp.einsum('bnhgqk,bnkhd->bnqhgd', p_win, vw) + jnp.einsum('bnhgqc,bchd->bnqhgd', p_ctx, vc)
    return out.reshape(b, t, hq * d)


def ctx_attention(qc, kc, vc, sink):
    b, tc, hq, d = qc.shape
    g = hq // SWA_KV_HEADS
    qg = qc.reshape(b, tc, SWA_KV_HEADS, g, d)
    s = jnp.einsum('bqhgd,bkhd->bhgqk', qg, kc).astype(F32) * (d ** -0.5)
    sink_col = jnp.broadcast_to(sink.astype(F32).reshape(SWA_KV_HEADS, g)[None, :, :, None, None], s.shape[:-1] + (1,))
    probs = jax.nn.softmax(jnp.concatenate([s, sink_col], axis=-1), axis=-1)
    out = jnp.einsum('bhgqk,bkhd->bqhgd', probs[..., :tc].astype(vc.dtype), vc)
    return out.reshape(b, tc, hq * d)


def swa_mixer(p, pc, sink, rope, with_ctx_out):
    sizes = (SWA_Q_HEADS * SWA_HEAD_DIM, SWA_KV_HEADS * SWA_HEAD_DIM, SWA_KV_HEADS * SWA_HEAD_DIM)

    def split_heads(t):
        b, t_len, _ = t.shape
        q, k, v = _split(t, sizes)
        return (q.reshape(b, t_len, SWA_Q_HEADS, SWA_HEAD_DIM),
                k.reshape(b, t_len, SWA_KV_HEADS, SWA_HEAD_DIM),
                v.reshape(b, t_len, SWA_KV_HEADS, SWA_HEAD_DIM))

    q, k, v = split_heads(p)
    qc, kc, vc = split_heads(pc)
    y = banded_attention_with_ctx(apply_axial_rope(q, rope), apply_axial_rope(k, rope), v, kc, vc, sink)
    y_c = ctx_attention(qc, kc, vc, sink) if with_ctx_out else None
    return y, y_c


def centred_token_shift(p, mu):
    prev = jnp.pad(p, ((0, 0), (1, 0), (0, 0)))[:, :-1]
    nxt = jnp.pad(p, ((0, 0), (0, 1), (0, 0)))[:, 1:]
    return p + mu[0] * (prev - p) + mu[1] * (nxt - p)


def rwkv_features(p, lp):
    p = centred_token_shift(p, lp['rwkv_mu'])
    r, k, v, wd, ad, gd = _split(p, RWKV_SPLITS)
    b, t, _ = p.shape
    wd = wd.reshape(b, t, 2, RWKV_W_RANK)
    ad = ad.reshape(b, t, 2, RWKV_A_RANK)
    w_raw = (lp['rwkv_w0'] + jnp.einsum('btnr,nrw->btnw', jnp.tanh(wd), lp['rwkv_w_up'])).astype(F32)
    log_decay = -jnp.exp(-jax.nn.softplus(-w_raw) - 0.5)
    a = jax.nn.sigmoid((lp['rwkv_a0'] + jnp.einsum('btnr,nrw->btnw', ad, lp['rwkv_a_up'])).astype(F32))
    g = jax.nn.sigmoid(gd) @ lp['rwkv_g_up']
    kk = (k * lp['rwkv_k_k']).astype(F32).reshape(b, t, RWKV_HEADS, RWKV_HEAD_DIM)
    kk = kk / jnp.maximum(jnp.sqrt(jnp.sum(jnp.square(kk), axis=-1, keepdims=True)), 1e-12)
    k_dir = k[:, :, None, :].astype(F32) * (1 + (a - 1) * lp['rwkv_k_a'])

    def hd(z):
        return z.reshape(z.shape[:-1] + (RWKV_HEADS, RWKV_HEAD_DIM))

    return {'r': hd(r.astype(F32)), 'k': hd(k_dir), 'v': hd(v.astype(F32)),
            'log_decay': hd(log_decay), 'a': hd(a), 'kk': kk, 'g': g}


def rwkv_scan(s0, r, w, k, v, a, b):
    def step(s, inp):
        r_t, w_t, k_t, v_t, a_t, b_t = inp
        sa = jnp.einsum('bhvk,bhk->bhv', s, a_t)
        s = s * w_t[:, :, None, :] + sa[..., None] * b_t[:, :, None, :] + v_t[..., None] * k_t[:, :, None, :]
        return s, jnp.einsum('bhvk,bhk->bhv', s, r_t)

    xs = tuple(jnp.moveaxis(z, 1, 0) for z in (r, w, k, v, a, b))
    s, ys = lax.scan(step, s0, xs)
    return s, jnp.moveaxis(ys, 0, 1)


def rwkv_direction_inputs(f, d, rev):
    ins = (f['r'], jnp.exp(f['log_decay'][:, :, d]), f['k'][:, :, d], f['v'], -f['kk'], f['kk'] * f['a'][:, :, d])
    return tuple(_flip(z, rev, 1) for z in ins)


def rwkv_output(y, f, lp):
    b, t = y.shape[:2]
    mean = jnp.mean(y, axis=-1, keepdims=True)
    var = jnp.mean(jnp.square(y - mean), axis=-1, keepdims=True)
    y = ((y - mean) * lax.rsqrt(var + RWKV_GN_EPS)).reshape(b, t, BRANCH_WIDTH) * lp['rwkv_ln_g']
    bonus = jnp.sum(f['r'][:, :, None] * f['k'] * lp['rwkv_r_k'], axis=-1, keepdims=True) * f['v'][:, :, None]
    bonus = jnp.sum(bonus, axis=2).reshape(b, t, BRANCH_WIDTH)
    return ((y + bonus) * f['g']).astype(f['g'].dtype)


def rwkv_mixer(p, pc, lp):
    f = rwkv_features(p, lp)
    fc = rwkv_features(pc, lp)
    bsz = p.shape[0]
    ys, ycs = [], []
    for d in range(2):
        rev = d == 1
        s0 = jnp.zeros((bsz, RWKV_HEADS, RWKV_HEAD_DIM, RWKV_HEAD_DIM), F32)
        s_ctx, y_c = rwkv_scan(s0, *rwkv_direction_inputs(fc, d, rev))
        _, y_l = rwkv_scan(s_ctx, *rwkv_direction_inputs(f, d, rev))
        ys.append(_flip(y_l, rev, 1))
        ycs.append(_flip(y_c, rev, 1))
    return rwkv_output(ys[0] + ys[1], f, lp), rwkv_output(ycs[0] + ycs[1], fc, lp)


def merge_branches(ys, gate_proj, w_branch, w_out):
    gates = jax.nn.sigmoid(gate_proj.reshape(gate_proj.shape[:-1] + (N_BRANCH, D_MODEL)))
    proj = jnp.einsum('btnw,nwd->btnd', jnp.stack(ys, axis=-2), w_branch)
    return jnp.sum(gates * proj, axis=-2) @ w_out


def hybrid_layer(h, hc, mod, mod_c, rope, lp, update_ctx):
    sh1, sc1, g1, sh2, sc2, g2 = jnp.split(mod, 6, axis=-1)
    sh1c, sc1c, g1c, sh2c, sc2c, g2c = jnp.split(mod_c, 6, axis=-1)
    u = modulate(rms_norm(h, lp['norm1_g']), sh1, sc1)
    uc = modulate(rms_norm(hc, lp['norm1_g']), sh1c, sc1c)
    p_ret, p_swa, p_rwkv, p_gate = _split(u @ lp['w_in'], IN_SPLITS)
    pc_ret, pc_swa, pc_rwkv, pc_gate = _split(uc @ lp['w_in'], IN_SPLITS)
    y_ret, yc_ret = retention_mixer(p_ret, pc_ret, lp['ret_decay'])
    y_swa, yc_swa = swa_mixer(p_swa, pc_swa, lp['swa_sink'], rope, update_ctx)
    y_rwkv, yc_rwkv = rwkv_mixer(p_rwkv, pc_rwkv, lp)
    h = h + g1 * merge_branches((y_ret, y_swa, y_rwkv), p_gate, lp['w_branch'], lp['w_out'])
    h = h + g2 * sq_relu_mlp(modulate(rms_norm(h, lp['norm2_g']), sh2, sc2), lp['w_ff1'], lp['w_ff2'])
    if update_ctx:
        hc = hc + g1c * merge_branches((yc_ret, yc_swa, yc_rwkv), pc_gate, lp['w_branch'], lp['w_out'])
        hc = hc + g2c * sq_relu_mlp(modulate(rms_norm(hc, lp['norm2_g']), sh2c, sc2c), lp['w_ff1'], lp['w_ff2'])
    return h, hc


def setup_inputs(seed: int = 0) -> dict:
    key = jax.random.key(seed)
    keys = iter(list(jax.random.split(key, 32)))

    def normal(shape, scale):
        return jax.random.normal(next(keys), shape, F32) * scale

    L, D, W = DEPTH, D_MODEL, BRANCH_WIDTH
    ret_base = jnp.log(2.0 ** (5.0 + jnp.arange(RET_HEADS, dtype=F32)) - 1.0)
    return {
        'x': normal((BATCH, SEQ, D), 1.0),
        'c': normal((BATCH, D), 1.0),
        'ctx': normal((BATCH, CTX_LEN, D), 1.0),
        'c_ctx': normal((D,), 1.0),
        'norm1_g': 1.0 + normal((L, D), 0.02),
        'norm2_g': 1.0 + normal((L, D), 0.02),
        'w_mod': normal((L, D, 6 * D), 0.5 * D ** -0.5),
        'b_mod': normal((L, 6 * D), 0.01),
        'w_in': normal((L, D, N_IN), D ** -0.5),
        'ret_decay': ret_base + normal((L, 2, RET_HEADS), 0.1),
        'swa_sink': normal((L, SWA_Q_HEADS), 0.5),
        'rwkv_mu': jax.random.uniform(next(keys), (L, 2, RWKV_IN), F32, 0.0, 0.5),
        'rwkv_w0': jnp.linspace(-6.5, -1.5, W, dtype=F32) + normal((L, 2, W), 0.1),
        'rwkv_w_up': normal((L, 2, RWKV_W_RANK, W), 0.1 * RWKV_W_RANK ** -0.5),
        'rwkv_a0': normal((L, 2, W), 0.1),
        'rwkv_a_up': normal((L, 2, RWKV_A_RANK, W), 0.1 * RWKV_A_RANK ** -0.5),
        'rwkv_g_up': normal((L, RWKV_G_RANK, W), RWKV_G_RANK ** -0.5),
        'rwkv_k_k': 0.85 + normal((L, W), 0.02),
        'rwkv_k_a': 1.0 + normal((L, W), 0.02),
        'rwkv_r_k': normal((L, RWKV_HEADS, RWKV_HEAD_DIM), 0.1),
        'rwkv_ln_g': 1.0 + normal((L, W), 0.02),
        'w_branch': normal((L, N_BRANCH, W, D), W ** -0.5),
        'w_out': normal((L, D, D), D ** -0.5),
        'w_ff1': normal((L, D, D_FF), D ** -0.5),
        'w_ff2': normal((L, D_FF, D), D_FF ** -0.5),
        'final_g': 1.0 + normal((D,), 0.02),
    }


def reference(x, c, ctx, c_ctx, norm1_g, norm2_g, w_mod, b_mod, w_in, ret_decay, swa_sink, rwkv_mu,
              rwkv_w0, rwkv_w_up, rwkv_a0, rwkv_a_up, rwkv_g_up, rwkv_k_k, rwkv_k_a, rwkv_r_k, rwkv_ln_g,
              w_branch, w_out, w_ff1, w_ff2, final_g):
    seq = x.shape[1]
    rows = seq // GRID_W
    row = jnp.repeat(jnp.arange(rows), GRID_W)
    col = jnp.tile(jnp.arange(GRID_W), rows)
    rope = axial_rope_tables(row, col)
    silu_c = jax.nn.silu(c)
    silu_cc = jax.nn.silu(c_ctx)
    h, hc = x, ctx
    for layer in range(DEPTH):
        mod = (silu_c @ w_mod[layer] + b_mod[layer])[:, None, :]
        mod_c = silu_cc @ w_mod[layer] + b_mod[layer]
        lp = {'norm1_g': norm1_g[layer], 'norm2_g': norm2_g[layer], 'w_in': w_in[layer],
              'ret_decay': ret_decay[layer], 'swa_sink': swa_sink[layer], 'rwkv_mu': rwkv_mu[layer],
              'rwkv_w0': rwkv_w0[layer], 'rwkv_w_up': rwkv_w_up[layer], 'rwkv_a0': rwkv_a0[layer],
              'rwkv_a_up': rwkv_a_up[layer], 'rwkv_g_up': rwkv_g_up[layer], 'rwkv_k_k': rwkv_k_k[layer],
              'rwkv_k_a': rwkv_k_a[layer], 'rwkv_r_k': rwkv_r_k[layer], 'rwkv_ln_g': rwkv_ln_g[layer],
              'w_branch': w_branch[layer], 'w_out': w_out[layer], 'w_ff1': w_ff1[layer], 'w_ff2': w_ff2[layer]}
        h, hc = hybrid_layer(h, hc, mod, mod_c, rope, lp, layer < DEPTH - 1)
    return rms_norm(h, final_g)
```

```python
import functools
import math

import jax
import jax.numpy as jnp
from jax import lax
from jax.experimental import pallas as pl
from jax.experimental.pallas import tpu as pltpu

F32 = jnp.float32
BF16 = jnp.bfloat16
HI = lax.Precision.HIGHEST

D = 2048
B = 4
SEQ = 2048
CTX = 256
TB = CTX + SEQ
R = B * TB
L = 4
GRID_W = 64
EPS = 1e-6
ROPE_BASE = 10000.0
NEG_INF = -1e30
W = D // 2
HD = 128
RET_HEADS = W // HD
RET_CHUNK = 128
RET_GN_EPS = 1e-5
SWA_Q_HEADS = W // HD
SWA_KV_HEADS = SWA_Q_HEADS // 4
SWA_GROUP = SWA_Q_HEADS // SWA_KV_HEADS
SWA_WINDOW = 128
SWA_BLOCK = 128
RW_N = 64
RW_HEADS = W // RW_N
RW_RANK = 64
RW_G_RANK = 128
RW_GN_EPS = 64e-5
RW_C = 64
RW_GW = 256
RW_GROUPS = W // RW_GW
RW_SMALL = 4 * RW_RANK + RW_G_RANK
D_FF = 4 * D
RET_IN = 4 * W
SWA_IN = (SWA_Q_HEADS + 2 * SWA_KV_HEADS) * HD
RWKV_IN = 3 * W + RW_SMALL
GATE_IN = 3 * D

VMEM_LIMIT = 56 * 1024 * 1024


def _cparams(sem):
    return pltpu.CompilerParams(dimension_semantics=sem, vmem_limit_bytes=VMEM_LIMIT)


def _mod_kernel(x_ref, w_ref, b_ref, o_ref):
    x = x_ref[...]
    s = x * jax.nn.sigmoid(x)
    acc = jnp.dot(s.astype(BF16), w_ref[0].astype(BF16), preferred_element_type=F32)
    o_ref[0] = acc + b_ref[0]


def mod_vectors(cond8, w_mod, b_mod):
    tn = 1024
    n = w_mod.shape[-1]
    return pl.pallas_call(
        _mod_kernel,
        grid=(L, n // tn),
        in_specs=[
            pl.BlockSpec((8, D), lambda l, j: (0, 0)),
            pl.BlockSpec((1, D, tn), lambda l, j: (l, 0, j)),
            pl.BlockSpec((1, 1, tn), lambda l, j: (l, 0, j)),
        ],
        out_specs=pl.BlockSpec((1, 8, tn), lambda l, j: (l, 0, j)),
        out_shape=jax.ShapeDtypeStruct((L, 8, n), F32),
        compiler_params=_cparams(("parallel", "parallel")),
        name="mod_vectors",
    )(cond8, w_mod, b_mod.reshape(L, 1, n))


NORM_TM = 256
TILES_PER_BATCH = TB // NORM_TM


def _mod_row(i):
    return jnp.where(i % TILES_PER_BATCH == 0, B, i // TILES_PER_BATCH)


def _norm_mod_kernel(x_ref, g_ref, sh_ref, sc_ref, o_ref):
    row = _mod_row(pl.program_id(0))
    x = x_ref[...]
    ms = jnp.mean(x * x, axis=-1, keepdims=True)
    y = x * lax.rsqrt(ms + EPS) * g_ref[...]
    shift = sh_ref[0, pl.ds(row, 1), :]
    scale = sc_ref[0, pl.ds(row, 1), :]
    o_ref[...] = (y * (1.0 + scale) + shift).astype(o_ref.dtype)


def norm_modulate(h, g, mods_l, which_shift, which_scale):
    return pl.pallas_call(
        _norm_mod_kernel,
        grid=(R // NORM_TM,),
        in_specs=[
            pl.BlockSpec((NORM_TM, D), lambda i: (i, 0)),
            pl.BlockSpec((1, D), lambda i: (0, 0)),
            pl.BlockSpec((1, 8, D), lambda i: (which_shift, 0, 0)),
            pl.BlockSpec((1, 8, D), lambda i: (which_scale, 0, 0)),
        ],
        out_specs=pl.BlockSpec((NORM_TM, D), lambda i: (i, 0)),
        out_shape=jax.ShapeDtypeStruct((R, D), BF16),
        compiler_params=_cparams(("parallel",)),
        name="norm_modulate",
    )(h, g.reshape(1, D), mods_l, mods_l)


def _final_norm_kernel(x_ref, g_ref, o_ref):
    x = x_ref[...]
    ms = jnp.mean(x * x, axis=-1, keepdims=True)
    o_ref[...] = x * lax.rsqrt(ms + EPS) * g_ref[...]


def final_norm(h, g):
    per = SEQ // NORM_TM
    return pl.pallas_call(
        _final_norm_kernel,
        grid=(B, per),
        in_specs=[
            pl.BlockSpec((NORM_TM, D), lambda b, j: (b * TILES_PER_BATCH + CTX // NORM_TM + j, 0)),
            pl.BlockSpec((1, D), lambda b, j: (0, 0)),
        ],
        out_specs=pl.BlockSpec((NORM_TM, D), lambda b, j: (b * per + j, 0)),
        out_shape=jax.ShapeDtypeStruct((B * SEQ, D), F32),
        compiler_params=_cparams(("parallel", "parallel")),
        name="final_norm",
    )(h, g.reshape(1, D))


MM_TM = 1152


def _row_is_ctx(i, tm):
    rows = lax.broadcasted_iota(jnp.int32, (tm, 1), 0)
    return jnp.logical_and(i % (TB // tm) == 0, rows < CTX)


def _mm_kernel(*refs, nk, epilogue, tm):
    if epilogue == "residual":
        x_ref, w_ref, h_ref, gate_ref, o_ref = refs[:5]
        rest = refs[5:]
    else:
        x_ref, w_ref, o_ref = refs[:3]
        rest = refs[3:]
    k = pl.program_id(2)
    part = jnp.dot(x_ref[...], w_ref[...], preferred_element_type=F32)

    def finish(acc):
        if epilogue == "relu2":
            a = jnp.maximum(acc, 0.0)
            o_ref[...] = (a * a).astype(o_ref.dtype)
        elif epilogue == "residual":
            i = pl.program_id(0)
            bidx = i // (TB // tm)
            g_b = gate_ref[0, pl.ds(bidx, 1), :]
            g_c = gate_ref[0, pl.ds(B, 1), :]
            gate = jnp.where(_row_is_ctx(i, tm), g_c, g_b)
            o_ref[...] = h_ref[...] + gate * acc
        else:
            o_ref[...] = acc.astype(o_ref.dtype)

    if nk == 1:
        finish(part)
    else:
        acc_ref = rest[0]

        @pl.when(k == 0)
        def _():
            acc_ref[...] = part

        @pl.when(jnp.logical_and(k > 0, k < nk - 1))
        def _():
            acc_ref[...] += part

        @pl.when(k == nk - 1)
        def _():
            finish(acc_ref[...] + part)


def matmul(x, w, *, tn, tk=None, out_dtype=F32, epilogue="plain", h=None, mods_l=None, which_gate=None,
           name="matmul"):
    m, kdim = x.shape
    n = w.shape[1]
    tm = MM_TM
    tk = kdim if tk is None else tk
    nk = kdim // tk
    in_specs = [
        pl.BlockSpec((tm, tk), lambda i, j, k: (i, k)),
        pl.BlockSpec((tk, tn), lambda i, j, k: (k, j)),
    ]
    args = [x, w]
    aliases = {}
    if epilogue == "residual":
        in_specs += [
            pl.BlockSpec((tm, tn), lambda i, j, k: (i, j)),
            pl.BlockSpec((1, 8, tn), lambda i, j, k: (which_gate, 0, j)),
        ]
        args += [h, mods_l]
        aliases = {2: 0}
    scratch = [pltpu.VMEM((tm, tn), F32)] if nk > 1 else []
    return pl.pallas_call(
        functools.partial(_mm_kernel, nk=nk, epilogue=epilogue, tm=tm),
        grid=(m // tm, n // tn, nk),
        in_specs=in_specs,
        out_specs=pl.BlockSpec((tm, tn), lambda i, j, k: (i, j)),
        out_shape=jax.ShapeDtypeStruct((m, n), out_dtype),
        scratch_shapes=scratch,
        input_output_aliases=aliases,
        compiler_params=_cparams(("parallel", "parallel", "arbitrary")),
        name=name,
    )(*args)


RET_NCHUNK = TB // RET_CHUNK
RET_CTX_CHUNKS = CTX // RET_CHUNK


def _log_sigmoid(x):
    return jnp.minimum(x, 0.0) - jnp.log(1.0 + jnp.exp(-jnp.abs(x)))


def _ret_kernel(dec_ref, q_ref, k_ref, v_ref, g_ref, o_ref, y_ref, s_ref):
    hh = pl.program_id(1)
    c = RET_CHUNK
    ri = lax.broadcasted_iota(jnp.int32, (c, c), 0).astype(F32)
    ci = lax.broadcasted_iota(jnp.int32, (c, c), 1).astype(F32)
    kscale = HD ** -0.5
    for direction in range(2):
        lg = _log_sigmoid(jnp.full((c, c), dec_ref[direction, hh], F32))
        diff = (ri - ci) if direction == 0 else (ci - ri)
        intra = jnp.where(diff >= 0, jnp.exp(lg * jnp.maximum(diff, 0.0)), 0.0)
        pos = ri if direction == 0 else (c - 1.0) - ri
        q_decay = jnp.exp(lg * (pos + 1.0))
        k_decay = jnp.exp(lg * ((c - 1.0) - pos))
        chunk_decay = jnp.exp(lg * float(c))
        s_ref[...] = jnp.zeros_like(s_ref)

        def step(s, carry, direction=direction, intra=intra, q_decay=q_decay, k_decay=k_decay,
                 chunk_decay=chunk_decay):
            if direction == 0:
                ch = s
            else:
                ch = jnp.where(s < RET_CTX_CHUNKS, RET_CTX_CHUNKS - 1 - s, RET_NCHUNK - 1 + RET_CTX_CHUNKS - s)
            rows = pl.ds(pl.multiple_of(ch * c, c), c)
            qc = q_ref[rows, :]
            kc = k_ref[rows, :] * kscale
            vc = v_ref[rows, :].astype(BF16)
            sc = lax.dot_general(qc.astype(BF16), kc.astype(BF16), (((1,), (1,)), ((), ())),
                                 preferred_element_type=F32) * intra
            st = s_ref[...]
            y = jnp.dot(sc.astype(BF16), vc, preferred_element_type=F32)
            y = y + jnp.dot((qc * q_decay).astype(BF16), st.astype(BF16), preferred_element_type=F32)
            kd_t = (kc * k_decay).T.astype(BF16)
            s_ref[...] = st * chunk_decay + jnp.dot(kd_t, vc, preferred_element_type=F32)
            if direction == 0:
                y_ref[rows, :] = y
            else:
                y_ref[rows, :] += y
            return carry

        lax.fori_loop(0, RET_NCHUNK, step, 0)

    y = y_ref[...]
    mean = jnp.mean(y, axis=-1, keepdims=True)
    yc = y - mean
    var = jnp.mean(yc * yc, axis=-1, keepdims=True)
    yn = yc * lax.rsqrt(var + RET_GN_EPS)
    g = g_ref[...]
    o_ref[...] = (g * jax.nn.sigmoid(g) * yn).astype(o_ref.dtype)


def retention_mixer(p_ret, decay_l):
    nh = RET_HEADS
    blk = lambda off: pl.BlockSpec((TB, HD), lambda b, h, off=off: (b, off + h))
    return pl.pallas_call(
        _ret_kernel,
        grid=(B, nh),
        in_specs=[pl.BlockSpec(memory_space=pltpu.SMEM), blk(0), blk(nh), blk(2 * nh), blk(3 * nh)],
        out_specs=pl.BlockSpec((TB, HD), lambda b, h: (b, h)),
        out_shape=jax.ShapeDtypeStruct((R, W), BF16),
        scratch_shapes=[pltpu.VMEM((TB, HD), F32), pltpu.VMEM((HD, HD), F32)],
        compiler_params=_cparams(("parallel", "parallel")),
        name="retention",
    )(decay_l, p_ret, p_ret, p_ret, p_ret)


SWA_NBLK = SEQ // SWA_BLOCK
SWA_WIN = 3 * SWA_BLOCK


def _rope(x, cos, sin_signed):
    lane = lax.broadcasted_iota(jnp.int32, x.shape, 1)
    partner = jnp.where(lane % 64 < 32, pltpu.roll(x, 96, 1), pltpu.roll(x, 32, 1))
    return x * cos + partner * sin_signed


def _swa_kernel(sink_ref, q_ref, k_ref, v_ref, cos_ref, sin_ref, o_ref, kr_ref, vb_ref):
    kvh = pl.program_id(1)
    scale = HD ** -0.5
    g = SWA_GROUP
    blk = SWA_BLOCK

    vb_ref[...] = v_ref[...].astype(BF16)
    kr_ref[...] = _rope(k_ref[CTX:, :], cos_ref[...], sin_ref[...]).astype(BF16)
    kc = k_ref[:CTX, :].astype(BF16)
    vc = vb_ref[:CTX, :]

    def sink_col(rows_per_head):
        parts = [jnp.full((rows_per_head, 1), sink_ref[kvh * g + gi], F32) for gi in range(g)]
        return jnp.concatenate(parts, axis=0)

    nt = (((1,), (1,)), ((), ()))

    qc = jnp.concatenate([q_ref[:CTX, gi * HD:(gi + 1) * HD] for gi in range(g)], axis=0).astype(BF16)
    s = lax.dot_general(qc, kc, nt, preferred_element_type=F32) * scale
    sk = sink_col(CTX)
    m = jnp.maximum(jnp.max(s, axis=-1, keepdims=True), sk)
    e = jnp.exp(s - m)
    den = jnp.sum(e, axis=-1, keepdims=True) + jnp.exp(sk - m)
    oc = jnp.dot(e.astype(BF16), vc, preferred_element_type=F32) / den
    for gi in range(g):
        o_ref[:CTX, gi * HD:(gi + 1) * HD] = oc[gi * CTX:(gi + 1) * CTX].astype(o_ref.dtype)

    sk_b = sink_col(blk)
    rloc = lax.broadcasted_iota(jnp.int32, (g * blk, SWA_WIN), 0) % blk
    cloc = lax.broadcasted_iota(jnp.int32, (g * blk, SWA_WIN), 1)

    def body(n, carry):
        r0 = pl.multiple_of(n * blk, blk)
        cos = cos_ref[pl.ds(r0, blk), :]
        sin = sin_ref[pl.ds(r0, blk), :]
        qs = [_rope(q_ref[pl.ds(CTX + r0, blk), gi * HD:(gi + 1) * HD], cos, sin) for gi in range(g)]
        qn = jnp.concatenate(qs, axis=0).astype(BF16)
        ws = pl.multiple_of(jnp.clip((n - 1) * blk, 0, SEQ - SWA_WIN), blk)
        kw = kr_ref[pl.ds(ws, SWA_WIN), :]
        vw = vb_ref[pl.ds(CTX + ws, SWA_WIN), :]
        s_win = lax.dot_general(qn, kw, nt, preferred_element_type=F32) * scale
        qpos = n * blk + rloc
        kpos = ws + cloc
        valid = jnp.abs(qpos - kpos) <= SWA_WINDOW
        s_win = jnp.where(valid, s_win, NEG_INF)
        s_ctx = lax.dot_general(qn, kc, nt, preferred_element_type=F32) * scale
        m = jnp.maximum(jnp.maximum(jnp.max(s_win, axis=-1, keepdims=True),
                                    jnp.max(s_ctx, axis=-1, keepdims=True)), sk_b)
        e_win = jnp.exp(s_win - m)
        e_ctx = jnp.exp(s_ctx - m)
        den = (jnp.sum(e_win, axis=-1, keepdims=True) + jnp.sum(e_ctx, axis=-1, keepdims=True)
               + jnp.exp(sk_b - m))
        o = (jnp.dot(e_win.astype(BF16), vw, preferred_element_type=F32)
             + jnp.dot(e_ctx.astype(BF16), vc, preferred_element_type=F32)) / den
        for gi in range(g):
            o_ref[pl.ds(CTX + r0, blk), gi * HD:(gi + 1) * HD] = o[gi * blk:(gi + 1) * blk].astype(o_ref.dtype)
        return carry

    lax.fori_loop(0, SWA_NBLK, body, 0)


def swa_mixer(p_swa, sink_l, cos_full, sin_signed):
    gw = SWA_GROUP * HD
    return pl.pallas_call(
        _swa_kernel,
        grid=(B, SWA_KV_HEADS),
        in_specs=[
            pl.BlockSpec(memory_space=pltpu.SMEM),
            pl.BlockSpec((TB, gw), lambda b, kv: (b, kv)),
            pl.BlockSpec((TB, HD), lambda b, kv: (b, SWA_Q_HEADS + kv)),
            pl.BlockSpec((TB, HD), lambda b, kv: (b, SWA_Q_HEADS + SWA_KV_HEADS + kv)),
            pl.BlockSpec((SEQ, HD), lambda b, kv: (0, 0)),
            pl.BlockSpec((SEQ, HD), lambda b, kv: (0, 0)),
        ],
        out_specs=pl.BlockSpec((TB, gw), lambda b, kv: (b, kv)),
        out_shape=jax.ShapeDtypeStruct((R, W), BF16),
        scratch_shapes=[pltpu.VMEM((SEQ, HD), BF16), pltpu.VMEM((TB, HD), BF16)],
        compiler_params=_cparams(("parallel", "parallel")),
        name="swa",
    )(sink_l, p_swa, p_swa, p_swa, cos_full, sin_signed)


def rope_tables():
    rows = SEQ // GRID_W
    row = jnp.repeat(jnp.arange(rows), GRID_W).astype(F32)
    col = jnp.tile(jnp.arange(GRID_W), rows).astype(F32)
    n_freq = HD // 4
    inv_freq = ROPE_BASE ** (-jnp.arange(n_freq, dtype=F32) / n_freq)
    ang_r = row[:, None] * inv_freq
    ang_c = col[:, None] * inv_freq
    cr, sr, cc, sc = jnp.cos(ang_r), jnp.sin(ang_r), jnp.cos(ang_c), jnp.sin(ang_c)
    cos_full = jnp.concatenate([cr, cr, cc, cc], axis=-1)
    sin_signed = jnp.concatenate([-sr, sr, -sc, sc], axis=-1)
    return cos_full, sin_signed


def _shift_kernel(x_ref, mu_ref, o_ref):
    x = x_ref[...]
    row = lax.broadcasted_iota(jnp.int32, (TB, 1), 0)
    prev = pltpu.roll(x, 1, 0)
    prev = jnp.where(jnp.logical_or(row == 0, row == CTX), 0.0, prev)
    nxt = pltpu.roll(x, TB - 1, 0)
    nxt = jnp.where(jnp.logical_or(row == CTX - 1, row == TB - 1), 0.0, nxt)
    o_ref[...] = x + mu_ref[0:1, :] * (prev - x) + mu_ref[1:2, :] * (nxt - x)


def token_shift(p, mu, tf):
    f = p.shape[1]
    return pl.pallas_call(
        _shift_kernel,
        grid=(B, f // tf),
        in_specs=[pl.BlockSpec((TB, tf), lambda b, j: (b, j)), pl.BlockSpec((2, tf), lambda b, j: (0, j))],
        out_specs=pl.BlockSpec((TB, tf), lambda b, j: (b, j)),
        out_shape=jax.ShapeDtypeStruct(p.shape, F32),
        compiler_params=_cparams(("parallel", "parallel")),
        name="token_shift",
    )(p, mu)


RW_NCHUNK = TB // RW_C
RW_CTX_CHUNKS = CTX // RW_C


def _rw_chunk(direction, s):
    back = jnp.where(s < RW_CTX_CHUNKS, RW_CTX_CHUNKS - 1 - s, RW_NCHUNK - 1 + RW_CTX_CHUNKS - s)
    return jnp.where(direction == 0, s, back)


def _head_ones():
    r = lax.broadcasted_iota(jnp.int32, (RW_GW, RW_GW), 0) // RW_N
    c = lax.broadcasted_iota(jnp.int32, (RW_GW, RW_GW), 1) // RW_N
    return r == c


def _dot_hi(a, b):
    return jnp.dot(a, b, preferred_element_type=F32, precision=HI)


def _dot_nt_hi(a, b):
    return lax.dot_general(a, b, (((1,), (1,)), ((), ())), preferred_element_type=F32, precision=HI)


def _rwkv_scan_kernel(r_ref, k_ref, v_ref, sm_ref, wup_ref, aup_ref, w0_ref, a0_ref, kk_ref, ka_ref, rk_ref,
                      y_ref, bonus_ref, st_ref):
    direction = pl.program_id(1)
    step = pl.program_id(3)
    c = RW_C

    @pl.when(step == 0)
    def _():
        st_ref[...] = jnp.zeros_like(st_ref)

    same_head = _head_ones()
    ones_bd = jnp.where(same_head, 1.0, 0.0)

    def bd(x):
        return jnp.where(same_head, jnp.concatenate([x] * (RW_GW // c), axis=0), 0.0)

    r = r_ref[...]
    k = k_ref[...]
    v = v_ref[...]
    sm = sm_ref[...]
    w_raw = w0_ref[0] + jnp.dot(jnp.tanh(sm[:, :2 * RW_RANK]).astype(BF16), wup_ref[0],
                                preferred_element_type=F32)
    logw = -jax.nn.sigmoid(w_raw) * math.exp(-0.5)
    alpha = jax.nn.sigmoid(a0_ref[0] + jnp.dot(sm[:, 2 * RW_RANK:4 * RW_RANK].astype(BF16), aup_ref[0],
                                               preferred_element_type=F32))
    kkp = k * kk_ref[...]
    ssq = _dot_hi(kkp * kkp, ones_bd)
    kk = kkp / jnp.maximum(jnp.sqrt(ssq), 1e-12)
    kdir = k * (1.0 + (alpha - 1.0) * ka_ref[...])
    bonus_ref[0] = _dot_hi(r * kdir * rk_ref[...], ones_bd) * v
    a_vec = -kk
    b_vec = kk * alpha

    sgn = 1 - 2 * direction
    tt = lax.broadcasted_iota(jnp.int32, (c, c), 0)
    ss = lax.broadcasted_iota(jnp.int32, (c, c), 1)
    tri = jnp.where((tt - ss) * sgn >= 0, 1.0, 0.0)
    cum = _dot_hi(tri, logw)
    total = jnp.sum(logw, axis=0, keepdims=True)
    e_in = jnp.exp(cum)
    e_out = jnp.exp(-cum)
    a_t = a_vec * jnp.exp(cum - logw)
    r_t = r * e_in
    b_t = b_vec * e_out
    k_t = kdir * e_out
    e_left = jnp.exp(total - cum)
    b_l = b_vec * e_left
    k_l = kdir * e_left

    t4 = lax.broadcasted_iota(jnp.int32, (c, RW_GW), 0)
    s4 = lax.broadcasted_iota(jnp.int32, (c, RW_GW), 1) % c
    d4 = (t4 - s4) * sgn
    strict = d4 > 0
    incl = d4 >= 0

    lhs = jnp.concatenate([a_t, r_t], axis=0)
    gb = _dot_nt_hi(lhs, bd(b_t))
    gk = _dot_nt_hi(lhs, bd(k_t))
    a_ab = jnp.where(strict, gb[:c], 0.0)
    a_ak = jnp.where(strict, gk[:c], 0.0)
    a_rb = jnp.where(incl, gb[c:], 0.0)
    a_rk = jnp.where(incl, gk[c:], 0.0)

    tinv = jnp.where(d4 == 0, 1.0, 0.0) + a_ab
    npow = a_ab
    for _ in range(int(math.log2(c)) - 1):
        npow = _dot_hi(npow, bd(npow))
        tinv = tinv + _dot_hi(tinv, bd(npow))

    st = st_ref[...]
    u = _dot_hi(lhs, st)
    bdv = bd(v)
    e_mat = _dot_hi(tinv, bd(u[:c] + _dot_hi(a_ak, bdv)))
    bde = bd(e_mat)
    y_ref[0] = u[c:] + _dot_hi(a_rb, bde) + _dot_hi(a_rk, bdv)

    left_t = jnp.concatenate([b_l, k_l], axis=0).T
    upd = _dot_hi(left_t, jnp.concatenate([e_mat, v], axis=0))
    decay_col = jnp.exp(jnp.sum(logw.T, axis=1, keepdims=True))
    st_ref[...] = st * decay_col + jnp.where(same_head, upd, 0.0)


def rwkv_scan(ps_rkv, ps_small, wup2, aup2, w0, a0, k_k, k_a, r_k):
    ng = RW_GROUPS
    nchunk = RW_NCHUNK

    def rowblk(b, d, s):
        return b * nchunk + _rw_chunk(d, s)

    def feat(off):
        return pl.BlockSpec((RW_C, RW_GW), lambda b, d, g, s, off=off: (rowblk(b, d, s), off + g))

    vec = pl.BlockSpec((1, RW_GW), lambda b, d, g, s: (0, g))
    dvec = pl.BlockSpec((1, 1, RW_GW), lambda b, d, g, s: (d, 0, g))
    lora = pl.BlockSpec((1, 2 * RW_RANK, RW_GW), lambda b, d, g, s: (d, 0, g))
    out = pl.BlockSpec((1, RW_C, RW_GW), lambda b, d, g, s: (d, rowblk(b, d, s), g))
    return pl.pallas_call(
        _rwkv_scan_kernel,
        grid=(B, 2, ng, nchunk),
        in_specs=[feat(0), feat(ng), feat(2 * ng),
                  pl.BlockSpec((RW_C, RW_SMALL), lambda b, d, g, s: (rowblk(b, d, s), 0)),
                  lora, lora, dvec, dvec, vec, vec, vec],
        out_specs=[out, out],
        out_shape=[jax.ShapeDtypeStruct((2, R, W), F32), jax.ShapeDtypeStruct((2, R, W), F32)],
        scratch_shapes=[pltpu.VMEM((RW_GW, RW_GW), F32)],
        compiler_params=_cparams(("parallel", "parallel", "parallel", "arbitrary")),
        name="rwkv_scan",
    )(ps_rkv, ps_rkv, ps_rkv, ps_small, wup2, aup2, w0, a0, k_k, k_a, r_k)


RWO_TM = 256


def _rwkv_out_kernel(y_ref, bonus_ref, sm_ref, gup_ref, lng_ref, o_ref):
    ones_bd = jnp.where(_head_ones(), 1.0 / RW_N, 0.0)
    y = y_ref[0] + y_ref[1]
    mean = _dot_hi(y, ones_bd)
    yc = y - mean
    var = _dot_hi(yc * yc, ones_bd)
    yn = yc * lax.rsqrt(var + RW_GN_EPS) * lng_ref[...]
    gate = jnp.dot(jax.nn.sigmoid(sm_ref[...]).astype(BF16), gup_ref[...], preferred_element_type=F32)
    o_ref[...] = ((yn + bonus_ref[0] + bonus_ref[1]) * gate).astype(o_ref.dtype)


def rwkv_output(y2, bonus2, ps_small, g_up, ln_g):
    pair = pl.BlockSpec((2, RWO_TM, RW_GW), lambda i, g: (0, i, g))
    return pl.pallas_call(
        _rwkv_out_kernel,
        grid=(R // RWO_TM, RW_GROUPS),
        in_specs=[pair, pair,
                  pl.BlockSpec((RWO_TM, RW_G_RANK), lambda i, g: (i, 4 * RW_RANK // RW_G_RANK)),
                  pl.BlockSpec((RW_G_RANK, RW_GW), lambda i, g: (0, g)),
                  pl.BlockSpec((1, RW_GW), lambda i, g: (0, g))],
        out_specs=pl.BlockSpec((RWO_TM, RW_GW), lambda i, g: (i, g)),
        out_shape=jax.ShapeDtypeStruct((R, W), BF16),
        compiler_params=_cparams(("parallel", "parallel")),
        name="rwkv_output",
    )(y2, bonus2, ps_small, g_up, ln_g)


MG_TM = 768
MG_TN = 512


def _merge_kernel(y0_ref, y1_ref, y2_ref, wb_ref, g0_ref, g1_ref, g2_ref, o_ref):
    acc = None
    for y_ref, g_ref, n in ((y0_ref, g0_ref, 0), (y1_ref, g1_ref, 1), (y2_ref, g2_ref, 2)):
        proj = jnp.dot(y_ref[...], wb_ref[n], preferred_element_type=F32)
        term = jax.nn.sigmoid(g_ref[...]) * proj
        acc = term if acc is None else acc + term
    o_ref[...] = acc.astype(o_ref.dtype)


def merge_branches(y_ret, y_swa, y_rwkv, p_gate, w_branch):
    nj = D // MG_TN
    ysp = pl.BlockSpec((MG_TM, W), lambda i, j: (i, 0))
    gsp = lambda n: pl.BlockSpec((MG_TM, MG_TN), lambda i, j, n=n: (i, n * nj + j))
    return pl.pallas_call(
        _merge_kernel,
        grid=(R // MG_TM, nj),
        in_specs=[ysp, ysp, ysp, pl.BlockSpec((3, W, MG_TN), lambda i, j: (0, 0, j)), gsp(0), gsp(1), gsp(2)],
        out_specs=pl.BlockSpec((MG_TM, MG_TN), lambda i, j: (i, j)),
        out_shape=jax.ShapeDtypeStruct((R, D), BF16),
        compiler_params=_cparams(("parallel", "parallel")),
        name="merge",
    )(y_ret, y_swa, y_rwkv, w_branch, p_gate, p_gate, p_gate)


def kernel(x, c, ctx, c_ctx, norm1_g, norm2_g, w_mod, b_mod, w_in, ret_decay, swa_sink, rwkv_mu, rwkv_w0,
           rwkv_w_up, rwkv_a0, rwkv_a_up, rwkv_g_up, rwkv_k_k, rwkv_k_a, rwkv_r_k, rwkv_ln_g, w_branch,
           w_out, w_ff1, w_ff2, final_g):
    h = jnp.concatenate([ctx, x], axis=1).reshape(R, D)
    cond8 = jnp.concatenate([c, c_ctx[None, :], jnp.zeros((8 - B - 1, D), F32)], axis=0)
    mods = mod_vectors(cond8, w_mod, b_mod)
    mods = mods.reshape(L, 8, 6, D).transpose(0, 2, 1, 3)
    cos_full, sin_signed = rope_tables()

    o_swa = RET_IN
    o_rw = RET_IN + SWA_IN
    o_small = o_rw + 3 * W
    o_gate = o_rw + RWKV_IN
    zeros_lora = jnp.zeros((L, RW_RANK, W), BF16)

    for l in range(L):
        m_l = mods[l]
        w_in_l = w_in[l].astype(BF16)
        u = norm_modulate(h, norm1_g[l], m_l, 0, 1)
        p_ret = matmul(u, w_in_l[:, :o_swa], tn=1024, name="in_ret")
        p_swa = matmul(u, w_in_l[:, o_swa:o_rw], tn=768, name="in_swa")
        p_rkv = matmul(u, w_in_l[:, o_rw:o_small], tn=1024, name="in_rkv")
        p_small = matmul(u, w_in_l[:, o_small:o_gate], tn=RW_SMALL, name="in_small")
        p_gate = matmul(u, w_in_l[:, o_gate:], tn=1024, name="in_gate")

        y_ret = retention_mixer(p_ret, ret_decay[l])
        y_swa = swa_mixer(p_swa, swa_sink[l], cos_full, sin_signed)

        mu = rwkv_mu[l]
        ps_rkv = token_shift(p_rkv, mu[:, :3 * W], 512)
        ps_small = token_shift(p_small, mu[:, 3 * W:], RW_SMALL)
        wup = rwkv_w_up[l].astype(BF16)
        aup = rwkv_a_up[l].astype(BF16)
        z = zeros_lora[l]
        wup2 = jnp.stack([jnp.concatenate([wup[0], z], axis=0), jnp.concatenate([z, wup[1]], axis=0)])
        aup2 = jnp.stack([jnp.concatenate([aup[0], z], axis=0), jnp.concatenate([z, aup[1]], axis=0)])
        y2, bonus2 = rwkv_scan(ps_rkv, ps_small, wup2, aup2,
                               rwkv_w0[l].reshape(2, 1, W), rwkv_a0[l].reshape(2, 1, W),
                               rwkv_k_k[l].reshape(1, W), rwkv_k_a[l].reshape(1, W), rwkv_r_k[l].reshape(1, W))
        y_rwkv = rwkv_output(y2, bonus2, ps_small, rwkv_g_up[l].astype(BF16), rwkv_ln_g[l].reshape(1, W))

        merged = merge_branches(y_ret, y_swa, y_rwkv, p_gate, w_branch[l].astype(BF16))
        h = matmul(merged, w_out[l].astype(BF16), tn=1024, epilogue="residual", h=h, mods_l=m_l,
                   which_gate=2, name="out_proj")
        u2 = norm_modulate(h, norm2_g[l], m_l, 3, 4)
        f = matmul(u2, w_ff1[l].astype(BF16), tn=1024, out_dtype=BF16, epilogue="relu2", name="ff1")
        h = matmul(f, w_ff2[l].astype(BF16), tn=1024, tk=2048, epilogue="residual", h=h, mods_l=m_l,
                   which_gate=5, name="ff2")

    return final_norm(h, final_g).reshape(B, SEQ, D)
```

```python
import functools
import math

import jax
import jax.numpy as jnp
from jax import lax
from jax.experimental import pallas as pl
from jax.experimental.pallas import tpu as pltpu

F32 = jnp.float32
BF16 = jnp.bfloat16

D = 2048
B = 4
SEQ = 2048
CTX = 256
TB = CTX + SEQ
R = B * TB
L = 4
GRID_W = 64
EPS = 1e-6
ROPE_BASE = 10000.0
NEG_INF = -1e30
W = D // 2
HD = 128
RET_HEADS = W // HD
RET_CHUNK = 128
RET_GN_EPS = 1e-5
SWA_Q_HEADS = W // HD
SWA_KV_HEADS = SWA_Q_HEADS // 4
SWA_GROUP = SWA_Q_HEADS // SWA_KV_HEADS
SWA_WINDOW = 128
SWA_BLOCK = 128
RW_N = 64
RW_HEADS = W // RW_N
RW_RANK = 64
RW_G_RANK = 128
RW_GN_EPS = 64e-5
RW_C = 64
RW_GW = 256
RW_GROUPS = W // RW_GW
RW_SMALL = 4 * RW_RANK + RW_G_RANK
D_FF = 4 * D
RET_IN = 4 * W
SWA_IN = (SWA_Q_HEADS + 2 * SWA_KV_HEADS) * HD
RWKV_IN = 3 * W + RW_SMALL
GATE_IN = 3 * D

VMEM_LIMIT = 56 * 1024 * 1024


def _cparams(sem):
    return pltpu.CompilerParams(dimension_semantics=sem, vmem_limit_bytes=VMEM_LIMIT)


def _mod_kernel(x_ref, w_ref, b_ref, o_ref):
    x = x_ref[...]
    s = x * jax.nn.sigmoid(x)
    acc = jnp.dot(s.astype(BF16), w_ref[0].astype(BF16), preferred_element_type=F32)
    o_ref[0] = acc + b_ref[0]


def mod_vectors(cond8, w_mod, b_mod):
    tn = 1024
    n = w_mod.shape[-1]
    return pl.pallas_call(
        _mod_kernel,
        grid=(L, n // tn),
        in_specs=[
            pl.BlockSpec((8, D), lambda l, j: (0, 0)),
            pl.BlockSpec((1, D, tn), lambda l, j: (l, 0, j)),
            pl.BlockSpec((1, 1, tn), lambda l, j: (l, 0, j)),
        ],
        out_specs=pl.BlockSpec((1, 8, tn), lambda l, j: (l, 0, j)),
        out_shape=jax.ShapeDtypeStruct((L, 8, n), F32),
        compiler_params=_cparams(("parallel", "parallel")),
        name="mod_vectors",
    )(cond8, w_mod, b_mod.reshape(L, 1, n))


NORM_TM = 256
TILES_PER_BATCH = TB // NORM_TM


def _mod_row(i):
    return jnp.where(i % TILES_PER_BATCH == 0, B, i // TILES_PER_BATCH)


def _norm_mod_kernel(x_ref, g_ref, sh_ref, sc_ref, o_ref):
    row = _mod_row(pl.program_id(0))
    x = x_ref[...]
    ms = jnp.mean(x * x, axis=-1, keepdims=True)
    y = x * lax.rsqrt(ms + EPS) * g_ref[...]
    shift = sh_ref[0, pl.ds(row, 1), :]
    scale = sc_ref[0, pl.ds(row, 1), :]
    o_ref[...] = (y * (1.0 + scale) + shift).astype(o_ref.dtype)


def norm_modulate(h, g, mods_l, which_shift, which_scale):
    return pl.pallas_call(
        _norm_mod_kernel,
        grid=(R // NORM_TM,),
        in_specs=[
            pl.BlockSpec((NORM_TM, D), lambda i: (i, 0)),
            pl.BlockSpec((1, D), lambda i: (0, 0)),
            pl.BlockSpec((1, 8, D), lambda i: (which_shift, 0, 0)),
            pl.BlockSpec((1, 8, D), lambda i: (which_scale, 0, 0)),
        ],
        out_specs=pl.BlockSpec((NORM_TM, D), lambda i: (i, 0)),
        out_shape=jax.ShapeDtypeStruct((R, D), BF16),
        compiler_params=_cparams(("parallel",)),
        name="norm_modulate",
    )(h, g.reshape(1, D), mods_l, mods_l)


def _final_norm_kernel(x_ref, g_ref, o_ref):
    x = x_ref[...]
    ms = jnp.mean(x * x, axis=-1, keepdims=True)
    o_ref[...] = x * lax.rsqrt(ms + EPS) * g_ref[...]


def final_norm(h, g):
    per = SEQ // NORM_TM
    return pl.pallas_call(
        _final_norm_kernel,
        grid=(B, per),
        in_specs=[
            pl.BlockSpec((NORM_TM, D), lambda b, j: (b * TILES_PER_BATCH + CTX // NORM_TM + j, 0)),
            pl.BlockSpec((1, D), lambda b, j: (0, 0)),
        ],
        out_specs=pl.BlockSpec((NORM_TM, D), lambda b, j: (b * per + j, 0)),
        out_shape=jax.ShapeDtypeStruct((B * SEQ, D), F32),
        compiler_params=_cparams(("parallel", "parallel")),
        name="final_norm",
    )(h, g.reshape(1, D))


MM_TM = 1152


def _row_is_ctx(i, tm):
    rows = lax.broadcasted_iota(jnp.int32, (tm, 1), 0)
    return jnp.logical_and(i % (TB // tm) == 0, rows < CTX)


def _mm_kernel(*refs, nk, epilogue, tm):
    if epilogue == "residual":
        x_ref, w_ref, h_ref, gate_ref, o_ref = refs[:5]
        rest = refs[5:]
    else:
        x_ref, w_ref, o_ref = refs[:3]
        rest = refs[3:]
    k = pl.program_id(2)
    part = jnp.dot(x_ref[...], w_ref[...], preferred_element_type=F32)

    def finish(acc):
        if epilogue == "relu2":
            a = jnp.maximum(acc, 0.0)
            o_ref[...] = (a * a).astype(o_ref.dtype)
        elif epilogue == "residual":
            i = pl.program_id(0)
            bidx = i // (TB // tm)
            g_b = gate_ref[0, pl.ds(bidx, 1), :]
            g_c = gate_ref[0, pl.ds(B, 1), :]
            gate = jnp.where(_row_is_ctx(i, tm), g_c, g_b)
            o_ref[...] = h_ref[...] + gate * acc
        else:
            o_ref[...] = acc.astype(o_ref.dtype)

    if nk == 1:
        finish(part)
    else:
        acc_ref = rest[0]

        @pl.when(k == 0)
        def _():
            acc_ref[...] = part

        @pl.when(jnp.logical_and(k > 0, k < nk - 1))
        def _():
            acc_ref[...] += part

        @pl.when(k == nk - 1)
        def _():
            finish(acc_ref[...] + part)


def matmul(x, w, *, tn, tk=None, out_dtype=F32, epilogue="plain", h=None, mods_l=None, which_gate=None,
           name="matmul"):
    m, kdim = x.shape
    n = w.shape[1]
    tm = MM_TM
    tk = kdim if tk is None else tk
    nk = kdim // tk
    in_specs = [
        pl.BlockSpec((tm, tk), lambda i, j, k: (i, k)),
        pl.BlockSpec((tk, tn), lambda i, j, k: (k, j)),
    ]
    args = [x, w]
    aliases = {}
    if epilogue == "residual":
        in_specs += [
            pl.BlockSpec((tm, tn), lambda i, j, k: (i, j)),
            pl.BlockSpec((1, 8, tn), lambda i, j, k: (which_gate, 0, j)),
        ]
        args += [h, mods_l]
        aliases = {2: 0}
    scratch = [pltpu.VMEM((tm, tn), F32)] if nk > 1 else []
    return pl.pallas_call(
        functools.partial(_mm_kernel, nk=nk, epilogue=epilogue, tm=tm),
        grid=(m // tm, n // tn, nk),
        in_specs=in_specs,
        out_specs=pl.BlockSpec((tm, tn), lambda i, j, k: (i, j)),
        out_shape=jax.ShapeDtypeStruct((m, n), out_dtype),
        scratch_shapes=scratch,
        input_output_aliases=aliases,
        compiler_params=_cparams(("parallel", "parallel", "arbitrary")),
        name=name,
    )(*args)


RET_NCHUNK = TB // RET_CHUNK
RET_CTX_CHUNKS = CTX // RET_CHUNK


def _log_sigmoid(x):
    return jnp.minimum(x, 0.0) - jnp.log(1.0 + jnp.exp(-jnp.abs(x)))


def _ret_kernel(dec_ref, q_ref, k_ref, v_ref, g_ref, o_ref, y_ref, s_ref):
    hh = pl.program_id(1)
    c = RET_CHUNK
    ri = lax.broadcasted_iota(jnp.int32, (c, c), 0).astype(F32)
    ci = lax.broadcasted_iota(jnp.int32, (c, c), 1).astype(F32)
    kscale = HD ** -0.5
    for direction in range(2):
        lg = _log_sigmoid(jnp.full((c, c), dec_ref[direction, hh], F32))
        diff = (ri - ci) if direction == 0 else (ci - ri)
        intra = jnp.where(diff >= 0, jnp.exp(lg * jnp.maximum(diff, 0.0)), 0.0)
        pos = ri if direction == 0 else (c - 1.0) - ri
        q_decay = jnp.exp(lg * (pos + 1.0))
        k_decay = jnp.exp(lg * ((c - 1.0) - pos))
        chunk_decay = jnp.exp(lg * float(c))
        s_ref[...] = jnp.zeros_like(s_ref)

        def step(s, carry, direction=direction, intra=intra, q_decay=q_decay, k_decay=k_decay,
                 chunk_decay=chunk_decay):
            if direction == 0:
                ch = s
            else:
                ch = jnp.where(s < RET_CTX_CHUNKS, RET_CTX_CHUNKS - 1 - s, RET_NCHUNK - 1 + RET_CTX_CHUNKS - s)
            rows = pl.ds(pl.multiple_of(ch * c, c), c)
            qc = q_ref[rows, :]
            kc = k_ref[rows, :] * kscale
            vc = v_ref[rows, :].astype(BF16)
            sc = lax.dot_general(qc.astype(BF16), kc.astype(BF16), (((1,), (1,)), ((), ())),
                                 preferred_element_type=F32) * intra
            st = s_ref[...]
            y = jnp.dot(sc.astype(BF16), vc, preferred_element_type=F32)
            y = y + jnp.dot((qc * q_decay).astype(BF16), st.astype(BF16), preferred_element_type=F32)
            kd_t = (kc * k_decay).T.astype(BF16)
            s_ref[...] = st * chunk_decay + jnp.dot(kd_t, vc, preferred_element_type=F32)
            if direction == 0:
                y_ref[rows, :] = y
            else:
                y_ref[rows, :] += y
            return carry

        lax.fori_loop(0, RET_NCHUNK, step, 0)

    y = y_ref[...]
    mean = jnp.mean(y, axis=-1, keepdims=True)
    yc = y - mean
    var = jnp.mean(yc * yc, axis=-1, keepdims=True)
    yn = yc * lax.rsqrt(var + RET_GN_EPS)
    g = g_ref[...]
    o_ref[...] = (g * jax.nn.sigmoid(g) * yn).astype(o_ref.dtype)


def retention_mixer(p_ret, decay_l):
    nh = RET_HEADS
    blk = lambda off: pl.BlockSpec((TB, HD), lambda b, h, off=off: (b, off + h))
    return pl.pallas_call(
        _ret_kernel,
        grid=(B, nh),
        in_specs=[pl.BlockSpec(memory_space=pltpu.SMEM), blk(0), blk(nh), blk(2 * nh), blk(3 * nh)],
        out_specs=pl.BlockSpec((TB, HD), lambda b, h: (b, h)),
        out_shape=jax.ShapeDtypeStruct((R, W), BF16),
        scratch_shapes=[pltpu.VMEM((TB, HD), F32), pltpu.VMEM((HD, HD), F32)],
        compiler_params=_cparams(("parallel", "parallel")),
        name="retention",
    )(decay_l, p_ret, p_ret, p_ret, p_ret)


SWA_NBLK = SEQ // SWA_BLOCK
SWA_WIN = 3 * SWA_BLOCK


def _rope(x, cos, sin_signed):
    lane = lax.broadcasted_iota(jnp.int32, x.shape, 1)
    partner = jnp.where(lane % 64 < 32, pltpu.roll(x, 96, 1), pltpu.roll(x, 32, 1))
    return x * cos + partner * sin_signed


def _swa_kernel(sink_ref, q_ref, k_ref, v_ref, cos_ref, sin_ref, o_ref, kr_ref, vb_ref):
    kvh = pl.program_id(1)
    scale = HD ** -0.5
    g = SWA_GROUP
    blk = SWA_BLOCK

    vb_ref[...] = v_ref[...].astype(BF16)
    kr_ref[...] = _rope(k_ref[CTX:, :], cos_ref[...], sin_ref[...]).astype(BF16)
    kc = k_ref[:CTX, :].astype(BF16)
    vc = vb_ref[:CTX, :]

    def sink_col(rows_per_head):
        parts = [jnp.full((rows_per_head, 1), sink_ref[kvh * g + gi], F32) for gi in range(g)]
        return jnp.concatenate(parts, axis=0)

    nt = (((1,), (1,)), ((), ()))

    qc = jnp.concatenate([q_ref[:CTX, gi * HD:(gi + 1) * HD] for gi in range(g)], axis=0).astype(BF16)
    s = lax.dot_general(qc, kc, nt, preferred_element_type=F32) * scale
    sk = sink_col(CTX)
    m = jnp.maximum(jnp.max(s, axis=-1, keepdims=True), sk)
    e = jnp.exp(s - m)
    den = jnp.sum(e, axis=-1, keepdims=True) + jnp.exp(sk - m)
    oc = jnp.dot(e.astype(BF16), vc, preferred_element_type=F32) / den
    for gi in range(g):
        o_ref[:CTX, gi * HD:(gi + 1) * HD] = oc[gi * CTX:(gi + 1) * CTX].astype(o_ref.dtype)

    sk_b = sink_col(blk)
    rloc = lax.broadcasted_iota(jnp.int32, (g * blk, SWA_WIN), 0) % blk
    cloc = lax.broadcasted_iota(jnp.int32, (g * blk, SWA_WIN), 1)

    def body(n, carry):
        r0 = pl.multiple_of(n * blk, blk)
        cos = cos_ref[pl.ds(r0, blk), :]
        sin = sin_ref[pl.ds(r0, blk), :]
        qs = [_rope(q_ref[pl.ds(CTX + r0, blk), gi * HD:(gi + 1) * HD], cos, sin) for gi in range(g)]
        qn = jnp.concatenate(qs, axis=0).astype(BF16)
        ws = pl.multiple_of(jnp.clip((n - 1) * blk, 0, SEQ - SWA_WIN), blk)
        kw = kr_ref[pl.ds(ws, SWA_WIN), :]
        vw = vb_ref[pl.ds(CTX + ws, SWA_WIN), :]
        s_win = lax.dot_general(qn, kw, nt, preferred_element_type=F32) * scale
        qpos = n * blk + rloc
        kpos = ws + cloc
        valid = jnp.abs(qpos - kpos) <= SWA_WINDOW
        s_win = jnp.where(valid, s_win, NEG_INF)
        s_ctx = lax.dot_general(qn, kc, nt, preferred_element_type=F32) * scale
        m = jnp.maximum(jnp.maximum(jnp.max(s_win, axis=-1, keepdims=True),
                                    jnp.max(s_ctx, axis=-1, keepdims=True)), sk_b)
        e_win = jnp.exp(s_win - m)
        e_ctx = jnp.exp(s_ctx - m)
        den = (jnp.sum(e_win, axis=-1, keepdims=True) + jnp.sum(e_ctx, axis=-1, keepdims=True)
               + jnp.exp(sk_b - m))
        o = (jnp.dot(e_win.astype(BF16), vw, preferred_element_type=F32)
             + jnp.dot(e_ctx.astype(BF16), vc, preferred_element_type=F32)) / den
        for gi in range(g):
            o_ref[pl.ds(CTX + r0, blk), gi * HD:(gi + 1) * HD] = o[gi * blk:(gi + 1) * blk].astype(o_ref.dtype)
        return carry

    lax.fori_loop(0, SWA_NBLK, body, 0)


def swa_mixer(p_swa, sink_l, cos_full, sin_signed):
    gw = SWA_GROUP * HD
    return pl.pallas_call(
        _swa_kernel,
        grid=(B, SWA_KV_HEADS),
        in_specs=[
            pl.BlockSpec(memory_space=pltpu.SMEM),
            pl.BlockSpec((TB, gw), lambda b, kv: (b, kv)),
            pl.BlockSpec((TB, HD), lambda b, kv: (b, SWA_Q_HEADS + kv)),
            pl.BlockSpec((TB, HD), lambda b, kv: (b, SWA_Q_HEADS + SWA_KV_HEADS + kv)),
            pl.BlockSpec((SEQ, HD), lambda b, kv: (0, 0)),
            pl.BlockSpec((SEQ, HD), lambda b, kv: (0, 0)),
        ],
        out_specs=pl.BlockSpec((TB, gw), lambda b, kv: (b, kv)),
        out_shape=jax.ShapeDtypeStruct((R, W), BF16),
        scratch_shapes=[pltpu.VMEM((SEQ, HD), BF16), pltpu.VMEM((TB, HD), BF16)],
        compiler_params=_cparams(("parallel", "parallel")),
        name="swa",
    )(sink_l, p_swa, p_swa, p_swa, cos_full, sin_signed)


def rope_tables():
    rows = SEQ // GRID_W
    row = jnp.repeat(jnp.arange(rows), GRID_W).astype(F32)
    col = jnp.tile(jnp.arange(GRID_W), rows).astype(F32)
    n_freq = HD // 4
    inv_freq = ROPE_BASE ** (-jnp.arange(n_freq, dtype=F32) / n_freq)
    ang_r = row[:, None] * inv_freq
    ang_c = col[:, None] * inv_freq
    cr, sr, cc, sc = jnp.cos(ang_r), jnp.sin(ang_r), jnp.cos(ang_c), jnp.sin(ang_c)
    cos_full = jnp.concatenate([cr, cr, cc, cc], axis=-1)
    sin_signed = jnp.concatenate([-sr, sr, -sc, sc], axis=-1)
    return cos_full, sin_signed


def _shift_kernel(x_ref, mu_ref, o_ref):
    x = x_ref[...]
    row = lax.broadcasted_iota(jnp.int32, (TB, 1), 0)
    prev = pltpu.roll(x, 1, 0)
    prev = jnp.where(jnp.logical_or(row == 0, row == CTX), 0.0, prev)
    nxt = pltpu.roll(x, TB - 1, 0)
    nxt = jnp.where(jnp.logical_or(row == CTX - 1, row == TB - 1), 0.0, nxt)
    o_ref[...] = x + mu_ref[0:1, :] * (prev - x) + mu_ref[1:2, :] * (nxt - x)


def token_shift(p, mu, tf):
    f = p.shape[1]
    return pl.pallas_call(
        _shift_kernel,
        grid=(p.shape[0] // TB, f // tf),
        in_specs=[pl.BlockSpec((TB, tf), lambda b, j: (b, j)), pl.BlockSpec((2, tf), lambda b, j: (0, j))],
        out_specs=pl.BlockSpec((TB, tf), lambda b, j: (b, j)),
        out_shape=jax.ShapeDtypeStruct(p.shape, F32),
        compiler_params=_cparams(("parallel", "parallel")),
        name="token_shift",
    )(p, mu)


RW_NCHUNK = TB // RW_C
RW_CTX_CHUNKS = CTX // RW_C
RW_GPC = RW_GW // RW_C


def _rw_chunk(direction, s):
    back = jnp.where(s < RW_CTX_CHUNKS, RW_CTX_CHUNKS - 1 - s, RW_NCHUNK - 1 + RW_CTX_CHUNKS - s)
    return jnp.where(direction == 0, s, back)


def _head_ones():
    r = lax.broadcasted_iota(jnp.int32, (RW_GW, RW_GW), 0) // RW_N
    c = lax.broadcasted_iota(jnp.int32, (RW_GW, RW_GW), 1) // RW_N
    return r == c


def _dot16(a, b):
    return jnp.dot(a.astype(BF16), b.astype(BF16), preferred_element_type=F32)


def _dot16_nt(a, b):
    return lax.dot_general(a.astype(BF16), b.astype(BF16), (((1,), (1,)), ((), ())), preferred_element_type=F32)


def _split_bf16(x, terms):
    parts = []
    for _ in range(terms):
        p = x.astype(BF16)
        parts.append(p)
        x = x - p.astype(F32)
    return parts


def _dot_exact_rhs(a16, x, terms):
    return sum(jnp.dot(a16, p, preferred_element_type=F32) for p in _split_bf16(x, terms))


def _dot_exact_lhs(x, b16, terms):
    return sum(jnp.dot(p, b16, preferred_element_type=F32) for p in _split_bf16(x, terms))


def _rwkv_prep_kernel(r_ref, k_ref, v_ref, sm_ref, wup_ref, aup_ref, w0_ref, a0_ref, kk_ref, ka_ref, rk_ref,
                      q_ref, y1_ref, m_ref, z_ref, bonus_ref):
    direction = pl.program_id(0)
    c = RW_C
    same_head = _head_ones()
    mask16 = jnp.where(same_head, 1.0, 0.0).astype(BF16)

    def bd(x):
        return jnp.concatenate([x.astype(BF16)] * RW_GPC, axis=0) * mask16

    def compact(full):
        kept = jnp.where(same_head, full, 0.0)
        return sum(kept[i * c:(i + 1) * c] for i in range(RW_GPC))

    sgn = 1 - 2 * direction
    tt = lax.broadcasted_iota(jnp.int32, (c, c), 0)
    ss = lax.broadcasted_iota(jnp.int32, (c, c), 1)
    tri16 = jnp.where((tt - ss) * sgn >= 0, 1.0, 0.0).astype(BF16)
    t4 = lax.broadcasted_iota(jnp.int32, (c, RW_GW), 0)
    s4 = lax.broadcasted_iota(jnp.int32, (c, RW_GW), 1) % c
    d4 = (t4 - s4) * sgn
    strict = d4 > 0
    incl = d4 >= 0
    eye = jnp.where(d4 == 0, 1.0, 0.0)

    sm = sm_ref[...]
    tanh_wd = jnp.tanh(sm[:, :2 * RW_RANK]).astype(BF16)
    ad = sm[:, 2 * RW_RANK:4 * RW_RANK].astype(BF16)

    groups = range(RW_GROUPS)
    sls = [slice(g * RW_GW, (g + 1) * RW_GW) for g in groups]
    r = [r_ref[:, sl] for sl in sls]
    k = [k_ref[:, sl] for sl in sls]
    v = [v_ref[:, sl] for sl in sls]
    logw = [-jax.nn.sigmoid(w0_ref[0, :, sl] + jnp.dot(tanh_wd, wup_ref[0, :, sl], preferred_element_type=F32))
            * math.exp(-0.5) for sl in sls]
    alpha = [jax.nn.sigmoid(a0_ref[0, :, sl] + jnp.dot(ad, aup_ref[0, :, sl], preferred_element_type=F32))
             for sl in sls]
    kkp = [k[g] * kk_ref[:, sls[g]] for g in groups]
    ssq = [_dot_exact_lhs(kkp[g] * kkp[g], mask16, 2) for g in groups]
    kk = [kkp[g] / jnp.maximum(jnp.sqrt(ssq[g]), 1e-12) for g in groups]
    kdir = [k[g] * (1.0 + (alpha[g] - 1.0) * ka_ref[:, sls[g]]) for g in groups]
    rkr = [_dot_exact_lhs(r[g] * kdir[g] * rk_ref[:, sls[g]], mask16, 2) for g in groups]
    for g in groups:
        bonus_ref[0, :, sls[g]] = rkr[g] * v[g]
    b_vec = [kk[g] * alpha[g] for g in groups]

    cum = [_dot_exact_rhs(tri16, logw[g], 3) for g in groups]
    total = [jnp.sum(logw[g], axis=0, keepdims=True) for g in groups]
    e_out = [jnp.exp(-cum[g]) for g in groups]
    e_left = [jnp.exp(total[g] - cum[g]) for g in groups]
    a_t = [-kk[g] * jnp.exp(cum[g] - logw[g]) for g in groups]
    r_t = [r[g] * jnp.exp(cum[g]) for g in groups]
    lhs = [jnp.concatenate([a_t[g], r_t[g]], axis=0) for g in groups]
    gb = [_dot16_nt(lhs[g], bd(b_vec[g] * e_out[g])) for g in groups]
    gk = [_dot16_nt(lhs[g], bd(kdir[g] * e_out[g])) for g in groups]
    a_ab = [jnp.where(strict, gb[g][:c], 0.0) for g in groups]
    a_ak = [jnp.where(strict, gk[g][:c], 0.0) for g in groups]
    a_rb = [jnp.where(incl, gb[g][c:], 0.0) for g in groups]
    a_rk = [jnp.where(incl, gk[g][c:], 0.0) for g in groups]

    pt = jnp.where(direction == 0, t4, c - 1 - t4)
    ps = jnp.where(direction == 0, s4, c - 1 - s4)
    px = pt ^ ps

    def level_mask(lv):
        return (px >> lv) * 2 + ((pt >> lv) & 1) == 3

    tinv = [eye + jnp.where(level_mask(0), a_ab[g], 0.0) for g in groups]
    for lv in range(1, int(math.log2(c))):
        lm = level_mask(lv)
        cross = [_dot16(jnp.where(lm, a_ab[g], 0.0), bd(tinv[g])) for g in groups]
        tinv = [tinv[g] + _dot16(tinv[g], bd(cross[g])) for g in groups]

    bdv = [bd(v[g]) for g in groups]
    avv = [_dot16(a_ak[g], bdv[g]) for g in groups]
    pw = [_dot16(tinv[g], jnp.concatenate([bd(a_t[g]), bd(avv[g])], axis=1)) for g in groups]
    qy = [_dot16(a_rb[g], jnp.concatenate([bd(pw[g][:, :RW_GW]), bd(pw[g][:, RW_GW:])], axis=1)) for g in groups]
    yv = [_dot16(a_rk[g], bdv[g]) for g in groups]
    left_t = [jnp.concatenate([b_vec[g] * e_left[g], kdir[g] * e_left[g]], axis=0).T for g in groups]
    bot = [jnp.concatenate([jnp.zeros_like(v[g]), v[g]], axis=1) for g in groups]
    mz = [_dot16(left_t[g], jnp.concatenate([pw[g], bot[g]], axis=0)) for g in groups]
    for g in groups:
        sl = sls[g]
        q_ref[0, :, sl] = r_t[g] + qy[g][:, :RW_GW]
        y1_ref[0, :, sl] = qy[g][:, RW_GW:] + yv[g]
        m_ref[0, :, sl] = eye * jnp.exp(total[g]) + compact(mz[g][:, :RW_GW])
        z_ref[0, :, sl] = compact(mz[g][:, RW_GW:])


def rwkv_prep(ps_rkv, ps_small, wup2, aup2, w0, a0, k_k, k_a, r_k):
    rows = ps_rkv.shape[0]
    feat = lambda off: pl.BlockSpec((RW_C, W), lambda d, i, off=off: (i, off))
    vec = pl.BlockSpec((1, W), lambda d, i: (0, 0))
    dvec = pl.BlockSpec((1, 1, W), lambda d, i: (d, 0, 0))
    lora = pl.BlockSpec((1, 2 * RW_RANK, W), lambda d, i: (d, 0, 0))
    out = pl.BlockSpec((1, RW_C, W), lambda d, i: (d, i, 0))
    shape = jax.ShapeDtypeStruct((2, rows, W), F32)
    return pl.pallas_call(
        _rwkv_prep_kernel,
        grid=(2, rows // RW_C),
        in_specs=[feat(0), feat(1), feat(2),
                  pl.BlockSpec((RW_C, RW_SMALL), lambda d, i: (i, 0)),
                  lora, lora, dvec, dvec, vec, vec, vec],
        out_specs=[out] * 5,
        out_shape=[shape] * 5,
        compiler_params=_cparams(("parallel", "parallel")),
        name="rwkv_prep",
    )(ps_rkv, ps_rkv, ps_rkv, ps_small, wup2, aup2, w0, a0, k_k, k_a, r_k)


def _rwkv_recur_kernel(q_ref, y1_ref, m_ref, z_ref, y_ref, st_ref):
    c = RW_C

    @pl.when(pl.program_id(2) == 0)
    def _():
        st_ref[...] = jnp.zeros_like(st_ref)

    same_head = _head_ones()

    def bd(x):
        return jnp.where(same_head, jnp.concatenate([x] * RW_GPC, axis=0), 0.0)

    for g in range(RW_GROUPS):
        sl = slice(g * RW_GW, (g + 1) * RW_GW)
        lhs = jnp.concatenate([q_ref[0, :, sl], bd(m_ref[0, :, sl])], axis=0)
        res = _dot16(lhs, st_ref[g])
        y_ref[0, :, sl] = res[:c] + y1_ref[0, :, sl]
        st_ref[g] = res[c:] + bd(z_ref[0, :, sl])


def rwkv_recur(q2, y12, m2, z2):
    rows = q2.shape[1]
    blk = pl.BlockSpec((1, RW_C, W), lambda b, d, s: (d, b * RW_NCHUNK + _rw_chunk(d, s), 0))
    return pl.pallas_call(
        _rwkv_recur_kernel,
        grid=(rows // TB, 2, RW_NCHUNK),
        in_specs=[blk] * 4,
        out_specs=blk,
        out_shape=jax.ShapeDtypeStruct((2, rows, W), F32),
        scratch_shapes=[pltpu.VMEM((RW_GROUPS, RW_GW, RW_GW), F32)],
        compiler_params=_cparams(("parallel", "parallel", "arbitrary")),
        name="rwkv_recur",
    )(q2, y12, m2, z2)


RWO_TM = 256


def _rwkv_out_kernel(y_ref, bonus_ref, sm_ref, gup_ref, lng_ref, o_ref):
    mask16 = jnp.where(_head_ones(), 1.0, 0.0).astype(BF16)
    y = y_ref[0] + y_ref[1]
    mean = _dot_exact_lhs(y, mask16, 3) * (1.0 / RW_N)
    yc = y - mean
    var = _dot_exact_lhs(yc * yc, mask16, 3) * (1.0 / RW_N)
    yn = yc * lax.rsqrt(var + RW_GN_EPS) * lng_ref[...]
    gate = jnp.dot(jax.nn.sigmoid(sm_ref[...]).astype(BF16), gup_ref[...], preferred_element_type=F32)
    o_ref[...] = ((yn + bonus_ref[0] + bonus_ref[1]) * gate).astype(o_ref.dtype)


def rwkv_output(y2, bonus2, ps_small, g_up, ln_g):
    rows = y2.shape[1]
    pair = pl.BlockSpec((2, RWO_TM, RW_GW), lambda i, g: (0, i, g))
    return pl.pallas_call(
        _rwkv_out_kernel,
        grid=(rows // RWO_TM, RW_GROUPS),
        in_specs=[pair, pair,
                  pl.BlockSpec((RWO_TM, RW_G_RANK), lambda i, g: (i, 4 * RW_RANK // RW_G_RANK)),
                  pl.BlockSpec((RW_G_RANK, RW_GW), lambda i, g: (0, g)),
                  pl.BlockSpec((1, RW_GW), lambda i, g: (0, g))],
        out_specs=pl.BlockSpec((RWO_TM, RW_GW), lambda i, g: (i, g)),
        out_shape=jax.ShapeDtypeStruct((rows, W), BF16),
        compiler_params=_cparams(("parallel", "parallel")),
        name="rwkv_output",
    )(y2, bonus2, ps_small, g_up, ln_g)


MG_TM = 768
MG_TN = 512


def _merge_kernel(y0_ref, y1_ref, y2_ref, wb_ref, g0_ref, g1_ref, g2_ref, o_ref):
    acc = None
    for y_ref, g_ref, n in ((y0_ref, g0_ref, 0), (y1_ref, g1_ref, 1), (y2_ref, g2_ref, 2)):
        proj = jnp.dot(y_ref[...], wb_ref[n], preferred_element_type=F32)
        term = jax.nn.sigmoid(g_ref[...]) * proj
        acc = term if acc is None else acc + term
    o_ref[...] = acc.astype(o_ref.dtype)


def merge_branches(y_ret, y_swa, y_rwkv, p_gate, w_branch):
    nj = D // MG_TN
    ysp = pl.BlockSpec((MG_TM, W), lambda i, j: (i, 0))
    gsp = lambda n: pl.BlockSpec((MG_TM, MG_TN), lambda i, j, n=n: (i, n * nj + j))
    return pl.pallas_call(
        _merge_kernel,
        grid=(R // MG_TM, nj),
        in_specs=[ysp, ysp, ysp, pl.BlockSpec((3, W, MG_TN), lambda i, j: (0, 0, j)), gsp(0), gsp(1), gsp(2)],
        out_specs=pl.BlockSpec((MG_TM, MG_TN), lambda i, j: (i, j)),
        out_shape=jax.ShapeDtypeStruct((R, D), BF16),
        compiler_params=_cparams(("parallel", "parallel")),
        name="merge",
    )(y_ret, y_swa, y_rwkv, w_branch, p_gate, p_gate, p_gate)


def kernel(x, c, ctx, c_ctx, norm1_g, norm2_g, w_mod, b_mod, w_in, ret_decay, swa_sink, rwkv_mu, rwkv_w0,
           rwkv_w_up, rwkv_a0, rwkv_a_up, rwkv_g_up, rwkv_k_k, rwkv_k_a, rwkv_r_k, rwkv_ln_g, w_branch,
           w_out, w_ff1, w_ff2, final_g):
    h = jnp.concatenate([ctx, x], axis=1).reshape(R, D)
    cond8 = jnp.concatenate([c, c_ctx[None, :], jnp.zeros((8 - B - 1, D), F32)], axis=0)
    mods = mod_vectors(cond8, w_mod, b_mod)
    mods = mods.reshape(L, 8, 6, D).transpose(0, 2, 1, 3)
    cos_full, sin_signed = rope_tables()

    o_swa = RET_IN
    o_rw = RET_IN + SWA_IN
    o_small = o_rw + 3 * W
    o_gate = o_rw + RWKV_IN
    zeros_lora = jnp.zeros((L, RW_RANK, W), BF16)

    for l in range(L):
        m_l = mods[l]
        w_in_l = w_in[l].astype(BF16)
        u = norm_modulate(h, norm1_g[l], m_l, 0, 1)
        p_ret = matmul(u, w_in_l[:, :o_swa], tn=1024, name="in_ret")
        p_swa = matmul(u, w_in_l[:, o_swa:o_rw], tn=768, name="in_swa")
        p_rkv = matmul(u, w_in_l[:, o_rw:o_small], tn=1024, name="in_rkv")
        p_small = matmul(u, w_in_l[:, o_small:o_gate], tn=RW_SMALL, name="in_small")
        p_gate = matmul(u, w_in_l[:, o_gate:], tn=1024, name="in_gate")

        y_ret = retention_mixer(p_ret, ret_decay[l])
        y_swa = swa_mixer(p_swa, swa_sink[l], cos_full, sin_signed)

        mu = rwkv_mu[l]
        ps_rkv = token_shift(p_rkv, mu[:, :3 * W], 512)
        ps_small = token_shift(p_small, mu[:, 3 * W:], RW_SMALL)
        wup = rwkv_w_up[l].astype(BF16)
        aup = rwkv_a_up[l].astype(BF16)
        z = zeros_lora[l]
        wup2 = jnp.stack([jnp.concatenate([wup[0], z], axis=0), jnp.concatenate([z, wup[1]], axis=0)])
        aup2 = jnp.stack([jnp.concatenate([aup[0], z], axis=0), jnp.concatenate([z, aup[1]], axis=0)])
        q2, y12, m2, z2, bonus2 = rwkv_prep(
            ps_rkv, ps_small, wup2, aup2, rwkv_w0[l].reshape(2, 1, W), rwkv_a0[l].reshape(2, 1, W),
            rwkv_k_k[l].reshape(1, W), rwkv_k_a[l].reshape(1, W), rwkv_r_k[l].reshape(1, W))
        y2 = rwkv_recur(q2, y12, m2, z2)
        y_rwkv = rwkv_output(y2, bonus2, ps_small, rwkv_g_up[l].astype(BF16), rwkv_ln_g[l].reshape(1, W))

        merged = merge_branches(y_ret, y_swa, y_rwkv, p_gate, w_branch[l].astype(BF16))
        h = matmul(merged, w_out[l].astype(BF16), tn=1024, epilogue="residual", h=h, mods_l=m_l,
                   which_gate=2, name="out_proj")
        u2 = norm_modulate(h, norm2_g[l], m_l, 3, 4)
        f = matmul(u2, w_ff1[l].astype(BF16), tn=1024, out_dtype=BF16, epilogue="relu2", name="ff1")
        h = matmul(f, w_ff2[l].astype(BF16), tn=1024, tk=2048, epilogue="residual", h=h, mods_l=m_l,
                   which_gate=5, name="ff2")

    return final_norm(h, final_g).reshape(B, SEQ, D)
```

```python
import functools
import math

import jax
import jax.numpy as jnp
from jax import lax
from jax.experimental import pallas as pl
from jax.experimental.pallas import tpu as pltpu

F32 = jnp.float32
BF16 = jnp.bfloat16

D = 2048
B = 4
SEQ = 2048
CTX = 256
TB = CTX + SEQ
R = B * TB
L = 4
GRID_W = 64
EPS = 1e-6
ROPE_BASE = 10000.0
NEG_INF = -1e30
W = D // 2
HD = 128
RET_HEADS = W // HD
RET_CHUNK = 128
RET_GN_EPS = 1e-5
SWA_Q_HEADS = W // HD
SWA_KV_HEADS = SWA_Q_HEADS // 4
SWA_GROUP = SWA_Q_HEADS // SWA_KV_HEADS
SWA_WINDOW = 128
SWA_BLOCK = 128
RW_N = 64
RW_HEADS = W // RW_N
RW_RANK = 64
RW_G_RANK = 128
RW_GN_EPS = 64e-5
RW_C = 64
RW_GW = 256
RW_GROUPS = W // RW_GW
RW_SMALL = 4 * RW_RANK + RW_G_RANK
D_FF = 4 * D
RET_IN = 4 * W
SWA_IN = (SWA_Q_HEADS + 2 * SWA_KV_HEADS) * HD
RWKV_IN = 3 * W + RW_SMALL
GATE_IN = 3 * D
N_IN = RET_IN + SWA_IN + RWKV_IN + GATE_IN

VMEM_LIMIT = 56 * 1024 * 1024


def _cparams(sem):
    return pltpu.CompilerParams(dimension_semantics=sem, vmem_limit_bytes=VMEM_LIMIT)


def _mod_kernel(x_ref, w_ref, b_ref, o_ref):
    x = x_ref[...]
    s = x * jax.nn.sigmoid(x)
    acc = jnp.dot(s.astype(BF16), w_ref[0].astype(BF16), preferred_element_type=F32)
    o_ref[0] = acc + b_ref[0]


def mod_vectors(cond8, w_mod, b_mod):
    tn = 1024
    n = w_mod.shape[-1]
    return pl.pallas_call(
        _mod_kernel,
        grid=(L, n // tn),
        in_specs=[
            pl.BlockSpec((8, D), lambda l, j: (0, 0)),
            pl.BlockSpec((1, D, tn), lambda l, j: (l, 0, j)),
            pl.BlockSpec((1, 1, tn), lambda l, j: (l, 0, j)),
        ],
        out_specs=pl.BlockSpec((1, 8, tn), lambda l, j: (l, 0, j)),
        out_shape=jax.ShapeDtypeStruct((L, 8, n), F32),
        compiler_params=_cparams(("parallel", "parallel")),
        name="mod_vectors",
    )(cond8, w_mod, b_mod.reshape(L, 1, n))


NORM_TM = 256
TILES_PER_BATCH = TB // NORM_TM


def _mod_row(i):
    return jnp.where(i % TILES_PER_BATCH == 0, B, i // TILES_PER_BATCH)


def _norm_mod_kernel(x_ref, g_ref, sh_ref, sc_ref, o_ref):
    row = _mod_row(pl.program_id(0))
    x = x_ref[...]
    ms = jnp.mean(x * x, axis=-1, keepdims=True)
    y = x * lax.rsqrt(ms + EPS) * g_ref[...]
    shift = sh_ref[0, pl.ds(row, 1), :]
    scale = sc_ref[0, pl.ds(row, 1), :]
    o_ref[...] = (y * (1.0 + scale) + shift).astype(o_ref.dtype)


def norm_modulate(h, g, mods_l, which_shift, which_scale):
    return pl.pallas_call(
        _norm_mod_kernel,
        grid=(R // NORM_TM,),
        in_specs=[
            pl.BlockSpec((NORM_TM, D), lambda i: (i, 0)),
            pl.BlockSpec((1, D), lambda i: (0, 0)),
            pl.BlockSpec((1, 8, D), lambda i: (which_shift, 0, 0)),
            pl.BlockSpec((1, 8, D), lambda i: (which_scale, 0, 0)),
        ],
        out_specs=pl.BlockSpec((NORM_TM, D), lambda i: (i, 0)),
        out_shape=jax.ShapeDtypeStruct((R, D), BF16),
        compiler_params=_cparams(("parallel",)),
        name="norm_modulate",
    )(h, g.reshape(1, D), mods_l, mods_l)


def _final_norm_kernel(x_ref, g_ref, o_ref):
    x = x_ref[...]
    ms = jnp.mean(x * x, axis=-1, keepdims=True)
    o_ref[...] = x * lax.rsqrt(ms + EPS) * g_ref[...]


def final_norm(h, g):
    per = SEQ // NORM_TM
    return pl.pallas_call(
        _final_norm_kernel,
        grid=(B, per),
        in_specs=[
            pl.BlockSpec((NORM_TM, D), lambda b, j: (b * TILES_PER_BATCH + CTX // NORM_TM + j, 0)),
            pl.BlockSpec((1, D), lambda b, j: (0, 0)),
        ],
        out_specs=pl.BlockSpec((NORM_TM, D), lambda b, j: (b * per + j, 0)),
        out_shape=jax.ShapeDtypeStruct((B * SEQ, D), F32),
        compiler_params=_cparams(("parallel", "parallel")),
        name="final_norm",
    )(h, g.reshape(1, D))


MM_TM = 1152


def _row_is_ctx(i, tm):
    rows = lax.broadcasted_iota(jnp.int32, (tm, 1), 0)
    return jnp.logical_and(i % (TB // tm) == 0, rows < CTX)


def _mm_kernel(*refs, nk, epilogue, tm):
    if epilogue == "residual":
        x_ref, w_ref, h_ref, gate_ref, o_ref = refs[:5]
        rest = refs[5:]
    else:
        x_ref, w_ref, o_ref = refs[:3]
        rest = refs[3:]
    k = pl.program_id(2)
    part = jnp.dot(x_ref[...], w_ref[...], preferred_element_type=F32)

    def finish(acc):
        if epilogue == "relu2":
            a = jnp.maximum(acc, 0.0)
            o_ref[...] = (a * a).astype(o_ref.dtype)
        elif epilogue == "residual":
            i = pl.program_id(0)
            bidx = i // (TB // tm)
            g_b = gate_ref[0, pl.ds(bidx, 1), :]
            g_c = gate_ref[0, pl.ds(B, 1), :]
            gate = jnp.where(_row_is_ctx(i, tm), g_c, g_b)
            o_ref[...] = h_ref[...] + gate * acc
        else:
            o_ref[...] = acc.astype(o_ref.dtype)

    if nk == 1:
        finish(part)
    else:
        acc_ref = rest[0]

        @pl.when(k == 0)
        def _():
            acc_ref[...] = part

        @pl.when(jnp.logical_and(k > 0, k < nk - 1))
        def _():
            acc_ref[...] += part

        @pl.when(k == nk - 1)
        def _():
            finish(acc_ref[...] + part)


def matmul(x, w, *, tn, tk=None, out_dtype=F32, epilogue="plain", h=None, mods_l=None, which_gate=None,
           name="matmul"):
    m, kdim = x.shape
    n = w.shape[1]
    tm = MM_TM
    tk = kdim if tk is None else tk
    nk = kdim // tk
    in_specs = [
        pl.BlockSpec((tm, tk), lambda i, j, k: (i, k)),
        pl.BlockSpec((tk, tn), lambda i, j, k: (k, j)),
    ]
    args = [x, w]
    aliases = {}
    if epilogue == "residual":
        in_specs += [
            pl.BlockSpec((tm, tn), lambda i, j, k: (i, j)),
            pl.BlockSpec((1, 8, tn), lambda i, j, k: (which_gate, 0, j)),
        ]
        args += [h, mods_l]
        aliases = {2: 0}
    scratch = [pltpu.VMEM((tm, tn), F32)] if nk > 1 else []
    return pl.pallas_call(
        functools.partial(_mm_kernel, nk=nk, epilogue=epilogue, tm=tm),
        grid=(m // tm, n // tn, nk),
        in_specs=in_specs,
        out_specs=pl.BlockSpec((tm, tn), lambda i, j, k: (i, j)),
        out_shape=jax.ShapeDtypeStruct((m, n), out_dtype),
        scratch_shapes=scratch,
        input_output_aliases=aliases,
        compiler_params=_cparams(("parallel", "parallel", "arbitrary")),
        name=name,
    )(*args)


RET_NCHUNK = TB // RET_CHUNK
RET_CTX_CHUNKS = CTX // RET_CHUNK


def _log_sigmoid(x):
    return jnp.minimum(x, 0.0) - jnp.log(1.0 + jnp.exp(-jnp.abs(x)))


RET_BATCH = 6


def _ret_kernel(dec_ref, q_ref, k_ref, v_ref, g_ref, o_ref, y_ref):
    hh = pl.program_id(1)
    c = RET_CHUNK
    ri = lax.broadcasted_iota(jnp.int32, (c, c), 0).astype(F32)
    ci = lax.broadcasted_iota(jnp.int32, (c, c), 1).astype(F32)
    kscale = HD ** -0.5
    nt = (((1,), (1,)), ((), ()))
    for direction in range(2):
        lg = _log_sigmoid(jnp.full((c, c), dec_ref[direction, hh], F32))
        diff = (ri - ci) if direction == 0 else (ci - ri)
        intra = jnp.where(diff >= 0, jnp.exp(lg * jnp.maximum(diff, 0.0)), 0.0)
        pos = ri if direction == 0 else (c - 1.0) - ri
        q_decay = jnp.exp(lg * (pos + 1.0))
        k_decay = jnp.exp(lg * ((c - 1.0) - pos))
        chunk_decay = jnp.exp(lg * float(c))
        if direction == 0:
            order = list(range(RET_NCHUNK))
        else:
            order = list(range(RET_CTX_CHUNKS - 1, -1, -1)) + list(range(RET_NCHUNK - 1, RET_CTX_CHUNKS - 1, -1))
        state = jnp.zeros((c, c), F32)
        for b0 in range(0, RET_NCHUNK, RET_BATCH):
            rows = [slice(ch * c, (ch + 1) * c) for ch in order[b0:b0 + RET_BATCH]]
            n = range(len(rows))
            q = [q_ref[r, :] for r in rows]
            k = [k_ref[r, :] * kscale for r in rows]
            v = [v_ref[r, :].astype(BF16) for r in rows]
            sc = [lax.dot_general(q[i].astype(BF16), k[i].astype(BF16), nt, preferred_element_type=F32) * intra
                  for i in n]
            y_in = [jnp.dot(sc[i].astype(BF16), v[i], preferred_element_type=F32) for i in n]
            kv = [jnp.dot((k[i] * k_decay).T.astype(BF16), v[i], preferred_element_type=F32) for i in n]
            states = []
            for i in n:
                states.append(state)
                state = state * chunk_decay + kv[i]
            y_x = [jnp.dot((q[i] * q_decay).astype(BF16), states[i].astype(BF16), preferred_element_type=F32)
                   for i in n]
            for i in n:
                y = y_in[i] + y_x[i]
                if direction == 0:
                    y_ref[rows[i], :] = y
                else:
                    y = y + y_ref[rows[i], :]
                    mean = jnp.mean(y, axis=-1, keepdims=True)
                    yc = y - mean
                    var = jnp.mean(yc * yc, axis=-1, keepdims=True)
                    g = g_ref[rows[i], :]
                    o_ref[rows[i], :] = (g * jax.nn.sigmoid(g) * (yc * lax.rsqrt(var + RET_GN_EPS))).astype(o_ref.dtype)


def retention_mixer(p_ret, decay_l):
    nh = RET_HEADS
    blk = lambda off: pl.BlockSpec((TB, HD), lambda b, h, off=off: (b, off + h))
    rows = p_ret.shape[0]
    return pl.pallas_call(
        _ret_kernel,
        grid=(rows // TB, nh),
        in_specs=[pl.BlockSpec(memory_space=pltpu.SMEM), blk(0), blk(nh), blk(2 * nh), blk(3 * nh)],
        out_specs=pl.BlockSpec((TB, HD), lambda b, h: (b, h)),
        out_shape=jax.ShapeDtypeStruct((rows, W), BF16),
        scratch_shapes=[pltpu.VMEM((TB, HD), F32)],
        compiler_params=_cparams(("parallel", "parallel")),
        name="retention",
    )(decay_l, p_ret, p_ret, p_ret, p_ret)


SWA_NBLK = SEQ // SWA_BLOCK
SWA_WIN = 3 * SWA_BLOCK


def _rope(x, cos, sin_signed):
    lane = lax.broadcasted_iota(jnp.int32, x.shape, 1)
    partner = jnp.where(lane % 64 < 32, pltpu.roll(x, 96, 1), pltpu.roll(x, 32, 1))
    return x * cos + partner * sin_signed


def _swa_kernel(sink_ref, q_ref, k_ref, v_ref, cos_ref, sin_ref, o_ref, kt_ref, vb_ref):
    kvh = pl.program_id(1)
    scale = HD ** -0.5
    g = SWA_GROUP
    blk = SWA_BLOCK

    vb_ref[...] = v_ref[...].astype(BF16)
    for j in range(SWA_NBLK):
        rows = slice(j * blk, (j + 1) * blk)
        k_rot = _rope(k_ref[CTX + j * blk:CTX + (j + 1) * blk, :], cos_ref[rows, :], sin_ref[rows, :])
        kt_ref[j] = k_rot.T.astype(BF16)
    kc_t = jnp.concatenate([k_ref[j * blk:(j + 1) * blk, :].T for j in range(CTX // blk)], axis=1).astype(BF16)
    vc = vb_ref[:CTX, :]

    def sink_col(rows_per_head):
        parts = [jnp.full((rows_per_head, 1), sink_ref[kvh * g + gi], F32) for gi in range(g)]
        return jnp.concatenate(parts, axis=0)

    def lane_tiles(x):
        return [x[:, j * HD:(j + 1) * HD] for j in range(x.shape[1] // HD)]

    def row_max(*xs):
        tiles = [t for x in xs for t in lane_tiles(x)]
        return jnp.max(functools.reduce(jnp.maximum, tiles), axis=-1, keepdims=True)

    def row_sum(*xs):
        tiles = [t for x in xs for t in lane_tiles(x)]
        return jnp.sum(functools.reduce(jnp.add, tiles), axis=-1, keepdims=True)

    qc = jnp.concatenate([q_ref[:CTX, gi * HD:(gi + 1) * HD] for gi in range(g)], axis=0).astype(BF16)
    s = jnp.dot(qc, kc_t, preferred_element_type=F32) * scale
    sk = sink_col(CTX)
    m = jnp.maximum(row_max(s), sk)
    e = jnp.exp(s - m)
    den = row_sum(e) + jnp.exp(sk - m)
    oc = jnp.dot(e.astype(BF16), vc, preferred_element_type=F32) / den
    for gi in range(g):
        o_ref[:CTX, gi * HD:(gi + 1) * HD] = oc[gi * CTX:(gi + 1) * CTX].astype(o_ref.dtype)

    sk_b = sink_col(blk)
    delta = (lax.broadcasted_iota(jnp.int32, (g * blk, SWA_WIN), 1)
             - lax.broadcasted_iota(jnp.int32, (g * blk, SWA_WIN), 0) % blk)

    def body(n, carry):
        r0 = pl.multiple_of(n * blk, blk)
        cos = cos_ref[pl.ds(r0, blk), :]
        sin = sin_ref[pl.ds(r0, blk), :]
        qs = [_rope(q_ref[pl.ds(CTX + r0, blk), gi * HD:(gi + 1) * HD], cos, sin) for gi in range(g)]
        qn = jnp.concatenate(qs, axis=0).astype(BF16)
        j0 = jnp.clip(n - 1, 0, SWA_NBLK - SWA_WIN // blk)
        ws = pl.multiple_of(j0 * blk, blk)
        kw_t = jnp.concatenate([kt_ref[j0 + i] for i in range(SWA_WIN // blk)], axis=1)
        vw = vb_ref[pl.ds(CTX + ws, SWA_WIN), :]
        s_win = jnp.dot(qn, kw_t, preferred_element_type=F32) * scale
        valid = jnp.abs(delta + (ws - r0)) <= SWA_WINDOW
        s_win = jnp.where(valid, s_win, NEG_INF)
        s_ctx = jnp.dot(qn, kc_t, preferred_element_type=F32) * scale
        m = jnp.maximum(row_max(s_win, s_ctx), sk_b)
        e_win = jnp.exp(s_win - m)
        e_ctx = jnp.exp(s_ctx - m)
        den = row_sum(e_win, e_ctx) + jnp.exp(sk_b - m)
        o = (jnp.dot(e_win.astype(BF16), vw, preferred_element_type=F32)
             + jnp.dot(e_ctx.astype(BF16), vc, preferred_element_type=F32)) / den
        for gi in range(g):
            o_ref[pl.ds(CTX + r0, blk), gi * HD:(gi + 1) * HD] = o[gi * blk:(gi + 1) * blk].astype(o_ref.dtype)
        return carry

    lax.fori_loop(0, SWA_NBLK, body, 0)


def swa_mixer(p_swa, sink_l, cos_full, sin_signed):
    gw = SWA_GROUP * HD
    rows = p_swa.shape[0]
    return pl.pallas_call(
        _swa_kernel,
        grid=(rows // TB, SWA_KV_HEADS),
        in_specs=[
            pl.BlockSpec(memory_space=pltpu.SMEM),
            pl.BlockSpec((TB, gw), lambda b, kv: (b, kv)),
            pl.BlockSpec((TB, HD), lambda b, kv: (b, SWA_Q_HEADS + kv)),
            pl.BlockSpec((TB, HD), lambda b, kv: (b, SWA_Q_HEADS + SWA_KV_HEADS + kv)),
            pl.BlockSpec((SEQ, HD), lambda b, kv: (0, 0)),
            pl.BlockSpec((SEQ, HD), lambda b, kv: (0, 0)),
        ],
        out_specs=pl.BlockSpec((TB, gw), lambda b, kv: (b, kv)),
        out_shape=jax.ShapeDtypeStruct((rows, W), BF16),
        scratch_shapes=[pltpu.VMEM((SWA_NBLK, HD, SWA_BLOCK), BF16), pltpu.VMEM((TB, HD), BF16)],
        compiler_params=_cparams(("parallel", "parallel")),
        name="swa",
    )(sink_l, p_swa, p_swa, p_swa, cos_full, sin_signed)


def rope_tables():
    rows = SEQ // GRID_W
    row = jnp.repeat(jnp.arange(rows), GRID_W).astype(F32)
    col = jnp.tile(jnp.arange(GRID_W), rows).astype(F32)
    n_freq = HD // 4
    inv_freq = ROPE_BASE ** (-jnp.arange(n_freq, dtype=F32) / n_freq)
    ang_r = row[:, None] * inv_freq
    ang_c = col[:, None] * inv_freq
    cr, sr, cc, sc = jnp.cos(ang_r), jnp.sin(ang_r), jnp.cos(ang_c), jnp.sin(ang_c)
    cos_full = jnp.concatenate([cr, cr, cc, cc], axis=-1)
    sin_signed = jnp.concatenate([-sr, sr, -sc, sc], axis=-1)
    return cos_full, sin_signed


def _shift_kernel(x_ref, mu_ref, o_ref):
    x = x_ref[...]
    row = lax.broadcasted_iota(jnp.int32, (TB, 1), 0)
    prev = pltpu.roll(x, 1, 0)
    prev = jnp.where(jnp.logical_or(row == 0, row == CTX), 0.0, prev)
    nxt = pltpu.roll(x, TB - 1, 0)
    nxt = jnp.where(jnp.logical_or(row == CTX - 1, row == TB - 1), 0.0, nxt)
    o_ref[...] = x + mu_ref[0:1, :] * (prev - x) + mu_ref[1:2, :] * (nxt - x)


def token_shift(p, mu, tf):
    f = p.shape[1]
    return pl.pallas_call(
        _shift_kernel,
        grid=(p.shape[0] // TB, f // tf),
        in_specs=[pl.BlockSpec((TB, tf), lambda b, j: (b, j)), pl.BlockSpec((2, tf), lambda b, j: (0, j))],
        out_specs=pl.BlockSpec((TB, tf), lambda b, j: (b, j)),
        out_shape=jax.ShapeDtypeStruct(p.shape, F32),
        compiler_params=_cparams(("parallel", "parallel")),
        name="token_shift",
    )(p, mu)


RW_NCHUNK = TB // RW_C
RW_CTX_CHUNKS = CTX // RW_C
RW_GPC = RW_GW // RW_C


def _rw_chunk(direction, s):
    back = jnp.where(s < RW_CTX_CHUNKS, RW_CTX_CHUNKS - 1 - s, RW_NCHUNK - 1 + RW_CTX_CHUNKS - s)
    return jnp.where(direction == 0, s, back)


def _head_ones():
    r = lax.broadcasted_iota(jnp.int32, (RW_GW, RW_GW), 0) // RW_N
    c = lax.broadcasted_iota(jnp.int32, (RW_GW, RW_GW), 1) // RW_N
    return r == c


def _dot16(a, b):
    return jnp.dot(a.astype(BF16), b.astype(BF16), preferred_element_type=F32)


def _dot16_nt(a, b):
    return lax.dot_general(a.astype(BF16), b.astype(BF16), (((1,), (1,)), ((), ())), preferred_element_type=F32)


def _split_bf16(x, terms):
    parts = []
    for _ in range(terms):
        p = x.astype(BF16)
        parts.append(p)
        x = x - p.astype(F32)
    return parts


def _dot_exact_lhs(x, b16, terms):
    return jnp.dot(jnp.concatenate(_split_bf16(x, terms), axis=1), jnp.concatenate([b16] * terms, axis=0),
                   preferred_element_type=F32)


def _rwkv_prep_kernel(r_ref, k_ref, v_ref, sm_ref, wup_ref, aup_ref, w0_ref, a0_ref, kk_ref, ka_ref, rk_ref,
                      q_ref, y1_ref, m_ref, z_ref, bonus_ref):
    direction = pl.program_id(0)
    c = RW_C
    same_head = _head_ones()
    mask16 = jnp.where(same_head, 1.0, 0.0).astype(BF16)

    def bd(x):
        return jnp.concatenate([x.astype(BF16)] * RW_GPC, axis=0) * mask16

    sgn = 1 - 2 * direction
    cum_terms = 3
    tt = lax.broadcasted_iota(jnp.int32, (c, cum_terms * c), 0)
    ss = lax.broadcasted_iota(jnp.int32, (c, cum_terms * c), 1) % c
    tri16 = jnp.where((tt - ss) * sgn >= 0, 1.0, 0.0).astype(BF16)
    t4 = lax.broadcasted_iota(jnp.int32, (c, RW_GW), 0)
    s4 = lax.broadcasted_iota(jnp.int32, (c, RW_GW), 1) % c
    d4 = (t4 - s4) * sgn
    strict = d4 > 0
    incl = d4 >= 0
    eye = jnp.where(d4 == 0, 1.0, 0.0)

    sm = sm_ref[...]
    tanh_wd = jnp.tanh(sm[:, :2 * RW_RANK]).astype(BF16)
    ad = sm[:, 2 * RW_RANK:4 * RW_RANK].astype(BF16)

    groups = range(RW_GROUPS)
    sls = [slice(g * RW_GW, (g + 1) * RW_GW) for g in groups]
    r = [r_ref[:, sl] for sl in sls]
    k = [k_ref[:, sl] for sl in sls]
    v = [v_ref[:, sl] for sl in sls]
    logw = [-jax.nn.sigmoid(w0_ref[0, :, sl] + jnp.dot(tanh_wd, wup_ref[0, :, sl], preferred_element_type=F32))
            * math.exp(-0.5) for sl in sls]
    alpha = [jax.nn.sigmoid(a0_ref[0, :, sl] + jnp.dot(ad, aup_ref[0, :, sl], preferred_element_type=F32))
             for sl in sls]
    kkp = [k[g] * kk_ref[:, sls[g]] for g in groups]
    ssq = [_dot_exact_lhs(kkp[g] * kkp[g], mask16, 2) for g in groups]
    kk = [kkp[g] / jnp.maximum(jnp.sqrt(ssq[g]), 1e-12) for g in groups]
    kdir = [k[g] * (1.0 + (alpha[g] - 1.0) * ka_ref[:, sls[g]]) for g in groups]
    rkr = [_dot_exact_lhs(r[g] * kdir[g] * rk_ref[:, sls[g]], mask16, 2) for g in groups]
    for g in groups:
        bonus_ref[0, :, sls[g]] = rkr[g] * v[g]
    b_vec = [kk[g] * alpha[g] for g in groups]

    cum = [jnp.dot(tri16, jnp.concatenate(_split_bf16(logw[g], cum_terms), axis=0), preferred_element_type=F32)
           for g in groups]
    total = [jnp.sum(logw[g], axis=0, keepdims=True) for g in groups]
    e_out = [jnp.exp(-cum[g]) for g in groups]
    e_left = [jnp.exp(total[g] - cum[g]) for g in groups]
    a_t = [-kk[g] * jnp.exp(cum[g] - logw[g]) for g in groups]
    r_t = [r[g] * jnp.exp(cum[g]) for g in groups]
    lhs = [jnp.concatenate([a_t[g], r_t[g]], axis=0) for g in groups]
    gb = [_dot16_nt(lhs[g], bd(b_vec[g] * e_out[g])) for g in groups]
    gk = [_dot16_nt(lhs[g], bd(kdir[g] * e_out[g])) for g in groups]
    a_ab = [jnp.where(strict, gb[g][:c], 0.0) for g in groups]
    a_ak = [jnp.where(strict, gk[g][:c], 0.0) for g in groups]
    a_rb = [jnp.where(incl, gb[g][c:], 0.0) for g in groups]
    a_rk = [jnp.where(incl, gk[g][c:], 0.0) for g in groups]

    pt = jnp.where(direction == 0, t4, c - 1 - t4)
    ps = jnp.where(direction == 0, s4, c - 1 - s4)
    px = pt ^ ps

    def level_mask(lv):
        return (px >> lv) * 2 + ((pt >> lv) & 1) == 3

    tinv = [eye + jnp.where(level_mask(0), a_ab[g], 0.0) for g in groups]
    for lv in range(1, int(math.log2(c))):
        lm = level_mask(lv)
        cross = [_dot16(jnp.where(lm, a_ab[g], 0.0), bd(tinv[g])) for g in groups]
        tinv = [tinv[g] + _dot16(tinv[g], bd(cross[g])) for g in groups]

    bdv = [bd(v[g]) for g in groups]
    avv = [_dot16(a_ak[g], bdv[g]) for g in groups]
    pw = [_dot16(tinv[g], jnp.concatenate([bd(a_t[g]), bd(avv[g])], axis=1)) for g in groups]
    eye16 = eye.astype(BF16)
    blt = [_dot16_nt(eye16, bd(b_vec[g] * e_left[g])) for g in groups]
    klt = [_dot16_nt(eye16, bd(kdir[g] * e_left[g])) for g in groups]
    top = [_dot16(jnp.concatenate([a_rb[g], blt[g]], axis=0),
                  jnp.concatenate([bd(pw[g][:, :RW_GW]), bd(pw[g][:, RW_GW:])], axis=1)) for g in groups]
    low = [_dot16(jnp.concatenate([a_rk[g], klt[g]], axis=0), bdv[g]) for g in groups]
    for g in groups:
        sl = sls[g]
        q_ref[0, :, sl] = (r_t[g] + top[g][:c, :RW_GW]).astype(q_ref.dtype)
        y1_ref[0, :, sl] = top[g][:c, RW_GW:] + low[g][:c]
        m_ref[0, :, sl] = (eye * jnp.exp(total[g]) + top[g][c:, :RW_GW]).astype(m_ref.dtype)
        z_ref[0, :, sl] = top[g][c:, RW_GW:] + low[g][c:]


def rwkv_prep(ps_rkv, ps_small, wup2, aup2, w0, a0, k_k, k_a, r_k):
    rows = ps_rkv.shape[0]
    feat = lambda off: pl.BlockSpec((RW_C, W), lambda d, i, off=off: (i, off))
    vec = pl.BlockSpec((1, W), lambda d, i: (0, 0))
    dvec = pl.BlockSpec((1, 1, W), lambda d, i: (d, 0, 0))
    lora = pl.BlockSpec((1, 2 * RW_RANK, W), lambda d, i: (d, 0, 0))
    out = pl.BlockSpec((1, RW_C, W), lambda d, i: (d, i, 0))
    shapes = [jax.ShapeDtypeStruct((2, rows, W), dt) for dt in (BF16, F32, BF16, F32, F32)]
    return pl.pallas_call(
        _rwkv_prep_kernel,
        grid=(2, rows // RW_C),
        in_specs=[feat(0), feat(1), feat(2),
                  pl.BlockSpec((RW_C, RW_SMALL), lambda d, i: (i, 0)),
                  lora, lora, dvec, dvec, vec, vec, vec],
        out_specs=[out] * 5,
        out_shape=shapes,
        compiler_params=_cparams(("parallel", "parallel")),
        name="rwkv_prep",
    )(ps_rkv, ps_rkv, ps_rkv, ps_small, wup2, aup2, w0, a0, k_k, k_a, r_k)


def _rwkv_recur_kernel(q_ref, y1_ref, m_ref, z_ref, y_ref, st_ref):
    c = RW_C

    @pl.when(pl.program_id(2) == 0)
    def _():
        st_ref[...] = jnp.zeros_like(st_ref)

    same_head = _head_ones()
    mask16 = jnp.where(same_head, 1.0, 0.0).astype(BF16)

    for g in range(RW_GROUPS):
        sl = slice(g * RW_GW, (g + 1) * RW_GW)
        m_bd = jnp.concatenate([m_ref[0, :, sl]] * RW_GPC, axis=0) * mask16
        z_bd = jnp.where(same_head, jnp.concatenate([z_ref[0, :, sl]] * RW_GPC, axis=0), 0.0)
        res = _dot16(jnp.concatenate([q_ref[0, :, sl], m_bd], axis=0), st_ref[g])
        y_ref[0, :, sl] = res[:c] + y1_ref[0, :, sl]
        st_ref[g] = res[c:] + z_bd


def rwkv_recur(q2, y12, m2, z2):
    rows = q2.shape[1]
    blk = pl.BlockSpec((1, RW_C, W), lambda b, d, s: (d, b * RW_NCHUNK + _rw_chunk(d, s), 0))
    return pl.pallas_call(
        _rwkv_recur_kernel,
        grid=(rows // TB, 2, RW_NCHUNK),
        in_specs=[blk] * 4,
        out_specs=blk,
        out_shape=jax.ShapeDtypeStruct((2, rows, W), F32),
        scratch_shapes=[pltpu.VMEM((RW_GROUPS, RW_GW, RW_GW), F32)],
        compiler_params=_cparams(("parallel", "parallel", "arbitrary")),
        name="rwkv_recur",
    )(q2, y12, m2, z2)


RWO_TM = 256


def _rwkv_out_kernel(y_ref, bonus_ref, sm_ref, gup_ref, lng_ref, o_ref):
    mask16 = jnp.where(_head_ones(), 1.0, 0.0).astype(BF16)
    y = y_ref[0] + y_ref[1]
    mean = _dot_exact_lhs(y, mask16, 3) * (1.0 / RW_N)
    yc = y - mean
    var = _dot_exact_lhs(yc * yc, mask16, 3) * (1.0 / RW_N)
    yn = yc * lax.rsqrt(var + RW_GN_EPS) * lng_ref[...]
    gate = jnp.dot(jax.nn.sigmoid(sm_ref[...]).astype(BF16), gup_ref[...], preferred_element_type=F32)
    o_ref[...] = ((yn + bonus_ref[0] + bonus_ref[1]) * gate).astype(o_ref.dtype)


def rwkv_output(y2, bonus2, ps_small, g_up, ln_g):
    rows = y2.shape[1]
    pair = pl.BlockSpec((2, RWO_TM, RW_GW), lambda i, g: (0, i, g))
    return pl.pallas_call(
        _rwkv_out_kernel,
        grid=(rows // RWO_TM, RW_GROUPS),
        in_specs=[pair, pair,
                  pl.BlockSpec((RWO_TM, RW_G_RANK), lambda i, g: (i, 4 * RW_RANK // RW_G_RANK)),
                  pl.BlockSpec((RW_G_RANK, RW_GW), lambda i, g: (0, g)),
                  pl.BlockSpec((1, RW_GW), lambda i, g: (0, g))],
        out_specs=pl.BlockSpec((RWO_TM, RW_GW), lambda i, g: (i, g)),
        out_shape=jax.ShapeDtypeStruct((rows, W), BF16),
        compiler_params=_cparams(("parallel", "parallel")),
        name="rwkv_output",
    )(y2, bonus2, ps_small, g_up, ln_g)


MG_TM = 768
MG_TN = 512


def _merge_kernel(y0_ref, y1_ref, y2_ref, wb_ref, g0_ref, g1_ref, g2_ref, o_ref):
    acc = None
    for y_ref, g_ref, n in ((y0_ref, g0_ref, 0), (y1_ref, g1_ref, 1), (y2_ref, g2_ref, 2)):
        proj = jnp.dot(y_ref[...], wb_ref[n], preferred_element_type=F32)
        term = jax.nn.sigmoid(g_ref[...]) * proj
        acc = term if acc is None else acc + term
    o_ref[...] = acc.astype(o_ref.dtype)


def merge_branches(y_ret, y_swa, y_rwkv, p_gate, w_branch):
    nj = D // MG_TN
    ysp = pl.BlockSpec((MG_TM, W), lambda i, j: (i, 0))
    gsp = lambda n: pl.BlockSpec((MG_TM, MG_TN), lambda i, j, n=n: (i, n * nj + j))
    return pl.pallas_call(
        _merge_kernel,
        grid=(R // MG_TM, nj),
        in_specs=[ysp, ysp, ysp, pl.BlockSpec((3, W, MG_TN), lambda i, j: (0, 0, j)), gsp(0), gsp(1), gsp(2)],
        out_specs=pl.BlockSpec((MG_TM, MG_TN), lambda i, j: (i, j)),
        out_shape=jax.ShapeDtypeStruct((R, D), BF16),
        compiler_params=_cparams(("parallel", "parallel")),
        name="merge",
    )(y_ret, y_swa, y_rwkv, w_branch, p_gate, p_gate, p_gate)


def kernel(x, c, ctx, c_ctx, norm1_g, norm2_g, w_mod, b_mod, w_in, ret_decay, swa_sink, rwkv_mu, rwkv_w0,
           rwkv_w_up, rwkv_a0, rwkv_a_up, rwkv_g_up, rwkv_k_k, rwkv_k_a, rwkv_r_k, rwkv_ln_g, w_branch,
           w_out, w_ff1, w_ff2, final_g):
    h = jnp.concatenate([ctx, x], axis=1).reshape(R, D)
    cond8 = jnp.concatenate([c, c_ctx[None, :], jnp.zeros((8 - B - 1, D), F32)], axis=0)
    mods = mod_vectors(cond8, w_mod, b_mod)
    mods = mods.reshape(L, 8, 6, D).transpose(0, 2, 1, 3)
    cos_full, sin_signed = rope_tables()

    o_swa = RET_IN
    o_rw = RET_IN + SWA_IN
    o_small = o_rw + 3 * W
    o_gate = o_rw + RWKV_IN
    zeros_lora = jnp.zeros((L, RW_RANK, W), BF16)

    for l in range(L):
        m_l = mods[l]
        w_in_cols = lambda a, b, l=l: w_in[l, :, a:b].astype(BF16)
        u = norm_modulate(h, norm1_g[l], m_l, 0, 1)
        p_ret = matmul(u, w_in_cols(0, o_swa), tn=1024, name="in_ret")
        p_swa = matmul(u, w_in_cols(o_swa, o_rw), tn=768, name="in_swa")
        p_rkv = matmul(u, w_in_cols(o_rw, o_small), tn=1024, name="in_rkv")
        p_small = matmul(u, w_in_cols(o_small, o_gate), tn=RW_SMALL, name="in_small")
        p_gate = matmul(u, w_in_cols(o_gate, N_IN), tn=1024, name="in_gate")

        y_ret = retention_mixer(p_ret, ret_decay[l])
        y_swa = swa_mixer(p_swa, swa_sink[l], cos_full, sin_signed)

        mu = rwkv_mu[l]
        ps_rkv = token_shift(p_rkv, mu[:, :3 * W], 512)
        ps_small = token_shift(p_small, mu[:, 3 * W:], RW_SMALL)
        wup = rwkv_w_up[l].astype(BF16)
        aup = rwkv_a_up[l].astype(BF16)
        z = zeros_lora[l]
        wup2 = jnp.stack([jnp.concatenate([wup[0], z], axis=0), jnp.concatenate([z, wup[1]], axis=0)])
        aup2 = jnp.stack([jnp.concatenate([aup[0], z], axis=0), jnp.concatenate([z, aup[1]], axis=0)])
        q2, y12, m2, z2, bonus2 = rwkv_prep(
            ps_rkv, ps_small, wup2, aup2, rwkv_w0[l].reshape(2, 1, W), rwkv_a0[l].reshape(2, 1, W),
            rwkv_k_k[l].reshape(1, W), rwkv_k_a[l].reshape(1, W), rwkv_r_k[l].reshape(1, W))
        y2 = rwkv_recur(q2, y12, m2, z2)
        y_rwkv = rwkv_output(y2, bonus2, ps_small, rwkv_g_up[l].astype(BF16), rwkv_ln_g[l].reshape(1, W))

        merged = merge_branches(y_ret, y_swa, y_rwkv, p_gate, w_branch[l].astype(BF16))
        h = matmul(merged, w_out[l].astype(BF16), tn=1024, epilogue="residual", h=h, mods_l=m_l,
                   which_gate=2, name="out_proj")
        u2 = norm_modulate(h, norm2_g[l], m_l, 3, 4)
        f = matmul(u2, w_ff1[l].astype(BF16), tn=1024, out_dtype=BF16, epilogue="relu2", name="ff1")
        h = matmul(f, w_ff2[l].astype(BF16), tn=1024, tk=2048, epilogue="residual", h=h, mods_l=m_l,
                   which_gate=5, name="ff2")

    return final_norm(h, final_g).reshape(B, SEQ, D)
```

```python
import functools
import math

import jax
import jax.numpy as jnp
from jax import lax
from jax.experimental import pallas as pl
from jax.experimental.pallas import tpu as pltpu

F32 = jnp.float32
BF16 = jnp.bfloat16

D = 2048
B = 4
SEQ = 2048
CTX = 256
TB = CTX + SEQ
R = B * TB
L = 4
GRID_W = 64
EPS = 1e-6
ROPE_BASE = 10000.0
NEG_INF = -1e30
W = D // 2
HD = 128
RET_HEADS = W // HD
RET_CHUNK = 128
RET_GN_EPS = 1e-5
SWA_Q_HEADS = W // HD
SWA_KV_HEADS = SWA_Q_HEADS // 4
SWA_GROUP = SWA_Q_HEADS // SWA_KV_HEADS
SWA_WINDOW = 128
SWA_BLOCK = 128
RW_N = 64
RW_HEADS = W // RW_N
RW_RANK = 64
RW_G_RANK = 128
RW_GN_EPS = 64e-5
RW_C = 64
RW_GW = 256
RW_GROUPS = W // RW_GW
RW_SMALL = 4 * RW_RANK + RW_G_RANK
D_FF = 4 * D
RET_IN = 4 * W
SWA_IN = (SWA_Q_HEADS + 2 * SWA_KV_HEADS) * HD
RWKV_IN = 3 * W + RW_SMALL
GATE_IN = 3 * D
N_IN = RET_IN + SWA_IN + RWKV_IN + GATE_IN

VMEM_LIMIT = 56 * 1024 * 1024


def _cparams(sem):
    return pltpu.CompilerParams(dimension_semantics=sem, vmem_limit_bytes=VMEM_LIMIT)


def _mod_kernel(x_ref, w_ref, b_ref, o_ref):
    x = x_ref[...]
    s = x * jax.nn.sigmoid(x)
    acc = jnp.dot(s.astype(BF16), w_ref[0].astype(BF16), preferred_element_type=F32)
    o_ref[0] = acc + b_ref[0]


def mod_vectors(cond8, w_mod, b_mod):
    tn = 1024
    n = w_mod.shape[-1]
    return pl.pallas_call(
        _mod_kernel,
        grid=(L, n // tn),
        in_specs=[
            pl.BlockSpec((8, D), lambda l, j: (0, 0)),
            pl.BlockSpec((1, D, tn), lambda l, j: (l, 0, j)),
            pl.BlockSpec((1, 1, tn), lambda l, j: (l, 0, j)),
        ],
        out_specs=pl.BlockSpec((1, 8, tn), lambda l, j: (l, 0, j)),
        out_shape=jax.ShapeDtypeStruct((L, 8, n), F32),
        compiler_params=_cparams(("parallel", "parallel")),
        name="mod_vectors",
    )(cond8, w_mod, b_mod.reshape(L, 1, n))


NORM_TM = 256
TILES_PER_BATCH = TB // NORM_TM


def _mod_row(i):
    return jnp.where(i % TILES_PER_BATCH == 0, B, i // TILES_PER_BATCH)


def _norm_mod_kernel(x_ref, g_ref, sh_ref, sc_ref, o_ref):
    row = _mod_row(pl.program_id(0))
    x = x_ref[...]
    ms = jnp.mean(x * x, axis=-1, keepdims=True)
    y = x * lax.rsqrt(ms + EPS) * g_ref[...]
    shift = sh_ref[0, pl.ds(row, 1), :]
    scale = sc_ref[0, pl.ds(row, 1), :]
    o_ref[...] = (y * (1.0 + scale) + shift).astype(o_ref.dtype)


def norm_modulate(h, g, mods_l, which_shift, which_scale):
    return pl.pallas_call(
        _norm_mod_kernel,
        grid=(R // NORM_TM,),
        in_specs=[
            pl.BlockSpec((NORM_TM, D), lambda i: (i, 0)),
            pl.BlockSpec((1, D), lambda i: (0, 0)),
            pl.BlockSpec((1, 8, D), lambda i: (which_shift, 0, 0)),
            pl.BlockSpec((1, 8, D), lambda i: (which_scale, 0, 0)),
        ],
        out_specs=pl.BlockSpec((NORM_TM, D), lambda i: (i, 0)),
        out_shape=jax.ShapeDtypeStruct((R, D), BF16),
        compiler_params=_cparams(("parallel",)),
        name="norm_modulate",
    )(h, g.reshape(1, D), mods_l, mods_l)


def _final_norm_kernel(x_ref, g_ref, o_ref):
    x = x_ref[...]
    ms = jnp.mean(x * x, axis=-1, keepdims=True)
    o_ref[...] = x * lax.rsqrt(ms + EPS) * g_ref[...]


def final_norm(h, g):
    per = SEQ // NORM_TM
    return pl.pallas_call(
        _final_norm_kernel,
        grid=(B, per),
        in_specs=[
            pl.BlockSpec((NORM_TM, D), lambda b, j: (b * TILES_PER_BATCH + CTX // NORM_TM + j, 0)),
            pl.BlockSpec((1, D), lambda b, j: (0, 0)),
        ],
        out_specs=pl.BlockSpec((NORM_TM, D), lambda b, j: (b * per + j, 0)),
        out_shape=jax.ShapeDtypeStruct((B * SEQ, D), F32),
        compiler_params=_cparams(("parallel", "parallel")),
        name="final_norm",
    )(h, g.reshape(1, D))


MM_TM = 1152


def _row_is_ctx(i, tm):
    rows = lax.broadcasted_iota(jnp.int32, (tm, 1), 0)
    return jnp.logical_and(i % (TB // tm) == 0, rows < CTX)


def _mm_finish(acc, i, o_ref, h_ref, gate_ref, epilogue, tm):
    if epilogue == "relu2":
        a = jnp.maximum(acc, 0.0)
        o_ref[...] = (a * a).astype(o_ref.dtype)
    elif epilogue == "residual":
        bidx = i // (TB // tm)
        g_b = gate_ref[0, pl.ds(bidx, 1), :]
        g_c = gate_ref[0, pl.ds(B, 1), :]
        gate = jnp.where(_row_is_ctx(i, tm), g_c, g_b)
        o_ref[...] = h_ref[...] + gate * acc
    else:
        o_ref[...] = acc.astype(o_ref.dtype)


def _mm_cached_kernel(*refs, epilogue, tm):
    if epilogue == "residual":
        x_ref, w_ref, h_ref, gate_ref, o_ref, wc_ref = refs
    else:
        x_ref, w_ref, o_ref, wc_ref = refs
        h_ref = gate_ref = None
    i = pl.program_id(1)

    @pl.when(i == 0)
    def _():
        wc_ref[...] = w_ref[...].astype(BF16)

    acc = jnp.dot(x_ref[...], wc_ref[...], preferred_element_type=F32)
    _mm_finish(acc, i, o_ref, h_ref, gate_ref, epilogue, tm)


def matmul_f32w(x, w_stack, layer, col0, n, *, tn, out_dtype=F32, epilogue="plain", h=None, mods_l=None,
                which_gate=None, name="matmul"):
    m, kdim = x.shape
    tm = MM_TM
    in_specs = [
        pl.BlockSpec((tm, kdim), lambda j, i: (i, 0)),
        pl.BlockSpec((pl.Element(kdim), pl.Element(tn)), lambda j, i: (layer * kdim, pl.multiple_of(col0 + j * tn, 128))),
    ]
    args = [x, w_stack.reshape(-1, w_stack.shape[-1])]
    aliases = {}
    if epilogue == "residual":
        in_specs += [
            pl.BlockSpec((tm, tn), lambda j, i: (i, j)),
            pl.BlockSpec((1, 8, tn), lambda j, i: (which_gate, 0, j)),
        ]
        args += [h, mods_l]
        aliases = {2: 0}
    return pl.pallas_call(
        functools.partial(_mm_cached_kernel, epilogue=epilogue, tm=tm),
        grid=(n // tn, m // tm),
        in_specs=in_specs,
        out_specs=pl.BlockSpec((tm, tn), lambda j, i: (i, j)),
        out_shape=jax.ShapeDtypeStruct((m, n), out_dtype),
        scratch_shapes=[pltpu.VMEM((kdim, tn), BF16)],
        input_output_aliases=aliases,
        compiler_params=_cparams(("arbitrary", "arbitrary")),
        name=name,
    )(*args)


def _mm_kernel(*refs, nk, epilogue, tm):
    if epilogue == "residual":
        x_ref, w_ref, h_ref, gate_ref, o_ref = refs[:5]
        rest = refs[5:]
    else:
        x_ref, w_ref, o_ref = refs[:3]
        rest = refs[3:]
        h_ref = gate_ref = None
    k = pl.program_id(2)
    part = jnp.dot(x_ref[...], w_ref[...], preferred_element_type=F32)

    def finish(acc):
        _mm_finish(acc, pl.program_id(0), o_ref, h_ref, gate_ref, epilogue, tm)

    if nk == 1:
        finish(part)
    else:
        acc_ref = rest[0]

        @pl.when(k == 0)
        def _():
            acc_ref[...] = part

        @pl.when(jnp.logical_and(k > 0, k < nk - 1))
        def _():
            acc_ref[...] += part

        @pl.when(k == nk - 1)
        def _():
            finish(acc_ref[...] + part)


def matmul(x, w, *, tn, tk=None, out_dtype=F32, epilogue="plain", h=None, mods_l=None, which_gate=None,
           name="matmul"):
    m, kdim = x.shape
    n = w.shape[1]
    tm = MM_TM
    tk = kdim if tk is None else tk
    nk = kdim // tk
    in_specs = [
        pl.BlockSpec((tm, tk), lambda i, j, k: (i, k)),
        pl.BlockSpec((tk, tn), lambda i, j, k: (k, j)),
    ]
    args = [x, w]
    aliases = {}
    if epilogue == "residual":
        in_specs += [
            pl.BlockSpec((tm, tn), lambda i, j, k: (i, j)),
            pl.BlockSpec((1, 8, tn), lambda i, j, k: (which_gate, 0, j)),
        ]
        args += [h, mods_l]
        aliases = {2: 0}
    scratch = [pltpu.VMEM((tm, tn), F32)] if nk > 1 else []
    return pl.pallas_call(
        functools.partial(_mm_kernel, nk=nk, epilogue=epilogue, tm=tm),
        grid=(m // tm, n // tn, nk),
        in_specs=in_specs,
        out_specs=pl.BlockSpec((tm, tn), lambda i, j, k: (i, j)),
        out_shape=jax.ShapeDtypeStruct((m, n), out_dtype),
        scratch_shapes=scratch,
        input_output_aliases=aliases,
        compiler_params=_cparams(("parallel", "parallel", "arbitrary")),
        name=name,
    )(*args)


RET_NCHUNK = TB // RET_CHUNK
RET_CTX_CHUNKS = CTX // RET_CHUNK


def _log_sigmoid(x):
    return jnp.minimum(x, 0.0) - jnp.log(1.0 + jnp.exp(-jnp.abs(x)))


RET_BATCH = 6


def _ret_kernel(dec_ref, q_ref, k_ref, v_ref, g_ref, o_ref, y_ref):
    hh = pl.program_id(1)
    c = RET_CHUNK
    ri = lax.broadcasted_iota(jnp.int32, (c, c), 0).astype(F32)
    ci = lax.broadcasted_iota(jnp.int32, (c, c), 1).astype(F32)
    kscale = HD ** -0.5
    nt = (((1,), (1,)), ((), ()))
    for direction in range(2):
        lg = _log_sigmoid(jnp.full((c, c), dec_ref[direction, hh], F32))
        diff = (ri - ci) if direction == 0 else (ci - ri)
        intra = jnp.where(diff >= 0, jnp.exp(lg * jnp.maximum(diff, 0.0)), 0.0)
        pos = ri if direction == 0 else (c - 1.0) - ri
        q_decay = jnp.exp(lg * (pos + 1.0))
        k_decay = jnp.exp(lg * ((c - 1.0) - pos))
        chunk_decay = jnp.exp(lg * float(c))
        if direction == 0:
            order = list(range(RET_NCHUNK))
        else:
            order = list(range(RET_CTX_CHUNKS - 1, -1, -1)) + list(range(RET_NCHUNK - 1, RET_CTX_CHUNKS - 1, -1))
        state = jnp.zeros((c, c), F32)
        for b0 in range(0, RET_NCHUNK, RET_BATCH):
            rows = [slice(ch * c, (ch + 1) * c) for ch in order[b0:b0 + RET_BATCH]]
            n = range(len(rows))
            q = [q_ref[r, :] for r in rows]
            k = [k_ref[r, :] * kscale for r in rows]
            v = [v_ref[r, :].astype(BF16) for r in rows]
            sc = [lax.dot_general(q[i].astype(BF16), k[i].astype(BF16), nt, preferred_element_type=F32) * intra
                  for i in n]
            y_in = [jnp.dot(sc[i].astype(BF16), v[i], preferred_element_type=F32) for i in n]
            kv = [jnp.dot((k[i] * k_decay).T.astype(BF16), v[i], preferred_element_type=F32) for i in n]
            states = []
            for i in n:
                states.append(state)
                state = state * chunk_decay + kv[i]
            y_x = [jnp.dot((q[i] * q_decay).astype(BF16), states[i].astype(BF16), preferred_element_type=F32)
                   for i in n]
            for i in n:
                y = y_in[i] + y_x[i]
                if direction == 0:
                    y_ref[rows[i], :] = y
                else:
                    y = y + y_ref[rows[i], :]
                    mean = jnp.mean(y, axis=-1, keepdims=True)
                    yc = y - mean
                    var = jnp.mean(yc * yc, axis=-1, keepdims=True)
                    g = g_ref[rows[i], :]
                    o_ref[rows[i], :] = (g * jax.nn.sigmoid(g) * (yc * lax.rsqrt(var + RET_GN_EPS))).astype(o_ref.dtype)


def retention_mixer(p_ret, decay_l):
    nh = RET_HEADS
    blk = lambda off: pl.BlockSpec((TB, HD), lambda b, h, off=off: (b, off + h))
    rows = p_ret.shape[0]
    return pl.pallas_call(
        _ret_kernel,
        grid=(rows // TB, nh),
        in_specs=[pl.BlockSpec(memory_space=pltpu.SMEM), blk(0), blk(nh), blk(2 * nh), blk(3 * nh)],
        out_specs=pl.BlockSpec((TB, HD), lambda b, h: (b, h)),
        out_shape=jax.ShapeDtypeStruct((rows, W), BF16),
        scratch_shapes=[pltpu.VMEM((TB, HD), F32)],
        compiler_params=_cparams(("parallel", "parallel")),
        name="retention",
    )(decay_l, p_ret, p_ret, p_ret, p_ret)


SWA_NBLK = SEQ // SWA_BLOCK
SWA_WIN = 3 * SWA_BLOCK


def _rope(x, cos, sin_signed):
    lane = lax.broadcasted_iota(jnp.int32, x.shape, 1)
    partner = jnp.where(lane % 64 < 32, pltpu.roll(x, 96, 1), pltpu.roll(x, 32, 1))
    return x * cos + partner * sin_signed


def _swa_kernel(sink_ref, q_ref, k_ref, v_ref, cos_ref, sin_ref, o_ref, kt_ref, vb_ref):
    kvh = pl.program_id(1)
    scale = HD ** -0.5
    g = SWA_GROUP
    blk = SWA_BLOCK

    vb_ref[...] = v_ref[...].astype(BF16)
    for j in range(SWA_NBLK):
        rows = slice(j * blk, (j + 1) * blk)
        k_rot = _rope(k_ref[CTX + j * blk:CTX + (j + 1) * blk, :], cos_ref[rows, :], sin_ref[rows, :])
        kt_ref[j] = k_rot.T.astype(BF16)
    kc_t = jnp.concatenate([k_ref[j * blk:(j + 1) * blk, :].T for j in range(CTX // blk)], axis=1).astype(BF16)
    vc = vb_ref[:CTX, :]

    def sink_col(rows_per_head):
        parts = [jnp.full((rows_per_head, 1), sink_ref[kvh * g + gi], F32) for gi in range(g)]
        return jnp.concatenate(parts, axis=0)

    def lane_tiles(x):
        return [x[:, j * HD:(j + 1) * HD] for j in range(x.shape[1] // HD)]

    def row_max(*xs):
        tiles = [t for x in xs for t in lane_tiles(x)]
        return jnp.max(functools.reduce(jnp.maximum, tiles), axis=-1, keepdims=True)

    def row_sum(*xs):
        tiles = [t for x in xs for t in lane_tiles(x)]
        return jnp.sum(functools.reduce(jnp.add, tiles), axis=-1, keepdims=True)

    qc = jnp.concatenate([q_ref[:CTX, gi * HD:(gi + 1) * HD] for gi in range(g)], axis=0).astype(BF16)
    s = jnp.dot(qc, kc_t, preferred_element_type=F32) * scale
    sk = sink_col(CTX)
    m = jnp.maximum(row_max(s), sk)
    e = jnp.exp(s - m)
    den = row_sum(e) + jnp.exp(sk - m)
    oc = jnp.dot(e.astype(BF16), vc, preferred_element_type=F32) / den
    for gi in range(g):
        o_ref[:CTX, gi * HD:(gi + 1) * HD] = oc[gi * CTX:(gi + 1) * CTX].astype(o_ref.dtype)

    sk_b = sink_col(blk)
    delta = (lax.broadcasted_iota(jnp.int32, (g * blk, SWA_WIN), 1)
             - lax.broadcasted_iota(jnp.int32, (g * blk, SWA_WIN), 0) % blk)

    def body(n, carry):
        r0 = pl.multiple_of(n * blk, blk)
        cos = cos_ref[pl.ds(r0, blk), :]
        sin = sin_ref[pl.ds(r0, blk), :]
        qs = [_rope(q_ref[pl.ds(CTX + r0, blk), gi * HD:(gi + 1) * HD], cos, sin) for gi in range(g)]
        qn = jnp.concatenate(qs, axis=0).astype(BF16)
        j0 = jnp.clip(n - 1, 0, SWA_NBLK - SWA_WIN // blk)
        ws = pl.multiple_of(j0 * blk, blk)
        kw_t = jnp.concatenate([kt_ref[j0 + i] for i in range(SWA_WIN // blk)], axis=1)
        vw = vb_ref[pl.ds(CTX + ws, SWA_WIN), :]
        s_win = jnp.dot(qn, kw_t, preferred_element_type=F32) * scale
        valid = jnp.abs(delta + (ws - r0)) <= SWA_WINDOW
        s_win = jnp.where(valid, s_win, NEG_INF)
        s_ctx = jnp.dot(qn, kc_t, preferred_element_type=F32) * scale
        m = jnp.maximum(row_max(s_win, s_ctx), sk_b)
        e_win = jnp.exp(s_win - m)
        e_ctx = jnp.exp(s_ctx - m)
        den = row_sum(e_win, e_ctx) + jnp.exp(sk_b - m)
        o = (jnp.dot(e_win.astype(BF16), vw, preferred_element_type=F32)
             + jnp.dot(e_ctx.astype(BF16), vc, preferred_element_type=F32)) / den
        for gi in range(g):
            o_ref[pl.ds(CTX + r0, blk), gi * HD:(gi + 1) * HD] = o[gi * blk:(gi + 1) * blk].astype(o_ref.dtype)
        return carry

    lax.fori_loop(0, SWA_NBLK, body, 0)


def swa_mixer(p_swa, sink_l, cos_full, sin_signed):
    gw = SWA_GROUP * HD
    rows = p_swa.shape[0]
    return pl.pallas_call(
        _swa_kernel,
        grid=(rows // TB, SWA_KV_HEADS),
        in_specs=[
            pl.BlockSpec(memory_space=pltpu.SMEM),
            pl.BlockSpec((TB, gw), lambda b, kv: (b, kv)),
            pl.BlockSpec((TB, HD), lambda b, kv: (b, SWA_Q_HEADS + kv)),
            pl.BlockSpec((TB, HD), lambda b, kv: (b, SWA_Q_HEADS + SWA_KV_HEADS + kv)),
            pl.BlockSpec((SEQ, HD), lambda b, kv: (0, 0)),
            pl.BlockSpec((SEQ, HD), lambda b, kv: (0, 0)),
        ],
        out_specs=pl.BlockSpec((TB, gw), lambda b, kv: (b, kv)),
        out_shape=jax.ShapeDtypeStruct((rows, W), BF16),
        scratch_shapes=[pltpu.VMEM((SWA_NBLK, HD, SWA_BLOCK), BF16), pltpu.VMEM((TB, HD), BF16)],
        compiler_params=_cparams(("parallel", "parallel")),
        name="swa",
    )(sink_l, p_swa, p_swa, p_swa, cos_full, sin_signed)


def rope_tables():
    rows = SEQ // GRID_W
    row = jnp.repeat(jnp.arange(rows), GRID_W).astype(F32)
    col = jnp.tile(jnp.arange(GRID_W), rows).astype(F32)
    n_freq = HD // 4
    inv_freq = ROPE_BASE ** (-jnp.arange(n_freq, dtype=F32) / n_freq)
    ang_r = row[:, None] * inv_freq
    ang_c = col[:, None] * inv_freq
    cr, sr, cc, sc = jnp.cos(ang_r), jnp.sin(ang_r), jnp.cos(ang_c), jnp.sin(ang_c)
    cos_full = jnp.concatenate([cr, cr, cc, cc], axis=-1)
    sin_signed = jnp.concatenate([-sr, sr, -sc, sc], axis=-1)
    return cos_full, sin_signed


def _shift_kernel(x_ref, mu_ref, o_ref):
    x = x_ref[...]
    row = lax.broadcasted_iota(jnp.int32, (TB, 1), 0)
    prev = pltpu.roll(x, 1, 0)
    prev = jnp.where(jnp.logical_or(row == 0, row == CTX), 0.0, prev)
    nxt = pltpu.roll(x, TB - 1, 0)
    nxt = jnp.where(jnp.logical_or(row == CTX - 1, row == TB - 1), 0.0, nxt)
    o_ref[...] = x + mu_ref[0:1, :] * (prev - x) + mu_ref[1:2, :] * (nxt - x)


def token_shift(p, mu, tf):
    f = p.shape[1]
    return pl.pallas_call(
        _shift_kernel,
        grid=(p.shape[0] // TB, f // tf),
        in_specs=[pl.BlockSpec((TB, tf), lambda b, j: (b, j)), pl.BlockSpec((2, tf), lambda b, j: (0, j))],
        out_specs=pl.BlockSpec((TB, tf), lambda b, j: (b, j)),
        out_shape=jax.ShapeDtypeStruct(p.shape, F32),
        compiler_params=_cparams(("parallel", "parallel")),
        name="token_shift",
    )(p, mu)


RW_NCHUNK = TB // RW_C
RW_CTX_CHUNKS = CTX // RW_C
RW_GPC = RW_GW // RW_C


def _rw_chunk(direction, s):
    back = jnp.where(s < RW_CTX_CHUNKS, RW_CTX_CHUNKS - 1 - s, RW_NCHUNK - 1 + RW_CTX_CHUNKS - s)
    return jnp.where(direction == 0, s, back)


def _head_ones():
    r = lax.broadcasted_iota(jnp.int32, (RW_GW, RW_GW), 0) // RW_N
    c = lax.broadcasted_iota(jnp.int32, (RW_GW, RW_GW), 1) // RW_N
    return r == c


def _dot16(a, b):
    return jnp.dot(a.astype(BF16), b.astype(BF16), preferred_element_type=F32)


def _dot16_nt(a, b):
    return lax.dot_general(a.astype(BF16), b.astype(BF16), (((1,), (1,)), ((), ())), preferred_element_type=F32)


def _split_bf16(x, terms):
    parts = []
    for _ in range(terms):
        p = x.astype(BF16)
        parts.append(p)
        x = x - p.astype(F32)
    return parts


def _dot_exact_lhs(x, b16, terms):
    return jnp.dot(jnp.concatenate(_split_bf16(x, terms), axis=1), jnp.concatenate([b16] * terms, axis=0),
                   preferred_element_type=F32)


def _rwkv_prep_kernel(r_ref, k_ref, v_ref, sm_ref, wup_ref, aup_ref, w0_ref, a0_ref, kk_ref, ka_ref, rk_ref,
                      q_ref, y1_ref, m_ref, z_ref, bonus_ref):
    direction = pl.program_id(0)
    c = RW_C
    same_head = _head_ones()
    mask16 = jnp.where(same_head, 1.0, 0.0).astype(BF16)

    def bd(x):
        return jnp.concatenate([x.astype(BF16)] * RW_GPC, axis=0) * mask16

    sgn = 1 - 2 * direction
    cum_terms = 3
    tt = lax.broadcasted_iota(jnp.int32, (c, cum_terms * c), 0)
    ss = lax.broadcasted_iota(jnp.int32, (c, cum_terms * c), 1) % c
    tri16 = jnp.where((tt - ss) * sgn >= 0, 1.0, 0.0).astype(BF16)
    t4 = lax.broadcasted_iota(jnp.int32, (c, RW_GW), 0)
    s4 = lax.broadcasted_iota(jnp.int32, (c, RW_GW), 1) % c
    d4 = (t4 - s4) * sgn
    strict = d4 > 0
    incl = d4 >= 0
    eye = jnp.where(d4 == 0, 1.0, 0.0)

    sm = sm_ref[...]
    tanh_wd = jnp.tanh(sm[:, :2 * RW_RANK]).astype(BF16)
    ad = sm[:, 2 * RW_RANK:4 * RW_RANK].astype(BF16)

    groups = range(RW_GROUPS)
    sls = [slice(g * RW_GW, (g + 1) * RW_GW) for g in groups]
    r = [r_ref[:, sl] for sl in sls]
    k = [k_ref[:, sl] for sl in sls]
    v = [v_ref[:, sl] for sl in sls]
    logw = [-jax.nn.sigmoid(w0_ref[0, :, sl] + jnp.dot(tanh_wd, wup_ref[0, :, sl], preferred_element_type=F32))
            * math.exp(-0.5) for sl in sls]
    alpha = [jax.nn.sigmoid(a0_ref[0, :, sl] + jnp.dot(ad, aup_ref[0, :, sl], preferred_element_type=F32))
             for sl in sls]
    kkp = [k[g] * kk_ref[:, sls[g]] for g in groups]
    ssq = [_dot_exact_lhs(kkp[g] * kkp[g], mask16, 2) for g in groups]
    kk = [kkp[g] / jnp.maximum(jnp.sqrt(ssq[g]), 1e-12) for g in groups]
    kdir = [k[g] * (1.0 + (alpha[g] - 1.0) * ka_ref[:, sls[g]]) for g in groups]
    rkr = [_dot_exact_lhs(r[g] * kdir[g] * rk_ref[:, sls[g]], mask16, 2) for g in groups]
    for g in groups:
        bonus_ref[0, :, sls[g]] = rkr[g] * v[g]
    b_vec = [kk[g] * alpha[g] for g in groups]

    cum = [jnp.dot(tri16, jnp.concatenate(_split_bf16(logw[g], cum_terms), axis=0), preferred_element_type=F32)
           for g in groups]
    total = [jnp.sum(logw[g], axis=0, keepdims=True) for g in groups]
    e_out = [jnp.exp(-cum[g]) for g in groups]
    e_left = [jnp.exp(total[g] - cum[g]) for g in groups]
    a_t = [-kk[g] * jnp.exp(cum[g] - logw[g]) for g in groups]
    r_t = [r[g] * jnp.exp(cum[g]) for g in groups]
    lhs = [jnp.concatenate([a_t[g], r_t[g]], axis=0) for g in groups]
    gb = [_dot16_nt(lhs[g], bd(b_vec[g] * e_out[g])) for g in groups]
    gk = [_dot16_nt(lhs[g], bd(kdir[g] * e_out[g])) for g in groups]
    a_ab = [jnp.where(strict, gb[g][:c], 0.0) for g in groups]
    a_ak = [jnp.where(strict, gk[g][:c], 0.0) for g in groups]
    a_rb = [jnp.where(incl, gb[g][c:], 0.0) for g in groups]
    a_rk = [jnp.where(incl, gk[g][c:], 0.0) for g in groups]

    pt = jnp.where(direction == 0, t4, c - 1 - t4)
    ps = jnp.where(direction == 0, s4, c - 1 - s4)
    px = pt ^ ps

    def level_mask(lv):
        return (px >> lv) * 2 + ((pt >> lv) & 1) == 3

    tinv = [eye + jnp.where(level_mask(0), a_ab[g], 0.0) for g in groups]
    for lv in range(1, int(math.log2(c))):
        lm = level_mask(lv)
        cross = [_dot16(jnp.where(lm, a_ab[g], 0.0), bd(tinv[g])) for g in groups]
        tinv = [tinv[g] + _dot16(tinv[g], bd(cross[g])) for g in groups]

    bdv = [bd(v[g]) for g in groups]
    avv = [_dot16(a_ak[g], bdv[g]) for g in groups]
    pw = [_dot16(tinv[g], jnp.concatenate([bd(a_t[g]), bd(avv[g])], axis=1)) for g in groups]
    eye16 = eye.astype(BF16)
    blt = [_dot16_nt(eye16, bd(b_vec[g] * e_left[g])) for g in groups]
    klt = [_dot16_nt(eye16, bd(kdir[g] * e_left[g])) for g in groups]
    top = [_dot16(jnp.concatenate([a_rb[g], blt[g]], axis=0),
                  jnp.concatenate([bd(pw[g][:, :RW_GW]), bd(pw[g][:, RW_GW:])], axis=1)) for g in groups]
    low = [_dot16(jnp.concatenate([a_rk[g], klt[g]], axis=0), bdv[g]) for g in groups]
    for g in groups:
        sl = sls[g]
        q_ref[0, :, sl] = (r_t[g] + top[g][:c, :RW_GW]).astype(q_ref.dtype)
        y1_ref[0, :, sl] = top[g][:c, RW_GW:] + low[g][:c]
        m_ref[0, :, sl] = (eye * jnp.exp(total[g]) + top[g][c:, :RW_GW]).astype(m_ref.dtype)
        z_ref[0, :, sl] = top[g][c:, RW_GW:] + low[g][c:]


def rwkv_prep(ps_rkv, ps_small, wup2, aup2, w0, a0, k_k, k_a, r_k):
    rows = ps_rkv.shape[0]
    feat = lambda off: pl.BlockSpec((RW_C, W), lambda d, i, off=off: (i, off))
    vec = pl.BlockSpec((1, W), lambda d, i: (0, 0))
    dvec = pl.BlockSpec((1, 1, W), lambda d, i: (d, 0, 0))
    lora = pl.BlockSpec((1, 2 * RW_RANK, W), lambda d, i: (d, 0, 0))
    out = pl.BlockSpec((1, RW_C, W), lambda d, i: (d, i, 0))
    shapes = [jax.ShapeDtypeStruct((2, rows, W), dt) for dt in (BF16, F32, BF16, F32, F32)]
    return pl.pallas_call(
        _rwkv_prep_kernel,
        grid=(2, rows // RW_C),
        in_specs=[feat(0), feat(1), feat(2),
                  pl.BlockSpec((RW_C, RW_SMALL), lambda d, i: (i, 0)),
                  lora, lora, dvec, dvec, vec, vec, vec],
        out_specs=[out] * 5,
        out_shape=shapes,
        compiler_params=_cparams(("parallel", "parallel")),
        name="rwkv_prep",
    )(ps_rkv, ps_rkv, ps_rkv, ps_small, wup2, aup2, w0, a0, k_k, k_a, r_k)


def _rwkv_recur_kernel(q_ref, y1_ref, m_ref, z_ref, y_ref, st_ref):
    c = RW_C

    @pl.when(pl.program_id(2) == 0)
    def _():
        st_ref[...] = jnp.zeros_like(st_ref)

    same_head = _head_ones()
    mask16 = jnp.where(same_head, 1.0, 0.0).astype(BF16)

    for g in range(RW_GROUPS):
        sl = slice(g * RW_GW, (g + 1) * RW_GW)
        m_bd = jnp.concatenate([m_ref[0, :, sl]] * RW_GPC, axis=0) * mask16
        z_bd = jnp.where(same_head, jnp.concatenate([z_ref[0, :, sl]] * RW_GPC, axis=0), 0.0)
        res = _dot16(jnp.concatenate([q_ref[0, :, sl], m_bd], axis=0), st_ref[g])
        y_ref[0, :, sl] = res[:c] + y1_ref[0, :, sl]
        st_ref[g] = res[c:] + z_bd


def rwkv_recur(q2, y12, m2, z2):
    rows = q2.shape[1]
    blk = pl.BlockSpec((1, RW_C, W), lambda b, d, s: (d, b * RW_NCHUNK + _rw_chunk(d, s), 0))
    return pl.pallas_call(
        _rwkv_recur_kernel,
        grid=(rows // TB, 2, RW_NCHUNK),
        in_specs=[blk] * 4,
        out_specs=blk,
        out_shape=jax.ShapeDtypeStruct((2, rows, W), F32),
        scratch_shapes=[pltpu.VMEM((RW_GROUPS, RW_GW, RW_GW), F32)],
        compiler_params=_cparams(("parallel", "parallel", "arbitrary")),
        name="rwkv_recur",
    )(q2, y12, m2, z2)


RWO_TM = 256


def _rwkv_out_kernel(y_ref, bonus_ref, sm_ref, gup_ref, lng_ref, o_ref):
    mask16 = jnp.where(_head_ones(), 1.0, 0.0).astype(BF16)
    y = y_ref[0] + y_ref[1]
    mean = _dot_exact_lhs(y, mask16, 3) * (1.0 / RW_N)
    yc = y - mean
    var = _dot_exact_lhs(yc * yc, mask16, 3) * (1.0 / RW_N)
    yn = yc * lax.rsqrt(var + RW_GN_EPS) * lng_ref[...]
    gate = jnp.dot(jax.nn.sigmoid(sm_ref[...]).astype(BF16), gup_ref[...], preferred_element_type=F32)
    o_ref[...] = ((yn + bonus_ref[0] + bonus_ref[1]) * gate).astype(o_ref.dtype)


def rwkv_output(y2, bonus2, ps_small, g_up, ln_g):
    rows = y2.shape[1]
    pair = pl.BlockSpec((2, RWO_TM, RW_GW), lambda i, g: (0, i, g))
    return pl.pallas_call(
        _rwkv_out_kernel,
        grid=(rows // RWO_TM, RW_GROUPS),
        in_specs=[pair, pair,
                  pl.BlockSpec((RWO_TM, RW_G_RANK), lambda i, g: (i, 4 * RW_RANK // RW_G_RANK)),
                  pl.BlockSpec((RW_G_RANK, RW_GW), lambda i, g: (0, g)),
                  pl.BlockSpec((1, RW_GW), lambda i, g: (0, g))],
        out_specs=pl.BlockSpec((RWO_TM, RW_GW), lambda i, g: (i, g)),
        out_shape=jax.ShapeDtypeStruct((rows, W), BF16),
        compiler_params=_cparams(("parallel", "parallel")),
        name="rwkv_output",
    )(y2, bonus2, ps_small, g_up, ln_g)


MG_TM = 768
MG_TN = 512


def _merge_kernel(y0_ref, y1_ref, y2_ref, wb_ref, g0_ref, g1_ref, g2_ref, o_ref, wc_ref):
    @pl.when(pl.program_id(1) == 0)
    def _():
        wc_ref[...] = wb_ref[...].astype(BF16)

    acc = None
    for y_ref, g_ref, n in ((y0_ref, g0_ref, 0), (y1_ref, g1_ref, 1), (y2_ref, g2_ref, 2)):
        proj = jnp.dot(y_ref[...], wc_ref[n], preferred_element_type=F32)
        term = jax.nn.sigmoid(g_ref[...]) * proj
        acc = term if acc is None else acc + term
    o_ref[...] = acc.astype(o_ref.dtype)


def merge_branches(y_ret, y_swa, y_rwkv, p_gate, w_branch, layer):
    nj = D // MG_TN
    ysp = pl.BlockSpec((MG_TM, W), lambda j, i: (i, 0))
    gsp = lambda n: pl.BlockSpec((MG_TM, MG_TN), lambda j, i, n=n: (i, n * nj + j))
    return pl.pallas_call(
        _merge_kernel,
        grid=(nj, R // MG_TM),
        in_specs=[ysp, ysp, ysp, pl.BlockSpec((None, 3, W, MG_TN), lambda j, i: (layer, 0, 0, j)),
                  gsp(0), gsp(1), gsp(2)],
        out_specs=pl.BlockSpec((MG_TM, MG_TN), lambda j, i: (i, j)),
        out_shape=jax.ShapeDtypeStruct((R, D), BF16),
        scratch_shapes=[pltpu.VMEM((3, W, MG_TN), BF16)],
        compiler_params=_cparams(("arbitrary", "arbitrary")),
        name="merge",
    )(y_ret, y_swa, y_rwkv, w_branch, p_gate, p_gate, p_gate)


def kernel(x, c, ctx, c_ctx, norm1_g, norm2_g, w_mod, b_mod, w_in, ret_decay, swa_sink, rwkv_mu, rwkv_w0,
           rwkv_w_up, rwkv_a0, rwkv_a_up, rwkv_g_up, rwkv_k_k, rwkv_k_a, rwkv_r_k, rwkv_ln_g, w_branch,
           w_out, w_ff1, w_ff2, final_g):
    h = jnp.concatenate([ctx, x], axis=1).reshape(R, D)
    cond8 = jnp.concatenate([c, c_ctx[None, :], jnp.zeros((8 - B - 1, D), F32)], axis=0)
    mods = mod_vectors(cond8, w_mod, b_mod)
    mods = mods.reshape(L, 8, 6, D).transpose(0, 2, 1, 3)
    cos_full, sin_signed = rope_tables()

    o_swa = RET_IN
    o_rw = RET_IN + SWA_IN
    o_small = o_rw + 3 * W
    o_gate = o_rw + RWKV_IN
    zeros_lora = jnp.zeros((L, RW_RANK, W), BF16)

    for l in range(L):
        m_l = mods[l]
        u = norm_modulate(h, norm1_g[l], m_l, 0, 1)
        p_ret = matmul_f32w(u, w_in, l, 0, RET_IN, tn=1024, name="in_ret")
        p_swa = matmul_f32w(u, w_in, l, o_swa, SWA_IN, tn=768, name="in_swa")
        p_rkv = matmul_f32w(u, w_in, l, o_rw, 3 * W, tn=1024, name="in_rkv")
        p_small = matmul_f32w(u, w_in, l, o_small, RW_SMALL, tn=RW_SMALL, name="in_small")
        p_gate = matmul_f32w(u, w_in, l, o_gate, GATE_IN, tn=1024, name="in_gate")

        y_ret = retention_mixer(p_ret, ret_decay[l])
        y_swa = swa_mixer(p_swa, swa_sink[l], cos_full, sin_signed)

        mu = rwkv_mu[l]
        ps_rkv = token_shift(p_rkv, mu[:, :3 * W], 512)
        ps_small = token_shift(p_small, mu[:, 3 * W:], RW_SMALL)
        wup = rwkv_w_up[l].astype(BF16)
        aup = rwkv_a_up[l].astype(BF16)
        z = zeros_lora[l]
        wup2 = jnp.stack([jnp.concatenate([wup[0], z], axis=0), jnp.concatenate([z, wup[1]], axis=0)])
        aup2 = jnp.stack([jnp.concatenate([aup[0], z], axis=0), jnp.concatenate([z, aup[1]], axis=0)])
        q2, y12, m2, z2, bonus2 = rwkv_prep(
            ps_rkv, ps_small, wup2, aup2, rwkv_w0[l].reshape(2, 1, W), rwkv_a0[l].reshape(2, 1, W),
            rwkv_k_k[l].reshape(1, W), rwkv_k_a[l].reshape(1, W), rwkv_r_k[l].reshape(1, W))
        y2 = rwkv_recur(q2, y12, m2, z2)
        y_rwkv = rwkv_output(y2, bonus2, ps_small, rwkv_g_up[l].astype(BF16), rwkv_ln_g[l].reshape(1, W))

        merged = merge_branches(y_ret, y_swa, y_rwkv, p_gate, w_branch, l)
        h = matmul_f32w(merged, w_out, l, 0, D, tn=1024, epilogue="residual", h=h, mods_l=m_l,
                        which_gate=2, name="out_proj")
        u2 = norm_modulate(h, norm2_g[l], m_l, 3, 4)
        f = matmul_f32w(u2, w_ff1, l, 0, D_FF, tn=1024, out_dtype=BF16, epilogue="relu2", name="ff1")
        h = matmul(f, w_ff2[l].astype(BF16), tn=1024, tk=2048, epilogue="residual", h=h, mods_l=m_l,
                   which_gate=5, name="ff2")

    return final_norm(h, final_g).reshape(B, SEQ, D)
```

```python
import functools
import math

import jax
import jax.numpy as jnp
from jax import lax
from jax.experimental import pallas as pl
from jax.experimental.pallas import tpu as pltpu

F32 = jnp.float32
BF16 = jnp.bfloat16

D = 2048
B = 4
SEQ = 2048
CTX = 256
TB = CTX + SEQ
R = B * TB
L = 4
GRID_W = 64
EPS = 1e-6
ROPE_BASE = 10000.0
NEG_INF = -1e30
W = D // 2
HD = 128
RET_HEADS = W // HD
RET_CHUNK = 128
RET_GN_EPS = 1e-5
SWA_Q_HEADS = W // HD
SWA_KV_HEADS = SWA_Q_HEADS // 4
SWA_GROUP = SWA_Q_HEADS // SWA_KV_HEADS
SWA_WINDOW = 128
SWA_BLOCK = 128
RW_N = 64
RW_HEADS = W // RW_N
RW_RANK = 64
RW_G_RANK = 128
RW_GN_EPS = 64e-5
RW_C = 64
RW_GW = 256
RW_GROUPS = W // RW_GW
RW_SMALL = 4 * RW_RANK + RW_G_RANK
D_FF = 4 * D
RET_IN = 4 * W
SWA_IN = (SWA_Q_HEADS + 2 * SWA_KV_HEADS) * HD
RWKV_IN = 3 * W + RW_SMALL
GATE_IN = 3 * D
N_IN = RET_IN + SWA_IN + RWKV_IN + GATE_IN

VMEM_LIMIT = 56 * 1024 * 1024


def _cparams(sem):
    return pltpu.CompilerParams(dimension_semantics=sem, vmem_limit_bytes=VMEM_LIMIT)


def _mod_kernel(x_ref, w_ref, b_ref, o_ref):
    x = x_ref[...]
    s = x * jax.nn.sigmoid(x)
    acc = jnp.dot(s.astype(BF16), w_ref[0].astype(BF16), preferred_element_type=F32)
    o_ref[0] = acc + b_ref[0]


def mod_vectors(cond8, w_mod, b_mod):
    tn = 1024
    n = w_mod.shape[-1]
    return pl.pallas_call(
        _mod_kernel,
        grid=(L, n // tn),
        in_specs=[
            pl.BlockSpec((8, D), lambda l, j: (0, 0)),
            pl.BlockSpec((1, D, tn), lambda l, j: (l, 0, j)),
            pl.BlockSpec((1, 1, tn), lambda l, j: (l, 0, j)),
        ],
        out_specs=pl.BlockSpec((1, 8, tn), lambda l, j: (l, 0, j)),
        out_shape=jax.ShapeDtypeStruct((L, 8, n), F32),
        compiler_params=_cparams(("parallel", "parallel")),
        name="mod_vectors",
    )(cond8, w_mod, b_mod.reshape(L, 1, n))


NORM_TM = 256
TILES_PER_BATCH = TB // NORM_TM


def _mod_row(i):
    return jnp.where(i % TILES_PER_BATCH == 0, B, i // TILES_PER_BATCH)


def _norm_mod_kernel(x_ref, g_ref, sh_ref, sc_ref, o_ref):
    row = _mod_row(pl.program_id(0))
    x = x_ref[...]
    ms = jnp.mean(x * x, axis=-1, keepdims=True)
    y = x * lax.rsqrt(ms + EPS) * g_ref[...]
    shift = sh_ref[0, pl.ds(row, 1), :]
    scale = sc_ref[0, pl.ds(row, 1), :]
    o_ref[...] = (y * (1.0 + scale) + shift).astype(o_ref.dtype)


def norm_modulate(h, g, mods_l, which_shift, which_scale):
    return pl.pallas_call(
        _norm_mod_kernel,
        grid=(R // NORM_TM,),
        in_specs=[
            pl.BlockSpec((NORM_TM, D), lambda i: (i, 0)),
            pl.BlockSpec((1, D), lambda i: (0, 0)),
            pl.BlockSpec((1, 8, D), lambda i: (which_shift, 0, 0)),
            pl.BlockSpec((1, 8, D), lambda i: (which_scale, 0, 0)),
        ],
        out_specs=pl.BlockSpec((NORM_TM, D), lambda i: (i, 0)),
        out_shape=jax.ShapeDtypeStruct((R, D), BF16),
        compiler_params=_cparams(("parallel",)),
        name="norm_modulate",
    )(h, g.reshape(1, D), mods_l, mods_l)


def _final_norm_kernel(x_ref, g_ref, o_ref):
    x = x_ref[...]
    ms = jnp.mean(x * x, axis=-1, keepdims=True)
    o_ref[...] = x * lax.rsqrt(ms + EPS) * g_ref[...]


def final_norm(h, g):
    per = SEQ // NORM_TM
    return pl.pallas_call(
        _final_norm_kernel,
        grid=(B, per),
        in_specs=[
            pl.BlockSpec((NORM_TM, D), lambda b, j: (b * TILES_PER_BATCH + CTX // NORM_TM + j, 0)),
            pl.BlockSpec((1, D), lambda b, j: (0, 0)),
        ],
        out_specs=pl.BlockSpec((NORM_TM, D), lambda b, j: (b * per + j, 0)),
        out_shape=jax.ShapeDtypeStruct((B * SEQ, D), F32),
        compiler_params=_cparams(("parallel", "parallel")),
        name="final_norm",
    )(h, g.reshape(1, D))


MM_TM = 1152


def _row_is_ctx(i, tm):
    rows = lax.broadcasted_iota(jnp.int32, (tm, 1), 0)
    return jnp.logical_and(i % (TB // tm) == 0, rows < CTX)


def _mm_finish(acc, i, o_ref, h_ref, gate_ref, epilogue, tm):
    if epilogue == "relu2":
        a = jnp.maximum(acc, 0.0)
        o_ref[...] = (a * a).astype(o_ref.dtype)
    elif epilogue == "residual":
        bidx = i // (TB // tm)
        g_b = gate_ref[0, pl.ds(bidx, 1), :]
        g_c = gate_ref[0, pl.ds(B, 1), :]
        gate = jnp.where(_row_is_ctx(i, tm), g_c, g_b)
        o_ref[...] = h_ref[...] + gate * acc
    else:
        o_ref[...] = acc.astype(o_ref.dtype)


def _mm_cached_kernel(*refs, epilogue, tm):
    if epilogue == "residual":
        x_ref, w_ref, h_ref, gate_ref, o_ref, wc_ref = refs
    else:
        x_ref, w_ref, o_ref, wc_ref = refs
        h_ref = gate_ref = None
    i = pl.program_id(1)

    @pl.when(i == 0)
    def _():
        wc_ref[...] = w_ref[...].astype(BF16)

    acc = jnp.dot(x_ref[...], wc_ref[...], preferred_element_type=F32)
    _mm_finish(acc, i, o_ref, h_ref, gate_ref, epilogue, tm)


def matmul_f32w(x, w_stack, layer, col0, n, *, tn, out_dtype=F32, epilogue="plain", h=None, mods_l=None,
                which_gate=None, name="matmul"):
    m, kdim = x.shape
    tm = MM_TM
    in_specs = [
        pl.BlockSpec((tm, kdim), lambda j, i: (i, 0)),
        pl.BlockSpec((pl.Element(kdim), pl.Element(tn)), lambda j, i: (layer * kdim, pl.multiple_of(col0 + j * tn, 128))),
    ]
    args = [x, w_stack.reshape(-1, w_stack.shape[-1])]
    aliases = {}
    if epilogue == "residual":
        in_specs += [
            pl.BlockSpec((tm, tn), lambda j, i: (i, j)),
            pl.BlockSpec((1, 8, tn), lambda j, i: (which_gate, 0, j)),
        ]
        args += [h, mods_l]
        aliases = {2: 0}
    return pl.pallas_call(
        functools.partial(_mm_cached_kernel, epilogue=epilogue, tm=tm),
        grid=(n // tn, m // tm),
        in_specs=in_specs,
        out_specs=pl.BlockSpec((tm, tn), lambda j, i: (i, j)),
        out_shape=jax.ShapeDtypeStruct((m, n), out_dtype),
        scratch_shapes=[pltpu.VMEM((kdim, tn), BF16)],
        input_output_aliases=aliases,
        compiler_params=_cparams(("arbitrary", "arbitrary")),
        name=name,
    )(*args)


def _mm_kernel(*refs, nk, epilogue, tm):
    if epilogue == "residual":
        x_ref, w_ref, h_ref, gate_ref, o_ref = refs[:5]
        rest = refs[5:]
    else:
        x_ref, w_ref, o_ref = refs[:3]
        rest = refs[3:]
        h_ref = gate_ref = None
    k = pl.program_id(2)
    part = jnp.dot(x_ref[...], w_ref[...], preferred_element_type=F32)

    def finish(acc):
        _mm_finish(acc, pl.program_id(0), o_ref, h_ref, gate_ref, epilogue, tm)

    if nk == 1:
        finish(part)
    else:
        acc_ref = rest[0]

        @pl.when(k == 0)
        def _():
            acc_ref[...] = part

        @pl.when(jnp.logical_and(k > 0, k < nk - 1))
        def _():
            acc_ref[...] += part

        @pl.when(k == nk - 1)
        def _():
            finish(acc_ref[...] + part)


def matmul(x, w, *, tn, tk=None, out_dtype=F32, epilogue="plain", h=None, mods_l=None, which_gate=None,
           name="matmul"):
    m, kdim = x.shape
    n = w.shape[1]
    tm = MM_TM
    tk = kdim if tk is None else tk
    nk = kdim // tk
    in_specs = [
        pl.BlockSpec((tm, tk), lambda i, j, k: (i, k)),
        pl.BlockSpec((tk, tn), lambda i, j, k: (k, j)),
    ]
    args = [x, w]
    aliases = {}
    if epilogue == "residual":
        in_specs += [
            pl.BlockSpec((tm, tn), lambda i, j, k: (i, j)),
            pl.BlockSpec((1, 8, tn), lambda i, j, k: (which_gate, 0, j)),
        ]
        args += [h, mods_l]
        aliases = {2: 0}
    scratch = [pltpu.VMEM((tm, tn), F32)] if nk > 1 else []
    return pl.pallas_call(
        functools.partial(_mm_kernel, nk=nk, epilogue=epilogue, tm=tm),
        grid=(m // tm, n // tn, nk),
        in_specs=in_specs,
        out_specs=pl.BlockSpec((tm, tn), lambda i, j, k: (i, j)),
        out_shape=jax.ShapeDtypeStruct((m, n), out_dtype),
        scratch_shapes=scratch,
        input_output_aliases=aliases,
        compiler_params=_cparams(("parallel", "parallel", "arbitrary")),
        name=name,
    )(*args)


RET_NCHUNK = TB // RET_CHUNK
RET_CTX_CHUNKS = CTX // RET_CHUNK


def _log_sigmoid(x):
    return jnp.minimum(x, 0.0) - jnp.log(1.0 + jnp.exp(-jnp.abs(x)))


RET_BATCH = 6


def _ret_kernel(dec_ref, q_ref, k_ref, v_ref, g_ref, o_ref, y_ref):
    hh = pl.program_id(1)
    c = RET_CHUNK
    ri = lax.broadcasted_iota(jnp.int32, (c, c), 0).astype(F32)
    ci = lax.broadcasted_iota(jnp.int32, (c, c), 1).astype(F32)
    kscale = HD ** -0.5
    nt = (((1,), (1,)), ((), ()))
    for direction in range(2):
        lg = _log_sigmoid(jnp.full((c, c), dec_ref[direction, hh], F32))
        diff = (ri - ci) if direction == 0 else (ci - ri)
        intra = jnp.where(diff >= 0, jnp.exp(lg * jnp.maximum(diff, 0.0)), 0.0)
        pos = ri if direction == 0 else (c - 1.0) - ri
        q_decay = jnp.exp(lg * (pos + 1.0))
        k_decay = jnp.exp(lg * ((c - 1.0) - pos))
        chunk_decay = jnp.exp(lg * float(c))
        if direction == 0:
            order = list(range(RET_NCHUNK))
        else:
            order = list(range(RET_CTX_CHUNKS - 1, -1, -1)) + list(range(RET_NCHUNK - 1, RET_CTX_CHUNKS - 1, -1))
        state = jnp.zeros((c, c), F32)
        for b0 in range(0, RET_NCHUNK, RET_BATCH):
            rows = [slice(ch * c, (ch + 1) * c) for ch in order[b0:b0 + RET_BATCH]]
            n = range(len(rows))
            q = [q_ref[r, :] for r in rows]
            k = [k_ref[r, :] * kscale for r in rows]
            v = [v_ref[r, :].astype(BF16) for r in rows]
            sc = [lax.dot_general(q[i].astype(BF16), k[i].astype(BF16), nt, preferred_element_type=F32) * intra
                  for i in n]
            y_in = [jnp.dot(sc[i].astype(BF16), v[i], preferred_element_type=F32) for i in n]
            kv = [jnp.dot((k[i] * k_decay).T.astype(BF16), v[i], preferred_element_type=F32) for i in n]
            states = []
            for i in n:
                states.append(state)
                state = state * chunk_decay + kv[i]
            y_x = [jnp.dot((q[i] * q_decay).astype(BF16), states[i].astype(BF16), preferred_element_type=F32)
                   for i in n]
            for i in n:
                y = y_in[i] + y_x[i]
                if direction == 0:
                    y_ref[rows[i], :] = y
                else:
                    y = y + y_ref[rows[i], :]
                    mean = jnp.mean(y, axis=-1, keepdims=True)
                    yc = y - mean
                    var = jnp.mean(yc * yc, axis=-1, keepdims=True)
                    g = g_ref[rows[i], :]
                    o_ref[rows[i], :] = (g * jax.nn.sigmoid(g) * (yc * lax.rsqrt(var + RET_GN_EPS))).astype(o_ref.dtype)


def retention_mixer(p_ret, decay_l):
    nh = RET_HEADS
    blk = lambda off: pl.BlockSpec((TB, HD), lambda b, h, off=off: (b, off + h))
    rows = p_ret.shape[0]
    return pl.pallas_call(
        _ret_kernel,
        grid=(rows // TB, nh),
        in_specs=[pl.BlockSpec(memory_space=pltpu.SMEM), blk(0), blk(nh), blk(2 * nh), blk(3 * nh)],
        out_specs=pl.BlockSpec((TB, HD), lambda b, h: (b, h)),
        out_shape=jax.ShapeDtypeStruct((rows, W), BF16),
        scratch_shapes=[pltpu.VMEM((TB, HD), F32)],
        compiler_params=_cparams(("parallel", "parallel")),
        name="retention",
    )(decay_l, p_ret, p_ret, p_ret, p_ret)


SWA_NBLK = SEQ // SWA_BLOCK
SWA_WIN = 3 * SWA_BLOCK


def _rope(x, cos, sin_signed):
    lane = lax.broadcasted_iota(jnp.int32, x.shape, 1)
    partner = jnp.where(lane % 64 < 32, pltpu.roll(x, 96, 1), pltpu.roll(x, 32, 1))
    return x * cos + partner * sin_signed


def _swa_kernel(sink_ref, q_ref, k_ref, v_ref, cos_ref, sin_ref, o_ref, kt_ref, vb_ref):
    kvh = pl.program_id(1)
    scale = HD ** -0.5
    g = SWA_GROUP
    blk = SWA_BLOCK

    vb_ref[...] = v_ref[...].astype(BF16)
    for j in range(SWA_NBLK):
        rows = slice(j * blk, (j + 1) * blk)
        k_rot = _rope(k_ref[CTX + j * blk:CTX + (j + 1) * blk, :], cos_ref[rows, :], sin_ref[rows, :])
        kt_ref[j] = k_rot.T.astype(BF16)
    kc_t = jnp.concatenate([k_ref[j * blk:(j + 1) * blk, :].T for j in range(CTX // blk)], axis=1).astype(BF16)
    vc = vb_ref[:CTX, :]

    def sink_col(rows_per_head):
        parts = [jnp.full((rows_per_head, 1), sink_ref[kvh * g + gi], F32) for gi in range(g)]
        return jnp.concatenate(parts, axis=0)

    def lane_tiles(x):
        return [x[:, j * HD:(j + 1) * HD] for j in range(x.shape[1] // HD)]

    def row_max(*xs):
        tiles = [t for x in xs for t in lane_tiles(x)]
        return jnp.max(functools.reduce(jnp.maximum, tiles), axis=-1, keepdims=True)

    def row_sum(*xs):
        tiles = [t for x in xs for t in lane_tiles(x)]
        return jnp.sum(functools.reduce(jnp.add, tiles), axis=-1, keepdims=True)

    qc = jnp.concatenate([q_ref[:CTX, gi * HD:(gi + 1) * HD] for gi in range(g)], axis=0).astype(BF16)
    s = jnp.dot(qc, kc_t, preferred_element_type=F32) * scale
    sk = sink_col(CTX)
    m = jnp.maximum(row_max(s), sk)
    e = jnp.exp(s - m)
    den = row_sum(e) + jnp.exp(sk - m)
    oc = jnp.dot(e.astype(BF16), vc, preferred_element_type=F32) / den
    for gi in range(g):
        o_ref[:CTX, gi * HD:(gi + 1) * HD] = oc[gi * CTX:(gi + 1) * CTX].astype(o_ref.dtype)

    sk_b = sink_col(blk)
    delta = (lax.broadcasted_iota(jnp.int32, (g * blk, SWA_WIN), 1)
             - lax.broadcasted_iota(jnp.int32, (g * blk, SWA_WIN), 0) % blk)

    def body(n, carry):
        r0 = pl.multiple_of(n * blk, blk)
        cos = cos_ref[pl.ds(r0, blk), :]
        sin = sin_ref[pl.ds(r0, blk), :]
        qs = [_rope(q_ref[pl.ds(CTX + r0, blk), gi * HD:(gi + 1) * HD], cos, sin) for gi in range(g)]
        qn = jnp.concatenate(qs, axis=0).astype(BF16)
        j0 = jnp.clip(n - 1, 0, SWA_NBLK - SWA_WIN // blk)
        ws = pl.multiple_of(j0 * blk, blk)
        kw_t = jnp.concatenate([kt_ref[j0 + i] for i in range(SWA_WIN // blk)], axis=1)
        vw = vb_ref[pl.ds(CTX + ws, SWA_WIN), :]
        s_win = jnp.dot(qn, kw_t, preferred_element_type=F32) * scale
        valid = jnp.abs(delta + (ws - r0)) <= SWA_WINDOW
        s_win = jnp.where(valid, s_win, NEG_INF)
        s_ctx = jnp.dot(qn, kc_t, preferred_element_type=F32) * scale
        m = jnp.maximum(row_max(s_win, s_ctx), sk_b)
        e_win = jnp.exp(s_win - m)
        e_ctx = jnp.exp(s_ctx - m)
        den = row_sum(e_win, e_ctx) + jnp.exp(sk_b - m)
        o = (jnp.dot(e_win.astype(BF16), vw, preferred_element_type=F32)
             + jnp.dot(e_ctx.astype(BF16), vc, preferred_element_type=F32)) / den
        for gi in range(g):
            o_ref[pl.ds(CTX + r0, blk), gi * HD:(gi + 1) * HD] = o[gi * blk:(gi + 1) * blk].astype(o_ref.dtype)
        return carry

    lax.fori_loop(0, SWA_NBLK, body, 0)


def swa_mixer(p_swa, sink_l, cos_full, sin_signed):
    gw = SWA_GROUP * HD
    rows = p_swa.shape[0]
    return pl.pallas_call(
        _swa_kernel,
        grid=(rows // TB, SWA_KV_HEADS),
        in_specs=[
            pl.BlockSpec(memory_space=pltpu.SMEM),
            pl.BlockSpec((TB, gw), lambda b, kv: (b, kv)),
            pl.BlockSpec((TB, HD), lambda b, kv: (b, SWA_Q_HEADS + kv)),
            pl.BlockSpec((TB, HD), lambda b, kv: (b, SWA_Q_HEADS + SWA_KV_HEADS + kv)),
            pl.BlockSpec((SEQ, HD), lambda b, kv: (0, 0)),
            pl.BlockSpec((SEQ, HD), lambda b, kv: (0, 0)),
        ],
        out_specs=pl.BlockSpec((TB, gw), lambda b, kv: (b, kv)),
        out_shape=jax.ShapeDtypeStruct((rows, W), BF16),
        scratch_shapes=[pltpu.VMEM((SWA_NBLK, HD, SWA_BLOCK), BF16), pltpu.VMEM((TB, HD), BF16)],
        compiler_params=_cparams(("parallel", "parallel")),
        name="swa",
    )(sink_l, p_swa, p_swa, p_swa, cos_full, sin_signed)


def rope_tables():
    rows = SEQ // GRID_W
    row = jnp.repeat(jnp.arange(rows), GRID_W).astype(F32)
    col = jnp.tile(jnp.arange(GRID_W), rows).astype(F32)
    n_freq = HD // 4
    inv_freq = ROPE_BASE ** (-jnp.arange(n_freq, dtype=F32) / n_freq)
    ang_r = row[:, None] * inv_freq
    ang_c = col[:, None] * inv_freq
    cr, sr, cc, sc = jnp.cos(ang_r), jnp.sin(ang_r), jnp.cos(ang_c), jnp.sin(ang_c)
    cos_full = jnp.concatenate([cr, cr, cc, cc], axis=-1)
    sin_signed = jnp.concatenate([-sr, sr, -sc, sc], axis=-1)
    return cos_full, sin_signed


def _shift_kernel(x_ref, mu_ref, o_ref):
    x = x_ref[...]
    row = lax.broadcasted_iota(jnp.int32, (TB, 1), 0)
    prev = pltpu.roll(x, 1, 0)
    prev = jnp.where(jnp.logical_or(row == 0, row == CTX), 0.0, prev)
    nxt = pltpu.roll(x, TB - 1, 0)
    nxt = jnp.where(jnp.logical_or(row == CTX - 1, row == TB - 1), 0.0, nxt)
    o_ref[...] = x + mu_ref[0:1, :] * (prev - x) + mu_ref[1:2, :] * (nxt - x)


def token_shift(p, mu, tf):
    f = p.shape[1]
    return pl.pallas_call(
        _shift_kernel,
        grid=(p.shape[0] // TB, f // tf),
        in_specs=[pl.BlockSpec((TB, tf), lambda b, j: (b, j)), pl.BlockSpec((2, tf), lambda b, j: (0, j))],
        out_specs=pl.BlockSpec((TB, tf), lambda b, j: (b, j)),
        out_shape=jax.ShapeDtypeStruct(p.shape, F32),
        compiler_params=_cparams(("parallel", "parallel")),
        name="token_shift",
    )(p, mu)


RW_NCHUNK = TB // RW_C
RW_CTX_CHUNKS = CTX // RW_C
RW_GPC = RW_GW // RW_C


def _rw_chunk(direction, s):
    back = jnp.where(s < RW_CTX_CHUNKS, RW_CTX_CHUNKS - 1 - s, RW_NCHUNK - 1 + RW_CTX_CHUNKS - s)
    return jnp.where(direction == 0, s, back)


def _head_ones():
    r = lax.broadcasted_iota(jnp.int32, (RW_GW, RW_GW), 0) // RW_N
    c = lax.broadcasted_iota(jnp.int32, (RW_GW, RW_GW), 1) // RW_N
    return r == c


def _dot16(a, b):
    return jnp.dot(a.astype(BF16), b.astype(BF16), preferred_element_type=F32)


def _dot16_nt(a, b):
    return lax.dot_general(a.astype(BF16), b.astype(BF16), (((1,), (1,)), ((), ())), preferred_element_type=F32)


def _split_bf16(x, terms):
    parts = []
    for _ in range(terms):
        p = x.astype(BF16)
        parts.append(p)
        x = x - p.astype(F32)
    return parts


def _dot_exact_lhs(x, b16, terms):
    return jnp.dot(jnp.concatenate(_split_bf16(x, terms), axis=1), jnp.concatenate([b16] * terms, axis=0),
                   preferred_element_type=F32)


def _rwkv_kernel(r_ref, k_ref, v_ref, sm_ref, wup_ref, aup_ref, w0_ref, a0_ref, kk_ref, ka_ref, rk_ref, gup_ref,
                 lng_ref, o_ref, q_s, y1_s, m_s, z_s, bonus_s, gate_s, st_ref, ybuf, bbuf):
    direction = pl.program_id(1)
    step = pl.program_id(2)
    cur = step % 2
    prev = 1 - cur
    c = RW_C
    same_head = _head_ones()
    mask16 = jnp.where(same_head, 1.0, 0.0).astype(BF16)
    groups = range(RW_GROUPS)
    sls = [slice(g * RW_GW, (g + 1) * RW_GW) for g in groups]

    @pl.when(step == 0)
    def _():
        st_ref[...] = jnp.zeros_like(st_ref)
        for ref in (q_s, y1_s, m_s, z_s, bonus_s, gate_s):
            ref[1] = jnp.zeros(ref.shape[1:], ref.dtype)

    ys = []
    for g in groups:
        m_bd = jnp.concatenate([m_s[prev, :, sls[g]]] * RW_GPC, axis=0) * mask16
        z_bd = jnp.where(same_head, jnp.concatenate([z_s[prev, :, sls[g]]] * RW_GPC, axis=0), 0.0)
        res = _dot16(jnp.concatenate([q_s[prev, :, sls[g]], m_bd], axis=0), st_ref[g])
        ys.append(res[:c] + y1_s[prev, :, sls[g]])
        st_ref[g] = res[c:] + z_bd

    def bd(x):
        return jnp.concatenate([x.astype(BF16)] * RW_GPC, axis=0) * mask16

    sgn = 1 - 2 * direction
    cum_terms = 3
    tt = lax.broadcasted_iota(jnp.int32, (c, cum_terms * c), 0)
    ss = lax.broadcasted_iota(jnp.int32, (c, cum_terms * c), 1) % c
    tri16 = jnp.where((tt - ss) * sgn >= 0, 1.0, 0.0).astype(BF16)
    t4 = lax.broadcasted_iota(jnp.int32, (c, RW_GW), 0)
    s4 = lax.broadcasted_iota(jnp.int32, (c, RW_GW), 1) % c
    d4 = (t4 - s4) * sgn
    strict = d4 > 0
    incl = d4 >= 0
    eye = jnp.where(d4 == 0, 1.0, 0.0)

    sm = sm_ref[...]
    tanh_wd = jnp.tanh(sm[:, :2 * RW_RANK]).astype(BF16)
    ad = sm[:, 2 * RW_RANK:4 * RW_RANK].astype(BF16)

    r = [r_ref[:, sl] for sl in sls]
    k = [k_ref[:, sl] for sl in sls]
    v = [v_ref[:, sl] for sl in sls]
    logw = [-jax.nn.sigmoid(w0_ref[0, :, sl] + jnp.dot(tanh_wd, wup_ref[0, :, sl], preferred_element_type=F32))
            * math.exp(-0.5) for sl in sls]
    alpha = [jax.nn.sigmoid(a0_ref[0, :, sl] + jnp.dot(ad, aup_ref[0, :, sl], preferred_element_type=F32))
             for sl in sls]
    kkp = [k[g] * kk_ref[:, sls[g]] for g in groups]
    kdir = [k[g] * (1.0 + (alpha[g] - 1.0) * ka_ref[:, sls[g]]) for g in groups]
    sums = _dot_exact_lhs(jnp.concatenate([kkp[g] * kkp[g] for g in groups]
                                          + [r[g] * kdir[g] * rk_ref[:, sls[g]] for g in groups], axis=0), mask16, 2)
    kk = [kkp[g] / jnp.maximum(jnp.sqrt(sums[g * c:(g + 1) * c]), 1e-12) for g in groups]
    for g in groups:
        bonus_s[cur, :, sls[g]] = sums[(RW_GROUPS + g) * c:(RW_GROUPS + g + 1) * c] * v[g]
    gate_s[cur] = jnp.dot(jax.nn.sigmoid(sm[:, 4 * RW_RANK:]).astype(BF16), gup_ref[...], preferred_element_type=F32)
    b_vec = [kk[g] * alpha[g] for g in groups]

    cum = [jnp.dot(tri16, jnp.concatenate(_split_bf16(logw[g], cum_terms), axis=0), preferred_element_type=F32)
           for g in groups]
    total = [jnp.sum(logw[g], axis=0, keepdims=True) for g in groups]
    e_out = [jnp.exp(-cum[g]) for g in groups]
    e_left = [jnp.exp(total[g] - cum[g]) for g in groups]
    a_t = [-kk[g] * jnp.exp(cum[g] - logw[g]) for g in groups]
    r_t = [r[g] * jnp.exp(cum[g]) for g in groups]
    lhs = [jnp.concatenate([a_t[g], r_t[g]], axis=0) for g in groups]
    gb = [_dot16_nt(lhs[g], bd(b_vec[g] * e_out[g])) for g in groups]
    gk = [_dot16_nt(lhs[g], bd(kdir[g] * e_out[g])) for g in groups]
    a_ab = [jnp.where(strict, gb[g][:c], 0.0) for g in groups]
    a_ak = [jnp.where(strict, gk[g][:c], 0.0) for g in groups]
    a_rb = [jnp.where(incl, gb[g][c:], 0.0) for g in groups]
    a_rk = [jnp.where(incl, gk[g][c:], 0.0) for g in groups]

    pt = jnp.where(direction == 0, t4, c - 1 - t4)
    ps = jnp.where(direction == 0, s4, c - 1 - s4)
    px = pt ^ ps

    def level_mask(lv):
        return (px >> lv) * 2 + ((pt >> lv) & 1) == 3

    tinv = [eye + jnp.where(level_mask(0), a_ab[g], 0.0) for g in groups]
    for lv in range(1, int(math.log2(c))):
        lm = level_mask(lv)
        cross = [_dot16(jnp.where(lm, a_ab[g], 0.0), bd(tinv[g])) for g in groups]
        tinv = [tinv[g] + _dot16(tinv[g], bd(cross[g])) for g in groups]

    bdv = [bd(v[g]) for g in groups]
    avv = [_dot16(a_ak[g], bdv[g]) for g in groups]
    pw = [_dot16(tinv[g], jnp.concatenate([bd(a_t[g]), bd(avv[g])], axis=1)) for g in groups]
    eye16 = eye.astype(BF16)
    blt = [_dot16_nt(eye16, bd(b_vec[g] * e_left[g])) for g in groups]
    klt = [_dot16_nt(eye16, bd(kdir[g] * e_left[g])) for g in groups]
    top = [_dot16(jnp.concatenate([a_rb[g], blt[g]], axis=0),
                  jnp.concatenate([bd(pw[g][:, :RW_GW]), bd(pw[g][:, RW_GW:])], axis=1)) for g in groups]
    low = [_dot16(jnp.concatenate([a_rk[g], klt[g]], axis=0), bdv[g]) for g in groups]
    for g in groups:
        sl = sls[g]
        q_s[cur, :, sl] = (r_t[g] + top[g][:c, :RW_GW]).astype(q_s.dtype)
        y1_s[cur, :, sl] = top[g][:c, RW_GW:] + low[g][:c]
        m_s[cur, :, sl] = (eye * jnp.exp(total[g]) + top[g][c:, :RW_GW]).astype(m_s.dtype)
        z_s[cur, :, sl] = top[g][c:, RW_GW:] + low[g][c:]

    chunk_prev = _rw_chunk(direction, jnp.maximum(step - 1, 0))

    @pl.when(direction == 0)
    def _():
        for g in groups:
            ybuf[chunk_prev, :, sls[g]] = ys[g]
            bbuf[chunk_prev, :, sls[g]] = bonus_s[prev, :, sls[g]]

    @pl.when(direction == 1)
    def _():
        y4 = jnp.concatenate([ys[g] + ybuf[chunk_prev, :, sls[g]] for g in groups], axis=0)
        mean = _dot_exact_lhs(y4, mask16, 2) * (1.0 / RW_N)
        yc = y4 - mean
        var = _dot_exact_lhs(yc * yc, mask16, 2) * (1.0 / RW_N)
        yn = yc * lax.rsqrt(var + RW_GN_EPS)
        for g in groups:
            sl = sls[g]
            bonus = bbuf[chunk_prev, :, sl] + bonus_s[prev, :, sl]
            o_ref[:, sl] = ((yn[g * c:(g + 1) * c] * lng_ref[:, sl] + bonus) * gate_s[prev, :, sl]).astype(o_ref.dtype)


def rwkv_mixer(ps_rkv, ps_small, wup2, aup2, w0, a0, k_k, k_a, r_k, g_up, ln_g):
    rows = ps_rkv.shape[0]
    nsteps = RW_NCHUNK + 1

    def in_blk(b, d, s):
        return b * RW_NCHUNK + _rw_chunk(d, jnp.minimum(s, RW_NCHUNK - 1))

    def out_blk(b, d, s):
        return b * RW_NCHUNK + _rw_chunk(1, jnp.where(d == 0, 0, jnp.maximum(s - 1, 0)))

    feat = lambda off: pl.BlockSpec((RW_C, W), lambda b, d, s, off=off: (in_blk(b, d, s), off))
    vec = pl.BlockSpec((1, W), lambda b, d, s: (0, 0))
    dvec = pl.BlockSpec((1, 1, W), lambda b, d, s: (d, 0, 0))
    lora = pl.BlockSpec((1, 2 * RW_RANK, W), lambda b, d, s: (d, 0, 0))
    slot = lambda dt: pltpu.VMEM((2, RW_C, W), dt)
    return pl.pallas_call(
        _rwkv_kernel,
        grid=(rows // TB, 2, nsteps),
        in_specs=[feat(0), feat(1), feat(2),
                  pl.BlockSpec((RW_C, RW_SMALL), lambda b, d, s: (in_blk(b, d, s), 0)),
                  lora, lora, dvec, dvec, vec, vec, vec,
                  pl.BlockSpec((RW_G_RANK, W), lambda b, d, s: (0, 0)), vec],
        out_specs=pl.BlockSpec((RW_C, W), lambda b, d, s: (out_blk(b, d, s), 0)),
        out_shape=jax.ShapeDtypeStruct((rows, W), BF16),
        scratch_shapes=[slot(BF16), slot(F32), slot(BF16), slot(F32), slot(F32), slot(F32),
                        pltpu.VMEM((RW_GROUPS, RW_GW, RW_GW), F32),
                        pltpu.VMEM((RW_NCHUNK, RW_C, W), F32), pltpu.VMEM((RW_NCHUNK, RW_C, W), F32)],
        compiler_params=_cparams(("arbitrary", "arbitrary", "arbitrary")),
        name="rwkv_mixer",
    )(ps_rkv, ps_rkv, ps_rkv, ps_small, wup2, aup2, w0, a0, k_k, k_a, r_k, g_up, ln_g)


MG_TM = 768
MG_TN = 512


def _merge_kernel(y0_ref, y1_ref, y2_ref, wb_ref, g0_ref, g1_ref, g2_ref, o_ref, wc_ref):
    @pl.when(pl.program_id(1) == 0)
    def _():
        wc_ref[...] = wb_ref[...].astype(BF16)

    acc = None
    for y_ref, g_ref, n in ((y0_ref, g0_ref, 0), (y1_ref, g1_ref, 1), (y2_ref, g2_ref, 2)):
        proj = jnp.dot(y_ref[...], wc_ref[n], preferred_element_type=F32)
        term = jax.nn.sigmoid(g_ref[...]) * proj
        acc = term if acc is None else acc + term
    o_ref[...] = acc.astype(o_ref.dtype)


def merge_branches(y_ret, y_swa, y_rwkv, p_gate, w_branch, layer):
    nj = D // MG_TN
    ysp = pl.BlockSpec((MG_TM, W), lambda j, i: (i, 0))
    gsp = lambda n: pl.BlockSpec((MG_TM, MG_TN), lambda j, i, n=n: (i, n * nj + j))
    return pl.pallas_call(
        _merge_kernel,
        grid=(nj, R // MG_TM),
        in_specs=[ysp, ysp, ysp, pl.BlockSpec((None, 3, W, MG_TN), lambda j, i: (layer, 0, 0, j)),
                  gsp(0), gsp(1), gsp(2)],
        out_specs=pl.BlockSpec((MG_TM, MG_TN), lambda j, i: (i, j)),
        out_shape=jax.ShapeDtypeStruct((R, D), BF16),
        scratch_shapes=[pltpu.VMEM((3, W, MG_TN), BF16)],
        compiler_params=_cparams(("arbitrary", "arbitrary")),
        name="merge",
    )(y_ret, y_swa, y_rwkv, w_branch, p_gate, p_gate, p_gate)


def kernel(x, c, ctx, c_ctx, norm1_g, norm2_g, w_mod, b_mod, w_in, ret_decay, swa_sink, rwkv_mu, rwkv_w0,
           rwkv_w_up, rwkv_a0, rwkv_a_up, rwkv_g_up, rwkv_k_k, rwkv_k_a, rwkv_r_k, rwkv_ln_g, w_branch,
           w_out, w_ff1, w_ff2, final_g):
    h = jnp.concatenate([ctx, x], axis=1).reshape(R, D)
    cond8 = jnp.concatenate([c, c_ctx[None, :], jnp.zeros((8 - B - 1, D), F32)], axis=0)
    mods = mod_vectors(cond8, w_mod, b_mod)
    mods = mods.reshape(L, 8, 6, D).transpose(0, 2, 1, 3)
    cos_full, sin_signed = rope_tables()

    o_swa = RET_IN
    o_rw = RET_IN + SWA_IN
    o_small = o_rw + 3 * W
    o_gate = o_rw + RWKV_IN
    zeros_lora = jnp.zeros((L, RW_RANK, W), BF16)

    for l in range(L):
        m_l = mods[l]
        u = norm_modulate(h, norm1_g[l], m_l, 0, 1)
        p_ret = matmul_f32w(u, w_in, l, 0, RET_IN, tn=1024, name="in_ret")
        p_swa = matmul_f32w(u, w_in, l, o_swa, SWA_IN, tn=768, name="in_swa")
        p_rkv = matmul_f32w(u, w_in, l, o_rw, 3 * W, tn=1024, name="in_rkv")
        p_small = matmul_f32w(u, w_in, l, o_small, RW_SMALL, tn=RW_SMALL, name="in_small")
        p_gate = matmul_f32w(u, w_in, l, o_gate, GATE_IN, tn=1024, name="in_gate")

        y_ret = retention_mixer(p_ret, ret_decay[l])
        y_swa = swa_mixer(p_swa, swa_sink[l], cos_full, sin_signed)

        mu = rwkv_mu[l]
        ps_rkv = token_shift(p_rkv, mu[:, :3 * W], 512)
        ps_small = token_shift(p_small, mu[:, 3 * W:], RW_SMALL)
        wup = rwkv_w_up[l].astype(BF16)
        aup = rwkv_a_up[l].astype(BF16)
        z = zeros_lora[l]
        wup2 = jnp.stack([jnp.concatenate([wup[0], z], axis=0), jnp.concatenate([z, wup[1]], axis=0)])
        aup2 = jnp.stack([jnp.concatenate([aup[0], z], axis=0), jnp.concatenate([z, aup[1]], axis=0)])
        y_rwkv = rwkv_mixer(
            ps_rkv, ps_small, wup2, aup2, rwkv_w0[l].reshape(2, 1, W), rwkv_a0[l].reshape(2, 1, W),
            rwkv_k_k[l].reshape(1, W), rwkv_k_a[l].reshape(1, W), rwkv_r_k[l].reshape(1, W),
            rwkv_g_up[l].astype(BF16), rwkv_ln_g[l].reshape(1, W))

        merged = merge_branches(y_ret, y_swa, y_rwkv, p_gate, w_branch, l)
        h = matmul_f32w(merged, w_out, l, 0, D, tn=1024, epilogue="residual", h=h, mods_l=m_l,
                        which_gate=2, name="out_proj")
        u2 = norm_modulate(h, norm2_g[l], m_l, 3, 4)
        f = matmul_f32w(u2, w_ff1, l, 0, D_FF, tn=1024, out_dtype=BF16, epilogue="relu2", name="ff1")
        h = matmul(f, w_ff2[l].astype(BF16), tn=1024, tk=2048, epilogue="residual", h=h, mods_l=m_l,
                   which_gate=5, name="ff2")

    return final_norm(h, final_g).reshape(B, SEQ, D)
```

```python
import functools
import math

import jax
import jax.numpy as jnp
from jax import lax
from jax.experimental import pallas as pl
from jax.experimental.pallas import tpu as pltpu

F32 = jnp.float32
BF16 = jnp.bfloat16

D = 2048
B = 4
SEQ = 2048
CTX = 256
TB = CTX + SEQ
R = B * TB
L = 4
GRID_W = 64
EPS = 1e-6
ROPE_BASE = 10000.0
NEG_INF = -1e30
W = D // 2
HD = 128
RET_HEADS = W // HD
RET_CHUNK = 128
RET_GN_EPS = 1e-5
SWA_Q_HEADS = W // HD
SWA_KV_HEADS = SWA_Q_HEADS // 4
SWA_GROUP = SWA_Q_HEADS // SWA_KV_HEADS
SWA_WINDOW = 128
SWA_BLOCK = 128
RW_N = 64
RW_HEADS = W // RW_N
RW_RANK = 64
RW_G_RANK = 128
RW_GN_EPS = 64e-5
RW_C = 64
RW_GW = 256
RW_GROUPS = W // RW_GW
RW_SMALL = 4 * RW_RANK + RW_G_RANK
D_FF = 4 * D
RET_IN = 4 * W
SWA_IN = (SWA_Q_HEADS + 2 * SWA_KV_HEADS) * HD
RWKV_IN = 3 * W + RW_SMALL
GATE_IN = 3 * D
N_IN = RET_IN + SWA_IN + RWKV_IN + GATE_IN

VMEM_LIMIT = 56 * 1024 * 1024


def _cparams(sem):
    return pltpu.CompilerParams(dimension_semantics=sem, vmem_limit_bytes=VMEM_LIMIT)


def _mod_kernel(x_ref, w_ref, b_ref, o_ref):
    x = x_ref[...]
    s = x * jax.nn.sigmoid(x)
    acc = jnp.dot(s.astype(BF16), w_ref[0].astype(BF16), preferred_element_type=F32)
    o_ref[0] = acc + b_ref[0]


def mod_vectors(cond8, w_mod, b_mod):
    tn = 1024
    n = w_mod.shape[-1]
    return pl.pallas_call(
        _mod_kernel,
        grid=(L, n // tn),
        in_specs=[
            pl.BlockSpec((8, D), lambda l, j: (0, 0)),
            pl.BlockSpec((1, D, tn), lambda l, j: (l, 0, j)),
            pl.BlockSpec((1, 1, tn), lambda l, j: (l, 0, j)),
        ],
        out_specs=pl.BlockSpec((1, 8, tn), lambda l, j: (l, 0, j)),
        out_shape=jax.ShapeDtypeStruct((L, 8, n), F32),
        compiler_params=_cparams(("parallel", "parallel")),
        name="mod_vectors",
    )(cond8, w_mod, b_mod.reshape(L, 1, n))


NORM_TM = 256
TILES_PER_BATCH = TB // NORM_TM


def _mod_row(i):
    return jnp.where(i % TILES_PER_BATCH == 0, B, i // TILES_PER_BATCH)


def _norm_mod_kernel(x_ref, g_ref, sh_ref, sc_ref, o_ref):
    row = _mod_row(pl.program_id(0))
    x = x_ref[...]
    ms = jnp.mean(x * x, axis=-1, keepdims=True)
    y = x * lax.rsqrt(ms + EPS) * g_ref[...]
    shift = sh_ref[0, pl.ds(row, 1), :]
    scale = sc_ref[0, pl.ds(row, 1), :]
    o_ref[...] = (y * (1.0 + scale) + shift).astype(o_ref.dtype)


def norm_modulate(h, g, mods_l, which_shift, which_scale):
    return pl.pallas_call(
        _norm_mod_kernel,
        grid=(R // NORM_TM,),
        in_specs=[
            pl.BlockSpec((NORM_TM, D), lambda i: (i, 0)),
            pl.BlockSpec((1, D), lambda i: (0, 0)),
            pl.BlockSpec((1, 8, D), lambda i: (which_shift, 0, 0)),
            pl.BlockSpec((1, 8, D), lambda i: (which_scale, 0, 0)),
        ],
        out_specs=pl.BlockSpec((NORM_TM, D), lambda i: (i, 0)),
        out_shape=jax.ShapeDtypeStruct((R, D), BF16),
        compiler_params=_cparams(("parallel",)),
        name="norm_modulate",
    )(h, g.reshape(1, D), mods_l, mods_l)


def _final_norm_kernel(x_ref, g_ref, o_ref):
    x = x_ref[...]
    ms = jnp.mean(x * x, axis=-1, keepdims=True)
    o_ref[...] = x * lax.rsqrt(ms + EPS) * g_ref[...]


def final_norm(h, g):
    per = SEQ // NORM_TM
    return pl.pallas_call(
        _final_norm_kernel,
        grid=(B, per),
        in_specs=[
            pl.BlockSpec((NORM_TM, D), lambda b, j: (b * TILES_PER_BATCH + CTX // NORM_TM + j, 0)),
            pl.BlockSpec((1, D), lambda b, j: (0, 0)),
        ],
        out_specs=pl.BlockSpec((NORM_TM, D), lambda b, j: (b * per + j, 0)),
        out_shape=jax.ShapeDtypeStruct((B * SEQ, D), F32),
        compiler_params=_cparams(("parallel", "parallel")),
        name="final_norm",
    )(h, g.reshape(1, D))


MM_TM = 1152


def _row_is_ctx(i, tm):
    rows = lax.broadcasted_iota(jnp.int32, (tm, 1), 0)
    return jnp.logical_and(i % (TB // tm) == 0, rows < CTX)


def _mm_finish(acc, i, o_ref, h_ref, gate_ref, epilogue, tm):
    if epilogue == "relu2":
        a = jnp.maximum(acc, 0.0)
        o_ref[...] = (a * a).astype(o_ref.dtype)
    elif epilogue == "residual":
        bidx = i // (TB // tm)
        g_b = gate_ref[0, pl.ds(bidx, 1), :]
        g_c = gate_ref[0, pl.ds(B, 1), :]
        gate = jnp.where(_row_is_ctx(i, tm), g_c, g_b)
        o_ref[...] = h_ref[...] + gate * acc
    else:
        o_ref[...] = acc.astype(o_ref.dtype)


def _mm_cached_kernel(*refs, epilogue, tm):
    if epilogue == "residual":
        x_ref, w_ref, h_ref, gate_ref, o_ref, wc_ref = refs
    else:
        x_ref, w_ref, o_ref, wc_ref = refs
        h_ref = gate_ref = None
    i = pl.program_id(1)

    @pl.when(i == 0)
    def _():
        wc_ref[...] = w_ref[...].astype(BF16)

    acc = jnp.dot(x_ref[...], wc_ref[...], preferred_element_type=F32)
    _mm_finish(acc, i, o_ref, h_ref, gate_ref, epilogue, tm)


def matmul_f32w(x, w_stack, layer, col0, n, *, tn, out_dtype=F32, epilogue="plain", h=None, mods_l=None,
                which_gate=None, name="matmul"):
    m, kdim = x.shape
    tm = MM_TM
    in_specs = [
        pl.BlockSpec((tm, kdim), lambda j, i: (i, 0)),
        pl.BlockSpec((pl.Element(kdim), pl.Element(tn)), lambda j, i: (layer * kdim, pl.multiple_of(col0 + j * tn, 128))),
    ]
    args = [x, w_stack.reshape(-1, w_stack.shape[-1])]
    aliases = {}
    if epilogue == "residual":
        in_specs += [
            pl.BlockSpec((tm, tn), lambda j, i: (i, j)),
            pl.BlockSpec((1, 8, tn), lambda j, i: (which_gate, 0, j)),
        ]
        args += [h, mods_l]
        aliases = {2: 0}
    return pl.pallas_call(
        functools.partial(_mm_cached_kernel, epilogue=epilogue, tm=tm),
        grid=(n // tn, m // tm),
        in_specs=in_specs,
        out_specs=pl.BlockSpec((tm, tn), lambda j, i: (i, j)),
        out_shape=jax.ShapeDtypeStruct((m, n), out_dtype),
        scratch_shapes=[pltpu.VMEM((kdim, tn), BF16)],
        input_output_aliases=aliases,
        compiler_params=_cparams(("arbitrary", "arbitrary")),
        name=name,
    )(*args)


def _mm_kernel(*refs, nk, epilogue, tm):
    if epilogue == "residual":
        x_ref, w_ref, h_ref, gate_ref, o_ref = refs[:5]
        rest = refs[5:]
    else:
        x_ref, w_ref, o_ref = refs[:3]
        rest = refs[3:]
        h_ref = gate_ref = None
    k = pl.program_id(2)
    part = jnp.dot(x_ref[...], w_ref[...], preferred_element_type=F32)

    def finish(acc):
        _mm_finish(acc, pl.program_id(0), o_ref, h_ref, gate_ref, epilogue, tm)

    if nk == 1:
        finish(part)
    else:
        acc_ref = rest[0]

        @pl.when(k == 0)
        def _():
            acc_ref[...] = part

        @pl.when(jnp.logical_and(k > 0, k < nk - 1))
        def _():
            acc_ref[...] += part

        @pl.when(k == nk - 1)
        def _():
            finish(acc_ref[...] + part)


def matmul(x, w, *, tn, tk=None, out_dtype=F32, epilogue="plain", h=None, mods_l=None, which_gate=None,
           name="matmul"):
    m, kdim = x.shape
    n = w.shape[1]
    tm = MM_TM
    tk = kdim if tk is None else tk
    nk = kdim // tk
    in_specs = [
        pl.BlockSpec((tm, tk), lambda i, j, k: (i, k)),
        pl.BlockSpec((tk, tn), lambda i, j, k: (k, j)),
    ]
    args = [x, w]
    aliases = {}
    if epilogue == "residual":
        in_specs += [
            pl.BlockSpec((tm, tn), lambda i, j, k: (i, j)),
            pl.BlockSpec((1, 8, tn), lambda i, j, k: (which_gate, 0, j)),
        ]
        args += [h, mods_l]
        aliases = {2: 0}
    scratch = [pltpu.VMEM((tm, tn), F32)] if nk > 1 else []
    return pl.pallas_call(
        functools.partial(_mm_kernel, nk=nk, epilogue=epilogue, tm=tm),
        grid=(m // tm, n // tn, nk),
        in_specs=in_specs,
        out_specs=pl.BlockSpec((tm, tn), lambda i, j, k: (i, j)),
        out_shape=jax.ShapeDtypeStruct((m, n), out_dtype),
        scratch_shapes=scratch,
        input_output_aliases=aliases,
        compiler_params=_cparams(("parallel", "parallel", "arbitrary")),
        name=name,
    )(*args)


RET_NCHUNK = TB // RET_CHUNK
RET_CTX_CHUNKS = CTX // RET_CHUNK


def _log_sigmoid(x):
    return jnp.minimum(x, 0.0) - jnp.log(1.0 + jnp.exp(-jnp.abs(x)))


RET_BATCH = 6


def _ret_kernel(dec_ref, q_ref, k_ref, v_ref, g_ref, o_ref, y_ref):
    hh = pl.program_id(1)
    c = RET_CHUNK
    ri = lax.broadcasted_iota(jnp.int32, (c, c), 0).astype(F32)
    ci = lax.broadcasted_iota(jnp.int32, (c, c), 1).astype(F32)
    kscale = HD ** -0.5
    nt = (((1,), (1,)), ((), ()))
    for direction in range(2):
        lg = _log_sigmoid(jnp.full((c, c), dec_ref[direction, hh], F32))
        diff = (ri - ci) if direction == 0 else (ci - ri)
        intra = jnp.where(diff >= 0, jnp.exp(lg * jnp.maximum(diff, 0.0)), 0.0)
        pos = ri if direction == 0 else (c - 1.0) - ri
        q_decay = jnp.exp(lg * (pos + 1.0))
        k_decay = jnp.exp(lg * ((c - 1.0) - pos))
        chunk_decay = jnp.exp(lg * float(c))
        if direction == 0:
            order = list(range(RET_NCHUNK))
        else:
            order = list(range(RET_CTX_CHUNKS - 1, -1, -1)) + list(range(RET_NCHUNK - 1, RET_CTX_CHUNKS - 1, -1))
        state = jnp.zeros((c, c), F32)
        for b0 in range(0, RET_NCHUNK, RET_BATCH):
            rows = [slice(ch * c, (ch + 1) * c) for ch in order[b0:b0 + RET_BATCH]]
            n = range(len(rows))
            q = [q_ref[r, :] for r in rows]
            k = [k_ref[r, :] * kscale for r in rows]
            v = [v_ref[r, :].astype(BF16) for r in rows]
            sc = [lax.dot_general(q[i].astype(BF16), k[i].astype(BF16), nt, preferred_element_type=F32) * intra
                  for i in n]
            y_in = [jnp.dot(sc[i].astype(BF16), v[i], preferred_element_type=F32) for i in n]
            kv = [jnp.dot((k[i] * k_decay).T.astype(BF16), v[i], preferred_element_type=F32) for i in n]
            states = []
            for i in n:
                states.append(state)
                state = state * chunk_decay + kv[i]
            y_x = [jnp.dot((q[i] * q_decay).astype(BF16), states[i].astype(BF16), preferred_element_type=F32)
                   for i in n]
            for i in n:
                y = y_in[i] + y_x[i]
                if direction == 0:
                    y_ref[rows[i], :] = y
                else:
                    y = y + y_ref[rows[i], :]
                    mean = jnp.mean(y, axis=-1, keepdims=True)
                    yc = y - mean
                    var = jnp.mean(yc * yc, axis=-1, keepdims=True)
                    g = g_ref[rows[i], :]
                    o_ref[rows[i], :] = (g * jax.nn.sigmoid(g) * (yc * lax.rsqrt(var + RET_GN_EPS))).astype(o_ref.dtype)


def retention_mixer(p_ret, decay_l):
    nh = RET_HEADS
    blk = lambda off: pl.BlockSpec((TB, HD), lambda b, h, off=off: (b, off + h))
    rows = p_ret.shape[0]
    return pl.pallas_call(
        _ret_kernel,
        grid=(rows // TB, nh),
        in_specs=[pl.BlockSpec(memory_space=pltpu.SMEM), blk(0), blk(nh), blk(2 * nh), blk(3 * nh)],
        out_specs=pl.BlockSpec((TB, HD), lambda b, h: (b, h)),
        out_shape=jax.ShapeDtypeStruct((rows, W), BF16),
        scratch_shapes=[pltpu.VMEM((TB, HD), F32)],
        compiler_params=_cparams(("parallel", "parallel")),
        name="retention",
    )(decay_l, p_ret, p_ret, p_ret, p_ret)


SWA_NBLK = SEQ // SWA_BLOCK
SWA_WIN = 3 * SWA_BLOCK


def _rope(x, cos, sin_signed):
    lane = lax.broadcasted_iota(jnp.int32, x.shape, 1)
    partner = jnp.where(lane % 64 < 32, pltpu.roll(x, 96, 1), pltpu.roll(x, 32, 1))
    return x * cos + partner * sin_signed


def _swa_kernel(sink_ref, q_ref, k_ref, v_ref, cos_ref, sin_ref, o_ref, kt_ref, vb_ref):
    kvh = pl.program_id(1)
    scale = HD ** -0.5
    g = SWA_GROUP
    blk = SWA_BLOCK

    vb_ref[...] = v_ref[...].astype(BF16)
    for j in range(SWA_NBLK):
        rows = slice(j * blk, (j + 1) * blk)
        k_rot = _rope(k_ref[CTX + j * blk:CTX + (j + 1) * blk, :], cos_ref[rows, :], sin_ref[rows, :])
        kt_ref[j] = k_rot.T.astype(BF16)
    kc_t = jnp.concatenate([k_ref[j * blk:(j + 1) * blk, :].T for j in range(CTX // blk)], axis=1).astype(BF16)
    vc = vb_ref[:CTX, :]

    def sink_col(rows_per_head):
        parts = [jnp.full((rows_per_head, 1), sink_ref[kvh * g + gi], F32) for gi in range(g)]
        return jnp.concatenate(parts, axis=0)

    def lane_tiles(x):
        return [x[:, j * HD:(j + 1) * HD] for j in range(x.shape[1] // HD)]

    def row_max(*xs):
        tiles = [t for x in xs for t in lane_tiles(x)]
        return jnp.max(functools.reduce(jnp.maximum, tiles), axis=-1, keepdims=True)

    def row_sum(*xs):
        tiles = [t for x in xs for t in lane_tiles(x)]
        return jnp.sum(functools.reduce(jnp.add, tiles), axis=-1, keepdims=True)

    qc = jnp.concatenate([q_ref[:CTX, gi * HD:(gi + 1) * HD] for gi in range(g)], axis=0).astype(BF16)
    s = jnp.dot(qc, kc_t, preferred_element_type=F32) * scale
    sk = sink_col(CTX)
    m = jnp.maximum(row_max(s), sk)
    e = jnp.exp(s - m)
    den = row_sum(e) + jnp.exp(sk - m)
    oc = jnp.dot(e.astype(BF16), vc, preferred_element_type=F32) / den
    for gi in range(g):
        o_ref[:CTX, gi * HD:(gi + 1) * HD] = oc[gi * CTX:(gi + 1) * CTX].astype(o_ref.dtype)

    sk_b = sink_col(blk)
    delta = (lax.broadcasted_iota(jnp.int32, (g * blk, SWA_WIN), 1)
             - lax.broadcasted_iota(jnp.int32, (g * blk, SWA_WIN), 0) % blk)

    def body(n, carry):
        r0 = pl.multiple_of(n * blk, blk)
        cos = cos_ref[pl.ds(r0, blk), :]
        sin = sin_ref[pl.ds(r0, blk), :]
        qs = [_rope(q_ref[pl.ds(CTX + r0, blk), gi * HD:(gi + 1) * HD], cos, sin) for gi in range(g)]
        qn = jnp.concatenate(qs, axis=0).astype(BF16)
        j0 = jnp.clip(n - 1, 0, SWA_NBLK - SWA_WIN // blk)
        ws = pl.multiple_of(j0 * blk, blk)
        kw_t = jnp.concatenate([kt_ref[j0 + i] for i in range(SWA_WIN // blk)], axis=1)
        vw = vb_ref[pl.ds(CTX + ws, SWA_WIN), :]
        s_win = jnp.dot(qn, kw_t, preferred_element_type=F32) * scale
        valid = jnp.abs(delta + (ws - r0)) <= SWA_WINDOW
        s_win = jnp.where(valid, s_win, NEG_INF)
        s_ctx = jnp.dot(qn, kc_t, preferred_element_type=F32) * scale
        m = jnp.maximum(row_max(s_win, s_ctx), sk_b)
        e_win = jnp.exp(s_win - m)
        e_ctx = jnp.exp(s_ctx - m)
        den = row_sum(e_win, e_ctx) + jnp.exp(sk_b - m)
        o = (jnp.dot(e_win.astype(BF16), vw, preferred_element_type=F32)
             + jnp.dot(e_ctx.astype(BF16), vc, preferred_element_type=F32)) / den
        for gi in range(g):
            o_ref[pl.ds(CTX + r0, blk), gi * HD:(gi + 1) * HD] = o[gi * blk:(gi + 1) * blk].astype(o_ref.dtype)
        return carry

    lax.fori_loop(0, SWA_NBLK, body, 0)


def swa_mixer(p_swa, sink_l, cos_full, sin_signed):
    gw = SWA_GROUP * HD
    rows = p_swa.shape[0]
    return pl.pallas_call(
        _swa_kernel,
        grid=(rows // TB, SWA_KV_HEADS),
        in_specs=[
            pl.BlockSpec(memory_space=pltpu.SMEM),
            pl.BlockSpec((TB, gw), lambda b, kv: (b, kv)),
            pl.BlockSpec((TB, HD), lambda b, kv: (b, SWA_Q_HEADS + kv)),
            pl.BlockSpec((TB, HD), lambda b, kv: (b, SWA_Q_HEADS + SWA_KV_HEADS + kv)),
            pl.BlockSpec((SEQ, HD), lambda b, kv: (0, 0)),
            pl.BlockSpec((SEQ, HD), lambda b, kv: (0, 0)),
        ],
        out_specs=pl.BlockSpec((TB, gw), lambda b, kv: (b, kv)),
        out_shape=jax.ShapeDtypeStruct((rows, W), BF16),
        scratch_shapes=[pltpu.VMEM((SWA_NBLK, HD, SWA_BLOCK), BF16), pltpu.VMEM((TB, HD), BF16)],
        compiler_params=_cparams(("parallel", "parallel")),
        name="swa",
    )(sink_l, p_swa, p_swa, p_swa, cos_full, sin_signed)


def rope_tables():
    rows = SEQ // GRID_W
    row = jnp.repeat(jnp.arange(rows), GRID_W).astype(F32)
    col = jnp.tile(jnp.arange(GRID_W), rows).astype(F32)
    n_freq = HD // 4
    inv_freq = ROPE_BASE ** (-jnp.arange(n_freq, dtype=F32) / n_freq)
    ang_r = row[:, None] * inv_freq
    ang_c = col[:, None] * inv_freq
    cr, sr, cc, sc = jnp.cos(ang_r), jnp.sin(ang_r), jnp.cos(ang_c), jnp.sin(ang_c)
    cos_full = jnp.concatenate([cr, cr, cc, cc], axis=-1)
    sin_signed = jnp.concatenate([-sr, sr, -sc, sc], axis=-1)
    return cos_full, sin_signed


def _shift_kernel(x_ref, mu_ref, o_ref):
    x = x_ref[...]
    row = lax.broadcasted_iota(jnp.int32, (TB, 1), 0)
    prev = pltpu.roll(x, 1, 0)
    prev = jnp.where(jnp.logical_or(row == 0, row == CTX), 0.0, prev)
    nxt = pltpu.roll(x, TB - 1, 0)
    nxt = jnp.where(jnp.logical_or(row == CTX - 1, row == TB - 1), 0.0, nxt)
    o_ref[...] = x + mu_ref[0:1, :] * (prev - x) + mu_ref[1:2, :] * (nxt - x)


def token_shift(p, mu, tf):
    f = p.shape[1]
    return pl.pallas_call(
        _shift_kernel,
        grid=(p.shape[0] // TB, f // tf),
        in_specs=[pl.BlockSpec((TB, tf), lambda b, j: (b, j)), pl.BlockSpec((2, tf), lambda b, j: (0, j))],
        out_specs=pl.BlockSpec((TB, tf), lambda b, j: (b, j)),
        out_shape=jax.ShapeDtypeStruct(p.shape, F32),
        compiler_params=_cparams(("parallel", "parallel")),
        name="token_shift",
    )(p, mu)


RW_NCHUNK = TB // RW_C
RW_CTX_CHUNKS = CTX // RW_C
RW_GPC = RW_GW // RW_C


def _rw_chunk(direction, s):
    back = jnp.where(s < RW_CTX_CHUNKS, RW_CTX_CHUNKS - 1 - s, RW_NCHUNK - 1 + RW_CTX_CHUNKS - s)
    return jnp.where(direction == 0, s, back)


def _head_ones():
    r = lax.broadcasted_iota(jnp.int32, (RW_GW, RW_GW), 0) // RW_N
    c = lax.broadcasted_iota(jnp.int32, (RW_GW, RW_GW), 1) // RW_N
    return r == c


def _dot16(a, b):
    return jnp.dot(a.astype(BF16), b.astype(BF16), preferred_element_type=F32)


def _dot16_nt(a, b):
    return lax.dot_general(a.astype(BF16), b.astype(BF16), (((1,), (1,)), ((), ())), preferred_element_type=F32)


def _split_bf16(x, terms):
    parts = []
    for _ in range(terms):
        p = x.astype(BF16)
        parts.append(p)
        x = x - p.astype(F32)
    return parts


def _dot_exact_lhs(x, b16, terms):
    return jnp.dot(jnp.concatenate(_split_bf16(x, terms), axis=1), jnp.concatenate([b16] * terms, axis=0),
                   preferred_element_type=F32)


def _rwkv_kernel(r_ref, k_ref, v_ref, sm_ref, wup_ref, aup_ref, w0_ref, a0_ref, kk_ref, ka_ref, rk_ref, gup_ref,
                 lng_ref, o_ref, q_s, y1_s, m_s, z_s, bonus_s, gate_s, st_ref, ybuf):
    direction = pl.program_id(1)
    step = pl.program_id(2)
    cur = step % 2
    prev = 1 - cur
    c = RW_C
    same_head = _head_ones()
    mask16 = jnp.where(same_head, 1.0, 0.0).astype(BF16)
    nbat = r_ref.shape[0]
    chains = [(bb, g) for bb in range(nbat) for g in range(RW_GROUPS)]
    groups = range(len(chains))
    bbs = [bb for bb, _ in chains]
    sls = [slice(g * RW_GW, (g + 1) * RW_GW) for _, g in chains]

    @pl.when(step == 0)
    def _():
        st_ref[...] = jnp.zeros_like(st_ref)
        for ref in (q_s, y1_s, m_s, z_s, bonus_s, gate_s):
            ref[1] = jnp.zeros(ref.shape[1:], ref.dtype)

    ys = []
    for g in groups:
        m_bd = jnp.concatenate([m_s[prev, bbs[g], :, sls[g]]] * RW_GPC, axis=0) * mask16
        z_bd = jnp.where(same_head, jnp.concatenate([z_s[prev, bbs[g], :, sls[g]]] * RW_GPC, axis=0), 0.0)
        res = _dot16(jnp.concatenate([q_s[prev, bbs[g], :, sls[g]], m_bd], axis=0), st_ref[g])
        ys.append(res[:c] + y1_s[prev, bbs[g], :, sls[g]])
        st_ref[g] = res[c:] + z_bd

    def bd(x):
        return jnp.concatenate([x.astype(BF16)] * RW_GPC, axis=0) * mask16

    sgn = 1 - 2 * direction
    cum_terms = 3
    tt = lax.broadcasted_iota(jnp.int32, (c, cum_terms * c), 0)
    ss = lax.broadcasted_iota(jnp.int32, (c, cum_terms * c), 1) % c
    tri16 = jnp.where((tt - ss) * sgn >= 0, 1.0, 0.0).astype(BF16)
    t4 = lax.broadcasted_iota(jnp.int32, (c, RW_GW), 0)
    s4 = lax.broadcasted_iota(jnp.int32, (c, RW_GW), 1) % c
    d4 = (t4 - s4) * sgn
    strict = d4 > 0
    incl = d4 >= 0
    eye = jnp.where(d4 == 0, 1.0, 0.0)

    sm = [sm_ref[bb] for bb in range(nbat)]
    tanh_wd = [jnp.tanh(x[:, :2 * RW_RANK]).astype(BF16) for x in sm]
    ad = [x[:, 2 * RW_RANK:4 * RW_RANK].astype(BF16) for x in sm]

    r = [r_ref[bbs[g], :, sls[g]] for g in groups]
    k = [k_ref[bbs[g], :, sls[g]] for g in groups]
    v = [v_ref[bbs[g], :, sls[g]] for g in groups]
    logw = [-jax.nn.sigmoid(w0_ref[0, :, sls[g]] + jnp.dot(tanh_wd[bbs[g]], wup_ref[0, :, sls[g]],
                                                          preferred_element_type=F32)) * math.exp(-0.5)
            for g in groups]

    def alpha_of(d, g):
        return jax.nn.sigmoid(a0_ref[d, :, sls[g]] + jnp.dot(ad[bbs[g]], aup_ref[d, :, sls[g]],
                                                             preferred_element_type=F32))

    alpha = [alpha_of(direction, g) for g in groups]
    alpha_both = [alpha[g] + alpha_of(1 - direction, g) for g in groups]
    kkp = [k[g] * kk_ref[:, sls[g]] for g in groups]
    kdir = [k[g] * (1.0 + (alpha[g] - 1.0) * ka_ref[:, sls[g]]) for g in groups]
    kboth = [k[g] * (2.0 + (alpha_both[g] - 2.0) * ka_ref[:, sls[g]]) for g in groups]
    nch = len(chains)
    sums = _dot_exact_lhs(jnp.concatenate([kkp[g] * kkp[g] for g in groups]
                                          + [r[g] * kboth[g] * rk_ref[:, sls[g]] for g in groups], axis=0), mask16, 2)
    kk = [kkp[g] / jnp.maximum(jnp.sqrt(sums[g * c:(g + 1) * c]), 1e-12) for g in groups]
    for g in groups:
        bonus_s[cur, bbs[g], :, sls[g]] = sums[(nch + g) * c:(nch + g + 1) * c] * v[g]
    for bb in range(nbat):
        gate_s[cur, bb] = jnp.dot(jax.nn.sigmoid(sm[bb][:, 4 * RW_RANK:]).astype(BF16), gup_ref[...],
                                  preferred_element_type=F32)
    b_vec = [kk[g] * alpha[g] for g in groups]

    cum = [jnp.dot(tri16, jnp.concatenate(_split_bf16(logw[g], cum_terms), axis=0), preferred_element_type=F32)
           for g in groups]
    total = [jnp.sum(logw[g], axis=0, keepdims=True) for g in groups]
    e_out = [jnp.exp(-cum[g]) for g in groups]
    e_left = [jnp.exp(total[g] - cum[g]) for g in groups]
    a_t = [-kk[g] * jnp.exp(cum[g] - logw[g]) for g in groups]
    r_t = [r[g] * jnp.exp(cum[g]) for g in groups]
    lhs = [jnp.concatenate([a_t[g], r_t[g]], axis=0) for g in groups]
    gb = [_dot16_nt(lhs[g], bd(b_vec[g] * e_out[g])) for g in groups]
    gk = [_dot16_nt(lhs[g], bd(kdir[g] * e_out[g])) for g in groups]
    a_ab = [jnp.where(strict, gb[g][:c], 0.0) for g in groups]
    a_ak = [jnp.where(strict, gk[g][:c], 0.0) for g in groups]
    a_rb = [jnp.where(incl, gb[g][c:], 0.0) for g in groups]
    a_rk = [jnp.where(incl, gk[g][c:], 0.0) for g in groups]

    pt = jnp.where(direction == 0, t4, c - 1 - t4)
    ps = jnp.where(direction == 0, s4, c - 1 - s4)
    px = pt ^ ps

    def level_mask(lv):
        return (px >> lv) * 2 + ((pt >> lv) & 1) == 3

    tinv = [eye + jnp.where(level_mask(0), a_ab[g], 0.0) for g in groups]
    for lv in range(1, int(math.log2(c))):
        lm = level_mask(lv)
        cross = [_dot16(jnp.where(lm, a_ab[g], 0.0), bd(tinv[g])) for g in groups]
        tinv = [tinv[g] + _dot16(tinv[g], bd(cross[g])) for g in groups]

    bdv = [bd(v[g]) for g in groups]
    avv = [_dot16(a_ak[g], bdv[g]) for g in groups]
    pw = [_dot16(tinv[g], jnp.concatenate([bd(a_t[g]), bd(avv[g])], axis=1)) for g in groups]
    eye16 = eye.astype(BF16)
    blt = [_dot16_nt(eye16, bd(b_vec[g] * e_left[g])) for g in groups]
    klt = [_dot16_nt(eye16, bd(kdir[g] * e_left[g])) for g in groups]
    top = [_dot16(jnp.concatenate([a_rb[g], blt[g]], axis=0),
                  jnp.concatenate([bd(pw[g][:, :RW_GW]), bd(pw[g][:, RW_GW:])], axis=1)) for g in groups]
    low = [_dot16(jnp.concatenate([a_rk[g], klt[g]], axis=0), bdv[g]) for g in groups]
    for g in groups:
        sl = sls[g]
        q_s[cur, bbs[g], :, sl] = (r_t[g] + top[g][:c, :RW_GW]).astype(q_s.dtype)
        y1_s[cur, bbs[g], :, sl] = top[g][:c, RW_GW:] + low[g][:c]
        m_s[cur, bbs[g], :, sl] = (eye * jnp.exp(total[g]) + top[g][c:, :RW_GW]).astype(m_s.dtype)
        z_s[cur, bbs[g], :, sl] = top[g][c:, RW_GW:] + low[g][c:]

    chunk_prev = _rw_chunk(direction, jnp.maximum(step - 1, 0))

    @pl.when(direction == 0)
    def _():
        for g in groups:
            ybuf[bbs[g], chunk_prev, :, sls[g]] = ys[g]

    @pl.when(direction == 1)
    def _():
        y4 = jnp.concatenate([ys[g] + ybuf[bbs[g], chunk_prev, :, sls[g]] for g in groups], axis=0)
        mean = _dot_exact_lhs(y4, mask16, 2) * (1.0 / RW_N)
        yc = y4 - mean
        var = _dot_exact_lhs(yc * yc, mask16, 2) * (1.0 / RW_N)
        yn = yc * lax.rsqrt(var + RW_GN_EPS)
        for g in groups:
            sl = sls[g]
            o_ref[bbs[g], :, sl] = ((yn[g * c:(g + 1) * c] * lng_ref[:, sl] + bonus_s[prev, bbs[g], :, sl])
                                    * gate_s[prev, bbs[g], :, sl]).astype(o_ref.dtype)


RW_BPS = 2


def rwkv_mixer(ps_rkv, ps_small, wup2, aup2, w0, a0, k_k, k_a, r_k, g_up, ln_g):
    rows = ps_rkv.shape[0]
    nb = rows // TB
    nsteps = RW_NCHUNK + 1

    def in_blk(d, s):
        return _rw_chunk(d, jnp.minimum(s, RW_NCHUNK - 1))

    def out_blk(d, s):
        return _rw_chunk(1, jnp.where(d == 0, 0, jnp.maximum(s - 1, 0)))

    feat = lambda off: pl.BlockSpec((RW_BPS, RW_C, W), lambda b, d, s, off=off: (b, in_blk(d, s), off))
    vec = pl.BlockSpec((1, W), lambda b, d, s: (0, 0))
    whole = lambda shape: pl.BlockSpec(shape, lambda b, d, s: (0,) * len(shape))
    slot = lambda dt: pltpu.VMEM((2, RW_BPS, RW_C, W), dt)
    out = pl.pallas_call(
        _rwkv_kernel,
        grid=(nb // RW_BPS, 2, nsteps),
        in_specs=[feat(0), feat(1), feat(2),
                  pl.BlockSpec((RW_BPS, RW_C, RW_SMALL), lambda b, d, s: (b, in_blk(d, s), 0)),
                  pl.BlockSpec((1, 2 * RW_RANK, W), lambda b, d, s: (d, 0, 0)), whole((2, 2 * RW_RANK, W)),
                  pl.BlockSpec((1, 1, W), lambda b, d, s: (d, 0, 0)), whole((2, 1, W)), vec, vec, vec,
                  whole((RW_G_RANK, W)), vec],
        out_specs=pl.BlockSpec((RW_BPS, RW_C, W), lambda b, d, s: (b, out_blk(d, s), 0)),
        out_shape=jax.ShapeDtypeStruct((nb, TB, W), BF16),
        scratch_shapes=[slot(BF16), slot(F32), slot(BF16), slot(F32), slot(F32), slot(F32),
                        pltpu.VMEM((RW_BPS * RW_GROUPS, RW_GW, RW_GW), F32),
                        pltpu.VMEM((RW_BPS, RW_NCHUNK, RW_C, W), F32)],
        compiler_params=_cparams(("arbitrary", "arbitrary", "arbitrary")),
        name="rwkv_mixer",
    )(*[x.reshape(nb, TB, -1) for x in (ps_rkv, ps_rkv, ps_rkv, ps_small)], wup2, aup2, w0, a0, k_k, k_a, r_k,
      g_up, ln_g)
    return out.reshape(rows, W)


MG_TM = 768
MG_TN = 512


def _merge_kernel(y0_ref, y1_ref, y2_ref, wb_ref, g0_ref, g1_ref, g2_ref, o_ref, wc_ref):
    @pl.when(pl.program_id(1) == 0)
    def _():
        wc_ref[...] = wb_ref[...].astype(BF16)

    acc = None
    for y_ref, g_ref, n in ((y0_ref, g0_ref, 0), (y1_ref, g1_ref, 1), (y2_ref, g2_ref, 2)):
        proj = jnp.dot(y_ref[...], wc_ref[n], preferred_element_type=F32)
        term = jax.nn.sigmoid(g_ref[...]) * proj
        acc = term if acc is None else acc + term
    o_ref[...] = acc.astype(o_ref.dtype)


def merge_branches(y_ret, y_swa, y_rwkv, p_gate, w_branch, layer):
    nj = D // MG_TN
    ysp = pl.BlockSpec((MG_TM, W), lambda j, i: (i, 0))
    gsp = lambda n: pl.BlockSpec((MG_TM, MG_TN), lambda j, i, n=n: (i, n * nj + j))
    return pl.pallas_call(
        _merge_kernel,
        grid=(nj, R // MG_TM),
        in_specs=[ysp, ysp, ysp, pl.BlockSpec((None, 3, W, MG_TN), lambda j, i: (layer, 0, 0, j)),
                  gsp(0), gsp(1), gsp(2)],
        out_specs=pl.BlockSpec((MG_TM, MG_TN), lambda j, i: (i, j)),
        out_shape=jax.ShapeDtypeStruct((R, D), BF16),
        scratch_shapes=[pltpu.VMEM((3, W, MG_TN), BF16)],
        compiler_params=_cparams(("arbitrary", "arbitrary")),
        name="merge",
    )(y_ret, y_swa, y_rwkv, w_branch, p_gate, p_gate, p_gate)


def kernel(x, c, ctx, c_ctx, norm1_g, norm2_g, w_mod, b_mod, w_in, ret_decay, swa_sink, rwkv_mu, rwkv_w0,
           rwkv_w_up, rwkv_a0, rwkv_a_up, rwkv_g_up, rwkv_k_k, rwkv_k_a, rwkv_r_k, rwkv_ln_g, w_branch,
           w_out, w_ff1, w_ff2, final_g):
    h = jnp.concatenate([ctx, x], axis=1).reshape(R, D)
    cond8 = jnp.concatenate([c, c_ctx[None, :], jnp.zeros((8 - B - 1, D), F32)], axis=0)
    mods = mod_vectors(cond8, w_mod, b_mod)
    mods = mods.reshape(L, 8, 6, D).transpose(0, 2, 1, 3)
    cos_full, sin_signed = rope_tables()

    o_swa = RET_IN
    o_rw = RET_IN + SWA_IN
    o_small = o_rw + 3 * W
    o_gate = o_rw + RWKV_IN
    zeros_lora = jnp.zeros((L, RW_RANK, W), BF16)

    for l in range(L):
        m_l = mods[l]
        u = norm_modulate(h, norm1_g[l], m_l, 0, 1)
        p_ret = matmul_f32w(u, w_in, l, 0, RET_IN, tn=1024, name="in_ret")
        p_swa = matmul_f32w(u, w_in, l, o_swa, SWA_IN, tn=768, name="in_swa")
        p_rkv = matmul_f32w(u, w_in, l, o_rw, 3 * W, tn=1024, name="in_rkv")
        p_small = matmul_f32w(u, w_in, l, o_small, RW_SMALL, tn=RW_SMALL, name="in_small")
        p_gate = matmul_f32w(u, w_in, l, o_gate, GATE_IN, tn=1024, name="in_gate")

        y_ret = retention_mixer(p_ret, ret_decay[l])
        y_swa = swa_mixer(p_swa, swa_sink[l], cos_full, sin_signed)

        mu = rwkv_mu[l]
        ps_rkv = token_shift(p_rkv, mu[:, :3 * W], 512)
        ps_small = token_shift(p_small, mu[:, 3 * W:], RW_SMALL)
        wup = rwkv_w_up[l].astype(BF16)
        aup = rwkv_a_up[l].astype(BF16)
        z = zeros_lora[l]
        wup2 = jnp.stack([jnp.concatenate([wup[0], z], axis=0), jnp.concatenate([z, wup[1]], axis=0)])
        aup2 = jnp.stack([jnp.concatenate([aup[0], z], axis=0), jnp.concatenate([z, aup[1]], axis=0)])
        y_rwkv = rwkv_mixer(
            ps_rkv, ps_small, wup2, aup2, rwkv_w0[l].reshape(2, 1, W), rwkv_a0[l].reshape(2, 1, W),
            rwkv_k_k[l].reshape(1, W), rwkv_k_a[l].reshape(1, W), rwkv_r_k[l].reshape(1, W),
            rwkv_g_up[l].astype(BF16), rwkv_ln_g[l].reshape(1, W))

        merged = merge_branches(y_ret, y_swa, y_rwkv, p_gate, w_branch, l)
        h = matmul_f32w(merged, w_out, l, 0, D, tn=1024, epilogue="residual", h=h, mods_l=m_l,
                        which_gate=2, name="out_proj")
        u2 = norm_modulate(h, norm2_g[l], m_l, 3, 4)
        f = matmul_f32w(u2, w_ff1, l, 0, D_FF, tn=1024, out_dtype=BF16, epilogue="relu2", name="ff1")
        h = matmul(f, w_ff2[l].astype(BF16), tn=1024, tk=2048, epilogue="residual", h=h, mods_l=m_l,
                   which_gate=5, name="ff2")

    return final_norm(h, final_g).reshape(B, SEQ, D)
```

```python
import functools
import math

import jax
import jax.numpy as jnp
from jax import lax
from jax.experimental import pallas as pl
from jax.experimental.pallas import tpu as pltpu

F32 = jnp.float32
BF16 = jnp.bfloat16

D = 2048
B = 4
SEQ = 2048
CTX = 256
TB = CTX + SEQ
R = B * TB
L = 4
GRID_W = 64
EPS = 1e-6
ROPE_BASE = 10000.0
NEG_INF = -1e30
W = D // 2
HD = 128
RET_HEADS = W // HD
RET_CHUNK = 128
RET_GN_EPS = 1e-5
SWA_Q_HEADS = W // HD
SWA_KV_HEADS = SWA_Q_HEADS // 4
SWA_GROUP = SWA_Q_HEADS // SWA_KV_HEADS
SWA_WINDOW = 128
SWA_BLOCK = 128
RW_N = 64
RW_HEADS = W // RW_N
RW_RANK = 64
RW_G_RANK = 128
RW_GN_EPS = 64e-5
RW_C = 64
RW_GW = 256
RW_GROUPS = W // RW_GW
RW_SMALL = 4 * RW_RANK + RW_G_RANK
D_FF = 4 * D
RET_IN = 4 * W
SWA_IN = (SWA_Q_HEADS + 2 * SWA_KV_HEADS) * HD
RWKV_IN = 3 * W + RW_SMALL
GATE_IN = 3 * D
N_IN = RET_IN + SWA_IN + RWKV_IN + GATE_IN

VMEM_LIMIT = 56 * 1024 * 1024


def _cparams(sem):
    return pltpu.CompilerParams(dimension_semantics=sem, vmem_limit_bytes=VMEM_LIMIT)


def _mod_kernel(x_ref, w_ref, b_ref, o_ref):
    x = x_ref[...]
    s = x * jax.nn.sigmoid(x)
    acc = jnp.dot(s.astype(BF16), w_ref[0].astype(BF16), preferred_element_type=F32)
    o_ref[0] = acc + b_ref[0]


def mod_vectors(cond8, w_mod, b_mod):
    tn = 1024
    n = w_mod.shape[-1]
    return pl.pallas_call(
        _mod_kernel,
        grid=(L, n // tn),
        in_specs=[
            pl.BlockSpec((8, D), lambda l, j: (0, 0)),
            pl.BlockSpec((1, D, tn), lambda l, j: (l, 0, j)),
            pl.BlockSpec((1, 1, tn), lambda l, j: (l, 0, j)),
        ],
        out_specs=pl.BlockSpec((1, 8, tn), lambda l, j: (l, 0, j)),
        out_shape=jax.ShapeDtypeStruct((L, 8, n), F32),
        compiler_params=_cparams(("parallel", "parallel")),
        name="mod_vectors",
    )(cond8, w_mod, b_mod.reshape(L, 1, n))


NORM_TM = 256
TILES_PER_BATCH = TB // NORM_TM


def _mod_row(i):
    return jnp.where(i % TILES_PER_BATCH == 0, B, i // TILES_PER_BATCH)


def _norm_mod_kernel(x_ref, g_ref, sh_ref, sc_ref, o_ref):
    row = _mod_row(pl.program_id(0))
    x = x_ref[...]
    ms = jnp.mean(x * x, axis=-1, keepdims=True)
    y = x * lax.rsqrt(ms + EPS) * g_ref[...]
    shift = sh_ref[0, pl.ds(row, 1), :]
    scale = sc_ref[0, pl.ds(row, 1), :]
    o_ref[...] = (y * (1.0 + scale) + shift).astype(o_ref.dtype)


def norm_modulate(h, g, mods_l, which_shift, which_scale):
    return pl.pallas_call(
        _norm_mod_kernel,
        grid=(R // NORM_TM,),
        in_specs=[
            pl.BlockSpec((NORM_TM, D), lambda i: (i, 0)),
            pl.BlockSpec((1, D), lambda i: (0, 0)),
            pl.BlockSpec((1, 8, D), lambda i: (which_shift, 0, 0)),
            pl.BlockSpec((1, 8, D), lambda i: (which_scale, 0, 0)),
        ],
        out_specs=pl.BlockSpec((NORM_TM, D), lambda i: (i, 0)),
        out_shape=jax.ShapeDtypeStruct((R, D), BF16),
        compiler_params=_cparams(("parallel",)),
        name="norm_modulate",
    )(h, g.reshape(1, D), mods_l, mods_l)


def _final_norm_kernel(x_ref, g_ref, o_ref):
    x = x_ref[...]
    ms = jnp.mean(x * x, axis=-1, keepdims=True)
    o_ref[...] = x * lax.rsqrt(ms + EPS) * g_ref[...]


def final_norm(h, g):
    per = SEQ // NORM_TM
    return pl.pallas_call(
        _final_norm_kernel,
        grid=(B, per),
        in_specs=[
            pl.BlockSpec((NORM_TM, D), lambda b, j: (b * TILES_PER_BATCH + CTX // NORM_TM + j, 0)),
            pl.BlockSpec((1, D), lambda b, j: (0, 0)),
        ],
        out_specs=pl.BlockSpec((NORM_TM, D), lambda b, j: (b * per + j, 0)),
        out_shape=jax.ShapeDtypeStruct((B * SEQ, D), F32),
        compiler_params=_cparams(("parallel", "parallel")),
        name="final_norm",
    )(h, g.reshape(1, D))


MM_TM = 1152


def _row_is_ctx(i, tm):
    rows = lax.broadcasted_iota(jnp.int32, (tm, 1), 0)
    return jnp.logical_and(i % (TB // tm) == 0, rows < CTX)


def _mm_finish(acc, i, o_ref, h_ref, gate_ref, epilogue, tm):
    if epilogue == "relu2":
        a = jnp.maximum(acc, 0.0)
        o_ref[...] = (a * a).astype(o_ref.dtype)
    elif epilogue == "residual":
        bidx = i // (TB // tm)
        g_b = gate_ref[0, pl.ds(bidx, 1), :]
        g_c = gate_ref[0, pl.ds(B, 1), :]
        gate = jnp.where(_row_is_ctx(i, tm), g_c, g_b)
        o_ref[...] = h_ref[...] + gate * acc
    elif epilogue == "token_shift":
        o_ref[...] = _token_shift(acc, gate_ref)
    else:
        o_ref[...] = acc.astype(o_ref.dtype)


def _token_shift(x, mu_ref):
    row = lax.broadcasted_iota(jnp.int32, (TB, 1), 0)
    prev = pltpu.roll(x, 1, 0)
    prev = jnp.where(jnp.logical_or(row == 0, row == CTX), 0.0, prev)
    nxt = pltpu.roll(x, TB - 1, 0)
    nxt = jnp.where(jnp.logical_or(row == CTX - 1, row == TB - 1), 0.0, nxt)
    return x + mu_ref[0:1, :] * (prev - x) + mu_ref[1:2, :] * (nxt - x)


def _mm_cached_kernel(*refs, epilogue, tm):
    if epilogue == "residual":
        x_ref, w_ref, h_ref, gate_ref, o_ref, wc_ref = refs
    elif epilogue == "token_shift":
        x_ref, w_ref, gate_ref, o_ref, wc_ref = refs
        h_ref = None
    else:
        x_ref, w_ref, o_ref, wc_ref = refs
        h_ref = gate_ref = None
    i = pl.program_id(1)

    @pl.when(i == 0)
    def _():
        wc_ref[...] = w_ref[...].astype(BF16)

    acc = jnp.dot(x_ref[...], wc_ref[...], preferred_element_type=F32)
    _mm_finish(acc, i, o_ref, h_ref, gate_ref, epilogue, tm)


def matmul_f32w(x, w_stack, layer, col0, n, *, tn, tm=MM_TM, out_dtype=F32, epilogue="plain", h=None, mods_l=None,
                which_gate=None, mu=None, name="matmul"):
    m, kdim = x.shape
    in_specs = [
        pl.BlockSpec((tm, kdim), lambda j, i: (i, 0)),
        pl.BlockSpec((pl.Element(kdim), pl.Element(tn)), lambda j, i: (layer * kdim, pl.multiple_of(col0 + j * tn, 128))),
    ]
    args = [x, w_stack.reshape(-1, w_stack.shape[-1])]
    aliases = {}
    if epilogue == "residual":
        in_specs += [
            pl.BlockSpec((tm, tn), lambda j, i: (i, j)),
            pl.BlockSpec((1, 8, tn), lambda j, i: (which_gate, 0, j)),
        ]
        args += [h, mods_l]
        aliases = {2: 0}
    elif epilogue == "token_shift":
        assert tm == TB
        in_specs.append(pl.BlockSpec((2, tn), lambda j, i: (0, j)))
        args.append(mu)
    return pl.pallas_call(
        functools.partial(_mm_cached_kernel, epilogue=epilogue, tm=tm),
        grid=(n // tn, m // tm),
        in_specs=in_specs,
        out_specs=pl.BlockSpec((tm, tn), lambda j, i: (i, j)),
        out_shape=jax.ShapeDtypeStruct((m, n), out_dtype),
        scratch_shapes=[pltpu.VMEM((kdim, tn), BF16)],
        input_output_aliases=aliases,
        compiler_params=_cparams(("arbitrary", "arbitrary")),
        name=name,
    )(*args)


def _mm_kernel(*refs, nk, epilogue, tm):
    if epilogue == "residual":
        x_ref, w_ref, h_ref, gate_ref, o_ref = refs[:5]
        rest = refs[5:]
    else:
        x_ref, w_ref, o_ref = refs[:3]
        rest = refs[3:]
        h_ref = gate_ref = None
    k = pl.program_id(2)
    part = jnp.dot(x_ref[...], w_ref[...], preferred_element_type=F32)

    def finish(acc):
        _mm_finish(acc, pl.program_id(0), o_ref, h_ref, gate_ref, epilogue, tm)

    if nk == 1:
        finish(part)
    else:
        acc_ref = rest[0]

        @pl.when(k == 0)
        def _():
            acc_ref[...] = part

        @pl.when(jnp.logical_and(k > 0, k < nk - 1))
        def _():
            acc_ref[...] += part

        @pl.when(k == nk - 1)
        def _():
            finish(acc_ref[...] + part)


def matmul(x, w, *, tn, tk=None, out_dtype=F32, epilogue="plain", h=None, mods_l=None, which_gate=None,
           name="matmul"):
    m, kdim = x.shape
    n = w.shape[1]
    tm = MM_TM
    tk = kdim if tk is None else tk
    nk = kdim // tk
    in_specs = [
        pl.BlockSpec((tm, tk), lambda i, j, k: (i, k)),
        pl.BlockSpec((tk, tn), lambda i, j, k: (k, j)),
    ]
    args = [x, w]
    aliases = {}
    if epilogue == "residual":
        in_specs += [
            pl.BlockSpec((tm, tn), lambda i, j, k: (i, j)),
            pl.BlockSpec((1, 8, tn), lambda i, j, k: (which_gate, 0, j)),
        ]
        args += [h, mods_l]
        aliases = {2: 0}
    scratch = [pltpu.VMEM((tm, tn), F32)] if nk > 1 else []
    return pl.pallas_call(
        functools.partial(_mm_kernel, nk=nk, epilogue=epilogue, tm=tm),
        grid=(m // tm, n // tn, nk),
        in_specs=in_specs,
        out_specs=pl.BlockSpec((tm, tn), lambda i, j, k: (i, j)),
        out_shape=jax.ShapeDtypeStruct((m, n), out_dtype),
        scratch_shapes=scratch,
        input_output_aliases=aliases,
        compiler_params=_cparams(("parallel", "parallel", "arbitrary")),
        name=name,
    )(*args)


RET_NCHUNK = TB // RET_CHUNK
RET_CTX_CHUNKS = CTX // RET_CHUNK


def _log_sigmoid(x):
    return jnp.minimum(x, 0.0) - jnp.log(1.0 + jnp.exp(-jnp.abs(x)))


RET_BATCH = 6


def _ret_kernel(dec_ref, q_ref, k_ref, v_ref, g_ref, o_ref, y_ref):
    hh = pl.program_id(1)
    c = RET_CHUNK
    ri = lax.broadcasted_iota(jnp.int32, (c, c), 0).astype(F32)
    ci = lax.broadcasted_iota(jnp.int32, (c, c), 1).astype(F32)
    kscale = HD ** -0.5
    nt = (((1,), (1,)), ((), ()))
    for direction in range(2):
        lg = _log_sigmoid(jnp.full((c, c), dec_ref[direction, hh], F32))
        diff = (ri - ci) if direction == 0 else (ci - ri)
        intra = jnp.where(diff >= 0, jnp.exp(lg * jnp.maximum(diff, 0.0)), 0.0)
        pos = ri if direction == 0 else (c - 1.0) - ri
        q_decay = jnp.exp(lg * (pos + 1.0))
        k_decay = jnp.exp(lg * ((c - 1.0) - pos))
        chunk_decay = jnp.exp(lg * float(c))
        if direction == 0:
            order = list(range(RET_NCHUNK))
        else:
            order = list(range(RET_CTX_CHUNKS - 1, -1, -1)) + list(range(RET_NCHUNK - 1, RET_CTX_CHUNKS - 1, -1))
        state = jnp.zeros((c, c), F32)
        for b0 in range(0, RET_NCHUNK, RET_BATCH):
            rows = [slice(ch * c, (ch + 1) * c) for ch in order[b0:b0 + RET_BATCH]]
            n = range(len(rows))
            q = [q_ref[r, :] for r in rows]
            k = [k_ref[r, :] * kscale for r in rows]
            v = [v_ref[r, :].astype(BF16) for r in rows]
            sc = [lax.dot_general(q[i].astype(BF16), k[i].astype(BF16), nt, preferred_element_type=F32) * intra
                  for i in n]
            y_in = [jnp.dot(sc[i].astype(BF16), v[i], preferred_element_type=F32) for i in n]
            kv = [jnp.dot((k[i] * k_decay).T.astype(BF16), v[i], preferred_element_type=F32) for i in n]
            states = []
            for i in n:
                states.append(state)
                state = state * chunk_decay + kv[i]
            y_x = [jnp.dot((q[i] * q_decay).astype(BF16), states[i].astype(BF16), preferred_element_type=F32)
                   for i in n]
            for i in n:
                y = y_in[i] + y_x[i]
                if direction == 0:
                    y_ref[rows[i], :] = y
                else:
                    y = y + y_ref[rows[i], :]
                    mean = jnp.mean(y, axis=-1, keepdims=True)
                    yc = y - mean
                    var = jnp.mean(yc * yc, axis=-1, keepdims=True)
                    g = g_ref[rows[i], :]
                    o_ref[rows[i], :] = (g * jax.nn.sigmoid(g) * (yc * lax.rsqrt(var + RET_GN_EPS))).astype(o_ref.dtype)


def retention_mixer(p_ret, decay_l):
    nh = RET_HEADS
    blk = lambda off: pl.BlockSpec((TB, HD), lambda b, h, off=off: (b, off + h))
    rows = p_ret.shape[0]
    return pl.pallas_call(
        _ret_kernel,
        grid=(rows // TB, nh),
        in_specs=[pl.BlockSpec(memory_space=pltpu.SMEM), blk(0), blk(nh), blk(2 * nh), blk(3 * nh)],
        out_specs=pl.BlockSpec((TB, HD), lambda b, h: (b, h)),
        out_shape=jax.ShapeDtypeStruct((rows, W), BF16),
        scratch_shapes=[pltpu.VMEM((TB, HD), F32)],
        compiler_params=_cparams(("parallel", "parallel")),
        name="retention",
    )(decay_l, p_ret, p_ret, p_ret, p_ret)


SWA_NBLK = SEQ // SWA_BLOCK
SWA_WIN = 3 * SWA_BLOCK


def _rope(x, cos, sin_signed):
    lane = lax.broadcasted_iota(jnp.int32, x.shape, 1)
    partner = jnp.where(lane % 64 < 32, pltpu.roll(x, 96, 1), pltpu.roll(x, 32, 1))
    return x * cos + partner * sin_signed


def _swa_kernel(sink_ref, q_ref, k_ref, v_ref, cos_ref, sin_ref, o_ref, kt_ref, vb_ref):
    kvh = pl.program_id(1)
    scale = HD ** -0.5
    g = SWA_GROUP
    blk = SWA_BLOCK

    vb_ref[...] = v_ref[...].astype(BF16)
    for j in range(SWA_NBLK):
        rows = slice(j * blk, (j + 1) * blk)
        k_rot = _rope(k_ref[CTX + j * blk:CTX + (j + 1) * blk, :], cos_ref[rows, :], sin_ref[rows, :])
        kt_ref[j] = k_rot.T.astype(BF16)
    kc_t = jnp.concatenate([k_ref[j * blk:(j + 1) * blk, :].T for j in range(CTX // blk)], axis=1).astype(BF16)
    vc = vb_ref[:CTX, :]

    def sink_col(rows_per_head):
        parts = [jnp.full((rows_per_head, 1), sink_ref[kvh * g + gi], F32) for gi in range(g)]
        return jnp.concatenate(parts, axis=0)

    def lane_tiles(x):
        return [x[:, j * HD:(j + 1) * HD] for j in range(x.shape[1] // HD)]

    def row_max(*xs):
        tiles = [t for x in xs for t in lane_tiles(x)]
        return jnp.max(functools.reduce(jnp.maximum, tiles), axis=-1, keepdims=True)

    def row_sum(*xs):
        tiles = [t for x in xs for t in lane_tiles(x)]
        return jnp.sum(functools.reduce(jnp.add, tiles), axis=-1, keepdims=True)

    qc = jnp.concatenate([q_ref[:CTX, gi * HD:(gi + 1) * HD] for gi in range(g)], axis=0).astype(BF16)
    s = jnp.dot(qc, kc_t, preferred_element_type=F32) * scale
    sk = sink_col(CTX)
    m = jnp.maximum(row_max(s), sk)
    e = jnp.exp(s - m)
    den = row_sum(e) + jnp.exp(sk - m)
    oc = jnp.dot(e.astype(BF16), vc, preferred_element_type=F32) / den
    for gi in range(g):
        o_ref[:CTX, gi * HD:(gi + 1) * HD] = oc[gi * CTX:(gi + 1) * CTX].astype(o_ref.dtype)

    sk_b = sink_col(blk)
    delta = (lax.broadcasted_iota(jnp.int32, (g * blk, SWA_WIN), 1)
             - lax.broadcasted_iota(jnp.int32, (g * blk, SWA_WIN), 0) % blk)

    def body(n, carry):
        r0 = pl.multiple_of(n * blk, blk)
        cos = cos_ref[pl.ds(r0, blk), :]
        sin = sin_ref[pl.ds(r0, blk), :]
        qs = [_rope(q_ref[pl.ds(CTX + r0, blk), gi * HD:(gi + 1) * HD], cos, sin) for gi in range(g)]
        qn = jnp.concatenate(qs, axis=0).astype(BF16)
        j0 = jnp.clip(n - 1, 0, SWA_NBLK - SWA_WIN // blk)
        ws = pl.multiple_of(j0 * blk, blk)
        kw_t = jnp.concatenate([kt_ref[j0 + i] for i in range(SWA_WIN // blk)], axis=1)
        vw = vb_ref[pl.ds(CTX + ws, SWA_WIN), :]
        s_win = jnp.dot(qn, kw_t, preferred_element_type=F32) * scale
        valid = jnp.abs(delta + (ws - r0)) <= SWA_WINDOW
        s_win = jnp.where(valid, s_win, NEG_INF)
        s_ctx = jnp.dot(qn, kc_t, preferred_element_type=F32) * scale
        m = jnp.maximum(row_max(s_win, s_ctx), sk_b)
        e_win = jnp.exp(s_win - m)
        e_ctx = jnp.exp(s_ctx - m)
        den = row_sum(e_win, e_ctx) + jnp.exp(sk_b - m)
        o = (jnp.dot(e_win.astype(BF16), vw, preferred_element_type=F32)
             + jnp.dot(e_ctx.astype(BF16), vc, preferred_element_type=F32)) / den
        for gi in range(g):
            o_ref[pl.ds(CTX + r0, blk), gi * HD:(gi + 1) * HD] = o[gi * blk:(gi + 1) * blk].astype(o_ref.dtype)
        return carry

    lax.fori_loop(0, SWA_NBLK, body, 0, unroll=2)


def swa_mixer(p_swa, sink_l, cos_full, sin_signed):
    gw = SWA_GROUP * HD
    rows = p_swa.shape[0]
    return pl.pallas_call(
        _swa_kernel,
        grid=(rows // TB, SWA_KV_HEADS),
        in_specs=[
            pl.BlockSpec(memory_space=pltpu.SMEM),
            pl.BlockSpec((TB, gw), lambda b, kv: (b, kv)),
            pl.BlockSpec((TB, HD), lambda b, kv: (b, SWA_Q_HEADS + kv)),
            pl.BlockSpec((TB, HD), lambda b, kv: (b, SWA_Q_HEADS + SWA_KV_HEADS + kv)),
            pl.BlockSpec((SEQ, HD), lambda b, kv: (0, 0)),
            pl.BlockSpec((SEQ, HD), lambda b, kv: (0, 0)),
        ],
        out_specs=pl.BlockSpec((TB, gw), lambda b, kv: (b, kv)),
        out_shape=jax.ShapeDtypeStruct((rows, W), BF16),
        scratch_shapes=[pltpu.VMEM((SWA_NBLK, HD, SWA_BLOCK), BF16), pltpu.VMEM((TB, HD), BF16)],
        compiler_params=_cparams(("parallel", "parallel")),
        name="swa",
    )(sink_l, p_swa, p_swa, p_swa, cos_full, sin_signed)


def rope_tables():
    rows = SEQ // GRID_W
    row = jnp.repeat(jnp.arange(rows), GRID_W).astype(F32)
    col = jnp.tile(jnp.arange(GRID_W), rows).astype(F32)
    n_freq = HD // 4
    inv_freq = ROPE_BASE ** (-jnp.arange(n_freq, dtype=F32) / n_freq)
    ang_r = row[:, None] * inv_freq
    ang_c = col[:, None] * inv_freq
    cr, sr, cc, sc = jnp.cos(ang_r), jnp.sin(ang_r), jnp.cos(ang_c), jnp.sin(ang_c)
    cos_full = jnp.concatenate([cr, cr, cc, cc], axis=-1)
    sin_signed = jnp.concatenate([-sr, sr, -sc, sc], axis=-1)
    return cos_full, sin_signed


RW_NCHUNK = TB // RW_C
RW_CTX_CHUNKS = CTX // RW_C
RW_GPC = RW_GW // RW_C


def _rw_chunk(direction, s):
    back = jnp.where(s < RW_CTX_CHUNKS, RW_CTX_CHUNKS - 1 - s, RW_NCHUNK - 1 + RW_CTX_CHUNKS - s)
    return jnp.where(direction == 0, s, back)


def _head_ones():
    r = lax.broadcasted_iota(jnp.int32, (RW_GW, RW_GW), 0) // RW_N
    c = lax.broadcasted_iota(jnp.int32, (RW_GW, RW_GW), 1) // RW_N
    return r == c


def _dot16(a, b):
    return jnp.dot(a.astype(BF16), b.astype(BF16), preferred_element_type=F32)


def _dot16_nt(a, b):
    return lax.dot_general(a.astype(BF16), b.astype(BF16), (((1,), (1,)), ((), ())), preferred_element_type=F32)


def _split_bf16(x, terms):
    parts = []
    for _ in range(terms):
        p = x.astype(BF16)
        parts.append(p)
        x = x - p.astype(F32)
    return parts


def _dot_exact_lhs(x, b16, terms):
    return jnp.dot(jnp.concatenate(_split_bf16(x, terms), axis=1), jnp.concatenate([b16] * terms, axis=0),
                   preferred_element_type=F32)


def _rwkv_kernel(r_ref, k_ref, v_ref, sm_ref, wup_ref, aup_ref, w0_ref, a0_ref, kk_ref, ka_ref, rk_ref, gup_ref,
                 lng_ref, o_ref, q_s, y1_s, m_s, z_s, bonus_s, gate_s, st_ref, ybuf):
    direction = pl.program_id(1)
    step = pl.program_id(2)
    cur = step % 2
    prev = 1 - cur
    c = RW_C
    same_head = _head_ones()
    mask16 = jnp.where(same_head, 1.0, 0.0).astype(BF16)
    nbat = r_ref.shape[0]
    chains = [(bb, g) for bb in range(nbat) for g in range(RW_GROUPS)]
    groups = range(len(chains))
    bbs = [bb for bb, _ in chains]
    sls = [slice(g * RW_GW, (g + 1) * RW_GW) for _, g in chains]

    @pl.when(step == 0)
    def _():
        st_ref[...] = jnp.zeros_like(st_ref)
        for ref in (q_s, y1_s, m_s, z_s, bonus_s, gate_s):
            ref[1] = jnp.zeros(ref.shape[1:], ref.dtype)

    ys = []
    for g in groups:
        m_bd = jnp.concatenate([m_s[prev, bbs[g], :, sls[g]]] * RW_GPC, axis=0) * mask16
        z_bd = jnp.where(same_head, jnp.concatenate([z_s[prev, bbs[g], :, sls[g]]] * RW_GPC, axis=0), 0.0)
        res = _dot16(jnp.concatenate([q_s[prev, bbs[g], :, sls[g]], m_bd], axis=0), st_ref[g])
        ys.append(res[:c] + y1_s[prev, bbs[g], :, sls[g]])
        st_ref[g] = res[c:] + z_bd

    def bd(x):
        return jnp.concatenate([x.astype(BF16)] * RW_GPC, axis=0) * mask16

    sgn = 1 - 2 * direction
    cum_terms = 3
    tt = lax.broadcasted_iota(jnp.int32, (c, cum_terms * c), 0)
    ss = lax.broadcasted_iota(jnp.int32, (c, cum_terms * c), 1) % c
    tri16 = jnp.where((tt - ss) * sgn >= 0, 1.0, 0.0).astype(BF16)
    t4 = lax.broadcasted_iota(jnp.int32, (c, RW_GW), 0)
    s4 = lax.broadcasted_iota(jnp.int32, (c, RW_GW), 1) % c
    d4 = (t4 - s4) * sgn
    strict = d4 > 0
    incl = d4 >= 0
    eye = jnp.where(d4 == 0, 1.0, 0.0)

    sm = [sm_ref[bb] for bb in range(nbat)]
    tanh_wd = [jnp.tanh(x[:, :2 * RW_RANK]).astype(BF16) for x in sm]
    ad = [x[:, 2 * RW_RANK:4 * RW_RANK].astype(BF16) for x in sm]

    r = [r_ref[bbs[g], :, sls[g]] for g in groups]
    k = [k_ref[bbs[g], :, sls[g]] for g in groups]
    v = [v_ref[bbs[g], :, sls[g]] for g in groups]
    logw = [-jax.nn.sigmoid(w0_ref[0, :, sls[g]] + jnp.dot(tanh_wd[bbs[g]], wup_ref[0, :, sls[g]],
                                                          preferred_element_type=F32)) * math.exp(-0.5)
            for g in groups]

    def alpha_of(d, g):
        return jax.nn.sigmoid(a0_ref[d, :, sls[g]] + jnp.dot(ad[bbs[g]], aup_ref[d, :, sls[g]],
                                                             preferred_element_type=F32))

    alpha = [alpha_of(direction, g) for g in groups]
    alpha_both = [alpha[g] + alpha_of(1 - direction, g) for g in groups]
    kkp = [k[g] * kk_ref[:, sls[g]] for g in groups]
    kdir = [k[g] * (1.0 + (alpha[g] - 1.0) * ka_ref[:, sls[g]]) for g in groups]
    kboth = [k[g] * (2.0 + (alpha_both[g] - 2.0) * ka_ref[:, sls[g]]) for g in groups]
    nch = len(chains)
    sums = _dot_exact_lhs(jnp.concatenate([kkp[g] * kkp[g] for g in groups]
                                          + [r[g] * kboth[g] * rk_ref[:, sls[g]] for g in groups], axis=0), mask16, 2)
    kk = [kkp[g] / jnp.maximum(jnp.sqrt(sums[g * c:(g + 1) * c]), 1e-12) for g in groups]
    for g in groups:
        bonus_s[cur, bbs[g], :, sls[g]] = sums[(nch + g) * c:(nch + g + 1) * c] * v[g]
    for bb in range(nbat):
        gate_s[cur, bb] = jnp.dot(jax.nn.sigmoid(sm[bb][:, 4 * RW_RANK:]).astype(BF16), gup_ref[...],
                                  preferred_element_type=F32)
    b_vec = [kk[g] * alpha[g] for g in groups]

    cum = [jnp.dot(tri16, jnp.concatenate(_split_bf16(logw[g], cum_terms), axis=0), preferred_element_type=F32)
           for g in groups]
    total = [jnp.sum(logw[g], axis=0, keepdims=True) for g in groups]
    e_out = [jnp.exp(-cum[g]) for g in groups]
    e_left = [jnp.exp(total[g] - cum[g]) for g in groups]
    a_t = [-kk[g] * jnp.exp(cum[g] - logw[g]) for g in groups]
    r_t = [r[g] * jnp.exp(cum[g]) for g in groups]
    lhs = [jnp.concatenate([a_t[g], r_t[g]], axis=0) for g in groups]
    gb = [_dot16_nt(lhs[g], bd(b_vec[g] * e_out[g])) for g in groups]
    gk = [_dot16_nt(lhs[g], bd(kdir[g] * e_out[g])) for g in groups]
    a_ab = [jnp.where(strict, gb[g][:c], 0.0) for g in groups]
    a_ak = [jnp.where(strict, gk[g][:c], 0.0) for g in groups]
    a_rb = [jnp.where(incl, gb[g][c:], 0.0) for g in groups]
    a_rk = [jnp.where(incl, gk[g][c:], 0.0) for g in groups]

    pt = jnp.where(direction == 0, t4, c - 1 - t4)
    ps = jnp.where(direction == 0, s4, c - 1 - s4)
    px = pt ^ ps

    def level_mask(lv):
        return (px >> lv) * 2 + ((pt >> lv) & 1) == 3

    tinv = [eye + jnp.where(level_mask(0), a_ab[g], 0.0) for g in groups]
    for lv in range(1, int(math.log2(c))):
        lm = level_mask(lv)
        cross = [_dot16(jnp.where(lm, a_ab[g], 0.0), bd(tinv[g])) for g in groups]
        tinv = [tinv[g] + _dot16(tinv[g], bd(cross[g])) for g in groups]

    eye16 = eye.astype(BF16)
    blt = [_dot16_nt(eye16, bd(b_vec[g] * e_left[g])) for g in groups]
    klt = [_dot16_nt(eye16, bd(kdir[g] * e_left[g])) for g in groups]
    vprod = [_dot16(jnp.concatenate([a_ak[g], a_rk[g], klt[g]], axis=0), bd(v[g])) for g in groups]
    pw = [_dot16(tinv[g], jnp.concatenate([bd(a_t[g]), bd(vprod[g][:c])], axis=1)) for g in groups]
    top = [_dot16(jnp.concatenate([a_rb[g], blt[g]], axis=0),
                  jnp.concatenate([bd(pw[g][:, :RW_GW]), bd(pw[g][:, RW_GW:])], axis=1)) for g in groups]
    low = [vprod[g][c:] for g in groups]
    for g in groups:
        sl = sls[g]
        q_s[cur, bbs[g], :, sl] = (r_t[g] + top[g][:c, :RW_GW]).astype(q_s.dtype)
        y1_s[cur, bbs[g], :, sl] = top[g][:c, RW_GW:] + low[g][:c]
        m_s[cur, bbs[g], :, sl] = (eye * jnp.exp(total[g]) + top[g][c:, :RW_GW]).astype(m_s.dtype)
        z_s[cur, bbs[g], :, sl] = top[g][c:, RW_GW:] + low[g][c:]

    chunk_prev = _rw_chunk(direction, jnp.maximum(step - 1, 0))

    @pl.when(direction == 0)
    def _():
        for g in groups:
            ybuf[bbs[g], chunk_prev, :, sls[g]] = ys[g].astype(ybuf.dtype)

    @pl.when(direction == 1)
    def _():
        y4 = jnp.concatenate([ys[g] + ybuf[bbs[g], chunk_prev, :, sls[g]] for g in groups], axis=0)
        mean = _dot_exact_lhs(y4, mask16, 2) * (1.0 / RW_N)
        yc = y4 - mean
        var = _dot_exact_lhs(yc * yc, mask16, 2) * (1.0 / RW_N)
        yn = yc * lax.rsqrt(var + RW_GN_EPS)
        for g in groups:
            sl = sls[g]
            o_ref[bbs[g], :, sl] = ((yn[g * c:(g + 1) * c] * lng_ref[:, sl] + bonus_s[prev, bbs[g], :, sl])
                                    * gate_s[prev, bbs[g], :, sl]).astype(o_ref.dtype)


RW_BPS = 4


def rwkv_mixer(ps_rkv, ps_small, wup2, aup2, w0, a0, k_k, k_a, r_k, g_up, ln_g):
    rows = ps_rkv.shape[0]
    nb = rows // TB
    nsteps = RW_NCHUNK + 1

    def in_blk(d, s):
        return _rw_chunk(d, jnp.minimum(s, RW_NCHUNK - 1))

    def out_blk(d, s):
        return _rw_chunk(1, jnp.where(d == 0, 0, jnp.maximum(s - 1, 0)))

    feat = lambda off: pl.BlockSpec((RW_BPS, RW_C, W), lambda b, d, s, off=off: (b, in_blk(d, s), off))
    vec = pl.BlockSpec((1, W), lambda b, d, s: (0, 0))
    whole = lambda shape: pl.BlockSpec(shape, lambda b, d, s: (0,) * len(shape))
    slot = lambda dt: pltpu.VMEM((2, RW_BPS, RW_C, W), dt)
    out = pl.pallas_call(
        _rwkv_kernel,
        grid=(nb // RW_BPS, 2, nsteps),
        in_specs=[feat(0), feat(1), feat(2),
                  pl.BlockSpec((RW_BPS, RW_C, RW_SMALL), lambda b, d, s: (b, in_blk(d, s), 0)),
                  pl.BlockSpec((1, 2 * RW_RANK, W), lambda b, d, s: (d, 0, 0)), whole((2, 2 * RW_RANK, W)),
                  pl.BlockSpec((1, 1, W), lambda b, d, s: (d, 0, 0)), whole((2, 1, W)), vec, vec, vec,
                  whole((RW_G_RANK, W)), vec],
        out_specs=pl.BlockSpec((RW_BPS, RW_C, W), lambda b, d, s: (b, out_blk(d, s), 0)),
        out_shape=jax.ShapeDtypeStruct((nb, TB, W), BF16),
        scratch_shapes=[slot(BF16), slot(F32), slot(BF16), slot(F32), slot(F32), slot(F32),
                        pltpu.VMEM((RW_BPS * RW_GROUPS, RW_GW, RW_GW), F32),
                        pltpu.VMEM((RW_BPS, RW_NCHUNK, RW_C, W), BF16)],
        compiler_params=_cparams(("arbitrary", "arbitrary", "arbitrary")),
        name="rwkv_mixer",
    )(*[x.reshape(nb, TB, -1) for x in (ps_rkv, ps_rkv, ps_rkv, ps_small)], wup2, aup2, w0, a0, k_k, k_a, r_k,
      g_up, ln_g)
    return out.reshape(rows, W)


MG_TM = 768
MG_TN = 512


def _merge_kernel(y0_ref, y1_ref, y2_ref, wb_ref, g0_ref, g1_ref, g2_ref, o_ref, wc_ref):
    @pl.when(pl.program_id(1) == 0)
    def _():
        wc_ref[...] = wb_ref[...].astype(BF16)

    acc = None
    for y_ref, g_ref, n in ((y0_ref, g0_ref, 0), (y1_ref, g1_ref, 1), (y2_ref, g2_ref, 2)):
        proj = jnp.dot(y_ref[...], wc_ref[n], preferred_element_type=F32)
        term = jax.nn.sigmoid(g_ref[...]) * proj
        acc = term if acc is None else acc + term
    o_ref[...] = acc.astype(o_ref.dtype)


def merge_branches(y_ret, y_swa, y_rwkv, p_gate, w_branch, layer):
    nj = D // MG_TN
    ysp = pl.BlockSpec((MG_TM, W), lambda j, i: (i, 0))
    gsp = lambda n: pl.BlockSpec((MG_TM, MG_TN), lambda j, i, n=n: (i, n * nj + j))
    return pl.pallas_call(
        _merge_kernel,
        grid=(nj, R // MG_TM),
        in_specs=[ysp, ysp, ysp, pl.BlockSpec((None, 3, W, MG_TN), lambda j, i: (layer, 0, 0, j)),
                  gsp(0), gsp(1), gsp(2)],
        out_specs=pl.BlockSpec((MG_TM, MG_TN), lambda j, i: (i, j)),
        out_shape=jax.ShapeDtypeStruct((R, D), BF16),
        scratch_shapes=[pltpu.VMEM((3, W, MG_TN), BF16)],
        compiler_params=_cparams(("arbitrary", "arbitrary")),
        name="merge",
    )(y_ret, y_swa, y_rwkv, w_branch, p_gate, p_gate, p_gate)


def kernel(x, c, ctx, c_ctx, norm1_g, norm2_g, w_mod, b_mod, w_in, ret_decay, swa_sink, rwkv_mu, rwkv_w0,
           rwkv_w_up, rwkv_a0, rwkv_a_up, rwkv_g_up, rwkv_k_k, rwkv_k_a, rwkv_r_k, rwkv_ln_g, w_branch,
           w_out, w_ff1, w_ff2, final_g):
    h = jnp.concatenate([ctx, x], axis=1).reshape(R, D)
    cond8 = jnp.concatenate([c, c_ctx[None, :], jnp.zeros((8 - B - 1, D), F32)], axis=0)
    mods = mod_vectors(cond8, w_mod, b_mod)
    mods = mods.reshape(L, 8, 6, D).transpose(0, 2, 1, 3)
    cos_full, sin_signed = rope_tables()

    o_swa = RET_IN
    o_rw = RET_IN + SWA_IN
    o_small = o_rw + 3 * W
    o_gate = o_rw + RWKV_IN
    zeros_lora = jnp.zeros((L, RW_RANK, W), BF16)

    for l in range(L):
        m_l = mods[l]
        u = norm_modulate(h, norm1_g[l], m_l, 0, 1)
        p_ret = matmul_f32w(u, w_in, l, 0, RET_IN, tn=1024, name="in_ret")
        p_swa = matmul_f32w(u, w_in, l, o_swa, SWA_IN, tn=768, name="in_swa")
        mu = rwkv_mu[l]
        ps_rkv = matmul_f32w(u, w_in, l, o_rw, 3 * W, tn=512, tm=TB, epilogue="token_shift", mu=mu[:, :3 * W],
                             name="in_rkv")
        ps_small = matmul_f32w(u, w_in, l, o_small, RW_SMALL, tn=RW_SMALL, tm=TB, epilogue="token_shift",
                               mu=mu[:, 3 * W:], name="in_small")
        p_gate = matmul_f32w(u, w_in, l, o_gate, GATE_IN, tn=1024, name="in_gate")

        y_ret = retention_mixer(p_ret, ret_decay[l])
        y_swa = swa_mixer(p_swa, swa_sink[l], cos_full, sin_signed)

        wup = rwkv_w_up[l].astype(BF16)
        aup = rwkv_a_up[l].astype(BF16)
        z = zeros_lora[l]
        wup2 = jnp.stack([jnp.concatenate([wup[0], z], axis=0), jnp.concatenate([z, wup[1]], axis=0)])
        aup2 = jnp.stack([jnp.concatenate([aup[0], z], axis=0), jnp.concatenate([z, aup[1]], axis=0)])
        y_rwkv = rwkv_mixer(
            ps_rkv, ps_small, wup2, aup2, rwkv_w0[l].reshape(2, 1, W), rwkv_a0[l].reshape(2, 1, W),
            rwkv_k_k[l].reshape(1, W), rwkv_k_a[l].reshape(1, W), rwkv_r_k[l].reshape(1, W),
            rwkv_g_up[l].astype(BF16), rwkv_ln_g[l].reshape(1, W))

        merged = merge_branches(y_ret, y_swa, y_rwkv, p_gate, w_branch, l)
        h = matmul_f32w(merged, w_out, l, 0, D, tn=1024, epilogue="residual", h=h, mods_l=m_l,
                        which_gate=2, name="out_proj")
        u2 = norm_modulate(h, norm2_g[l], m_l, 3, 4)
        f = matmul_f32w(u2, w_ff1, l, 0, D_FF, tn=1024, out_dtype=BF16, epilogue="relu2", name="ff1")
        h = matmul(f, w_ff2[l].astype(BF16), tn=1024, tk=2048, epilogue="residual", h=h, mods_l=m_l,
                   which_gate=5, name="ff2")

    return final_norm(h, final_g).reshape(B, SEQ, D)
```

```python
import functools
import math

import jax
import jax.numpy as jnp
from jax import lax
from jax.experimental import pallas as pl
from jax.experimental.pallas import tpu as pltpu

F32 = jnp.float32
BF16 = jnp.bfloat16

D = 2048
B = 4
SEQ = 2048
CTX = 256
TB = CTX + SEQ
R = B * TB
L = 4
GRID_W = 64
EPS = 1e-6
ROPE_BASE = 10000.0
NEG_INF = -1e30
W = D // 2
HD = 128
RET_HEADS = W // HD
RET_CHUNK = 128
RET_GN_EPS = 1e-5
SWA_Q_HEADS = W // HD
SWA_KV_HEADS = SWA_Q_HEADS // 4
SWA_GROUP = SWA_Q_HEADS // SWA_KV_HEADS
SWA_WINDOW = 128
SWA_BLOCK = 128
RW_N = 64
RW_HEADS = W // RW_N
RW_RANK = 64
RW_G_RANK = 128
RW_GN_EPS = 64e-5
RW_C = 64
RW_GW = 256
RW_GROUPS = W // RW_GW
RW_SMALL = 4 * RW_RANK + RW_G_RANK
D_FF = 4 * D
RET_IN = 4 * W
SWA_IN = (SWA_Q_HEADS + 2 * SWA_KV_HEADS) * HD
RWKV_IN = 3 * W + RW_SMALL
GATE_IN = 3 * D
N_IN = RET_IN + SWA_IN + RWKV_IN + GATE_IN

VMEM_LIMIT = 56 * 1024 * 1024


def _cparams(sem):
    return pltpu.CompilerParams(dimension_semantics=sem, vmem_limit_bytes=VMEM_LIMIT)


def _mod_kernel(x_ref, w_ref, b_ref, o_ref):
    x = x_ref[...]
    s = x * jax.nn.sigmoid(x)
    acc = jnp.dot(s.astype(BF16), w_ref[0].astype(BF16), preferred_element_type=F32)
    o_ref[0] = acc + b_ref[0]


def mod_vectors(cond8, w_mod, b_mod):
    tn = 1024
    n = w_mod.shape[-1]
    return pl.pallas_call(
        _mod_kernel,
        grid=(L, n // tn),
        in_specs=[
            pl.BlockSpec((8, D), lambda l, j: (0, 0)),
            pl.BlockSpec((1, D, tn), lambda l, j: (l, 0, j)),
            pl.BlockSpec((1, 1, tn), lambda l, j: (l, 0, j)),
        ],
        out_specs=pl.BlockSpec((1, 8, tn), lambda l, j: (l, 0, j)),
        out_shape=jax.ShapeDtypeStruct((L, 8, n), F32),
        compiler_params=_cparams(("parallel", "parallel")),
        name="mod_vectors",
    )(cond8, w_mod, b_mod.reshape(L, 1, n))


NORM_TM = 256
TILES_PER_BATCH = TB // NORM_TM


NORM_MOD_TM = 768


def _norm_mod_kernel(x_ref, g_ref, sh_ref, sc_ref, o_ref):
    i = pl.program_id(0)
    tm = NORM_MOD_TM
    bidx = i // (TB // tm)
    is_ctx = _row_is_ctx(i, tm)
    x = x_ref[...]
    ms = jnp.mean(x * x, axis=-1, keepdims=True)
    y = x * lax.rsqrt(ms + EPS) * g_ref[...]
    shift = jnp.where(is_ctx, sh_ref[0, B:B + 1, :], sh_ref[0, pl.ds(bidx, 1), :])
    scale = jnp.where(is_ctx, sc_ref[0, B:B + 1, :], sc_ref[0, pl.ds(bidx, 1), :])
    o_ref[...] = (y * (1.0 + scale) + shift).astype(o_ref.dtype)


def norm_modulate(h, g, mods_l, which_shift, which_scale):
    return pl.pallas_call(
        _norm_mod_kernel,
        grid=(R // NORM_MOD_TM,),
        in_specs=[
            pl.BlockSpec((NORM_MOD_TM, D), lambda i: (i, 0)),
            pl.BlockSpec((1, D), lambda i: (0, 0)),
            pl.BlockSpec((1, 8, D), lambda i: (which_shift, 0, 0)),
            pl.BlockSpec((1, 8, D), lambda i: (which_scale, 0, 0)),
        ],
        out_specs=pl.BlockSpec((NORM_MOD_TM, D), lambda i: (i, 0)),
        out_shape=jax.ShapeDtypeStruct((R, D), BF16),
        compiler_params=_cparams(("parallel",)),
        name="norm_modulate",
    )(h, g.reshape(1, D), mods_l, mods_l)


def _final_norm_kernel(x_ref, g_ref, o_ref):
    x = x_ref[...]
    ms = jnp.mean(x * x, axis=-1, keepdims=True)
    o_ref[...] = x * lax.rsqrt(ms + EPS) * g_ref[...]


def final_norm(h, g):
    per = SEQ // NORM_TM
    return pl.pallas_call(
        _final_norm_kernel,
        grid=(B, per),
        in_specs=[
            pl.BlockSpec((NORM_TM, D), lambda b, j: (b * TILES_PER_BATCH + CTX // NORM_TM + j, 0)),
            pl.BlockSpec((1, D), lambda b, j: (0, 0)),
        ],
        out_specs=pl.BlockSpec((NORM_TM, D), lambda b, j: (b * per + j, 0)),
        out_shape=jax.ShapeDtypeStruct((B * SEQ, D), F32),
        compiler_params=_cparams(("parallel", "parallel")),
        name="final_norm",
    )(h, g.reshape(1, D))


MM_TM = 1152


def _row_is_ctx(i, tm):
    rows = lax.broadcasted_iota(jnp.int32, (tm, 1), 0)
    return jnp.logical_and(i % (TB // tm) == 0, rows < CTX)


def _mm_finish(acc, i, o_ref, h_ref, gate_ref, epilogue, tm):
    if epilogue == "relu2":
        a = jnp.maximum(acc, 0.0)
        o_ref[...] = (a * a).astype(o_ref.dtype)
    elif epilogue == "residual":
        bidx = i // (TB // tm)
        g_b = gate_ref[0, pl.ds(bidx, 1), :]
        g_c = gate_ref[0, pl.ds(B, 1), :]
        gate = jnp.where(_row_is_ctx(i, tm), g_c, g_b)
        o_ref[...] = h_ref[...] + gate * acc
    elif epilogue == "token_shift":
        o_ref[...] = _token_shift(acc, gate_ref)
    else:
        o_ref[...] = acc.astype(o_ref.dtype)


def _token_shift(x, mu_ref):
    row = lax.broadcasted_iota(jnp.int32, (TB, 1), 0)
    prev = pltpu.roll(x, 1, 0)
    prev = jnp.where(jnp.logical_or(row == 0, row == CTX), 0.0, prev)
    nxt = pltpu.roll(x, TB - 1, 0)
    nxt = jnp.where(jnp.logical_or(row == CTX - 1, row == TB - 1), 0.0, nxt)
    return x + mu_ref[0:1, :] * (prev - x) + mu_ref[1:2, :] * (nxt - x)


def _mm_cached_kernel(*refs, epilogue, tm):
    if epilogue == "residual":
        x_ref, w_ref, h_ref, gate_ref, o_ref, wc_ref = refs
    elif epilogue == "token_shift":
        x_ref, w_ref, gate_ref, o_ref, wc_ref = refs
        h_ref = None
    else:
        x_ref, w_ref, o_ref, wc_ref = refs
        h_ref = gate_ref = None
    i = pl.program_id(1)

    @pl.when(i == 0)
    def _():
        wc_ref[...] = w_ref[...].astype(BF16)

    acc = jnp.dot(x_ref[...], wc_ref[...], preferred_element_type=F32)
    _mm_finish(acc, i, o_ref, h_ref, gate_ref, epilogue, tm)


def matmul_f32w(x, w_stack, layer, col0, n, *, tn, tm=MM_TM, out_dtype=F32, epilogue="plain", h=None, mods_l=None,
                which_gate=None, mu=None, name="matmul"):
    m, kdim = x.shape
    in_specs = [
        pl.BlockSpec((tm, kdim), lambda j, i: (i, 0)),
        pl.BlockSpec((pl.Element(kdim), pl.Element(tn)), lambda j, i: (layer * kdim, pl.multiple_of(col0 + j * tn, 128))),
    ]
    args = [x, w_stack.reshape(-1, w_stack.shape[-1])]
    aliases = {}
    if epilogue == "residual":
        in_specs += [
            pl.BlockSpec((tm, tn), lambda j, i: (i, j)),
            pl.BlockSpec((1, 8, tn), lambda j, i: (which_gate, 0, j)),
        ]
        args += [h, mods_l]
        aliases = {2: 0}
    elif epilogue == "token_shift":
        assert tm == TB
        in_specs.append(pl.BlockSpec((2, tn), lambda j, i: (0, j)))
        args.append(mu)
    return pl.pallas_call(
        functools.partial(_mm_cached_kernel, epilogue=epilogue, tm=tm),
        grid=(n // tn, m // tm),
        in_specs=in_specs,
        out_specs=pl.BlockSpec((tm, tn), lambda j, i: (i, j)),
        out_shape=jax.ShapeDtypeStruct((m, n), out_dtype),
        scratch_shapes=[pltpu.VMEM((kdim, tn), BF16)],
        input_output_aliases=aliases,
        compiler_params=_cparams(("arbitrary", "arbitrary")),
        name=name,
    )(*args)


def _mm_kernel(*refs, nk, epilogue, tm):
    if epilogue == "residual":
        x_ref, w_ref, h_ref, gate_ref, o_ref = refs[:5]
        rest = refs[5:]
    else:
        x_ref, w_ref, o_ref = refs[:3]
        rest = refs[3:]
        h_ref = gate_ref = None
    k = pl.program_id(2)
    part = jnp.dot(x_ref[...], w_ref[...], preferred_element_type=F32)

    def finish(acc):
        _mm_finish(acc, pl.program_id(0), o_ref, h_ref, gate_ref, epilogue, tm)

    if nk == 1:
        finish(part)
    else:
        acc_ref = rest[0]

        @pl.when(k == 0)
        def _():
            acc_ref[...] = part

        @pl.when(jnp.logical_and(k > 0, k < nk - 1))
        def _():
            acc_ref[...] += part

        @pl.when(k == nk - 1)
        def _():
            finish(acc_ref[...] + part)


def matmul(x, w, *, tn, tk=None, out_dtype=F32, epilogue="plain", h=None, mods_l=None, which_gate=None,
           name="matmul"):
    m, kdim = x.shape
    n = w.shape[1]
    tm = MM_TM
    tk = kdim if tk is None else tk
    nk = kdim // tk
    in_specs = [
        pl.BlockSpec((tm, tk), lambda i, j, k: (i, k)),
        pl.BlockSpec((tk, tn), lambda i, j, k: (k, j)),
    ]
    args = [x, w]
    aliases = {}
    if epilogue == "residual":
        in_specs += [
            pl.BlockSpec((tm, tn), lambda i, j, k: (i, j)),
            pl.BlockSpec((1, 8, tn), lambda i, j, k: (which_gate, 0, j)),
        ]
        args += [h, mods_l]
        aliases = {2: 0}
    scratch = [pltpu.VMEM((tm, tn), F32)] if nk > 1 else []
    return pl.pallas_call(
        functools.partial(_mm_kernel, nk=nk, epilogue=epilogue, tm=tm),
        grid=(m // tm, n // tn, nk),
        in_specs=in_specs,
        out_specs=pl.BlockSpec((tm, tn), lambda i, j, k: (i, j)),
        out_shape=jax.ShapeDtypeStruct((m, n), out_dtype),
        scratch_shapes=scratch,
        input_output_aliases=aliases,
        compiler_params=_cparams(("parallel", "parallel", "arbitrary")),
        name=name,
    )(*args)


RET_NCHUNK = TB // RET_CHUNK
RET_CTX_CHUNKS = CTX // RET_CHUNK


def _log_sigmoid(x):
    return jnp.minimum(x, 0.0) - jnp.log(1.0 + jnp.exp(-jnp.abs(x)))


RET_BATCH = 6


def _ret_kernel(dec_ref, q_ref, k_ref, v_ref, g_ref, o_ref, y_ref):
    hh = pl.program_id(1)
    c = RET_CHUNK
    ri = lax.broadcasted_iota(jnp.int32, (c, c), 0).astype(F32)
    ci = lax.broadcasted_iota(jnp.int32, (c, c), 1).astype(F32)
    kscale = HD ** -0.5
    nt = (((1,), (1,)), ((), ()))
    for direction in range(2):
        lg = _log_sigmoid(jnp.full((c, c), dec_ref[direction, hh], F32))
        diff = (ri - ci) if direction == 0 else (ci - ri)
        intra = jnp.where(diff >= 0, jnp.exp(lg * jnp.maximum(diff, 0.0)), 0.0)
        pos = ri if direction == 0 else (c - 1.0) - ri
        q_decay = jnp.exp(lg * (pos + 1.0))
        k_decay = jnp.exp(lg * ((c - 1.0) - pos))
        chunk_decay = jnp.exp(lg * float(c))
        if direction == 0:
            order = list(range(RET_NCHUNK))
        else:
            order = list(range(RET_CTX_CHUNKS - 1, -1, -1)) + list(range(RET_NCHUNK - 1, RET_CTX_CHUNKS - 1, -1))
        state = jnp.zeros((c, c), F32)
        for b0 in range(0, RET_NCHUNK, RET_BATCH):
            rows = [slice(ch * c, (ch + 1) * c) for ch in order[b0:b0 + RET_BATCH]]
            n = range(len(rows))
            q = [q_ref[r, :] for r in rows]
            k = [k_ref[r, :] * kscale for r in rows]
            v = [v_ref[r, :].astype(BF16) for r in rows]
            sc = [lax.dot_general(q[i].astype(BF16), k[i].astype(BF16), nt, preferred_element_type=F32) * intra
                  for i in n]
            y_in = [jnp.dot(sc[i].astype(BF16), v[i], preferred_element_type=F32) for i in n]
            kv = [jnp.dot((k[i] * k_decay).T.astype(BF16), v[i], preferred_element_type=F32) for i in n]
            states = []
            for i in n:
                states.append(state)
                state = state * chunk_decay + kv[i]
            y_x = [jnp.dot((q[i] * q_decay).astype(BF16), states[i].astype(BF16), preferred_element_type=F32)
                   for i in n]
            for i in n:
                y = y_in[i] + y_x[i]
                if direction == 0:
                    y_ref[rows[i], :] = y
                else:
                    y = y + y_ref[rows[i], :]
                    mean = jnp.mean(y, axis=-1, keepdims=True)
                    yc = y - mean
                    var = jnp.mean(yc * yc, axis=-1, keepdims=True)
                    g = g_ref[rows[i], :]
                    o_ref[rows[i], :] = (g * jax.nn.sigmoid(g) * (yc * lax.rsqrt(var + RET_GN_EPS))).astype(o_ref.dtype)


def retention_mixer(p_ret, decay_l):
    nh = RET_HEADS
    blk = lambda off: pl.BlockSpec((TB, HD), lambda b, h, off=off: (b, off + h))
    rows = p_ret.shape[0]
    return pl.pallas_call(
        _ret_kernel,
        grid=(rows // TB, nh),
        in_specs=[pl.BlockSpec(memory_space=pltpu.SMEM), blk(0), blk(nh), blk(2 * nh), blk(3 * nh)],
        out_specs=pl.BlockSpec((TB, HD), lambda b, h: (b, h)),
        out_shape=jax.ShapeDtypeStruct((rows, W), BF16),
        scratch_shapes=[pltpu.VMEM((TB, HD), F32)],
        compiler_params=_cparams(("parallel", "parallel")),
        name="retention",
    )(decay_l, p_ret, p_ret, p_ret, p_ret)


SWA_NBLK = SEQ // SWA_BLOCK
SWA_WIN = 3 * SWA_BLOCK
SWA_BATCH = 4


def _rope(x, cos, sin_signed):
    lane = lax.broadcasted_iota(jnp.int32, x.shape, 1)
    partner = jnp.where(lane % 64 < 32, pltpu.roll(x, 96, 1), pltpu.roll(x, 32, 1))
    return x * cos + partner * sin_signed


def _swa_kernel(sink_ref, q_ref, k_ref, v_ref, cos_ref, sin_ref, o_ref, kt_ref, vb_ref):
    kvh = pl.program_id(1)
    scale = HD ** -0.5
    g = SWA_GROUP
    blk = SWA_BLOCK

    vb_ref[...] = v_ref[...].astype(BF16)
    for j in range(SWA_NBLK):
        rows = slice(j * blk, (j + 1) * blk)
        k_rot = _rope(k_ref[CTX + j * blk:CTX + (j + 1) * blk, :], cos_ref[rows, :], sin_ref[rows, :])
        kt_ref[j] = k_rot.T.astype(BF16)
    kc_t = jnp.concatenate([k_ref[j * blk:(j + 1) * blk, :].T for j in range(CTX // blk)], axis=1).astype(BF16)
    vc = vb_ref[:CTX, :]

    def sink_col(rows_per_head):
        parts = [jnp.full((rows_per_head, 1), sink_ref[kvh * g + gi], F32) for gi in range(g)]
        return jnp.concatenate(parts, axis=0)

    def lane_tiles(x):
        return [x[:, j * HD:(j + 1) * HD] for j in range(x.shape[1] // HD)]

    def row_max(*xs):
        tiles = [t for x in xs for t in lane_tiles(x)]
        return jnp.max(functools.reduce(jnp.maximum, tiles), axis=-1, keepdims=True)

    def row_sum(*xs):
        tiles = [t for x in xs for t in lane_tiles(x)]
        return jnp.sum(functools.reduce(jnp.add, tiles), axis=-1, keepdims=True)

    qc = jnp.concatenate([q_ref[:CTX, gi * HD:(gi + 1) * HD] for gi in range(g)], axis=0).astype(BF16)
    s = jnp.dot(qc, kc_t, preferred_element_type=F32) * scale
    sk = sink_col(CTX)
    m = jnp.maximum(row_max(s), sk)
    e = jnp.exp(s - m)
    den = row_sum(e) + jnp.exp(sk - m)
    oc = jnp.dot(e.astype(BF16), vc, preferred_element_type=F32) / den
    for gi in range(g):
        o_ref[:CTX, gi * HD:(gi + 1) * HD] = oc[gi * CTX:(gi + 1) * CTX].astype(o_ref.dtype)

    sk_b = sink_col(blk)
    delta = (lax.broadcasted_iota(jnp.int32, (g * blk, SWA_WIN), 1)
             - lax.broadcasted_iota(jnp.int32, (g * blk, SWA_WIN), 0) % blk)

    for n0 in range(0, SWA_NBLK, SWA_BATCH):
        ns = list(range(n0, n0 + SWA_BATCH))
        idx = range(len(ns))
        r0 = [n * blk for n in ns]
        j0 = [min(max(n - 1, 0), SWA_NBLK - SWA_WIN // blk) for n in ns]
        qn = [jnp.concatenate([_rope(q_ref[CTX + r0[i]:CTX + r0[i] + blk, gi * HD:(gi + 1) * HD],
                                     cos_ref[r0[i]:r0[i] + blk, :], sin_ref[r0[i]:r0[i] + blk, :])
                               for gi in range(g)], axis=0).astype(BF16) for i in idx]
        kw_t = [jnp.concatenate([kt_ref[j0[i] + w] for w in range(SWA_WIN // blk)], axis=1) for i in idx]
        s_win = [jnp.dot(qn[i], kw_t[i], preferred_element_type=F32) * scale for i in idx]
        s_win = [jnp.where(jnp.abs(delta + (j0[i] * blk - r0[i])) <= SWA_WINDOW, s_win[i], NEG_INF) for i in idx]
        s_ctx = [jnp.dot(qn[i], kc_t, preferred_element_type=F32) * scale for i in idx]
        m = [jnp.maximum(row_max(s_win[i], s_ctx[i]), sk_b) for i in idx]
        e_win = [jnp.exp(s_win[i] - m[i]) for i in idx]
        e_ctx = [jnp.exp(s_ctx[i] - m[i]) for i in idx]
        den = [row_sum(e_win[i], e_ctx[i]) + jnp.exp(sk_b - m[i]) for i in idx]
        o = [(jnp.dot(e_win[i].astype(BF16), vb_ref[CTX + j0[i] * blk:CTX + j0[i] * blk + SWA_WIN, :],
                      preferred_element_type=F32)
              + jnp.dot(e_ctx[i].astype(BF16), vc, preferred_element_type=F32)) / den[i] for i in idx]
        for i in idx:
            for gi in range(g):
                o_ref[CTX + r0[i]:CTX + r0[i] + blk, gi * HD:(gi + 1) * HD] = (
                    o[i][gi * blk:(gi + 1) * blk].astype(o_ref.dtype))


def swa_mixer(p_swa, sink_l, cos_full, sin_signed):
    gw = SWA_GROUP * HD
    rows = p_swa.shape[0]
    return pl.pallas_call(
        _swa_kernel,
        grid=(rows // TB, SWA_KV_HEADS),
        in_specs=[
            pl.BlockSpec(memory_space=pltpu.SMEM),
            pl.BlockSpec((TB, gw), lambda b, kv: (b, kv)),
            pl.BlockSpec((TB, HD), lambda b, kv: (b, SWA_Q_HEADS + kv)),
            pl.BlockSpec((TB, HD), lambda b, kv: (b, SWA_Q_HEADS + SWA_KV_HEADS + kv)),
            pl.BlockSpec((SEQ, HD), lambda b, kv: (0, 0)),
            pl.BlockSpec((SEQ, HD), lambda b, kv: (0, 0)),
        ],
        out_specs=pl.BlockSpec((TB, gw), lambda b, kv: (b, kv)),
        out_shape=jax.ShapeDtypeStruct((rows, W), BF16),
        scratch_shapes=[pltpu.VMEM((SWA_NBLK, HD, SWA_BLOCK), BF16), pltpu.VMEM((TB, HD), BF16)],
        compiler_params=_cparams(("parallel", "parallel")),
        name="swa",
    )(sink_l, p_swa, p_swa, p_swa, cos_full, sin_signed)


def rope_tables():
    rows = SEQ // GRID_W
    row = jnp.repeat(jnp.arange(rows), GRID_W).astype(F32)
    col = jnp.tile(jnp.arange(GRID_W), rows).astype(F32)
    n_freq = HD // 4
    inv_freq = ROPE_BASE ** (-jnp.arange(n_freq, dtype=F32) / n_freq)
    ang_r = row[:, None] * inv_freq
    ang_c = col[:, None] * inv_freq
    cr, sr, cc, sc = jnp.cos(ang_r), jnp.sin(ang_r), jnp.cos(ang_c), jnp.sin(ang_c)
    cos_full = jnp.concatenate([cr, cr, cc, cc], axis=-1)
    sin_signed = jnp.concatenate([-sr, sr, -sc, sc], axis=-1)
    return cos_full, sin_signed


RW_NCHUNK = TB // RW_C
RW_CTX_CHUNKS = CTX // RW_C
RW_GPC = RW_GW // RW_C


def _rw_chunk(direction, s):
    back = jnp.where(s < RW_CTX_CHUNKS, RW_CTX_CHUNKS - 1 - s, RW_NCHUNK - 1 + RW_CTX_CHUNKS - s)
    return jnp.where(direction == 0, s, back)


def _head_ones():
    r = lax.broadcasted_iota(jnp.int32, (RW_GW, RW_GW), 0) // RW_N
    c = lax.broadcasted_iota(jnp.int32, (RW_GW, RW_GW), 1) // RW_N
    return r == c


def _dot16(a, b):
    return jnp.dot(a.astype(BF16), b.astype(BF16), preferred_element_type=F32)


def _dot16_nt(a, b):
    return lax.dot_general(a.astype(BF16), b.astype(BF16), (((1,), (1,)), ((), ())), preferred_element_type=F32)


def _split_bf16(x, terms):
    parts = []
    for _ in range(terms):
        p = x.astype(BF16)
        parts.append(p)
        x = x - p.astype(F32)
    return parts


def _dot_exact_lhs(x, b16, terms):
    return jnp.dot(jnp.concatenate(_split_bf16(x, terms), axis=1), jnp.concatenate([b16] * terms, axis=0),
                   preferred_element_type=F32)


def _rwkv_kernel(r_ref, k_ref, v_ref, sm_ref, wup_ref, aup_ref, w0_ref, a0_ref, kk_ref, ka_ref, rk_ref, gup_ref,
                 lng_ref, o_ref, q_s, y1_s, m_s, z_s, bonus_s, gate_s, st_ref, ybuf):
    direction = pl.program_id(1)
    step = pl.program_id(2)
    cur = step % 2
    prev = 1 - cur
    c = RW_C
    same_head = _head_ones()
    mask16 = jnp.where(same_head, 1.0, 0.0).astype(BF16)
    nbat = r_ref.shape[0]
    chains = [(bb, g) for bb in range(nbat) for g in range(RW_GROUPS)]
    groups = range(len(chains))
    bbs = [bb for bb, _ in chains]
    sls = [slice(g * RW_GW, (g + 1) * RW_GW) for _, g in chains]

    @pl.when(step == 0)
    def _():
        st_ref[...] = jnp.zeros_like(st_ref)
        for ref in (q_s, y1_s, m_s, z_s, bonus_s, gate_s):
            ref[1] = jnp.zeros(ref.shape[1:], ref.dtype)

    ys = []
    for g in groups:
        m_bd = jnp.concatenate([m_s[prev, bbs[g], :, sls[g]]] * RW_GPC, axis=0) * mask16
        z_bd = jnp.where(same_head, jnp.concatenate([z_s[prev, bbs[g], :, sls[g]]] * RW_GPC, axis=0), 0.0)
        res = _dot16(jnp.concatenate([q_s[prev, bbs[g], :, sls[g]], m_bd], axis=0), st_ref[g])
        ys.append(res[:c] + y1_s[prev, bbs[g], :, sls[g]])
        st_ref[g] = res[c:] + z_bd

    def bd(x):
        return jnp.concatenate([x.astype(BF16)] * RW_GPC, axis=0) * mask16

    sgn = 1 - 2 * direction
    cum_terms = 3
    tt = lax.broadcasted_iota(jnp.int32, (c, cum_terms * c), 0)
    ss = lax.broadcasted_iota(jnp.int32, (c, cum_terms * c), 1) % c
    tri16 = jnp.where((tt - ss) * sgn >= 0, 1.0, 0.0).astype(BF16)
    t4 = lax.broadcasted_iota(jnp.int32, (c, RW_GW), 0)
    s4 = lax.broadcasted_iota(jnp.int32, (c, RW_GW), 1) % c
    d4 = (t4 - s4) * sgn
    strict = d4 > 0
    incl = d4 >= 0
    eye = jnp.where(d4 == 0, 1.0, 0.0)

    sm = [sm_ref[bb] for bb in range(nbat)]
    tanh_wd = [jnp.tanh(x[:, :2 * RW_RANK]).astype(BF16) for x in sm]
    ad = [x[:, 2 * RW_RANK:4 * RW_RANK].astype(BF16) for x in sm]

    r = [r_ref[bbs[g], :, sls[g]] for g in groups]
    k = [k_ref[bbs[g], :, sls[g]] for g in groups]
    v = [v_ref[bbs[g], :, sls[g]] for g in groups]
    logw = [-jax.nn.sigmoid(w0_ref[0, :, sls[g]] + jnp.dot(tanh_wd[bbs[g]], wup_ref[0, :, sls[g]],
                                                          preferred_element_type=F32)) * math.exp(-0.5)
            for g in groups]

    def alpha_of(d, g):
        return jax.nn.sigmoid(a0_ref[d, :, sls[g]] + jnp.dot(ad[bbs[g]], aup_ref[d, :, sls[g]],
                                                             preferred_element_type=F32))

    alpha = [alpha_of(direction, g) for g in groups]
    alpha_both = [alpha[g] + alpha_of(1 - direction, g) for g in groups]
    kkp = [k[g] * kk_ref[:, sls[g]] for g in groups]
    kdir = [k[g] * (1.0 + (alpha[g] - 1.0) * ka_ref[:, sls[g]]) for g in groups]
    kboth = [k[g] * (2.0 + (alpha_both[g] - 2.0) * ka_ref[:, sls[g]]) for g in groups]
    nch = len(chains)
    sums = _dot_exact_lhs(jnp.concatenate([kkp[g] * kkp[g] for g in groups]
                                          + [r[g] * kboth[g] * rk_ref[:, sls[g]] for g in groups], axis=0), mask16, 2)
    kk = [kkp[g] / jnp.maximum(jnp.sqrt(sums[g * c:(g + 1) * c]), 1e-12) for g in groups]
    for g in groups:
        bonus_s[cur, bbs[g], :, sls[g]] = sums[(nch + g) * c:(nch + g + 1) * c] * v[g]
    for bb in range(nbat):
        gate_s[cur, bb] = jnp.dot(jax.nn.sigmoid(sm[bb][:, 4 * RW_RANK:]).astype(BF16), gup_ref[...],
                                  preferred_element_type=F32)
    b_vec = [kk[g] * alpha[g] for g in groups]

    cum = [jnp.dot(tri16, jnp.concatenate(_split_bf16(logw[g], cum_terms), axis=0), preferred_element_type=F32)
           for g in groups]
    total = [jnp.sum(logw[g], axis=0, keepdims=True) for g in groups]
    e_out = [jnp.exp(-cum[g]) for g in groups]
    e_left = [jnp.exp(total[g] - cum[g]) for g in groups]
    a_t = [-kk[g] * jnp.exp(cum[g] - logw[g]) for g in groups]
    r_t = [r[g] * jnp.exp(cum[g]) for g in groups]
    lhs = [jnp.concatenate([a_t[g], r_t[g]], axis=0) for g in groups]
    gb = [_dot16_nt(lhs[g], bd(b_vec[g] * e_out[g])) for g in groups]
    gk = [_dot16_nt(lhs[g], bd(kdir[g] * e_out[g])) for g in groups]
    a_ab = [jnp.where(strict, gb[g][:c], 0.0) for g in groups]
    a_ak = [jnp.where(strict, gk[g][:c], 0.0) for g in groups]
    a_rb = [jnp.where(incl, gb[g][c:], 0.0) for g in groups]
    a_rk = [jnp.where(incl, gk[g][c:], 0.0) for g in groups]

    pt = jnp.where(direction == 0, t4, c - 1 - t4)
    ps = jnp.where(direction == 0, s4, c - 1 - s4)
    px = pt ^ ps

    def level_mask(lv):
        return (px >> lv) * 2 + ((pt >> lv) & 1) == 3

    tinv = [eye + jnp.where(level_mask(0), a_ab[g], 0.0) for g in groups]
    for lv in range(1, int(math.log2(c))):
        lm = level_mask(lv)
        cross = [_dot16(jnp.where(lm, a_ab[g], 0.0), bd(tinv[g])) for g in groups]
        tinv = [tinv[g] + _dot16(tinv[g], bd(cross[g])) for g in groups]

    eye16 = eye.astype(BF16)
    blt = [_dot16_nt(eye16, bd(b_vec[g] * e_left[g])) for g in groups]
    klt = [_dot16_nt(eye16, bd(kdir[g] * e_left[g])) for g in groups]
    vprod = [_dot16(jnp.concatenate([a_ak[g], a_rk[g], klt[g]], axis=0), bd(v[g])) for g in groups]
    pw = [_dot16(tinv[g], jnp.concatenate([bd(a_t[g]), bd(vprod[g][:c])], axis=1)) for g in groups]
    top = [_dot16(jnp.concatenate([a_rb[g], blt[g]], axis=0),
                  jnp.concatenate([bd(pw[g][:, :RW_GW]), bd(pw[g][:, RW_GW:])], axis=1)) for g in groups]
    low = [vprod[g][c:] for g in groups]
    for g in groups:
        sl = sls[g]
        q_s[cur, bbs[g], :, sl] = (r_t[g] + top[g][:c, :RW_GW]).astype(q_s.dtype)
        y1_s[cur, bbs[g], :, sl] = top[g][:c, RW_GW:] + low[g][:c]
        m_s[cur, bbs[g], :, sl] = (eye * jnp.exp(total[g]) + top[g][c:, :RW_GW]).astype(m_s.dtype)
        z_s[cur, bbs[g], :, sl] = top[g][c:, RW_GW:] + low[g][c:]

    chunk_prev = _rw_chunk(direction, jnp.maximum(step - 1, 0))

    @pl.when(direction == 0)
    def _():
        for g in groups:
            ybuf[bbs[g], chunk_prev, :, sls[g]] = ys[g].astype(ybuf.dtype)

    @pl.when(direction == 1)
    def _():
        y4 = jnp.concatenate([ys[g] + ybuf[bbs[g], chunk_prev, :, sls[g]] for g in groups], axis=0)
        mean = _dot_exact_lhs(y4, mask16, 2) * (1.0 / RW_N)
        yc = y4 - mean
        var = _dot_exact_lhs(yc * yc, mask16, 2) * (1.0 / RW_N)
        yn = yc * lax.rsqrt(var + RW_GN_EPS)
        for g in groups:
            sl = sls[g]
            o_ref[bbs[g], :, sl] = ((yn[g * c:(g + 1) * c] * lng_ref[:, sl] + bonus_s[prev, bbs[g], :, sl])
                                    * gate_s[prev, bbs[g], :, sl]).astype(o_ref.dtype)


RW_BPS = 4


def rwkv_mixer(ps_rkv, ps_small, wup2, aup2, w0, a0, k_k, k_a, r_k, g_up, ln_g):
    rows = ps_rkv.shape[0]
    nb = rows // TB
    nsteps = RW_NCHUNK + 1

    def in_blk(d, s):
        return _rw_chunk(d, jnp.minimum(s, RW_NCHUNK - 1))

    def out_blk(d, s):
        return _rw_chunk(1, jnp.where(d == 0, 0, jnp.maximum(s - 1, 0)))

    feat = lambda off: pl.BlockSpec((RW_BPS, RW_C, W), lambda b, d, s, off=off: (b, in_blk(d, s), off))
    vec = pl.BlockSpec((1, W), lambda b, d, s: (0, 0))
    whole = lambda shape: pl.BlockSpec(shape, lambda b, d, s: (0,) * len(shape))
    slot = lambda dt: pltpu.VMEM((2, RW_BPS, RW_C, W), dt)
    out = pl.pallas_call(
        _rwkv_kernel,
        grid=(nb // RW_BPS, 2, nsteps),
        in_specs=[feat(0), feat(1), feat(2),
                  pl.BlockSpec((RW_BPS, RW_C, RW_SMALL), lambda b, d, s: (b, in_blk(d, s), 0)),
                  pl.BlockSpec((1, 2 * RW_RANK, W), lambda b, d, s: (d, 0, 0)), whole((2, 2 * RW_RANK, W)),
                  pl.BlockSpec((1, 1, W), lambda b, d, s: (d, 0, 0)), whole((2, 1, W)), vec, vec, vec,
                  whole((RW_G_RANK, W)), vec],
        out_specs=pl.BlockSpec((RW_BPS, RW_C, W), lambda b, d, s: (b, out_blk(d, s), 0)),
        out_shape=jax.ShapeDtypeStruct((nb, TB, W), BF16),
        scratch_shapes=[slot(BF16), slot(F32), slot(BF16), slot(F32), slot(F32), slot(F32),
                        pltpu.VMEM((RW_BPS * RW_GROUPS, RW_GW, RW_GW), F32),
                        pltpu.VMEM((RW_BPS, RW_NCHUNK, RW_C, W), BF16)],
        compiler_params=_cparams(("arbitrary", "arbitrary", "arbitrary")),
        name="rwkv_mixer",
    )(*[x.reshape(nb, TB, -1) for x in (ps_rkv, ps_rkv, ps_rkv, ps_small)], wup2, aup2, w0, a0, k_k, k_a, r_k,
      g_up, ln_g)
    return out.reshape(rows, W)


MG_TM = 768
MG_TN = 512


def _merge_kernel(y0_ref, y1_ref, y2_ref, wb_ref, g0_ref, g1_ref, g2_ref, o_ref, wc_ref):
    @pl.when(pl.program_id(1) == 0)
    def _():
        wc_ref[...] = wb_ref[...].astype(BF16)

    acc = None
    for y_ref, g_ref, n in ((y0_ref, g0_ref, 0), (y1_ref, g1_ref, 1), (y2_ref, g2_ref, 2)):
        proj = jnp.dot(y_ref[...], wc_ref[n], preferred_element_type=F32)
        term = jax.nn.sigmoid(g_ref[...]) * proj
        acc = term if acc is None else acc + term
    o_ref[...] = acc.astype(o_ref.dtype)


def merge_branches(y_ret, y_swa, y_rwkv, p_gate, w_branch, layer):
    nj = D // MG_TN
    ysp = pl.BlockSpec((MG_TM, W), lambda j, i: (i, 0))
    gsp = lambda n: pl.BlockSpec((MG_TM, MG_TN), lambda j, i, n=n: (i, n * nj + j))
    return pl.pallas_call(
        _merge_kernel,
        grid=(nj, R // MG_TM),
        in_specs=[ysp, ysp, ysp, pl.BlockSpec((None, 3, W, MG_TN), lambda j, i: (layer, 0, 0, j)),
                  gsp(0), gsp(1), gsp(2)],
        out_specs=pl.BlockSpec((MG_TM, MG_TN), lambda j, i: (i, j)),
        out_shape=jax.ShapeDtypeStruct((R, D), BF16),
        scratch_shapes=[pltpu.VMEM((3, W, MG_TN), BF16)],
        compiler_params=_cparams(("arbitrary", "arbitrary")),
        name="merge",
    )(y_ret, y_swa, y_rwkv, w_branch, p_gate, p_gate, p_gate)


def kernel(x, c, ctx, c_ctx, norm1_g, norm2_g, w_mod, b_mod, w_in, ret_decay, swa_sink, rwkv_mu, rwkv_w0,
           rwkv_w_up, rwkv_a0, rwkv_a_up, rwkv_g_up, rwkv_k_k, rwkv_k_a, rwkv_r_k, rwkv_ln_g, w_branch,
           w_out, w_ff1, w_ff2, final_g):
    h = jnp.concatenate([ctx, x], axis=1).reshape(R, D)
    cond8 = jnp.concatenate([c, c_ctx[None, :], jnp.zeros((8 - B - 1, D), F32)], axis=0)
    mods = mod_vectors(cond8, w_mod, b_mod)
    mods = mods.reshape(L, 8, 6, D).transpose(0, 2, 1, 3)
    cos_full, sin_signed = rope_tables()

    o_swa = RET_IN
    o_rw = RET_IN + SWA_IN
    o_small = o_rw + 3 * W
    o_gate = o_rw + RWKV_IN
    zeros_lora = jnp.zeros((L, RW_RANK, W), BF16)

    for l in range(L):
        m_l = mods[l]
        u = norm_modulate(h, norm1_g[l], m_l, 0, 1)
        p_ret = matmul_f32w(u, w_in, l, 0, RET_IN, tn=1024, name="in_ret")
        p_swa = matmul_f32w(u, w_in, l, o_swa, SWA_IN, tn=768, name="in_swa")
        mu = rwkv_mu[l]
        ps_rkv = matmul_f32w(u, w_in, l, o_rw, 3 * W, tn=512, tm=TB, epilogue="token_shift", mu=mu[:, :3 * W],
                             name="in_rkv")
        ps_small = matmul_f32w(u, w_in, l, o_small, RW_SMALL, tn=RW_SMALL, tm=TB, epilogue="token_shift",
                               mu=mu[:, 3 * W:], name="in_small")
        p_gate = matmul_f32w(u, w_in, l, o_gate, GATE_IN, tn=1024, name="in_gate")

        y_ret = retention_mixer(p_ret, ret_decay[l])
        y_swa = swa_mixer(p_swa, swa_sink[l], cos_full, sin_signed)

        wup = rwkv_w_up[l].astype(BF16)
        aup = rwkv_a_up[l].astype(BF16)
        z = zeros_lora[l]
        wup2 = jnp.stack([jnp.concatenate([wup[0], z], axis=0), jnp.concatenate([z, wup[1]], axis=0)])
        aup2 = jnp.stack([jnp.concatenate([aup[0], z], axis=0), jnp.concatenate([z, aup[1]], axis=0)])
        y_rwkv = rwkv_mixer(
            ps_rkv, ps_small, wup2, aup2, rwkv_w0[l].reshape(2, 1, W), rwkv_a0[l].reshape(2, 1, W),
            rwkv_k_k[l].reshape(1, W), rwkv_k_a[l].reshape(1, W), rwkv_r_k[l].reshape(1, W),
            rwkv_g_up[l].astype(BF16), rwkv_ln_g[l].reshape(1, W))

        merged = merge_branches(y_ret, y_swa, y_rwkv, p_gate, w_branch, l)
        h = matmul_f32w(merged, w_out, l, 0, D, tn=1024, epilogue="residual", h=h, mods_l=m_l,
                        which_gate=2, name="out_proj")
        u2 = norm_modulate(h, norm2_g[l], m_l, 3, 4)
        f = matmul_f32w(u2, w_ff1, l, 0, D_FF, tn=1024, out_dtype=BF16, epilogue="relu2", name="ff1")
        h = matmul(f, w_ff2[l].astype(BF16), tn=512, tk=4096, epilogue="residual", h=h, mods_l=m_l,
                   which_gate=5, name="ff2")

    return final_norm(h, final_g).reshape(B, SEQ, D)
```

```python
import functools
import math

import jax
import jax.numpy as jnp
from jax import lax
from jax.experimental import pallas as pl
from jax.experimental.pallas import tpu as pltpu

F32 = jnp.float32
BF16 = jnp.bfloat16

D = 2048
B = 4
SEQ = 2048
CTX = 256
TB = CTX + SEQ
R = B * TB
L = 4
GRID_W = 64
EPS = 1e-6
ROPE_BASE = 10000.0
NEG_INF = -1e30
W = D // 2
HD = 128
RET_HEADS = W // HD
RET_CHUNK = 128
RET_GN_EPS = 1e-5
SWA_Q_HEADS = W // HD
SWA_KV_HEADS = SWA_Q_HEADS // 4
SWA_GROUP = SWA_Q_HEADS // SWA_KV_HEADS
SWA_WINDOW = 128
SWA_BLOCK = 128
RW_N = 64
RW_HEADS = W // RW_N
RW_RANK = 64
RW_G_RANK = 128
RW_GN_EPS = 64e-5
RW_C = 64
RW_GW = 256
RW_GROUPS = W // RW_GW
RW_SMALL = 4 * RW_RANK + RW_G_RANK
D_FF = 4 * D
RET_IN = 4 * W
SWA_IN = (SWA_Q_HEADS + 2 * SWA_KV_HEADS) * HD
RWKV_IN = 3 * W + RW_SMALL
GATE_IN = 3 * D
N_IN = RET_IN + SWA_IN + RWKV_IN + GATE_IN

VMEM_LIMIT = 56 * 1024 * 1024


def _cparams(sem):
    return pltpu.CompilerParams(dimension_semantics=sem, vmem_limit_bytes=VMEM_LIMIT)


def _mod_kernel(x_ref, w_ref, b_ref, o_ref):
    x = x_ref[...]
    s = x * jax.nn.sigmoid(x)
    acc = jnp.dot(s.astype(BF16), w_ref[0].astype(BF16), preferred_element_type=F32)
    o_ref[0] = acc + b_ref[0]


def mod_vectors(cond8, w_mod, b_mod):
    tn = 1024
    n = w_mod.shape[-1]
    return pl.pallas_call(
        _mod_kernel,
        grid=(L, n // tn),
        in_specs=[
            pl.BlockSpec((8, D), lambda l, j: (0, 0)),
            pl.BlockSpec((1, D, tn), lambda l, j: (l, 0, j)),
            pl.BlockSpec((1, 1, tn), lambda l, j: (l, 0, j)),
        ],
        out_specs=pl.BlockSpec((1, 8, tn), lambda l, j: (l, 0, j)),
        out_shape=jax.ShapeDtypeStruct((L, 8, n), F32),
        compiler_params=_cparams(("parallel", "parallel")),
        name="mod_vectors",
    )(cond8, w_mod, b_mod.reshape(L, 1, n))


NORM_TM = 256
TILES_PER_BATCH = TB // NORM_TM


NORM_MOD_TM = 768


def _norm_mod_kernel(x_ref, g_ref, sh_ref, sc_ref, o_ref):
    i = pl.program_id(0)
    tm = NORM_MOD_TM
    bidx = i // (TB // tm)
    is_ctx = _row_is_ctx(i, tm)
    x = x_ref[...]
    ms = jnp.mean(x * x, axis=-1, keepdims=True)
    y = x * lax.rsqrt(ms + EPS) * g_ref[...]
    shift = jnp.where(is_ctx, sh_ref[0, B:B + 1, :], sh_ref[0, pl.ds(bidx, 1), :])
    scale = jnp.where(is_ctx, sc_ref[0, B:B + 1, :], sc_ref[0, pl.ds(bidx, 1), :])
    o_ref[...] = (y * (1.0 + scale) + shift).astype(o_ref.dtype)


def norm_modulate(h, g, mods_l, which_shift, which_scale):
    return pl.pallas_call(
        _norm_mod_kernel,
        grid=(R // NORM_MOD_TM,),
        in_specs=[
            pl.BlockSpec((NORM_MOD_TM, D), lambda i: (i, 0)),
            pl.BlockSpec((1, D), lambda i: (0, 0)),
            pl.BlockSpec((1, 8, D), lambda i: (which_shift, 0, 0)),
            pl.BlockSpec((1, 8, D), lambda i: (which_scale, 0, 0)),
        ],
        out_specs=pl.BlockSpec((NORM_MOD_TM, D), lambda i: (i, 0)),
        out_shape=jax.ShapeDtypeStruct((R, D), BF16),
        compiler_params=_cparams(("parallel",)),
        name="norm_modulate",
    )(h, g.reshape(1, D), mods_l, mods_l)


def _final_norm_kernel(x_ref, g_ref, o_ref):
    x = x_ref[...]
    ms = jnp.mean(x * x, axis=-1, keepdims=True)
    o_ref[...] = x * lax.rsqrt(ms + EPS) * g_ref[...]


def final_norm(h, g):
    per = SEQ // NORM_TM
    return pl.pallas_call(
        _final_norm_kernel,
        grid=(B, per),
        in_specs=[
            pl.BlockSpec((NORM_TM, D), lambda b, j: (b * TILES_PER_BATCH + CTX // NORM_TM + j, 0)),
            pl.BlockSpec((1, D), lambda b, j: (0, 0)),
        ],
        out_specs=pl.BlockSpec((NORM_TM, D), lambda b, j: (b * per + j, 0)),
        out_shape=jax.ShapeDtypeStruct((B * SEQ, D), F32),
        compiler_params=_cparams(("parallel", "parallel")),
        name="final_norm",
    )(h, g.reshape(1, D))


MM_TM = 1152


def _row_is_ctx(i, tm):
    rows = lax.broadcasted_iota(jnp.int32, (tm, 1), 0)
    return jnp.logical_and(i % (TB // tm) == 0, rows < CTX)


def _mm_finish(acc, i, o_ref, h_ref, gate_ref, epilogue, tm):
    if epilogue == "relu2":
        a = jnp.maximum(acc, 0.0)
        o_ref[...] = (a * a).astype(o_ref.dtype)
    elif epilogue == "residual":
        bidx = i // (TB // tm)
        g_b = gate_ref[0, pl.ds(bidx, 1), :]
        g_c = gate_ref[0, pl.ds(B, 1), :]
        gate = jnp.where(_row_is_ctx(i, tm), g_c, g_b)
        o_ref[...] = h_ref[...] + gate * acc
    elif epilogue == "token_shift":
        o_ref[...] = _token_shift(acc, gate_ref)
    else:
        o_ref[...] = acc.astype(o_ref.dtype)


def _token_shift(x, mu_ref):
    row = lax.broadcasted_iota(jnp.int32, (TB, 1), 0)
    prev = pltpu.roll(x, 1, 0)
    prev = jnp.where(jnp.logical_or(row == 0, row == CTX), 0.0, prev)
    nxt = pltpu.roll(x, TB - 1, 0)
    nxt = jnp.where(jnp.logical_or(row == CTX - 1, row == TB - 1), 0.0, nxt)
    return x + mu_ref[0:1, :] * (prev - x) + mu_ref[1:2, :] * (nxt - x)


def _mm_cached_kernel(*refs, epilogue, tm):
    if epilogue == "residual":
        x_ref, w_ref, h_ref, gate_ref, o_ref, wc_ref = refs
    elif epilogue == "token_shift":
        x_ref, w_ref, gate_ref, o_ref, wc_ref = refs
        h_ref = None
    else:
        x_ref, w_ref, o_ref, wc_ref = refs
        h_ref = gate_ref = None
    i = pl.program_id(1)

    @pl.when(i == 0)
    def _():
        wc_ref[...] = w_ref[...].astype(BF16)

    acc = jnp.dot(x_ref[...], wc_ref[...], preferred_element_type=F32)
    _mm_finish(acc, i, o_ref, h_ref, gate_ref, epilogue, tm)


def matmul_f32w(x, w_stack, layer, col0, n, *, tn, tm=MM_TM, out_dtype=F32, epilogue="plain", h=None, mods_l=None,
                which_gate=None, mu=None, name="matmul"):
    m, kdim = x.shape
    in_specs = [
        pl.BlockSpec((tm, kdim), lambda j, i: (i, 0)),
        pl.BlockSpec((pl.Element(kdim), pl.Element(tn)), lambda j, i: (layer * kdim, pl.multiple_of(col0 + j * tn, 128))),
    ]
    args = [x, w_stack.reshape(-1, w_stack.shape[-1])]
    aliases = {}
    if epilogue == "residual":
        in_specs += [
            pl.BlockSpec((tm, tn), lambda j, i: (i, j)),
            pl.BlockSpec((1, 8, tn), lambda j, i: (which_gate, 0, j)),
        ]
        args += [h, mods_l]
        aliases = {2: 0}
    elif epilogue == "token_shift":
        assert tm == TB
        in_specs.append(pl.BlockSpec((2, tn), lambda j, i: (0, j)))
        args.append(mu)
    return pl.pallas_call(
        functools.partial(_mm_cached_kernel, epilogue=epilogue, tm=tm),
        grid=(pl.cdiv(n, tn), m // tm),
        in_specs=in_specs,
        out_specs=pl.BlockSpec((tm, tn), lambda j, i: (i, j)),
        out_shape=jax.ShapeDtypeStruct((m, n), out_dtype),
        scratch_shapes=[pltpu.VMEM((kdim, tn), BF16)],
        input_output_aliases=aliases,
        compiler_params=_cparams(("arbitrary", "arbitrary")),
        name=name,
    )(*args)


def _mm_kernel(*refs, nk, epilogue, tm):
    if epilogue == "residual":
        x_ref, w_ref, h_ref, gate_ref, o_ref = refs[:5]
        rest = refs[5:]
    else:
        x_ref, w_ref, o_ref = refs[:3]
        rest = refs[3:]
        h_ref = gate_ref = None
    k = pl.program_id(2)
    part = jnp.dot(x_ref[...], w_ref[...], preferred_element_type=F32)

    def finish(acc):
        _mm_finish(acc, pl.program_id(0), o_ref, h_ref, gate_ref, epilogue, tm)

    if nk == 1:
        finish(part)
    else:
        acc_ref = rest[0]

        @pl.when(k == 0)
        def _():
            acc_ref[...] = part

        @pl.when(jnp.logical_and(k > 0, k < nk - 1))
        def _():
            acc_ref[...] += part

        @pl.when(k == nk - 1)
        def _():
            finish(acc_ref[...] + part)


def matmul(x, w, *, tn, tk=None, out_dtype=F32, epilogue="plain", h=None, mods_l=None, which_gate=None,
           name="matmul"):
    m, kdim = x.shape
    n = w.shape[1]
    tm = MM_TM
    tk = kdim if tk is None else tk
    nk = kdim // tk
    in_specs = [
        pl.BlockSpec((tm, tk), lambda i, j, k: (i, k)),
        pl.BlockSpec((tk, tn), lambda i, j, k: (k, j)),
    ]
    args = [x, w]
    aliases = {}
    if epilogue == "residual":
        in_specs += [
            pl.BlockSpec((tm, tn), lambda i, j, k: (i, j)),
            pl.BlockSpec((1, 8, tn), lambda i, j, k: (which_gate, 0, j)),
        ]
        args += [h, mods_l]
        aliases = {2: 0}
    scratch = [pltpu.VMEM((tm, tn), F32)] if nk > 1 else []
    return pl.pallas_call(
        functools.partial(_mm_kernel, nk=nk, epilogue=epilogue, tm=tm),
        grid=(m // tm, n // tn, nk),
        in_specs=in_specs,
        out_specs=pl.BlockSpec((tm, tn), lambda i, j, k: (i, j)),
        out_shape=jax.ShapeDtypeStruct((m, n), out_dtype),
        scratch_shapes=scratch,
        input_output_aliases=aliases,
        compiler_params=_cparams(("parallel", "parallel", "arbitrary")),
        name=name,
    )(*args)


RET_NCHUNK = TB // RET_CHUNK
RET_CTX_CHUNKS = CTX // RET_CHUNK


def _log_sigmoid(x):
    return jnp.minimum(x, 0.0) - jnp.log(1.0 + jnp.exp(-jnp.abs(x)))


RET_BATCH = 6


def _ret_kernel(dec_ref, q_ref, k_ref, v_ref, g_ref, o_ref, y_ref):
    hh = pl.program_id(1)
    c = RET_CHUNK
    ri = lax.broadcasted_iota(jnp.int32, (c, c), 0).astype(F32)
    ci = lax.broadcasted_iota(jnp.int32, (c, c), 1).astype(F32)
    kscale = HD ** -0.5
    nt = (((1,), (1,)), ((), ()))
    for direction in range(2):
        lg = _log_sigmoid(jnp.full((c, c), dec_ref[direction, hh], F32))
        diff = (ri - ci) if direction == 0 else (ci - ri)
        intra = jnp.where(diff >= 0, jnp.exp(lg * jnp.maximum(diff, 0.0)), 0.0)
        pos = ri if direction == 0 else (c - 1.0) - ri
        q_decay = jnp.exp(lg * (pos + 1.0))
        k_decay = jnp.exp(lg * ((c - 1.0) - pos))
        chunk_decay = jnp.exp(lg * float(c))
        if direction == 0:
            order = list(range(RET_NCHUNK))
        else:
            order = list(range(RET_CTX_CHUNKS - 1, -1, -1)) + list(range(RET_NCHUNK - 1, RET_CTX_CHUNKS - 1, -1))
        state = jnp.zeros((c, c), F32)
        for b0 in range(0, RET_NCHUNK, RET_BATCH):
            rows = [slice(ch * c, (ch + 1) * c) for ch in order[b0:b0 + RET_BATCH]]
            n = range(len(rows))
            q = [q_ref[r, :] for r in rows]
            k = [k_ref[r, :] * kscale for r in rows]
            v = [v_ref[r, :].astype(BF16) for r in rows]
            sc = [lax.dot_general(q[i].astype(BF16), k[i].astype(BF16), nt, preferred_element_type=F32) * intra
                  for i in n]
            y_in = [jnp.dot(sc[i].astype(BF16), v[i], preferred_element_type=F32) for i in n]
            kv = [jnp.dot((k[i] * k_decay).T.astype(BF16), v[i], preferred_element_type=F32) for i in n]
            states = []
            for i in n:
                states.append(state)
                state = state * chunk_decay + kv[i]
            y_x = [jnp.dot((q[i] * q_decay).astype(BF16), states[i].astype(BF16), preferred_element_type=F32)
                   for i in n]
            for i in n:
                y = y_in[i] + y_x[i]
                if direction == 0:
                    y_ref[rows[i], :] = y
                else:
                    y = y + y_ref[rows[i], :]
                    mean = jnp.mean(y, axis=-1, keepdims=True)
                    yc = y - mean
                    var = jnp.mean(yc * yc, axis=-1, keepdims=True)
                    g = g_ref[rows[i], :]
                    o_ref[rows[i], :] = (g * jax.nn.sigmoid(g) * (yc * lax.rsqrt(var + RET_GN_EPS))).astype(o_ref.dtype)


def retention_mixer(p_ret, decay_l):
    nh = RET_HEADS
    blk = lambda off: pl.BlockSpec((TB, HD), lambda b, h, off=off: (b, off + h))
    rows = p_ret.shape[0]
    return pl.pallas_call(
        _ret_kernel,
        grid=(rows // TB, nh),
        in_specs=[pl.BlockSpec(memory_space=pltpu.SMEM), blk(0), blk(nh), blk(2 * nh), blk(3 * nh)],
        out_specs=pl.BlockSpec((TB, HD), lambda b, h: (b, h)),
        out_shape=jax.ShapeDtypeStruct((rows, W), BF16),
        scratch_shapes=[pltpu.VMEM((TB, HD), F32)],
        compiler_params=_cparams(("parallel", "parallel")),
        name="retention",
    )(decay_l, p_ret, p_ret, p_ret, p_ret)


SWA_NBLK = SEQ // SWA_BLOCK
SWA_WIN = 3 * SWA_BLOCK
SWA_BATCH = 4


def _rope(x, cos, sin_signed):
    lane = lax.broadcasted_iota(jnp.int32, x.shape, 1)
    partner = jnp.where(lane % 64 < 32, pltpu.roll(x, 96, 1), pltpu.roll(x, 32, 1))
    return x * cos + partner * sin_signed


def _swa_kernel(sink_ref, q_ref, k_ref, v_ref, cos_ref, sin_ref, o_ref, kt_ref, vb_ref):
    kvh = pl.program_id(1)
    scale = HD ** -0.5
    g = SWA_GROUP
    blk = SWA_BLOCK

    vb_ref[...] = v_ref[...].astype(BF16)
    for j in range(SWA_NBLK):
        rows = slice(j * blk, (j + 1) * blk)
        k_rot = _rope(k_ref[CTX + j * blk:CTX + (j + 1) * blk, :], cos_ref[rows, :], sin_ref[rows, :])
        kt_ref[j] = k_rot.T.astype(BF16)
    kc_t = jnp.concatenate([k_ref[j * blk:(j + 1) * blk, :].T for j in range(CTX // blk)], axis=1).astype(BF16)
    vc = vb_ref[:CTX, :]

    def sink_col(rows_per_head):
        parts = [jnp.full((rows_per_head, 1), sink_ref[kvh * g + gi], F32) for gi in range(g)]
        return jnp.concatenate(parts, axis=0)

    def lane_tiles(x):
        return [x[:, j * HD:(j + 1) * HD] for j in range(x.shape[1] // HD)]

    def row_max(*xs):
        tiles = [t for x in xs for t in lane_tiles(x)]
        return jnp.max(functools.reduce(jnp.maximum, tiles), axis=-1, keepdims=True)

    def row_sum(*xs):
        tiles = [t for x in xs for t in lane_tiles(x)]
        return jnp.sum(functools.reduce(jnp.add, tiles), axis=-1, keepdims=True)

    qc = jnp.concatenate([q_ref[:CTX, gi * HD:(gi + 1) * HD] for gi in range(g)], axis=0).astype(BF16)
    s = jnp.dot(qc, kc_t, preferred_element_type=F32) * scale
    sk = sink_col(CTX)
    m = jnp.maximum(row_max(s), sk)
    e = jnp.exp(s - m)
    den = row_sum(e) + jnp.exp(sk - m)
    oc = jnp.dot(e.astype(BF16), vc, preferred_element_type=F32) / den
    for gi in range(g):
        o_ref[:CTX, gi * HD:(gi + 1) * HD] = oc[gi * CTX:(gi + 1) * CTX].astype(o_ref.dtype)

    sk_b = sink_col(blk)
    delta = (lax.broadcasted_iota(jnp.int32, (g * blk, SWA_WIN), 1)
             - lax.broadcasted_iota(jnp.int32, (g * blk, SWA_WIN), 0) % blk)

    for n0 in range(0, SWA_NBLK, SWA_BATCH):
        ns = list(range(n0, n0 + SWA_BATCH))
        idx = range(len(ns))
        r0 = [n * blk for n in ns]
        j0 = [min(max(n - 1, 0), SWA_NBLK - SWA_WIN // blk) for n in ns]
        qn = [jnp.concatenate([_rope(q_ref[CTX + r0[i]:CTX + r0[i] + blk, gi * HD:(gi + 1) * HD],
                                     cos_ref[r0[i]:r0[i] + blk, :], sin_ref[r0[i]:r0[i] + blk, :])
                               for gi in range(g)], axis=0).astype(BF16) for i in idx]
        kw_t = [jnp.concatenate([kt_ref[j0[i] + w] for w in range(SWA_WIN // blk)], axis=1) for i in idx]
        s_win = [jnp.dot(qn[i], kw_t[i], preferred_element_type=F32) * scale for i in idx]
        s_win = [jnp.where(jnp.abs(delta + (j0[i] * blk - r0[i])) <= SWA_WINDOW, s_win[i], NEG_INF) for i in idx]
        s_ctx = [jnp.dot(qn[i], kc_t, preferred_element_type=F32) * scale for i in idx]
        m = [jnp.maximum(row_max(s_win[i], s_ctx[i]), sk_b) for i in idx]
        e_win = [jnp.exp(s_win[i] - m[i]) for i in idx]
        e_ctx = [jnp.exp(s_ctx[i] - m[i]) for i in idx]
        den = [row_sum(e_win[i], e_ctx[i]) + jnp.exp(sk_b - m[i]) for i in idx]
        o = [(jnp.dot(e_win[i].astype(BF16), vb_ref[CTX + j0[i] * blk:CTX + j0[i] * blk + SWA_WIN, :],
                      preferred_element_type=F32)
              + jnp.dot(e_ctx[i].astype(BF16), vc, preferred_element_type=F32)) / den[i] for i in idx]
        for i in idx:
            for gi in range(g):
                o_ref[CTX + r0[i]:CTX + r0[i] + blk, gi * HD:(gi + 1) * HD] = (
                    o[i][gi * blk:(gi + 1) * blk].astype(o_ref.dtype))


def swa_mixer(p_swa, sink_l, cos_full, sin_signed):
    gw = SWA_GROUP * HD
    rows = p_swa.shape[0]
    return pl.pallas_call(
        _swa_kernel,
        grid=(rows // TB, SWA_KV_HEADS),
        in_specs=[
            pl.BlockSpec(memory_space=pltpu.SMEM),
            pl.BlockSpec((TB, gw), lambda b, kv: (b, kv)),
            pl.BlockSpec((TB, HD), lambda b, kv: (b, SWA_Q_HEADS + kv)),
            pl.BlockSpec((TB, HD), lambda b, kv: (b, SWA_Q_HEADS + SWA_KV_HEADS + kv)),
            pl.BlockSpec((SEQ, HD), lambda b, kv: (0, 0)),
            pl.BlockSpec((SEQ, HD), lambda b, kv: (0, 0)),
        ],
        out_specs=pl.BlockSpec((TB, gw), lambda b, kv: (b, kv)),
        out_shape=jax.ShapeDtypeStruct((rows, W), BF16),
        scratch_shapes=[pltpu.VMEM((SWA_NBLK, HD, SWA_BLOCK), BF16), pltpu.VMEM((TB, HD), BF16)],
        compiler_params=_cparams(("parallel", "parallel")),
        name="swa",
    )(sink_l, p_swa, p_swa, p_swa, cos_full, sin_signed)


def rope_tables():
    rows = SEQ // GRID_W
    row = jnp.repeat(jnp.arange(rows), GRID_W).astype(F32)
    col = jnp.tile(jnp.arange(GRID_W), rows).astype(F32)
    n_freq = HD // 4
    inv_freq = ROPE_BASE ** (-jnp.arange(n_freq, dtype=F32) / n_freq)
    ang_r = row[:, None] * inv_freq
    ang_c = col[:, None] * inv_freq
    cr, sr, cc, sc = jnp.cos(ang_r), jnp.sin(ang_r), jnp.cos(ang_c), jnp.sin(ang_c)
    cos_full = jnp.concatenate([cr, cr, cc, cc], axis=-1)
    sin_signed = jnp.concatenate([-sr, sr, -sc, sc], axis=-1)
    return cos_full, sin_signed


RW_NCHUNK = TB // RW_C
RW_CTX_CHUNKS = CTX // RW_C
RW_GPC = RW_GW // RW_C


def _rw_chunk(direction, s):
    back = jnp.where(s < RW_CTX_CHUNKS, RW_CTX_CHUNKS - 1 - s, RW_NCHUNK - 1 + RW_CTX_CHUNKS - s)
    return jnp.where(direction == 0, s, back)


def _head_ones():
    r = lax.broadcasted_iota(jnp.int32, (RW_GW, RW_GW), 0) // RW_N
    c = lax.broadcasted_iota(jnp.int32, (RW_GW, RW_GW), 1) // RW_N
    return r == c


def _dot16(a, b):
    return jnp.dot(a.astype(BF16), b.astype(BF16), preferred_element_type=F32)


def _dot16_nt(a, b):
    return lax.dot_general(a.astype(BF16), b.astype(BF16), (((1,), (1,)), ((), ())), preferred_element_type=F32)


def _split_bf16(x, terms):
    parts = []
    for _ in range(terms):
        p = x.astype(BF16)
        parts.append(p)
        x = x - p.astype(F32)
    return parts


def _dot_exact_lhs(x, b16, terms):
    return jnp.dot(jnp.concatenate(_split_bf16(x, terms), axis=1), jnp.concatenate([b16] * terms, axis=0),
                   preferred_element_type=F32)


def _rwkv_kernel(r_ref, k_ref, v_ref, sm_ref, wup_ref, aup_ref, w0_ref, a0_ref, kk_ref, ka_ref, rk_ref, gup_ref,
                 lng_ref, o_ref, q_s, y1_s, m_s, z_s, bonus_s, gate_s, st_ref, ybuf):
    direction = pl.program_id(1)
    step = pl.program_id(2)
    cur = step % 2
    prev = 1 - cur
    c = RW_C
    same_head = _head_ones()
    mask16 = jnp.where(same_head, 1.0, 0.0).astype(BF16)
    nbat = r_ref.shape[0]
    chains = [(bb, g) for bb in range(nbat) for g in range(RW_GROUPS)]
    groups = range(len(chains))
    bbs = [bb for bb, _ in chains]
    sls = [slice(g * RW_GW, (g + 1) * RW_GW) for _, g in chains]

    @pl.when(step == 0)
    def _():
        st_ref[...] = jnp.zeros_like(st_ref)
        for ref in (q_s, y1_s, m_s, z_s, bonus_s, gate_s):
            ref[1] = jnp.zeros(ref.shape[1:], ref.dtype)

    ys = []
    for g in groups:
        m_bd = jnp.concatenate([m_s[prev, bbs[g], :, sls[g]]] * RW_GPC, axis=0) * mask16
        z_bd = jnp.where(same_head, jnp.concatenate([z_s[prev, bbs[g], :, sls[g]]] * RW_GPC, axis=0), 0.0)
        res = _dot16(jnp.concatenate([q_s[prev, bbs[g], :, sls[g]], m_bd], axis=0), st_ref[g])
        ys.append(res[:c] + y1_s[prev, bbs[g], :, sls[g]])
        st_ref[g] = res[c:] + z_bd

    def bd(x):
        return jnp.concatenate([x.astype(BF16)] * RW_GPC, axis=0) * mask16

    sgn = 1 - 2 * direction
    cum_terms = 3
    tt = lax.broadcasted_iota(jnp.int32, (c, cum_terms * c), 0)
    ss = lax.broadcasted_iota(jnp.int32, (c, cum_terms * c), 1) % c
    tri16 = jnp.where((tt - ss) * sgn >= 0, 1.0, 0.0).astype(BF16)
    t4 = lax.broadcasted_iota(jnp.int32, (c, RW_GW), 0)
    s4 = lax.broadcasted_iota(jnp.int32, (c, RW_GW), 1) % c
    d4 = (t4 - s4) * sgn
    strict = d4 > 0
    incl = d4 >= 0
    eye = jnp.where(d4 == 0, 1.0, 0.0)

    sm = [sm_ref[bb] for bb in range(nbat)]
    tanh_wd = [jnp.tanh(x[:, :2 * RW_RANK]).astype(BF16) for x in sm]
    ad = [x[:, 2 * RW_RANK:4 * RW_RANK].astype(BF16) for x in sm]

    r = [r_ref[bbs[g], :, sls[g]] for g in groups]
    k = [k_ref[bbs[g], :, sls[g]] for g in groups]
    v = [v_ref[bbs[g], :, sls[g]] for g in groups]
    logw = [-jax.nn.sigmoid(w0_ref[0, :, sls[g]] + jnp.dot(tanh_wd[bbs[g]], wup_ref[0, :, sls[g]],
                                                          preferred_element_type=F32)) * math.exp(-0.5)
            for g in groups]

    def alpha_of(d, g):
        return jax.nn.sigmoid(a0_ref[d, :, sls[g]] + jnp.dot(ad[bbs[g]], aup_ref[d, :, sls[g]],
                                                             preferred_element_type=F32))

    alpha = [alpha_of(direction, g) for g in groups]
    alpha_both = [alpha[g] + alpha_of(1 - direction, g) for g in groups]
    kkp = [k[g] * kk_ref[:, sls[g]] for g in groups]
    kdir = [k[g] * (1.0 + (alpha[g] - 1.0) * ka_ref[:, sls[g]]) for g in groups]
    kboth = [k[g] * (2.0 + (alpha_both[g] - 2.0) * ka_ref[:, sls[g]]) for g in groups]
    nch = len(chains)
    sums = _dot_exact_lhs(jnp.concatenate([kkp[g] * kkp[g] for g in groups]
                                          + [r[g] * kboth[g] * rk_ref[:, sls[g]] for g in groups], axis=0), mask16, 2)
    kk = [kkp[g] / jnp.maximum(jnp.sqrt(sums[g * c:(g + 1) * c]), 1e-12) for g in groups]
    for g in groups:
        bonus_s[cur, bbs[g], :, sls[g]] = sums[(nch + g) * c:(nch + g + 1) * c] * v[g]
    for bb in range(nbat):
        gate_s[cur, bb] = jnp.dot(jax.nn.sigmoid(sm[bb][:, 4 * RW_RANK:]).astype(BF16), gup_ref[...],
                                  preferred_element_type=F32)
    b_vec = [kk[g] * alpha[g] for g in groups]

    cum = [jnp.dot(tri16, jnp.concatenate(_split_bf16(logw[g], cum_terms), axis=0), preferred_element_type=F32)
           for g in groups]
    total = [jnp.sum(logw[g], axis=0, keepdims=True) for g in groups]
    e_out = [jnp.exp(-cum[g]) for g in groups]
    e_left = [jnp.exp(total[g] - cum[g]) for g in groups]
    a_t = [-kk[g] * jnp.exp(cum[g] - logw[g]) for g in groups]
    r_t = [r[g] * jnp.exp(cum[g]) for g in groups]
    lhs = [jnp.concatenate([a_t[g], r_t[g]], axis=0) for g in groups]
    gb = [_dot16_nt(lhs[g], bd(b_vec[g] * e_out[g])) for g in groups]
    gk = [_dot16_nt(lhs[g], bd(kdir[g] * e_out[g])) for g in groups]
    a_ab = [jnp.where(strict, gb[g][:c], 0.0) for g in groups]
    a_ak = [jnp.where(strict, gk[g][:c], 0.0) for g in groups]
    a_rb = [jnp.where(incl, gb[g][c:], 0.0) for g in groups]
    a_rk = [jnp.where(incl, gk[g][c:], 0.0) for g in groups]

    pt = jnp.where(direction == 0, t4, c - 1 - t4)
    ps = jnp.where(direction == 0, s4, c - 1 - s4)
    px = pt ^ ps

    def level_mask(lv):
        return (px >> lv) * 2 + ((pt >> lv) & 1) == 3

    tinv = [eye + jnp.where(level_mask(0), a_ab[g], 0.0) for g in groups]
    for lv in range(1, int(math.log2(c))):
        lm = level_mask(lv)
        cross = [_dot16(jnp.where(lm, a_ab[g], 0.0), bd(tinv[g])) for g in groups]
        tinv = [tinv[g] + _dot16(tinv[g], bd(cross[g])) for g in groups]

    eye16 = eye.astype(BF16)
    blt = [_dot16_nt(eye16, bd(b_vec[g] * e_left[g])) for g in groups]
    klt = [_dot16_nt(eye16, bd(kdir[g] * e_left[g])) for g in groups]
    vprod = [_dot16(jnp.concatenate([a_ak[g], a_rk[g], klt[g]], axis=0), bd(v[g])) for g in groups]
    pw = [_dot16(tinv[g], jnp.concatenate([bd(a_t[g]), bd(vprod[g][:c])], axis=1)) for g in groups]
    top = [_dot16(jnp.concatenate([a_rb[g], blt[g]], axis=0),
                  jnp.concatenate([bd(pw[g][:, :RW_GW]), bd(pw[g][:, RW_GW:])], axis=1)) for g in groups]
    low = [vprod[g][c:] for g in groups]
    for g in groups:
        sl = sls[g]
        q_s[cur, bbs[g], :, sl] = (r_t[g] + top[g][:c, :RW_GW]).astype(q_s.dtype)
        y1_s[cur, bbs[g], :, sl] = top[g][:c, RW_GW:] + low[g][:c]
        m_s[cur, bbs[g], :, sl] = (eye * jnp.exp(total[g]) + top[g][c:, :RW_GW]).astype(m_s.dtype)
        z_s[cur, bbs[g], :, sl] = top[g][c:, RW_GW:] + low[g][c:]

    chunk_prev = _rw_chunk(direction, jnp.maximum(step - 1, 0))

    @pl.when(direction == 0)
    def _():
        for g in groups:
            ybuf[bbs[g], chunk_prev, :, sls[g]] = ys[g].astype(ybuf.dtype)

    @pl.when(direction == 1)
    def _():
        y4 = jnp.concatenate([ys[g] + ybuf[bbs[g], chunk_prev, :, sls[g]] for g in groups], axis=0)
        mean = _dot_exact_lhs(y4, mask16, 2) * (1.0 / RW_N)
        yc = y4 - mean
        var = _dot_exact_lhs(yc * yc, mask16, 2) * (1.0 / RW_N)
        yn = yc * lax.rsqrt(var + RW_GN_EPS)
        for g in groups:
            sl = sls[g]
            o_ref[bbs[g], :, sl] = ((yn[g * c:(g + 1) * c] * lng_ref[:, sl] + bonus_s[prev, bbs[g], :, sl])
                                    * gate_s[prev, bbs[g], :, sl]).astype(o_ref.dtype)


RW_BPS = 4


def rwkv_mixer(ps, wup2, aup2, w0, a0, k_k, k_a, r_k, g_up, ln_g):
    rows = ps.shape[0]
    nb = rows // TB
    nsteps = RW_NCHUNK + 1

    def in_blk(d, s):
        return _rw_chunk(d, jnp.minimum(s, RW_NCHUNK - 1))

    def out_blk(d, s):
        return _rw_chunk(1, jnp.where(d == 0, 0, jnp.maximum(s - 1, 0)))

    feat = lambda off: pl.BlockSpec((RW_BPS, RW_C, W), lambda b, d, s, off=off: (b, in_blk(d, s), off))
    vec = pl.BlockSpec((1, W), lambda b, d, s: (0, 0))
    whole = lambda shape: pl.BlockSpec(shape, lambda b, d, s: (0,) * len(shape))
    slot = lambda dt: pltpu.VMEM((2, RW_BPS, RW_C, W), dt)
    out = pl.pallas_call(
        _rwkv_kernel,
        grid=(nb // RW_BPS, 2, nsteps),
        in_specs=[feat(0), feat(1), feat(2),
                  pl.BlockSpec((RW_BPS, RW_C, RW_SMALL), lambda b, d, s: (b, in_blk(d, s), 3 * W // RW_SMALL)),
                  pl.BlockSpec((1, 2 * RW_RANK, W), lambda b, d, s: (d, 0, 0)), whole((2, 2 * RW_RANK, W)),
                  pl.BlockSpec((1, 1, W), lambda b, d, s: (d, 0, 0)), whole((2, 1, W)), vec, vec, vec,
                  whole((RW_G_RANK, W)), vec],
        out_specs=pl.BlockSpec((RW_BPS, RW_C, W), lambda b, d, s: (b, out_blk(d, s), 0)),
        out_shape=jax.ShapeDtypeStruct((nb, TB, W), BF16),
        scratch_shapes=[slot(BF16), slot(F32), slot(BF16), slot(F32), slot(F32), slot(F32),
                        pltpu.VMEM((RW_BPS * RW_GROUPS, RW_GW, RW_GW), F32),
                        pltpu.VMEM((RW_BPS, RW_NCHUNK, RW_C, W), BF16)],
        compiler_params=_cparams(("arbitrary", "arbitrary", "arbitrary")),
        name="rwkv_mixer",
    )(*[ps.reshape(nb, TB, RWKV_IN)] * 4, wup2, aup2, w0, a0, k_k, k_a, r_k,
      g_up, ln_g)
    return out.reshape(rows, W)


MG_TM = 1152
MG_TN = 512


def _merge_kernel(y0_ref, y1_ref, y2_ref, wb_ref, g0_ref, g1_ref, g2_ref, o_ref, wc_ref):
    @pl.when(pl.program_id(1) == 0)
    def _():
        wc_ref[...] = wb_ref[...].astype(BF16)

    acc = None
    for y_ref, g_ref, n in ((y0_ref, g0_ref, 0), (y1_ref, g1_ref, 1), (y2_ref, g2_ref, 2)):
        proj = jnp.dot(y_ref[...], wc_ref[n], preferred_element_type=F32)
        term = jax.nn.sigmoid(g_ref[...]) * proj
        acc = term if acc is None else acc + term
    o_ref[...] = acc.astype(o_ref.dtype)


def merge_branches(y_ret, y_swa, y_rwkv, p_gate, w_branch, layer):
    nj = D // MG_TN
    ysp = pl.BlockSpec((MG_TM, W), lambda j, i: (i, 0))
    gsp = lambda n: pl.BlockSpec((MG_TM, MG_TN), lambda j, i, n=n: (i, n * nj + j))
    return pl.pallas_call(
        _merge_kernel,
        grid=(nj, R // MG_TM),
        in_specs=[ysp, ysp, ysp, pl.BlockSpec((None, 3, W, MG_TN), lambda j, i: (layer, 0, 0, j)),
                  gsp(0), gsp(1), gsp(2)],
        out_specs=pl.BlockSpec((MG_TM, MG_TN), lambda j, i: (i, j)),
        out_shape=jax.ShapeDtypeStruct((R, D), BF16),
        scratch_shapes=[pltpu.VMEM((3, W, MG_TN), BF16)],
        compiler_params=_cparams(("arbitrary", "arbitrary")),
        name="merge",
    )(y_ret, y_swa, y_rwkv, w_branch, p_gate, p_gate, p_gate)


def kernel(x, c, ctx, c_ctx, norm1_g, norm2_g, w_mod, b_mod, w_in, ret_decay, swa_sink, rwkv_mu, rwkv_w0,
           rwkv_w_up, rwkv_a0, rwkv_a_up, rwkv_g_up, rwkv_k_k, rwkv_k_a, rwkv_r_k, rwkv_ln_g, w_branch,
           w_out, w_ff1, w_ff2, final_g):
    h = jnp.concatenate([ctx, x], axis=1).reshape(R, D)
    cond8 = jnp.concatenate([c, c_ctx[None, :], jnp.zeros((8 - B - 1, D), F32)], axis=0)
    mods = mod_vectors(cond8, w_mod, b_mod)
    mods = mods.reshape(L, 8, 6, D).transpose(0, 2, 1, 3)
    cos_full, sin_signed = rope_tables()

    o_swa = RET_IN
    o_rw = RET_IN + SWA_IN
    o_small = o_rw + 3 * W
    o_gate = o_rw + RWKV_IN
    zeros_lora = jnp.zeros((L, RW_RANK, W), BF16)

    for l in range(L):
        m_l = mods[l]
        u = norm_modulate(h, norm1_g[l], m_l, 0, 1)
        p_ret = matmul_f32w(u, w_in, l, 0, RET_IN, tn=1024, name="in_ret")
        p_swa = matmul_f32w(u, w_in, l, o_swa, SWA_IN, tn=768, name="in_swa")
        ps_rwkv = matmul_f32w(u, w_in, l, o_rw, RWKV_IN, tn=512, tm=TB, epilogue="token_shift", mu=rwkv_mu[l],
                              name="in_rwkv")
        p_gate = matmul_f32w(u, w_in, l, o_gate, GATE_IN, tn=1024, name="in_gate")

        y_ret = retention_mixer(p_ret, ret_decay[l])
        y_swa = swa_mixer(p_swa, swa_sink[l], cos_full, sin_signed)

        wup = rwkv_w_up[l].astype(BF16)
        aup = rwkv_a_up[l].astype(BF16)
        z = zeros_lora[l]
        wup2 = jnp.stack([jnp.concatenate([wup[0], z], axis=0), jnp.concatenate([z, wup[1]], axis=0)])
        aup2 = jnp.stack([jnp.concatenate([aup[0], z], axis=0), jnp.concatenate([z, aup[1]], axis=0)])
        y_rwkv = rwkv_mixer(
            ps_rwkv, wup2, aup2, rwkv_w0[l].reshape(2, 1, W), rwkv_a0[l].reshape(2, 1, W),
            rwkv_k_k[l].reshape(1, W), rwkv_k_a[l].reshape(1, W), rwkv_r_k[l].reshape(1, W),
            rwkv_g_up[l].astype(BF16), rwkv_ln_g[l].reshape(1, W))

        merged = merge_branches(y_ret, y_swa, y_rwkv, p_gate, w_branch, l)
        h = matmul_f32w(merged, w_out, l, 0, D, tn=1024, epilogue="residual", h=h, mods_l=m_l,
                        which_gate=2, name="out_proj")
        u2 = norm_modulate(h, norm2_g[l], m_l, 3, 4)
        f = matmul_f32w(u2, w_ff1, l, 0, D_FF, tn=1024, out_dtype=BF16, epilogue="relu2", name="ff1")
        h = matmul(f, w_ff2[l].astype(BF16), tn=512, tk=4096, epilogue="residual", h=h, mods_l=m_l,
                   which_gate=5, name="ff2")

    return final_norm(h, final_g).reshape(B, SEQ, D)
```

```python
import functools
import math

import jax
import jax.numpy as jnp
from jax import lax
from jax.experimental import pallas as pl
from jax.experimental.pallas import tpu as pltpu

F32 = jnp.float32
BF16 = jnp.bfloat16

D = 2048
B = 4
SEQ = 2048
CTX = 256
TB = CTX + SEQ
R = B * TB
L = 4
GRID_W = 64
EPS = 1e-6
ROPE_BASE = 10000.0
NEG_INF = -1e30
W = D // 2
HD = 128
RET_HEADS = W // HD
RET_CHUNK = 128
RET_GN_EPS = 1e-5
SWA_Q_HEADS = W // HD
SWA_KV_HEADS = SWA_Q_HEADS // 4
SWA_GROUP = SWA_Q_HEADS // SWA_KV_HEADS
SWA_WINDOW = 128
SWA_BLOCK = 128
RW_N = 64
RW_RANK = 64
RW_G_RANK = 128
RW_GN_EPS = 64e-5
RW_C = 64
RW_GW = 256
RW_GROUPS = W // RW_GW
RW_SMALL = 4 * RW_RANK + RW_G_RANK
D_FF = 4 * D
RET_IN = 4 * W
SWA_IN = (SWA_Q_HEADS + 2 * SWA_KV_HEADS) * HD
RWKV_IN = 3 * W + RW_SMALL
GATE_IN = 3 * D
N_IN = RET_IN + SWA_IN + RWKV_IN + GATE_IN

VMEM_LIMIT = 56 * 1024 * 1024


def _cparams(sem):
    return pltpu.CompilerParams(dimension_semantics=sem, vmem_limit_bytes=VMEM_LIMIT)


def _mod_kernel(x_ref, w_ref, b_ref, o_ref):
    x = x_ref[...]
    s = x * jax.nn.sigmoid(x)
    acc = jnp.dot(s.astype(BF16), w_ref[0].astype(BF16), preferred_element_type=F32)
    o_ref[0] = acc + b_ref[0]


def mod_vectors(cond8, w_mod, b_mod):
    tn = 1024
    n = w_mod.shape[-1]
    return pl.pallas_call(
        _mod_kernel,
        grid=(L, n // tn),
        in_specs=[
            pl.BlockSpec((8, D), lambda l, j: (0, 0)),
            pl.BlockSpec((1, D, tn), lambda l, j: (l, 0, j)),
            pl.BlockSpec((1, 1, tn), lambda l, j: (l, 0, j)),
        ],
        out_specs=pl.BlockSpec((1, 8, tn), lambda l, j: (l, 0, j)),
        out_shape=jax.ShapeDtypeStruct((L, 8, n), F32),
        compiler_params=_cparams(("parallel", "parallel")),
        name="mod_vectors",
    )(cond8, w_mod, b_mod.reshape(L, 1, n))


NORM_TM = 256
TILES_PER_BATCH = TB // NORM_TM


NORM_MOD_TM = 768


def _norm_mod_kernel(x_ref, g_ref, sh_ref, sc_ref, o_ref):
    i = pl.program_id(0)
    tm = NORM_MOD_TM
    bidx = i // (TB // tm)
    is_ctx = _row_is_ctx(i, tm)
    x = x_ref[...]
    ms = jnp.mean(x * x, axis=-1, keepdims=True)
    y = x * lax.rsqrt(ms + EPS) * g_ref[...]
    shift = jnp.where(is_ctx, sh_ref[0, B:B + 1, :], sh_ref[0, pl.ds(bidx, 1), :])
    scale = jnp.where(is_ctx, sc_ref[0, B:B + 1, :], sc_ref[0, pl.ds(bidx, 1), :])
    o_ref[...] = (y * (1.0 + scale) + shift).astype(o_ref.dtype)


def norm_modulate(h, g, mods_l, which_shift, which_scale):
    return pl.pallas_call(
        _norm_mod_kernel,
        grid=(R // NORM_MOD_TM,),
        in_specs=[
            pl.BlockSpec((NORM_MOD_TM, D), lambda i: (i, 0)),
            pl.BlockSpec((1, D), lambda i: (0, 0)),
            pl.BlockSpec((1, 8, D), lambda i: (which_shift, 0, 0)),
            pl.BlockSpec((1, 8, D), lambda i: (which_scale, 0, 0)),
        ],
        out_specs=pl.BlockSpec((NORM_MOD_TM, D), lambda i: (i, 0)),
        out_shape=jax.ShapeDtypeStruct((R, D), BF16),
        compiler_params=_cparams(("parallel",)),
        name="norm_modulate",
    )(h, g.reshape(1, D), mods_l, mods_l)


def _final_norm_kernel(x_ref, g_ref, o_ref):
    x = x_ref[...]
    ms = jnp.mean(x * x, axis=-1, keepdims=True)
    o_ref[...] = x * lax.rsqrt(ms + EPS) * g_ref[...]


def final_norm(h, g):
    per = SEQ // NORM_TM
    return pl.pallas_call(
        _final_norm_kernel,
        grid=(B, per),
        in_specs=[
            pl.BlockSpec((NORM_TM, D), lambda b, j: (b * TILES_PER_BATCH + CTX // NORM_TM + j, 0)),
            pl.BlockSpec((1, D), lambda b, j: (0, 0)),
        ],
        out_specs=pl.BlockSpec((NORM_TM, D), lambda b, j: (b * per + j, 0)),
        out_shape=jax.ShapeDtypeStruct((B * SEQ, D), F32),
        compiler_params=_cparams(("parallel", "parallel")),
        name="final_norm",
    )(h, g.reshape(1, D))


MM_TM = 1152


def _row_is_ctx(i, tm):
    rows = lax.broadcasted_iota(jnp.int32, (tm, 1), 0)
    return jnp.logical_and(i % (TB // tm) == 0, rows < CTX)


def _mm_finish(acc, i, o_ref, h_ref, gate_ref, epilogue, tm):
    if epilogue == "relu2":
        a = jnp.maximum(acc, 0.0)
        o_ref[...] = (a * a).astype(o_ref.dtype)
    elif epilogue == "residual":
        bidx = i // (TB // tm)
        g_b = gate_ref[0, pl.ds(bidx, 1), :]
        g_c = gate_ref[0, pl.ds(B, 1), :]
        gate = jnp.where(_row_is_ctx(i, tm), g_c, g_b)
        o_ref[...] = h_ref[...] + gate * acc
    elif epilogue == "token_shift":
        o_ref[...] = _token_shift(acc, gate_ref)
    else:
        o_ref[...] = acc.astype(o_ref.dtype)


def _token_shift(x, mu_ref):
    row = lax.broadcasted_iota(jnp.int32, (TB, 1), 0)
    prev = pltpu.roll(x, 1, 0)
    prev = jnp.where(jnp.logical_or(row == 0, row == CTX), 0.0, prev)
    nxt = pltpu.roll(x, TB - 1, 0)
    nxt = jnp.where(jnp.logical_or(row == CTX - 1, row == TB - 1), 0.0, nxt)
    return x + mu_ref[0:1, :] * (prev - x) + mu_ref[1:2, :] * (nxt - x)


def _mm_cached_kernel(*refs, epilogue, tm):
    if epilogue == "residual":
        x_ref, w_ref, h_ref, gate_ref, o_ref, wc_ref = refs
    elif epilogue == "token_shift":
        x_ref, w_ref, gate_ref, o_ref, wc_ref = refs
        h_ref = None
    else:
        x_ref, w_ref, o_ref, wc_ref = refs
        h_ref = gate_ref = None
    i = pl.program_id(1)

    @pl.when(i == 0)
    def _():
        wc_ref[...] = w_ref[...].astype(BF16)

    acc = jnp.dot(x_ref[...], wc_ref[...], preferred_element_type=F32)
    _mm_finish(acc, i, o_ref, h_ref, gate_ref, epilogue, tm)


def matmul_f32w(x, w_stack, layer, col0, n, *, tn, tm=MM_TM, out_dtype=F32, epilogue="plain", h=None, mods_l=None,
                which_gate=None, mu=None, name="matmul"):
    m, kdim = x.shape
    in_specs = [
        pl.BlockSpec((tm, kdim), lambda j, i: (i, 0)),
        pl.BlockSpec((pl.Element(kdim), pl.Element(tn)), lambda j, i: (layer * kdim, pl.multiple_of(col0 + j * tn, 128))),
    ]
    args = [x, w_stack.reshape(-1, w_stack.shape[-1])]
    aliases = {}
    if epilogue == "residual":
        in_specs += [
            pl.BlockSpec((tm, tn), lambda j, i: (i, j)),
            pl.BlockSpec((1, 8, tn), lambda j, i: (which_gate, 0, j)),
        ]
        args += [h, mods_l]
        aliases = {2: 0}
    elif epilogue == "token_shift":
        assert tm == TB
        in_specs.append(pl.BlockSpec((2, tn), lambda j, i: (0, j)))
        args.append(mu)
    return pl.pallas_call(
        functools.partial(_mm_cached_kernel, epilogue=epilogue, tm=tm),
        grid=(pl.cdiv(n, tn), m // tm),
        in_specs=in_specs,
        out_specs=pl.BlockSpec((tm, tn), lambda j, i: (i, j)),
        out_shape=jax.ShapeDtypeStruct((m, n), out_dtype),
        scratch_shapes=[pltpu.VMEM((kdim, tn), BF16)],
        input_output_aliases=aliases,
        compiler_params=_cparams(("arbitrary", "arbitrary")),
        name=name,
    )(*args)


def _mm_kernel(*refs, nk, epilogue, tm):
    if epilogue == "residual":
        x_ref, w_ref, h_ref, gate_ref, o_ref = refs[:5]
        rest = refs[5:]
    else:
        x_ref, w_ref, o_ref = refs[:3]
        rest = refs[3:]
        h_ref = gate_ref = None
    k = pl.program_id(2)
    part = jnp.dot(x_ref[...], w_ref[...], preferred_element_type=F32)

    def finish(acc):
        _mm_finish(acc, pl.program_id(0), o_ref, h_ref, gate_ref, epilogue, tm)

    if nk == 1:
        finish(part)
    else:
        acc_ref = rest[0]

        @pl.when(k == 0)
        def _():
            acc_ref[...] = part

        @pl.when(jnp.logical_and(k > 0, k < nk - 1))
        def _():
            acc_ref[...] += part

        @pl.when(k == nk - 1)
        def _():
            finish(acc_ref[...] + part)


def matmul(x, w, *, tn, tk=None, out_dtype=F32, epilogue="plain", h=None, mods_l=None, which_gate=None,
           name="matmul"):
    m, kdim = x.shape
    n = w.shape[1]
    tm = MM_TM
    tk = kdim if tk is None else tk
    nk = kdim // tk
    in_specs = [
        pl.BlockSpec((tm, tk), lambda i, j, k: (i, k)),
        pl.BlockSpec((tk, tn), lambda i, j, k: (k, j)),
    ]
    args = [x, w]
    aliases = {}
    if epilogue == "residual":
        in_specs += [
            pl.BlockSpec((tm, tn), lambda i, j, k: (i, j)),
            pl.BlockSpec((1, 8, tn), lambda i, j, k: (which_gate, 0, j)),
        ]
        args += [h, mods_l]
        aliases = {2: 0}
    scratch = [pltpu.VMEM((tm, tn), F32)] if nk > 1 else []
    return pl.pallas_call(
        functools.partial(_mm_kernel, nk=nk, epilogue=epilogue, tm=tm),
        grid=(m // tm, n // tn, nk),
        in_specs=in_specs,
        out_specs=pl.BlockSpec((tm, tn), lambda i, j, k: (i, j)),
        out_shape=jax.ShapeDtypeStruct((m, n), out_dtype),
        scratch_shapes=scratch,
        input_output_aliases=aliases,
        compiler_params=_cparams(("parallel", "parallel", "arbitrary")),
        name=name,
    )(*args)


RET_NCHUNK = TB // RET_CHUNK
RET_CTX_CHUNKS = CTX // RET_CHUNK


def _log_sigmoid(x):
    return jnp.minimum(x, 0.0) - jnp.log(1.0 + jnp.exp(-jnp.abs(x)))


RET_BATCH = 9


def _ret_kernel(dec_ref, q_ref, k_ref, v_ref, g_ref, o_ref, y_ref):
    hh = pl.program_id(1)
    c = RET_CHUNK
    ri = lax.broadcasted_iota(jnp.int32, (c, c), 0).astype(F32)
    ci = lax.broadcasted_iota(jnp.int32, (c, c), 1).astype(F32)
    kscale = HD ** -0.5
    nt = (((1,), (1,)), ((), ()))
    for direction in range(2):
        lg = _log_sigmoid(jnp.full((c, c), dec_ref[direction, hh], F32))
        diff = (ri - ci) if direction == 0 else (ci - ri)
        intra = jnp.where(diff >= 0, jnp.exp(lg * jnp.maximum(diff, 0.0)), 0.0)
        pos = ri if direction == 0 else (c - 1.0) - ri
        q_decay = jnp.exp(lg * (pos + 1.0))
        k_decay = jnp.exp(lg * ((c - 1.0) - pos))
        chunk_decay = jnp.exp(lg * float(c))
        if direction == 0:
            order = list(range(RET_NCHUNK))
        else:
            order = list(range(RET_CTX_CHUNKS - 1, -1, -1)) + list(range(RET_NCHUNK - 1, RET_CTX_CHUNKS - 1, -1))
        state = jnp.zeros((c, c), F32)
        for b0 in range(0, RET_NCHUNK, RET_BATCH):
            rows = [slice(ch * c, (ch + 1) * c) for ch in order[b0:b0 + RET_BATCH]]
            n = range(len(rows))
            q = [q_ref[r, :] for r in rows]
            k = [k_ref[r, :] * kscale for r in rows]
            v = [v_ref[r, :].astype(BF16) for r in rows]
            sc = [lax.dot_general(q[i].astype(BF16), k[i].astype(BF16), nt, preferred_element_type=F32) * intra
                  for i in n]
            y_in = [jnp.dot(sc[i].astype(BF16), v[i], preferred_element_type=F32) for i in n]
            kv = [jnp.dot((k[i] * k_decay).T.astype(BF16), v[i], preferred_element_type=F32) for i in n]
            states = []
            for i in n:
                states.append(state)
                state = state * chunk_decay + kv[i]
            y_x = [jnp.dot((q[i] * q_decay).astype(BF16), states[i].astype(BF16), preferred_element_type=F32)
                   for i in n]
            for i in n:
                y = y_in[i] + y_x[i]
                if direction == 0:
                    y_ref[rows[i], :] = y
                else:
                    y = y + y_ref[rows[i], :]
                    mean = jnp.mean(y, axis=-1, keepdims=True)
                    yc = y - mean
                    var = jnp.mean(yc * yc, axis=-1, keepdims=True)
                    g = g_ref[rows[i], :]
                    o_ref[rows[i], :] = (g * jax.nn.sigmoid(g) * (yc * lax.rsqrt(var + RET_GN_EPS))).astype(o_ref.dtype)


def retention_mixer(p_ret, decay_l):
    nh = RET_HEADS
    blk = lambda off: pl.BlockSpec((TB, HD), lambda b, h, off=off: (b, off + h))
    rows = p_ret.shape[0]
    return pl.pallas_call(
        _ret_kernel,
        grid=(rows // TB, nh),
        in_specs=[pl.BlockSpec(memory_space=pltpu.SMEM), blk(0), blk(nh), blk(2 * nh), blk(3 * nh)],
        out_specs=pl.BlockSpec((TB, HD), lambda b, h: (b, h)),
        out_shape=jax.ShapeDtypeStruct((rows, W), BF16),
        scratch_shapes=[pltpu.VMEM((TB, HD), F32)],
        compiler_params=_cparams(("parallel", "parallel")),
        name="retention",
    )(decay_l, p_ret, p_ret, p_ret, p_ret)


SWA_NBLK = SEQ // SWA_BLOCK
SWA_WIN = 3 * SWA_BLOCK
SWA_BATCH = 4


def _rope(x, cos, sin_signed):
    lane = lax.broadcasted_iota(jnp.int32, x.shape, 1)
    partner = jnp.where(lane % 64 < 32, pltpu.roll(x, 96, 1), pltpu.roll(x, 32, 1))
    return x * cos + partner * sin_signed


def _swa_kernel(sink_ref, q_ref, k_ref, v_ref, cos_ref, sin_ref, o_ref, kt_ref, vb_ref):
    kvh = pl.program_id(1)
    scale = HD ** -0.5
    g = SWA_GROUP
    blk = SWA_BLOCK

    vb_ref[...] = v_ref[...].astype(BF16)
    for j in range(SWA_NBLK):
        rows = slice(j * blk, (j + 1) * blk)
        k_rot = _rope(k_ref[CTX + j * blk:CTX + (j + 1) * blk, :], cos_ref[rows, :], sin_ref[rows, :])
        kt_ref[j] = k_rot.T.astype(BF16)
    kc_t = jnp.concatenate([k_ref[j * blk:(j + 1) * blk, :].T for j in range(CTX // blk)], axis=1).astype(BF16)
    vc = vb_ref[:CTX, :]

    def sink_col(rows_per_head):
        parts = [jnp.full((rows_per_head, 1), sink_ref[kvh * g + gi], F32) for gi in range(g)]
        return jnp.concatenate(parts, axis=0)

    def lane_tiles(x):
        return [x[:, j * HD:(j + 1) * HD] for j in range(x.shape[1] // HD)]

    def row_max(*xs):
        tiles = [t for x in xs for t in lane_tiles(x)]
        return jnp.max(functools.reduce(jnp.maximum, tiles), axis=-1, keepdims=True)

    def row_sum(*xs):
        tiles = [t for x in xs for t in lane_tiles(x)]
        return jnp.sum(functools.reduce(jnp.add, tiles), axis=-1, keepdims=True)

    qc = jnp.concatenate([q_ref[:CTX, gi * HD:(gi + 1) * HD] for gi in range(g)], axis=0).astype(BF16)
    s = jnp.dot(qc, kc_t, preferred_element_type=F32) * scale
    sk = sink_col(CTX)
    m = jnp.maximum(row_max(s), sk)
    e = jnp.exp(s - m)
    den = row_sum(e) + jnp.exp(sk - m)
    oc = jnp.dot(e.astype(BF16), vc, preferred_element_type=F32) / den
    for gi in range(g):
        o_ref[:CTX, gi * HD:(gi + 1) * HD] = oc[gi * CTX:(gi + 1) * CTX].astype(o_ref.dtype)

    sk_b = sink_col(blk)
    delta = (lax.broadcasted_iota(jnp.int32, (g * blk, SWA_WIN), 1)
             - lax.broadcasted_iota(jnp.int32, (g * blk, SWA_WIN), 0) % blk)

    for n0 in range(0, SWA_NBLK, SWA_BATCH):
        ns = list(range(n0, n0 + SWA_BATCH))
        idx = range(len(ns))
        r0 = [n * blk for n in ns]
        j0 = [min(max(n - 1, 0), SWA_NBLK - SWA_WIN // blk) for n in ns]
        qn = [jnp.concatenate([_rope(q_ref[CTX + r0[i]:CTX + r0[i] + blk, gi * HD:(gi + 1) * HD],
                                     cos_ref[r0[i]:r0[i] + blk, :], sin_ref[r0[i]:r0[i] + blk, :])
                               for gi in range(g)], axis=0).astype(BF16) for i in idx]
        kw_t = [jnp.concatenate([kt_ref[j0[i] + w] for w in range(SWA_WIN // blk)], axis=1) for i in idx]
        s_win = [jnp.dot(qn[i], kw_t[i], preferred_element_type=F32) * scale for i in idx]
        s_win = [jnp.where(jnp.abs(delta + (j0[i] * blk - r0[i])) <= SWA_WINDOW, s_win[i], NEG_INF) for i in idx]
        s_ctx = [jnp.dot(qn[i], kc_t, preferred_element_type=F32) * scale for i in idx]
        m = [jnp.maximum(row_max(s_win[i], s_ctx[i]), sk_b) for i in idx]
        e_win = [jnp.exp(s_win[i] - m[i]) for i in idx]
        e_ctx = [jnp.exp(s_ctx[i] - m[i]) for i in idx]
        den = [row_sum(e_win[i], e_ctx[i]) + jnp.exp(sk_b - m[i]) for i in idx]
        o = [(jnp.dot(e_win[i].astype(BF16), vb_ref[CTX + j0[i] * blk:CTX + j0[i] * blk + SWA_WIN, :],
                      preferred_element_type=F32)
              + jnp.dot(e_ctx[i].astype(BF16), vc, preferred_element_type=F32)) / den[i] for i in idx]
        for i in idx:
            for gi in range(g):
                o_ref[CTX + r0[i]:CTX + r0[i] + blk, gi * HD:(gi + 1) * HD] = (
                    o[i][gi * blk:(gi + 1) * blk].astype(o_ref.dtype))


def swa_mixer(p_swa, sink_l, cos_full, sin_signed):
    gw = SWA_GROUP * HD
    rows = p_swa.shape[0]
    return pl.pallas_call(
        _swa_kernel,
        grid=(rows // TB, SWA_KV_HEADS),
        in_specs=[
            pl.BlockSpec(memory_space=pltpu.SMEM),
            pl.BlockSpec((TB, gw), lambda b, kv: (b, kv)),
            pl.BlockSpec((TB, HD), lambda b, kv: (b, SWA_Q_HEADS + kv)),
            pl.BlockSpec((TB, HD), lambda b, kv: (b, SWA_Q_HEADS + SWA_KV_HEADS + kv)),
            pl.BlockSpec((SEQ, HD), lambda b, kv: (0, 0)),
            pl.BlockSpec((SEQ, HD), lambda b, kv: (0, 0)),
        ],
        out_specs=pl.BlockSpec((TB, gw), lambda b, kv: (b, kv)),
        out_shape=jax.ShapeDtypeStruct((rows, W), BF16),
        scratch_shapes=[pltpu.VMEM((SWA_NBLK, HD, SWA_BLOCK), BF16), pltpu.VMEM((TB, HD), BF16)],
        compiler_params=_cparams(("parallel", "parallel")),
        name="swa",
    )(sink_l, p_swa, p_swa, p_swa, cos_full, sin_signed)


def rope_tables():
    rows = SEQ // GRID_W
    row = jnp.repeat(jnp.arange(rows), GRID_W).astype(F32)
    col = jnp.tile(jnp.arange(GRID_W), rows).astype(F32)
    n_freq = HD // 4
    inv_freq = ROPE_BASE ** (-jnp.arange(n_freq, dtype=F32) / n_freq)
    ang_r = row[:, None] * inv_freq
    ang_c = col[:, None] * inv_freq
    cr, sr, cc, sc = jnp.cos(ang_r), jnp.sin(ang_r), jnp.cos(ang_c), jnp.sin(ang_c)
    cos_full = jnp.concatenate([cr, cr, cc, cc], axis=-1)
    sin_signed = jnp.concatenate([-sr, sr, -sc, sc], axis=-1)
    return cos_full, sin_signed


RW_NCHUNK = TB // RW_C
RW_CTX_CHUNKS = CTX // RW_C
RW_GPC = RW_GW // RW_C


def _rw_chunk(direction, s):
    back = jnp.where(s < RW_CTX_CHUNKS, RW_CTX_CHUNKS - 1 - s, RW_NCHUNK - 1 + RW_CTX_CHUNKS - s)
    return jnp.where(direction == 0, s, back)


def _head_ones():
    r = lax.broadcasted_iota(jnp.int32, (RW_GW, RW_GW), 0) // RW_N
    c = lax.broadcasted_iota(jnp.int32, (RW_GW, RW_GW), 1) // RW_N
    return r == c


def _dot16(a, b):
    return jnp.dot(a.astype(BF16), b.astype(BF16), preferred_element_type=F32)


def _dot16_nt(a, b):
    return lax.dot_general(a.astype(BF16), b.astype(BF16), (((1,), (1,)), ((), ())), preferred_element_type=F32)


def _split_bf16(x, terms):
    parts = []
    for _ in range(terms):
        p = x.astype(BF16)
        parts.append(p)
        x = x - p.astype(F32)
    return parts


def _dot_exact_lhs(x, b16, terms):
    return jnp.dot(jnp.concatenate(_split_bf16(x, terms), axis=1), jnp.concatenate([b16] * terms, axis=0),
                   preferred_element_type=F32)


def _rwkv_kernel(r_ref, k_ref, v_ref, sm_ref, wup_ref, aup_ref, w0_ref, a0_ref, kk_ref, ka_ref, rk_ref, gup_ref,
                 lng_ref, o_ref, q_s, y1_s, m_s, z_s, bonus_s, gate_s, st_ref, ybuf):
    direction = pl.program_id(1)
    step = pl.program_id(2)
    cur = step % 2
    prev = 1 - cur
    c = RW_C
    same_head = _head_ones()
    mask16 = jnp.where(same_head, 1.0, 0.0).astype(BF16)
    nbat = r_ref.shape[0]
    chains = [(bb, g) for bb in range(nbat) for g in range(RW_GROUPS)]
    groups = range(len(chains))
    bbs = [bb for bb, _ in chains]
    sls = [slice(g * RW_GW, (g + 1) * RW_GW) for _, g in chains]

    @pl.when(step == 0)
    def _():
        st_ref[...] = jnp.zeros_like(st_ref)
        for ref in (q_s, y1_s, m_s, z_s, bonus_s, gate_s):
            ref[1] = jnp.zeros(ref.shape[1:], ref.dtype)

    ys = []
    for g in groups:
        m_bd = jnp.concatenate([m_s[prev, bbs[g], :, sls[g]]] * RW_GPC, axis=0) * mask16
        z_bd = jnp.where(same_head, jnp.concatenate([z_s[prev, bbs[g], :, sls[g]]] * RW_GPC, axis=0), 0.0)
        res = _dot16(jnp.concatenate([q_s[prev, bbs[g], :, sls[g]], m_bd], axis=0), st_ref[g])
        ys.append(res[:c] + y1_s[prev, bbs[g], :, sls[g]])
        st_ref[g] = res[c:] + z_bd

    def bd(x):
        return jnp.concatenate([x.astype(BF16)] * RW_GPC, axis=0) * mask16

    sgn = 1 - 2 * direction
    cum_terms = 2
    tt = lax.broadcasted_iota(jnp.int32, (c, cum_terms * c), 0)
    ss = lax.broadcasted_iota(jnp.int32, (c, cum_terms * c), 1) % c
    tri16 = jnp.where((tt - ss) * sgn >= 0, 1.0, 0.0).astype(BF16)
    t4 = lax.broadcasted_iota(jnp.int32, (c, RW_GW), 0)
    s4 = lax.broadcasted_iota(jnp.int32, (c, RW_GW), 1) % c
    d4 = (t4 - s4) * sgn
    strict = d4 > 0
    incl = d4 >= 0
    eye = jnp.where(d4 == 0, 1.0, 0.0)

    sm = [sm_ref[bb] for bb in range(nbat)]
    tanh_wd = [jnp.tanh(x[:, :2 * RW_RANK]).astype(BF16) for x in sm]
    ad = [x[:, 2 * RW_RANK:4 * RW_RANK].astype(BF16) for x in sm]

    r = [r_ref[bbs[g], :, sls[g]] for g in groups]
    k = [k_ref[bbs[g], :, sls[g]] for g in groups]
    v = [v_ref[bbs[g], :, sls[g]] for g in groups]
    logw = [-jax.nn.sigmoid(w0_ref[0, :, sls[g]] + jnp.dot(tanh_wd[bbs[g]], wup_ref[0, :, sls[g]],
                                                          preferred_element_type=F32)) * math.exp(-0.5)
            for g in groups]

    def alpha_of(d, g):
        return jax.nn.sigmoid(a0_ref[d, :, sls[g]] + jnp.dot(ad[bbs[g]], aup_ref[d, :, sls[g]],
                                                             preferred_element_type=F32))

    alpha = [alpha_of(direction, g) for g in groups]
    alpha_both = [alpha[g] + alpha_of(1 - direction, g) for g in groups]
    kkp = [k[g] * kk_ref[:, sls[g]] for g in groups]
    kdir = [k[g] * (1.0 + (alpha[g] - 1.0) * ka_ref[:, sls[g]]) for g in groups]
    kboth = [k[g] * (2.0 + (alpha_both[g] - 2.0) * ka_ref[:, sls[g]]) for g in groups]
    nch = len(chains)
    sums = _dot_exact_lhs(jnp.concatenate([kkp[g] * kkp[g] for g in groups]
                                          + [r[g] * kboth[g] * rk_ref[:, sls[g]] for g in groups], axis=0), mask16, 2)
    kk = [kkp[g] / jnp.maximum(jnp.sqrt(sums[g * c:(g + 1) * c]), 1e-12) for g in groups]
    for g in groups:
        bonus_s[cur, bbs[g], :, sls[g]] = sums[(nch + g) * c:(nch + g + 1) * c] * v[g]
    for bb in range(nbat):
        gate_s[cur, bb] = jnp.dot(jax.nn.sigmoid(sm[bb][:, 4 * RW_RANK:]).astype(BF16), gup_ref[...],
                                  preferred_element_type=F32)
    b_vec = [kk[g] * alpha[g] for g in groups]

    cum = [jnp.dot(tri16, jnp.concatenate(_split_bf16(logw[g], cum_terms), axis=0), preferred_element_type=F32)
           for g in groups]
    total = [jnp.sum(logw[g], axis=0, keepdims=True) for g in groups]
    e_out = [jnp.exp(-cum[g]) for g in groups]
    e_left = [jnp.exp(total[g] - cum[g]) for g in groups]
    a_t = [-kk[g] * jnp.exp(cum[g] - logw[g]) for g in groups]
    r_t = [r[g] * jnp.exp(cum[g]) for g in groups]
    lhs = [jnp.concatenate([a_t[g], r_t[g]], axis=0) for g in groups]
    gb = [_dot16_nt(lhs[g], bd(b_vec[g] * e_out[g])) for g in groups]
    gk = [_dot16_nt(lhs[g], bd(kdir[g] * e_out[g])) for g in groups]
    a_ab = [jnp.where(strict, gb[g][:c], 0.0) for g in groups]
    a_ak = [jnp.where(strict, gk[g][:c], 0.0) for g in groups]
    a_rb = [jnp.where(incl, gb[g][c:], 0.0) for g in groups]
    a_rk = [jnp.where(incl, gk[g][c:], 0.0) for g in groups]

    pt = jnp.where(direction == 0, t4, c - 1 - t4)
    ps = jnp.where(direction == 0, s4, c - 1 - s4)
    px = pt ^ ps

    def level_mask(lv):
        return (px >> lv) * 2 + ((pt >> lv) & 1) == 3

    tinv = [eye + jnp.where(level_mask(0), a_ab[g], 0.0) for g in groups]
    for lv in range(1, int(math.log2(c))):
        lm = level_mask(lv)
        cross = [_dot16(jnp.where(lm, a_ab[g], 0.0), bd(tinv[g])) for g in groups]
        tinv = [tinv[g] + _dot16(tinv[g], bd(cross[g])) for g in groups]

    eye16 = eye.astype(BF16)
    blt = [_dot16_nt(eye16, bd(b_vec[g] * e_left[g])) for g in groups]
    klt = [_dot16_nt(eye16, bd(kdir[g] * e_left[g])) for g in groups]
    vprod = [_dot16(jnp.concatenate([a_ak[g], a_rk[g], klt[g]], axis=0), bd(v[g])) for g in groups]
    pw = [_dot16(tinv[g], jnp.concatenate([bd(a_t[g]), bd(vprod[g][:c])], axis=1)) for g in groups]
    top = [_dot16(jnp.concatenate([a_rb[g], blt[g]], axis=0),
                  jnp.concatenate([bd(pw[g][:, :RW_GW]), bd(pw[g][:, RW_GW:])], axis=1)) for g in groups]
    low = [vprod[g][c:] for g in groups]
    for g in groups:
        sl = sls[g]
        q_s[cur, bbs[g], :, sl] = (r_t[g] + top[g][:c, :RW_GW]).astype(q_s.dtype)
        y1_s[cur, bbs[g], :, sl] = top[g][:c, RW_GW:] + low[g][:c]
        m_s[cur, bbs[g], :, sl] = (eye * jnp.exp(total[g]) + top[g][c:, :RW_GW]).astype(m_s.dtype)
        z_s[cur, bbs[g], :, sl] = top[g][c:, RW_GW:] + low[g][c:]

    chunk_prev = _rw_chunk(direction, jnp.maximum(step - 1, 0))

    @pl.when(direction == 0)
    def _():
        for g in groups:
            ybuf[bbs[g], chunk_prev, :, sls[g]] = ys[g].astype(ybuf.dtype)

    @pl.when(direction == 1)
    def _():
        y4 = jnp.concatenate([ys[g] + ybuf[bbs[g], chunk_prev, :, sls[g]] for g in groups], axis=0)
        mean = _dot_exact_lhs(y4, mask16, 2) * (1.0 / RW_N)
        yc = y4 - mean
        var = _dot_exact_lhs(yc * yc, mask16, 2) * (1.0 / RW_N)
        yn = yc * lax.rsqrt(var + RW_GN_EPS)
        for g in groups:
            sl = sls[g]
            o_ref[bbs[g], :, sl] = ((yn[g * c:(g + 1) * c] * lng_ref[:, sl] + bonus_s[prev, bbs[g], :, sl])
                                    * gate_s[prev, bbs[g], :, sl]).astype(o_ref.dtype)


RW_BPS = 4


def rwkv_mixer(ps, wup2, aup2, w0, a0, k_k, k_a, r_k, g_up, ln_g):
    rows = ps.shape[0]
    nb = rows // TB
    nsteps = RW_NCHUNK + 1

    def in_blk(d, s):
        return _rw_chunk(d, jnp.minimum(s, RW_NCHUNK - 1))

    def out_blk(d, s):
        return _rw_chunk(1, jnp.where(d == 0, 0, jnp.maximum(s - 1, 0)))

    feat = lambda off: pl.BlockSpec((RW_BPS, RW_C, W), lambda b, d, s, off=off: (b, in_blk(d, s), off))
    vec = pl.BlockSpec((1, W), lambda b, d, s: (0, 0))
    whole = lambda shape: pl.BlockSpec(shape, lambda b, d, s: (0,) * len(shape))
    slot = lambda dt: pltpu.VMEM((2, RW_BPS, RW_C, W), dt)
    out = pl.pallas_call(
        _rwkv_kernel,
        grid=(nb // RW_BPS, 2, nsteps),
        in_specs=[feat(0), feat(1), feat(2),
                  pl.BlockSpec((RW_BPS, RW_C, RW_SMALL), lambda b, d, s: (b, in_blk(d, s), 3 * W // RW_SMALL)),
                  pl.BlockSpec((1, 2 * RW_RANK, W), lambda b, d, s: (d, 0, 0)), whole((2, 2 * RW_RANK, W)),
                  pl.BlockSpec((1, 1, W), lambda b, d, s: (d, 0, 0)), whole((2, 1, W)), vec, vec, vec,
                  whole((RW_G_RANK, W)), vec],
        out_specs=pl.BlockSpec((RW_BPS, RW_C, W), lambda b, d, s: (b, out_blk(d, s), 0)),
        out_shape=jax.ShapeDtypeStruct((nb, TB, W), BF16),
        scratch_shapes=[slot(BF16), slot(F32), slot(BF16), slot(F32), slot(F32), slot(F32),
                        pltpu.VMEM((RW_BPS * RW_GROUPS, RW_GW, RW_GW), F32),
                        pltpu.VMEM((RW_BPS, RW_NCHUNK, RW_C, W), BF16)],
        compiler_params=_cparams(("arbitrary", "arbitrary", "arbitrary")),
        name="rwkv_mixer",
    )(*[ps.reshape(nb, TB, RWKV_IN)] * 4, wup2, aup2, w0, a0, k_k, k_a, r_k,
      g_up, ln_g)
    return out.reshape(rows, W)


MG_TM = 1152
MG_TN = 512


def _merge_kernel(y0_ref, y1_ref, y2_ref, wb_ref, g0_ref, g1_ref, g2_ref, o_ref, wc_ref):
    @pl.when(pl.program_id(1) == 0)
    def _():
        wc_ref[...] = wb_ref[...].astype(BF16)

    acc = None
    for y_ref, g_ref, n in ((y0_ref, g0_ref, 0), (y1_ref, g1_ref, 1), (y2_ref, g2_ref, 2)):
        proj = jnp.dot(y_ref[...], wc_ref[n], preferred_element_type=F32)
        term = jax.nn.sigmoid(g_ref[...]) * proj
        acc = term if acc is None else acc + term
    o_ref[...] = acc.astype(o_ref.dtype)


def merge_branches(y_ret, y_swa, y_rwkv, p_gate, w_branch, layer):
    nj = D // MG_TN
    ysp = pl.BlockSpec((MG_TM, W), lambda j, i: (i, 0))
    gsp = lambda n: pl.BlockSpec((MG_TM, MG_TN), lambda j, i, n=n: (i, n * nj + j))
    return pl.pallas_call(
        _merge_kernel,
        grid=(nj, R // MG_TM),
        in_specs=[ysp, ysp, ysp, pl.BlockSpec((None, 3, W, MG_TN), lambda j, i: (layer, 0, 0, j)),
                  gsp(0), gsp(1), gsp(2)],
        out_specs=pl.BlockSpec((MG_TM, MG_TN), lambda j, i: (i, j)),
        out_shape=jax.ShapeDtypeStruct((R, D), BF16),
        scratch_shapes=[pltpu.VMEM((3, W, MG_TN), BF16)],
        compiler_params=_cparams(("arbitrary", "arbitrary")),
        name="merge",
    )(y_ret, y_swa, y_rwkv, w_branch, p_gate, p_gate, p_gate)


def kernel(x, c, ctx, c_ctx, norm1_g, norm2_g, w_mod, b_mod, w_in, ret_decay, swa_sink, rwkv_mu, rwkv_w0,
           rwkv_w_up, rwkv_a0, rwkv_a_up, rwkv_g_up, rwkv_k_k, rwkv_k_a, rwkv_r_k, rwkv_ln_g, w_branch,
           w_out, w_ff1, w_ff2, final_g):
    h = jnp.concatenate([ctx, x], axis=1).reshape(R, D)
    cond8 = jnp.concatenate([c, c_ctx[None, :], jnp.zeros((8 - B - 1, D), F32)], axis=0)
    mods = mod_vectors(cond8, w_mod, b_mod)
    mods = mods.reshape(L, 8, 6, D).transpose(0, 2, 1, 3)
    cos_full, sin_signed = rope_tables()

    o_swa = RET_IN
    o_rw = RET_IN + SWA_IN
    o_small = o_rw + 3 * W
    o_gate = o_rw + RWKV_IN
    zeros_lora = jnp.zeros((L, RW_RANK, W), BF16)

    for l in range(L):
        m_l = mods[l]
        u = norm_modulate(h, norm1_g[l], m_l, 0, 1)
        p_ret = matmul_f32w(u, w_in, l, 0, RET_IN, tn=1024, name="in_ret")
        p_swa = matmul_f32w(u, w_in, l, o_swa, SWA_IN, tn=768, name="in_swa")
        ps_rwkv = matmul_f32w(u, w_in, l, o_rw, RWKV_IN, tn=512, tm=TB, epilogue="token_shift", mu=rwkv_mu[l],
                              name="in_rwkv")
        p_gate = matmul_f32w(u, w_in, l, o_gate, GATE_IN, tn=1024, name="in_gate")

        y_ret = retention_mixer(p_ret, ret_decay[l])
        y_swa = swa_mixer(p_swa, swa_sink[l], cos_full, sin_signed)

        wup = rwkv_w_up[l].astype(BF16)
        aup = rwkv_a_up[l].astype(BF16)
        z = zeros_lora[l]
        wup2 = jnp.stack([jnp.concatenate([wup[0], z], axis=0), jnp.concatenate([z, wup[1]], axis=0)])
        aup2 = jnp.stack([jnp.concatenate([aup[0], z], axis=0), jnp.concatenate([z, aup[1]], axis=0)])
        y_rwkv = rwkv_mixer(
            ps_rwkv, wup2, aup2, rwkv_w0[l].reshape(2, 1, W), rwkv_a0[l].reshape(2, 1, W),
            rwkv_k_k[l].reshape(1, W), rwkv_k_a[l].reshape(1, W), rwkv_r_k[l].reshape(1, W),
            rwkv_g_up[l].astype(BF16), rwkv_ln_g[l].reshape(1, W))

        merged = merge_branches(y_ret, y_swa, y_rwkv, p_gate, w_branch, l)
        h = matmul_f32w(merged, w_out, l, 0, D, tn=1024, epilogue="residual", h=h, mods_l=m_l,
                        which_gate=2, name="out_proj")
        u2 = norm_modulate(h, norm2_g[l], m_l, 3, 4)
        f = matmul_f32w(u2, w_ff1, l, 0, D_FF, tn=1024, out_dtype=BF16, epilogue="relu2", name="ff1")
        h = matmul(f, w_ff2[l].astype(BF16), tn=512, tk=4096, epilogue="residual", h=h, mods_l=m_l,
                   which_gate=5, name="ff2")

    return final_norm(h, final_g).reshape(B, SEQ, D)
```

```python
import functools
import math

import jax
import jax.numpy as jnp
from jax import lax
from jax.experimental import pallas as pl
from jax.experimental.pallas import tpu as pltpu

F32 = jnp.float32
BF16 = jnp.bfloat16

D = 2048
B = 4
SEQ = 2048
CTX = 256
TB = CTX + SEQ
R = B * TB
L = 4
GRID_W = 64
EPS = 1e-6
ROPE_BASE = 10000.0
NEG_INF = -1e30
W = D // 2
HD = 128
RET_HEADS = W // HD
RET_CHUNK = 128
RET_GN_EPS = 1e-5
SWA_Q_HEADS = W // HD
SWA_KV_HEADS = SWA_Q_HEADS // 4
SWA_GROUP = SWA_Q_HEADS // SWA_KV_HEADS
SWA_WINDOW = 128
SWA_BLOCK = 128
RW_N = 64
RW_RANK = 64
RW_G_RANK = 128
RW_GN_EPS = 64e-5
RW_C = 64
RW_GW = 256
RW_GROUPS = W // RW_GW
RW_SMALL = 4 * RW_RANK + RW_G_RANK
D_FF = 4 * D
RET_IN = 4 * W
SWA_IN = (SWA_Q_HEADS + 2 * SWA_KV_HEADS) * HD
RWKV_IN = 3 * W + RW_SMALL
GATE_IN = 3 * D
N_IN = RET_IN + SWA_IN + RWKV_IN + GATE_IN

VMEM_LIMIT = 56 * 1024 * 1024


def _cparams(sem):
    return pltpu.CompilerParams(dimension_semantics=sem, vmem_limit_bytes=VMEM_LIMIT)


def _mod_kernel(x_ref, w_ref, b_ref, o_ref):
    x = x_ref[...]
    s = x * jax.nn.sigmoid(x)
    acc = jnp.dot(s.astype(BF16), w_ref[0].astype(BF16), preferred_element_type=F32)
    o_ref[0] = acc + b_ref[0]


def mod_vectors(cond8, w_mod, b_mod):
    tn = 1024
    n = w_mod.shape[-1]
    return pl.pallas_call(
        _mod_kernel,
        grid=(L, n // tn),
        in_specs=[
            pl.BlockSpec((8, D), lambda l, j: (0, 0)),
            pl.BlockSpec((1, D, tn), lambda l, j: (l, 0, j)),
            pl.BlockSpec((1, 1, tn), lambda l, j: (l, 0, j)),
        ],
        out_specs=pl.BlockSpec((1, 8, tn), lambda l, j: (l, 0, j)),
        out_shape=jax.ShapeDtypeStruct((L, 8, n), F32),
        compiler_params=_cparams(("parallel", "parallel")),
        name="mod_vectors",
    )(cond8, w_mod, b_mod.reshape(L, 1, n))


NORM_TM = 256
TILES_PER_BATCH = TB // NORM_TM


NORM_MOD_TM = 768


def _norm_mod_kernel(x_ref, g_ref, sh_ref, sc_ref, o_ref):
    i = pl.program_id(0)
    tm = NORM_MOD_TM
    bidx = i // (TB // tm)
    is_ctx = _row_is_ctx(i, tm)
    x = x_ref[...]
    ms = jnp.mean(x * x, axis=-1, keepdims=True)
    y = x * lax.rsqrt(ms + EPS) * g_ref[...]
    shift = jnp.where(is_ctx, sh_ref[0, B:B + 1, :], sh_ref[0, pl.ds(bidx, 1), :])
    scale = jnp.where(is_ctx, sc_ref[0, B:B + 1, :], sc_ref[0, pl.ds(bidx, 1), :])
    o_ref[...] = (y * (1.0 + scale) + shift).astype(o_ref.dtype)


def norm_modulate(h, g, mods_l, which_shift, which_scale):
    return pl.pallas_call(
        _norm_mod_kernel,
        grid=(R // NORM_MOD_TM,),
        in_specs=[
            pl.BlockSpec((NORM_MOD_TM, D), lambda i: (i, 0)),
            pl.BlockSpec((1, D), lambda i: (0, 0)),
            pl.BlockSpec((1, 8, D), lambda i: (which_shift, 0, 0)),
            pl.BlockSpec((1, 8, D), lambda i: (which_scale, 0, 0)),
        ],
        out_specs=pl.BlockSpec((NORM_MOD_TM, D), lambda i: (i, 0)),
        out_shape=jax.ShapeDtypeStruct((R, D), BF16),
        compiler_params=_cparams(("parallel",)),
        name="norm_modulate",
    )(h, g.reshape(1, D), mods_l, mods_l)


def _final_norm_kernel(x_ref, g_ref, o_ref):
    x = x_ref[...]
    ms = jnp.mean(x * x, axis=-1, keepdims=True)
    o_ref[...] = x * lax.rsqrt(ms + EPS) * g_ref[...]


def final_norm(h, g):
    per = SEQ // NORM_TM
    return pl.pallas_call(
        _final_norm_kernel,
        grid=(B, per),
        in_specs=[
            pl.BlockSpec((NORM_TM, D), lambda b, j: (b * TILES_PER_BATCH + CTX // NORM_TM + j, 0)),
            pl.BlockSpec((1, D), lambda b, j: (0, 0)),
        ],
        out_specs=pl.BlockSpec((NORM_TM, D), lambda b, j: (b * per + j, 0)),
        out_shape=jax.ShapeDtypeStruct((B * SEQ, D), F32),
        compiler_params=_cparams(("parallel", "parallel")),
        name="final_norm",
    )(h, g.reshape(1, D))


MM_TM = 1152


def _row_is_ctx(i, tm):
    rows = lax.broadcasted_iota(jnp.int32, (tm, 1), 0)
    return jnp.logical_and(i % (TB // tm) == 0, rows < CTX)


def _mm_finish(acc, i, o_ref, h_ref, gate_ref, epilogue, tm):
    if epilogue == "relu2":
        a = jnp.maximum(acc, 0.0)
        o_ref[...] = (a * a).astype(o_ref.dtype)
    elif epilogue == "residual":
        bidx = i // (TB // tm)
        g_b = gate_ref[0, pl.ds(bidx, 1), :]
        g_c = gate_ref[0, pl.ds(B, 1), :]
        gate = jnp.where(_row_is_ctx(i, tm), g_c, g_b)
        o_ref[...] = h_ref[...] + gate * acc
    elif epilogue == "token_shift":
        o_ref[...] = _token_shift(acc, gate_ref)
    else:
        o_ref[...] = acc.astype(o_ref.dtype)


def _token_shift(x, mu_ref):
    row = lax.broadcasted_iota(jnp.int32, (TB, 1), 0)
    prev = pltpu.roll(x, 1, 0)
    prev = jnp.where(jnp.logical_or(row == 0, row == CTX), 0.0, prev)
    nxt = pltpu.roll(x, TB - 1, 0)
    nxt = jnp.where(jnp.logical_or(row == CTX - 1, row == TB - 1), 0.0, nxt)
    return x + mu_ref[0:1, :] * (prev - x) + mu_ref[1:2, :] * (nxt - x)


def _mm_cached_kernel(*refs, epilogue, tm):
    if epilogue == "residual":
        x_ref, w_ref, h_ref, gate_ref, o_ref, wc_ref = refs
    elif epilogue == "token_shift":
        x_ref, w_ref, gate_ref, o_ref, wc_ref = refs
        h_ref = None
    else:
        x_ref, w_ref, o_ref, wc_ref = refs
        h_ref = gate_ref = None
    i = pl.program_id(1)

    @pl.when(i == 0)
    def _():
        wc_ref[...] = w_ref[...].astype(BF16)

    acc = jnp.dot(x_ref[...], wc_ref[...], preferred_element_type=F32)
    _mm_finish(acc, i, o_ref, h_ref, gate_ref, epilogue, tm)


def matmul_f32w(x, w_stack, layer, col0, n, *, tn, tm=MM_TM, out_dtype=F32, epilogue="plain", h=None, mods_l=None,
                which_gate=None, mu=None, name="matmul"):
    m, kdim = x.shape
    in_specs = [
        pl.BlockSpec((tm, kdim), lambda j, i: (i, 0)),
        pl.BlockSpec((pl.Element(kdim), pl.Element(tn)), lambda j, i: (layer * kdim, pl.multiple_of(col0 + j * tn, 128))),
    ]
    args = [x, w_stack.reshape(-1, w_stack.shape[-1])]
    aliases = {}
    if epilogue == "residual":
        in_specs += [
            pl.BlockSpec((tm, tn), lambda j, i: (i, j)),
            pl.BlockSpec((1, 8, tn), lambda j, i: (which_gate, 0, j)),
        ]
        args += [h, mods_l]
        aliases = {2: 0}
    elif epilogue == "token_shift":
        assert tm == TB
        in_specs.append(pl.BlockSpec((2, tn), lambda j, i: (0, j)))
        args.append(mu)
    return pl.pallas_call(
        functools.partial(_mm_cached_kernel, epilogue=epilogue, tm=tm),
        grid=(pl.cdiv(n, tn), m // tm),
        in_specs=in_specs,
        out_specs=pl.BlockSpec((tm, tn), lambda j, i: (i, j)),
        out_shape=jax.ShapeDtypeStruct((m, n), out_dtype),
        scratch_shapes=[pltpu.VMEM((kdim, tn), BF16)],
        input_output_aliases=aliases,
        compiler_params=_cparams(("arbitrary", "arbitrary")),
        name=name,
    )(*args)


def _mm_kernel(*refs, nk, epilogue, tm):
    if epilogue == "residual":
        x_ref, w_ref, h_ref, gate_ref, o_ref = refs[:5]
        rest = refs[5:]
    else:
        x_ref, w_ref, o_ref = refs[:3]
        rest = refs[3:]
        h_ref = gate_ref = None
    k = pl.program_id(2)
    part = jnp.dot(x_ref[...], w_ref[...], preferred_element_type=F32)

    def finish(acc):
        _mm_finish(acc, pl.program_id(0), o_ref, h_ref, gate_ref, epilogue, tm)

    if nk == 1:
        finish(part)
    else:
        acc_ref = rest[0]

        @pl.when(k == 0)
        def _():
            acc_ref[...] = part

        @pl.when(jnp.logical_and(k > 0, k < nk - 1))
        def _():
            acc_ref[...] += part

        @pl.when(k == nk - 1)
        def _():
            finish(acc_ref[...] + part)


def matmul(x, w, *, tn, tm=MM_TM, tk=None, out_dtype=F32, epilogue="plain", h=None, mods_l=None, which_gate=None,
           name="matmul"):
    m, kdim = x.shape
    n = w.shape[1]
    tk = kdim if tk is None else tk
    nk = kdim // tk
    in_specs = [
        pl.BlockSpec((tm, tk), lambda i, j, k: (i, k)),
        pl.BlockSpec((tk, tn), lambda i, j, k: (k, j)),
    ]
    args = [x, w]
    aliases = {}
    if epilogue == "residual":
        in_specs += [
            pl.BlockSpec((tm, tn), lambda i, j, k: (i, j)),
            pl.BlockSpec((1, 8, tn), lambda i, j, k: (which_gate, 0, j)),
        ]
        args += [h, mods_l]
        aliases = {2: 0}
    scratch = [pltpu.VMEM((tm, tn), F32)] if nk > 1 else []
    return pl.pallas_call(
        functools.partial(_mm_kernel, nk=nk, epilogue=epilogue, tm=tm),
        grid=(m // tm, n // tn, nk),
        in_specs=in_specs,
        out_specs=pl.BlockSpec((tm, tn), lambda i, j, k: (i, j)),
        out_shape=jax.ShapeDtypeStruct((m, n), out_dtype),
        scratch_shapes=scratch,
        input_output_aliases=aliases,
        compiler_params=_cparams(("parallel", "parallel", "arbitrary")),
        name=name,
    )(*args)


RET_NCHUNK = TB // RET_CHUNK
RET_CTX_CHUNKS = CTX // RET_CHUNK


def _log_sigmoid(x):
    return jnp.minimum(x, 0.0) - jnp.log(1.0 + jnp.exp(-jnp.abs(x)))


RET_BATCH = 9


def _ret_kernel(dec_ref, q_ref, k_ref, v_ref, g_ref, o_ref, y_ref):
    hh = pl.program_id(1)
    c = RET_CHUNK
    ri = lax.broadcasted_iota(jnp.int32, (c, c), 0).astype(F32)
    ci = lax.broadcasted_iota(jnp.int32, (c, c), 1).astype(F32)
    kscale = HD ** -0.5
    nt = (((1,), (1,)), ((), ()))
    for direction in range(2):
        lg = _log_sigmoid(jnp.full((c, c), dec_ref[direction, hh], F32))
        diff = (ri - ci) if direction == 0 else (ci - ri)
        intra = jnp.where(diff >= 0, jnp.exp(lg * jnp.maximum(diff, 0.0)), 0.0)
        pos = ri if direction == 0 else (c - 1.0) - ri
        q_decay = jnp.exp(lg * (pos + 1.0))
        k_decay = jnp.exp(lg * ((c - 1.0) - pos))
        chunk_decay = jnp.exp(lg * float(c))
        if direction == 0:
            order = list(range(RET_NCHUNK))
        else:
            order = list(range(RET_CTX_CHUNKS - 1, -1, -1)) + list(range(RET_NCHUNK - 1, RET_CTX_CHUNKS - 1, -1))
        state = jnp.zeros((c, c), F32)
        for b0 in range(0, RET_NCHUNK, RET_BATCH):
            rows = [slice(ch * c, (ch + 1) * c) for ch in order[b0:b0 + RET_BATCH]]
            n = range(len(rows))
            q = [q_ref[r, :] for r in rows]
            k = [k_ref[r, :] * kscale for r in rows]
            v = [v_ref[r, :].astype(BF16) for r in rows]
            sc = [lax.dot_general(q[i].astype(BF16), k[i].astype(BF16), nt, preferred_element_type=F32) * intra
                  for i in n]
            y_in = [jnp.dot(sc[i].astype(BF16), v[i], preferred_element_type=F32) for i in n]
            kv = [jnp.dot((k[i] * k_decay).T.astype(BF16), v[i], preferred_element_type=F32) for i in n]
            states = []
            for i in n:
                states.append(state)
                state = state * chunk_decay + kv[i]
            y_x = [jnp.dot((q[i] * q_decay).astype(BF16), states[i].astype(BF16), preferred_element_type=F32)
                   for i in n]
            for i in n:
                y = y_in[i] + y_x[i]
                if direction == 0:
                    y_ref[rows[i], :] = y
                else:
                    y = y + y_ref[rows[i], :]
                    mean = jnp.mean(y, axis=-1, keepdims=True)
                    yc = y - mean
                    var = jnp.mean(yc * yc, axis=-1, keepdims=True)
                    g = g_ref[rows[i], :]
                    o_ref[rows[i], :] = (g * jax.nn.sigmoid(g) * (yc * lax.rsqrt(var + RET_GN_EPS))).astype(o_ref.dtype)


def retention_mixer(p_ret, decay_l):
    nh = RET_HEADS
    blk = lambda off: pl.BlockSpec((TB, HD), lambda b, h, off=off: (b, off + h))
    rows = p_ret.shape[0]
    return pl.pallas_call(
        _ret_kernel,
        grid=(rows // TB, nh),
        in_specs=[pl.BlockSpec(memory_space=pltpu.SMEM), blk(0), blk(nh), blk(2 * nh), blk(3 * nh)],
        out_specs=pl.BlockSpec((TB, HD), lambda b, h: (b, h)),
        out_shape=jax.ShapeDtypeStruct((rows, W), BF16),
        scratch_shapes=[pltpu.VMEM((TB, HD), F32)],
        compiler_params=_cparams(("parallel", "parallel")),
        name="retention",
    )(decay_l, p_ret, p_ret, p_ret, p_ret)


SWA_NBLK = SEQ // SWA_BLOCK
SWA_WIN = 3 * SWA_BLOCK
SWA_BATCH = 4


def _rope(x, cos, sin_signed):
    lane = lax.broadcasted_iota(jnp.int32, x.shape, 1)
    partner = jnp.where(lane % 64 < 32, pltpu.roll(x, 96, 1), pltpu.roll(x, 32, 1))
    return x * cos + partner * sin_signed


def _swa_kernel(sink_ref, q_ref, k_ref, v_ref, cos_ref, sin_ref, o_ref, kt_ref, vb_ref):
    kvh = pl.program_id(1)
    scale = HD ** -0.5
    g = SWA_GROUP
    blk = SWA_BLOCK

    vb_ref[...] = v_ref[...].astype(BF16)
    for j in range(SWA_NBLK):
        rows = slice(j * blk, (j + 1) * blk)
        k_rot = _rope(k_ref[CTX + j * blk:CTX + (j + 1) * blk, :], cos_ref[rows, :], sin_ref[rows, :])
        kt_ref[j] = k_rot.T.astype(BF16)
    kc_t = jnp.concatenate([k_ref[j * blk:(j + 1) * blk, :].T for j in range(CTX // blk)], axis=1).astype(BF16)
    vc = vb_ref[:CTX, :]

    def sink_col(rows_per_head):
        parts = [jnp.full((rows_per_head, 1), sink_ref[kvh * g + gi], F32) for gi in range(g)]
        return jnp.concatenate(parts, axis=0)

    def lane_tiles(x):
        return [x[:, j * HD:(j + 1) * HD] for j in range(x.shape[1] // HD)]

    def row_max(*xs):
        tiles = [t for x in xs for t in lane_tiles(x)]
        return jnp.max(functools.reduce(jnp.maximum, tiles), axis=-1, keepdims=True)

    def row_sum(*xs):
        tiles = [t for x in xs for t in lane_tiles(x)]
        return jnp.sum(functools.reduce(jnp.add, tiles), axis=-1, keepdims=True)

    qc = jnp.concatenate([q_ref[:CTX, gi * HD:(gi + 1) * HD] for gi in range(g)], axis=0).astype(BF16)
    s = jnp.dot(qc, kc_t, preferred_element_type=F32) * scale
    sk = sink_col(CTX)
    m = jnp.maximum(row_max(s), sk)
    e = jnp.exp(s - m)
    den = row_sum(e) + jnp.exp(sk - m)
    oc = jnp.dot(e.astype(BF16), vc, preferred_element_type=F32) / den
    for gi in range(g):
        o_ref[:CTX, gi * HD:(gi + 1) * HD] = oc[gi * CTX:(gi + 1) * CTX].astype(o_ref.dtype)

    sk_b = sink_col(blk)
    delta = (lax.broadcasted_iota(jnp.int32, (g * blk, SWA_WIN), 1)
             - lax.broadcasted_iota(jnp.int32, (g * blk, SWA_WIN), 0) % blk)

    for n0 in range(0, SWA_NBLK, SWA_BATCH):
        ns = list(range(n0, n0 + SWA_BATCH))
        idx = range(len(ns))
        r0 = [n * blk for n in ns]
        j0 = [min(max(n - 1, 0), SWA_NBLK - SWA_WIN // blk) for n in ns]
        qn = [jnp.concatenate([_rope(q_ref[CTX + r0[i]:CTX + r0[i] + blk, gi * HD:(gi + 1) * HD],
                                     cos_ref[r0[i]:r0[i] + blk, :], sin_ref[r0[i]:r0[i] + blk, :])
                               for gi in range(g)], axis=0).astype(BF16) for i in idx]
        kw_t = [jnp.concatenate([kt_ref[j0[i] + w] for w in range(SWA_WIN // blk)], axis=1) for i in idx]
        s_win = [jnp.dot(qn[i], kw_t[i], preferred_element_type=F32) * scale for i in idx]
        s_win = [jnp.where(jnp.abs(delta + (j0[i] * blk - r0[i])) <= SWA_WINDOW, s_win[i], NEG_INF) for i in idx]
        s_ctx = [jnp.dot(qn[i], kc_t, preferred_element_type=F32) * scale for i in idx]
        m = [jnp.maximum(row_max(s_win[i], s_ctx[i]), sk_b) for i in idx]
        e_win = [jnp.exp(s_win[i] - m[i]) for i in idx]
        e_ctx = [jnp.exp(s_ctx[i] - m[i]) for i in idx]
        den = [row_sum(e_win[i], e_ctx[i]) + jnp.exp(sk_b - m[i]) for i in idx]
        o = [(jnp.dot(e_win[i].astype(BF16), vb_ref[CTX + j0[i] * blk:CTX + j0[i] * blk + SWA_WIN, :],
                      preferred_element_type=F32)
              + jnp.dot(e_ctx[i].astype(BF16), vc, preferred_element_type=F32)) / den[i] for i in idx]
        for i in idx:
            for gi in range(g):
                o_ref[CTX + r0[i]:CTX + r0[i] + blk, gi * HD:(gi + 1) * HD] = (
                    o[i][gi * blk:(gi + 1) * blk].astype(o_ref.dtype))


def swa_mixer(p_swa, sink_l, cos_full, sin_signed):
    gw = SWA_GROUP * HD
    rows = p_swa.shape[0]
    return pl.pallas_call(
        _swa_kernel,
        grid=(rows // TB, SWA_KV_HEADS),
        in_specs=[
            pl.BlockSpec(memory_space=pltpu.SMEM),
            pl.BlockSpec((TB, gw), lambda b, kv: (b, kv)),
            pl.BlockSpec((TB, HD), lambda b, kv: (b, SWA_Q_HEADS + kv)),
            pl.BlockSpec((TB, HD), lambda b, kv: (b, SWA_Q_HEADS + SWA_KV_HEADS + kv)),
            pl.BlockSpec((SEQ, HD), lambda b, kv: (0, 0)),
            pl.BlockSpec((SEQ, HD), lambda b, kv: (0, 0)),
        ],
        out_specs=pl.BlockSpec((TB, gw), lambda b, kv: (b, kv)),
        out_shape=jax.ShapeDtypeStruct((rows, W), BF16),
        scratch_shapes=[pltpu.VMEM((SWA_NBLK, HD, SWA_BLOCK), BF16), pltpu.VMEM((TB, HD), BF16)],
        compiler_params=_cparams(("parallel", "parallel")),
        name="swa",
    )(sink_l, p_swa, p_swa, p_swa, cos_full, sin_signed)


def rope_tables():
    rows = SEQ // GRID_W
    row = jnp.repeat(jnp.arange(rows), GRID_W).astype(F32)
    col = jnp.tile(jnp.arange(GRID_W), rows).astype(F32)
    n_freq = HD // 4
    inv_freq = ROPE_BASE ** (-jnp.arange(n_freq, dtype=F32) / n_freq)
    ang_r = row[:, None] * inv_freq
    ang_c = col[:, None] * inv_freq
    cr, sr, cc, sc = jnp.cos(ang_r), jnp.sin(ang_r), jnp.cos(ang_c), jnp.sin(ang_c)
    cos_full = jnp.concatenate([cr, cr, cc, cc], axis=-1)
    sin_signed = jnp.concatenate([-sr, sr, -sc, sc], axis=-1)
    return cos_full, sin_signed


RW_NCHUNK = TB // RW_C
RW_CTX_CHUNKS = CTX // RW_C
RW_GPC = RW_GW // RW_C


def _rw_chunk(direction, s):
    back = jnp.where(s < RW_CTX_CHUNKS, RW_CTX_CHUNKS - 1 - s, RW_NCHUNK - 1 + RW_CTX_CHUNKS - s)
    return jnp.where(direction == 0, s, back)


def _head_ones():
    r = lax.broadcasted_iota(jnp.int32, (RW_GW, RW_GW), 0) // RW_N
    c = lax.broadcasted_iota(jnp.int32, (RW_GW, RW_GW), 1) // RW_N
    return r == c


def _dot16(a, b):
    return jnp.dot(a.astype(BF16), b.astype(BF16), preferred_element_type=F32)


def _dot16_nt(a, b):
    return lax.dot_general(a.astype(BF16), b.astype(BF16), (((1,), (1,)), ((), ())), preferred_element_type=F32)


def _split_bf16(x, terms):
    parts = []
    for _ in range(terms):
        p = x.astype(BF16)
        parts.append(p)
        x = x - p.astype(F32)
    return parts


def _dot_exact_lhs(x, b16, terms):
    return jnp.dot(jnp.concatenate(_split_bf16(x, terms), axis=1), jnp.concatenate([b16] * terms, axis=0),
                   preferred_element_type=F32)


def _rwkv_kernel(r_ref, k_ref, v_ref, sm_ref, wup_ref, aup_ref, w0_ref, a0_ref, kk_ref, ka_ref, rk_ref, gup_ref,
                 lng_ref, o_ref, q_s, y1_s, m_s, z_s, bonus_s, gate_s, st_ref, ybuf):
    direction = pl.program_id(1)
    step = pl.program_id(2)
    cur = step % 2
    prev = 1 - cur
    c = RW_C
    same_head = _head_ones()
    mask16 = jnp.where(same_head, 1.0, 0.0).astype(BF16)
    nbat = r_ref.shape[0]
    chains = [(bb, g) for bb in range(nbat) for g in range(RW_GROUPS)]
    groups = range(len(chains))
    bbs = [bb for bb, _ in chains]
    sls = [slice(g * RW_GW, (g + 1) * RW_GW) for _, g in chains]

    @pl.when(step == 0)
    def _():
        st_ref[...] = jnp.zeros_like(st_ref)
        for ref in (q_s, y1_s, m_s, z_s, bonus_s, gate_s):
            ref[1] = jnp.zeros(ref.shape[1:], ref.dtype)

    ys = []
    for g in groups:
        m_bd = jnp.concatenate([m_s[prev, bbs[g], :, sls[g]]] * RW_GPC, axis=0) * mask16
        z_bd = jnp.where(same_head, jnp.concatenate([z_s[prev, bbs[g], :, sls[g]]] * RW_GPC, axis=0), 0.0)
        res = _dot16(jnp.concatenate([q_s[prev, bbs[g], :, sls[g]], m_bd], axis=0), st_ref[g])
        ys.append(res[:c] + y1_s[prev, bbs[g], :, sls[g]])
        st_ref[g] = res[c:] + z_bd

    def bd(x):
        return jnp.concatenate([x.astype(BF16)] * RW_GPC, axis=0) * mask16

    sgn = 1 - 2 * direction
    cum_terms = 2
    tt = lax.broadcasted_iota(jnp.int32, (c, cum_terms * c), 0)
    ss = lax.broadcasted_iota(jnp.int32, (c, cum_terms * c), 1) % c
    tri16 = jnp.where((tt - ss) * sgn >= 0, 1.0, 0.0).astype(BF16)
    t4 = lax.broadcasted_iota(jnp.int32, (c, RW_GW), 0)
    s4 = lax.broadcasted_iota(jnp.int32, (c, RW_GW), 1) % c
    d4 = (t4 - s4) * sgn
    strict = d4 > 0
    incl = d4 >= 0
    eye = jnp.where(d4 == 0, 1.0, 0.0)

    sm = [sm_ref[bb] for bb in range(nbat)]
    tanh_wd = [jnp.tanh(x[:, :2 * RW_RANK]).astype(BF16) for x in sm]
    ad = [x[:, 2 * RW_RANK:4 * RW_RANK].astype(BF16) for x in sm]

    r = [r_ref[bbs[g], :, sls[g]] for g in groups]
    k = [k_ref[bbs[g], :, sls[g]] for g in groups]
    v = [v_ref[bbs[g], :, sls[g]] for g in groups]
    logw = [-jax.nn.sigmoid(w0_ref[0, :, sls[g]] + jnp.dot(tanh_wd[bbs[g]], wup_ref[0, :, sls[g]],
                                                          preferred_element_type=F32)) * math.exp(-0.5)
            for g in groups]

    def alpha_of(d, g):
        return jax.nn.sigmoid(a0_ref[d, :, sls[g]] + jnp.dot(ad[bbs[g]], aup_ref[d, :, sls[g]],
                                                             preferred_element_type=F32))

    alpha = [alpha_of(direction, g) for g in groups]
    alpha_both = [alpha[g] + alpha_of(1 - direction, g) for g in groups]
    kkp = [k[g] * kk_ref[:, sls[g]] for g in groups]
    kdir = [k[g] * (1.0 + (alpha[g] - 1.0) * ka_ref[:, sls[g]]) for g in groups]
    kboth = [k[g] * (2.0 + (alpha_both[g] - 2.0) * ka_ref[:, sls[g]]) for g in groups]
    nch = len(chains)
    sums = _dot_exact_lhs(jnp.concatenate([kkp[g] * kkp[g] for g in groups]
                                          + [r[g] * kboth[g] * rk_ref[:, sls[g]] for g in groups], axis=0), mask16, 2)
    kk = [kkp[g] / jnp.maximum(jnp.sqrt(sums[g * c:(g + 1) * c]), 1e-12) for g in groups]
    for g in groups:
        bonus_s[cur, bbs[g], :, sls[g]] = sums[(nch + g) * c:(nch + g + 1) * c] * v[g]
    for bb in range(nbat):
        gate_s[cur, bb] = jnp.dot(jax.nn.sigmoid(sm[bb][:, 4 * RW_RANK:]).astype(BF16), gup_ref[...],
                                  preferred_element_type=F32)
    b_vec = [kk[g] * alpha[g] for g in groups]

    cum = [jnp.dot(tri16, jnp.concatenate(_split_bf16(logw[g], cum_terms), axis=0), preferred_element_type=F32)
           for g in groups]
    total = [jnp.sum(logw[g], axis=0, keepdims=True) for g in groups]
    e_out = [jnp.exp(-cum[g]) for g in groups]
    e_left = [jnp.exp(total[g] - cum[g]) for g in groups]
    a_t = [-kk[g] * jnp.exp(cum[g] - logw[g]) for g in groups]
    r_t = [r[g] * jnp.exp(cum[g]) for g in groups]
    lhs = [jnp.concatenate([a_t[g], r_t[g]], axis=0) for g in groups]
    gb = [_dot16_nt(lhs[g], bd(b_vec[g] * e_out[g])) for g in groups]
    gk = [_dot16_nt(lhs[g], bd(kdir[g] * e_out[g])) for g in groups]
    a_ab = [jnp.where(strict, gb[g][:c], 0.0) for g in groups]
    a_ak = [jnp.where(strict, gk[g][:c], 0.0) for g in groups]
    a_rb = [jnp.where(incl, gb[g][c:], 0.0) for g in groups]
    a_rk = [jnp.where(incl, gk[g][c:], 0.0) for g in groups]

    pt = jnp.where(direction == 0, t4, c - 1 - t4)
    ps = jnp.where(direction == 0, s4, c - 1 - s4)
    px = pt ^ ps

    def level_mask(lv):
        return (px >> lv) * 2 + ((pt >> lv) & 1) == 3

    tinv = [eye + jnp.where(level_mask(0), a_ab[g], 0.0) for g in groups]
    for lv in range(1, int(math.log2(c))):
        lm = level_mask(lv)
        cross = [_dot16(jnp.where(lm, a_ab[g], 0.0), bd(tinv[g])) for g in groups]
        tinv = [tinv[g] + _dot16(tinv[g], bd(cross[g])) for g in groups]

    eye16 = eye.astype(BF16)
    blt = [_dot16_nt(eye16, bd(b_vec[g] * e_left[g])) for g in groups]
    klt = [_dot16_nt(eye16, bd(kdir[g] * e_left[g])) for g in groups]
    vprod = [_dot16(jnp.concatenate([a_ak[g], a_rk[g], klt[g]], axis=0), bd(v[g])) for g in groups]
    pw = [_dot16(tinv[g], jnp.concatenate([bd(a_t[g]), bd(vprod[g][:c])], axis=1)) for g in groups]
    top = [_dot16(jnp.concatenate([a_rb[g], blt[g]], axis=0),
                  jnp.concatenate([bd(pw[g][:, :RW_GW]), bd(pw[g][:, RW_GW:])], axis=1)) for g in groups]
    low = [vprod[g][c:] for g in groups]
    for g in groups:
        sl = sls[g]
        q_s[cur, bbs[g], :, sl] = (r_t[g] + top[g][:c, :RW_GW]).astype(q_s.dtype)
        y1_s[cur, bbs[g], :, sl] = top[g][:c, RW_GW:] + low[g][:c]
        m_s[cur, bbs[g], :, sl] = (eye * jnp.exp(total[g]) + top[g][c:, :RW_GW]).astype(m_s.dtype)
        z_s[cur, bbs[g], :, sl] = top[g][c:, RW_GW:] + low[g][c:]

    chunk_prev = _rw_chunk(direction, jnp.maximum(step - 1, 0))

    @pl.when(direction == 0)
    def _():
        for g in groups:
            ybuf[bbs[g], chunk_prev, :, sls[g]] = ys[g].astype(ybuf.dtype)

    @pl.when(direction == 1)
    def _():
        y4 = jnp.concatenate([ys[g] + ybuf[bbs[g], chunk_prev, :, sls[g]] for g in groups], axis=0)
        mean = _dot_exact_lhs(y4, mask16, 2) * (1.0 / RW_N)
        yc = y4 - mean
        var = _dot_exact_lhs(yc * yc, mask16, 2) * (1.0 / RW_N)
        yn = yc * lax.rsqrt(var + RW_GN_EPS)
        for g in groups:
            sl = sls[g]
            o_ref[bbs[g], :, sl] = ((yn[g * c:(g + 1) * c] * lng_ref[:, sl] + bonus_s[prev, bbs[g], :, sl])
                                    * gate_s[prev, bbs[g], :, sl]).astype(o_ref.dtype)


RW_BPS = 4


def rwkv_mixer(ps, wup2, aup2, w0, a0, k_k, k_a, r_k, g_up, ln_g):
    rows = ps.shape[0]
    nb = rows // TB
    nsteps = RW_NCHUNK + 1

    def in_blk(d, s):
        return _rw_chunk(d, jnp.minimum(s, RW_NCHUNK - 1))

    def out_blk(d, s):
        return _rw_chunk(1, jnp.where(d == 0, 0, jnp.maximum(s - 1, 0)))

    feat = lambda off: pl.BlockSpec((RW_BPS, RW_C, W), lambda b, d, s, off=off: (b, in_blk(d, s), off))
    vec = pl.BlockSpec((1, W), lambda b, d, s: (0, 0))
    whole = lambda shape: pl.BlockSpec(shape, lambda b, d, s: (0,) * len(shape))
    slot = lambda dt: pltpu.VMEM((2, RW_BPS, RW_C, W), dt)
    out = pl.pallas_call(
        _rwkv_kernel,
        grid=(nb // RW_BPS, 2, nsteps),
        in_specs=[feat(0), feat(1), feat(2),
                  pl.BlockSpec((RW_BPS, RW_C, RW_SMALL), lambda b, d, s: (b, in_blk(d, s), 3 * W // RW_SMALL)),
                  pl.BlockSpec((1, 2 * RW_RANK, W), lambda b, d, s: (d, 0, 0)), whole((2, 2 * RW_RANK, W)),
                  pl.BlockSpec((1, 1, W), lambda b, d, s: (d, 0, 0)), whole((2, 1, W)), vec, vec, vec,
                  whole((RW_G_RANK, W)), vec],
        out_specs=pl.BlockSpec((RW_BPS, RW_C, W), lambda b, d, s: (b, out_blk(d, s), 0)),
        out_shape=jax.ShapeDtypeStruct((nb, TB, W), BF16),
        scratch_shapes=[slot(BF16), slot(F32), slot(BF16), slot(F32), slot(F32), slot(F32),
                        pltpu.VMEM((RW_BPS * RW_GROUPS, RW_GW, RW_GW), F32),
                        pltpu.VMEM((RW_BPS, RW_NCHUNK, RW_C, W), BF16)],
        compiler_params=_cparams(("arbitrary", "arbitrary", "arbitrary")),
        name="rwkv_mixer",
    )(*[ps.reshape(nb, TB, RWKV_IN)] * 4, wup2, aup2, w0, a0, k_k, k_a, r_k,
      g_up, ln_g)
    return out.reshape(rows, W)


MG_TM = 1152
MG_TN = 512


def _merge_kernel(y0_ref, y1_ref, y2_ref, wb_ref, g0_ref, g1_ref, g2_ref, o_ref, wc_ref):
    @pl.when(pl.program_id(1) == 0)
    def _():
        wc_ref[...] = wb_ref[...].astype(BF16)

    acc = None
    for y_ref, g_ref, n in ((y0_ref, g0_ref, 0), (y1_ref, g1_ref, 1), (y2_ref, g2_ref, 2)):
        proj = jnp.dot(y_ref[...], wc_ref[n], preferred_element_type=F32)
        term = jax.nn.sigmoid(g_ref[...]) * proj
        acc = term if acc is None else acc + term
    o_ref[...] = acc.astype(o_ref.dtype)


def merge_branches(y_ret, y_swa, y_rwkv, p_gate, w_branch, layer):
    nj = D // MG_TN
    ysp = pl.BlockSpec((MG_TM, W), lambda j, i: (i, 0))
    gsp = lambda n: pl.BlockSpec((MG_TM, MG_TN), lambda j, i, n=n: (i, n * nj + j))
    return pl.pallas_call(
        _merge_kernel,
        grid=(nj, R // MG_TM),
        in_specs=[ysp, ysp, ysp, pl.BlockSpec((None, 3, W, MG_TN), lambda j, i: (layer, 0, 0, j)),
                  gsp(0), gsp(1), gsp(2)],
        out_specs=pl.BlockSpec((MG_TM, MG_TN), lambda j, i: (i, j)),
        out_shape=jax.ShapeDtypeStruct((R, D), BF16),
        scratch_shapes=[pltpu.VMEM((3, W, MG_TN), BF16)],
        compiler_params=_cparams(("arbitrary", "arbitrary")),
        name="merge",
    )(y_ret, y_swa, y_rwkv, w_branch, p_gate, p_gate, p_gate)


def kernel(x, c, ctx, c_ctx, norm1_g, norm2_g, w_mod, b_mod, w_in, ret_decay, swa_sink, rwkv_mu, rwkv_w0,
           rwkv_w_up, rwkv_a0, rwkv_a_up, rwkv_g_up, rwkv_k_k, rwkv_k_a, rwkv_r_k, rwkv_ln_g, w_branch,
           w_out, w_ff1, w_ff2, final_g):
    h = jnp.concatenate([ctx, x], axis=1).reshape(R, D)
    cond8 = jnp.concatenate([c, c_ctx[None, :], jnp.zeros((8 - B - 1, D), F32)], axis=0)
    mods = mod_vectors(cond8, w_mod, b_mod)
    mods = mods.reshape(L, 8, 6, D).transpose(0, 2, 1, 3)
    cos_full, sin_signed = rope_tables()

    o_swa = RET_IN
    o_rw = RET_IN + SWA_IN
    o_small = o_rw + 3 * W
    o_gate = o_rw + RWKV_IN
    zeros_lora = jnp.zeros((L, RW_RANK, W), BF16)

    for l in range(L):
        m_l = mods[l]
        u = norm_modulate(h, norm1_g[l], m_l, 0, 1)
        p_ret = matmul_f32w(u, w_in, l, 0, RET_IN, tn=1024, name="in_ret")
        p_swa = matmul_f32w(u, w_in, l, o_swa, SWA_IN, tn=768, name="in_swa")
        ps_rwkv = matmul_f32w(u, w_in, l, o_rw, RWKV_IN, tn=512, tm=TB, epilogue="token_shift", mu=rwkv_mu[l],
                              name="in_rwkv")
        p_gate = matmul_f32w(u, w_in, l, o_gate, GATE_IN, tn=1024, name="in_gate")

        y_ret = retention_mixer(p_ret, ret_decay[l])
        y_swa = swa_mixer(p_swa, swa_sink[l], cos_full, sin_signed)

        wup = rwkv_w_up[l].astype(BF16)
        aup = rwkv_a_up[l].astype(BF16)
        z = zeros_lora[l]
        wup2 = jnp.stack([jnp.concatenate([wup[0], z], axis=0), jnp.concatenate([z, wup[1]], axis=0)])
        aup2 = jnp.stack([jnp.concatenate([aup[0], z], axis=0), jnp.concatenate([z, aup[1]], axis=0)])
        y_rwkv = rwkv_mixer(
            ps_rwkv, wup2, aup2, rwkv_w0[l].reshape(2, 1, W), rwkv_a0[l].reshape(2, 1, W),
            rwkv_k_k[l].reshape(1, W), rwkv_k_a[l].reshape(1, W), rwkv_r_k[l].reshape(1, W),
            rwkv_g_up[l].astype(BF16), rwkv_ln_g[l].reshape(1, W))

        merged = merge_branches(y_ret, y_swa, y_rwkv, p_gate, w_branch, l)
        h = matmul_f32w(merged, w_out, l, 0, D, tn=1024, epilogue="residual", h=h, mods_l=m_l,
                        which_gate=2, name="out_proj")
        u2 = norm_modulate(h, norm2_g[l], m_l, 3, 4)
        f = matmul_f32w(u2, w_ff1, l, 0, D_FF, tn=1024, out_dtype=BF16, epilogue="relu2", name="ff1")
        h = matmul(f, w_ff2[l].astype(BF16), tn=512, tm=576, epilogue="residual", h=h, mods_l=m_l,
                   which_gate=5, name="ff2")

    return final_norm(h, final_g).reshape(B, SEQ, D)
```

```python
import functools
import math

import jax
import jax.numpy as jnp
from jax import lax
from jax.experimental import pallas as pl
from jax.experimental.pallas import tpu as pltpu

F32 = jnp.float32
BF16 = jnp.bfloat16

D = 2048
B = 4
SEQ = 2048
CTX = 256
TB = CTX + SEQ
R = B * TB
L = 4
GRID_W = 64
EPS = 1e-6
ROPE_BASE = 10000.0
NEG_INF = -1e30
W = D // 2
HD = 128
RET_HEADS = W // HD
RET_CHUNK = 128
RET_GN_EPS = 1e-5
SWA_Q_HEADS = W // HD
SWA_KV_HEADS = SWA_Q_HEADS // 4
SWA_GROUP = SWA_Q_HEADS // SWA_KV_HEADS
SWA_WINDOW = 128
SWA_BLOCK = 128
RW_N = 64
RW_RANK = 64
RW_G_RANK = 128
RW_GN_EPS = 64e-5
RW_C = 64
RW_GW = 256
RW_GROUPS = W // RW_GW
RW_SMALL = 4 * RW_RANK + RW_G_RANK
D_FF = 4 * D
RET_IN = 4 * W
SWA_IN = (SWA_Q_HEADS + 2 * SWA_KV_HEADS) * HD
RWKV_IN = 3 * W + RW_SMALL
GATE_IN = 3 * D
N_IN = RET_IN + SWA_IN + RWKV_IN + GATE_IN

VMEM_LIMIT = 56 * 1024 * 1024


def _cparams(sem):
    return pltpu.CompilerParams(dimension_semantics=sem, vmem_limit_bytes=VMEM_LIMIT)


def _mod_kernel(x_ref, w_ref, b_ref, o_ref):
    x = x_ref[...]
    s = x * jax.nn.sigmoid(x)
    acc = jnp.dot(s.astype(BF16), w_ref[0].astype(BF16), preferred_element_type=F32)
    o_ref[0] = acc + b_ref[0]


def mod_vectors(cond8, w_mod, b_mod):
    tn = 1024
    n = w_mod.shape[-1]
    return pl.pallas_call(
        _mod_kernel,
        grid=(L, n // tn),
        in_specs=[
            pl.BlockSpec((8, D), lambda l, j: (0, 0)),
            pl.BlockSpec((1, D, tn), lambda l, j: (l, 0, j)),
            pl.BlockSpec((1, 1, tn), lambda l, j: (l, 0, j)),
        ],
        out_specs=pl.BlockSpec((1, 8, tn), lambda l, j: (l, 0, j)),
        out_shape=jax.ShapeDtypeStruct((L, 8, n), F32),
        compiler_params=_cparams(("parallel", "parallel")),
        name="mod_vectors",
    )(cond8, w_mod, b_mod.reshape(L, 1, n))


NORM_TM = 256
TILES_PER_BATCH = TB // NORM_TM


NORM_MOD_TM = 768


def _norm_mod_kernel(x_ref, g_ref, sh_ref, sc_ref, o_ref):
    i = pl.program_id(0)
    tm = NORM_MOD_TM
    bidx = i // (TB // tm)
    is_ctx = _row_is_ctx(i, tm)
    x = x_ref[...]
    ms = jnp.mean(x * x, axis=-1, keepdims=True)
    y = x * lax.rsqrt(ms + EPS) * g_ref[...]
    shift = jnp.where(is_ctx, sh_ref[0, B:B + 1, :], sh_ref[0, pl.ds(bidx, 1), :])
    scale = jnp.where(is_ctx, sc_ref[0, B:B + 1, :], sc_ref[0, pl.ds(bidx, 1), :])
    o_ref[...] = (y * (1.0 + scale) + shift).astype(o_ref.dtype)


def norm_modulate(h, g, mods_l, which_shift, which_scale):
    return pl.pallas_call(
        _norm_mod_kernel,
        grid=(R // NORM_MOD_TM,),
        in_specs=[
            pl.BlockSpec((NORM_MOD_TM, D), lambda i: (i, 0)),
            pl.BlockSpec((1, D), lambda i: (0, 0)),
            pl.BlockSpec((1, 8, D), lambda i: (which_shift, 0, 0)),
            pl.BlockSpec((1, 8, D), lambda i: (which_scale, 0, 0)),
        ],
        out_specs=pl.BlockSpec((NORM_MOD_TM, D), lambda i: (i, 0)),
        out_shape=jax.ShapeDtypeStruct((R, D), BF16),
        compiler_params=_cparams(("parallel",)),
        name="norm_modulate",
    )(h, g.reshape(1, D), mods_l, mods_l)


def _final_norm_kernel(x_ref, g_ref, o_ref):
    x = x_ref[...]
    ms = jnp.mean(x * x, axis=-1, keepdims=True)
    o_ref[...] = x * lax.rsqrt(ms + EPS) * g_ref[...]


def final_norm(h, g):
    per = SEQ // NORM_TM
    return pl.pallas_call(
        _final_norm_kernel,
        grid=(B, per),
        in_specs=[
            pl.BlockSpec((NORM_TM, D), lambda b, j: (b * TILES_PER_BATCH + CTX // NORM_TM + j, 0)),
            pl.BlockSpec((1, D), lambda b, j: (0, 0)),
        ],
        out_specs=pl.BlockSpec((NORM_TM, D), lambda b, j: (b * per + j, 0)),
        out_shape=jax.ShapeDtypeStruct((B * SEQ, D), F32),
        compiler_params=_cparams(("parallel", "parallel")),
        name="final_norm",
    )(h, g.reshape(1, D))


MM_TM = 1152


def _row_is_ctx(i, tm):
    rows = lax.broadcasted_iota(jnp.int32, (tm, 1), 0)
    return jnp.logical_and(i % (TB // tm) == 0, rows < CTX)


def _mm_finish(acc, i, o_ref, h_ref, gate_ref, epilogue, tm):
    if epilogue == "relu2":
        a = jnp.maximum(acc, 0.0)
        o_ref[...] = (a * a).astype(o_ref.dtype)
    elif epilogue == "residual":
        bidx = i // (TB // tm)
        g_b = gate_ref[0, pl.ds(bidx, 1), :]
        g_c = gate_ref[0, pl.ds(B, 1), :]
        gate = jnp.where(_row_is_ctx(i, tm), g_c, g_b)
        o_ref[...] = h_ref[...] + gate * acc
    elif epilogue == "token_shift":
        o_ref[...] = _token_shift(acc, gate_ref)
    else:
        o_ref[...] = acc.astype(o_ref.dtype)


def _token_shift(x, mu_ref):
    row = lax.broadcasted_iota(jnp.int32, (TB, 1), 0)
    prev = pltpu.roll(x, 1, 0)
    prev = jnp.where(jnp.logical_or(row == 0, row == CTX), 0.0, prev)
    nxt = pltpu.roll(x, TB - 1, 0)
    nxt = jnp.where(jnp.logical_or(row == CTX - 1, row == TB - 1), 0.0, nxt)
    return x + mu_ref[0:1, :] * (prev - x) + mu_ref[1:2, :] * (nxt - x)


def _mm_cached_kernel(*refs, epilogue, tm):
    if epilogue == "residual":
        x_ref, w_ref, h_ref, gate_ref, o_ref, wc_ref = refs
    elif epilogue == "token_shift":
        x_ref, w_ref, gate_ref, o_ref, wc_ref = refs
        h_ref = None
    else:
        x_ref, w_ref, o_ref, wc_ref = refs
        h_ref = gate_ref = None
    i = pl.program_id(1)

    @pl.when(i == 0)
    def _():
        wc_ref[...] = w_ref[...].astype(BF16)

    acc = jnp.dot(x_ref[...], wc_ref[...], preferred_element_type=F32)
    _mm_finish(acc, i, o_ref, h_ref, gate_ref, epilogue, tm)


def matmul_f32w(x, w_stack, layer, col0, n, *, tn, tm=MM_TM, out_dtype=F32, epilogue="plain", h=None, mods_l=None,
                which_gate=None, mu=None, name="matmul"):
    m, kdim = x.shape
    in_specs = [
        pl.BlockSpec((tm, kdim), lambda j, i: (i, 0)),
        pl.BlockSpec((pl.Element(kdim), pl.Element(tn)), lambda j, i: (layer * kdim, pl.multiple_of(col0 + j * tn, 128))),
    ]
    args = [x, w_stack.reshape(-1, w_stack.shape[-1])]
    aliases = {}
    if epilogue == "residual":
        in_specs += [
            pl.BlockSpec((tm, tn), lambda j, i: (i, j)),
            pl.BlockSpec((1, 8, tn), lambda j, i: (which_gate, 0, j)),
        ]
        args += [h, mods_l]
        aliases = {2: 0}
    elif epilogue == "token_shift":
        assert tm == TB
        in_specs.append(pl.BlockSpec((2, tn), lambda j, i: (0, j)))
        args.append(mu)
    return pl.pallas_call(
        functools.partial(_mm_cached_kernel, epilogue=epilogue, tm=tm),
        grid=(pl.cdiv(n, tn), m // tm),
        in_specs=in_specs,
        out_specs=pl.BlockSpec((tm, tn), lambda j, i: (i, j)),
        out_shape=jax.ShapeDtypeStruct((m, n), out_dtype),
        scratch_shapes=[pltpu.VMEM((kdim, tn), BF16)],
        input_output_aliases=aliases,
        compiler_params=_cparams(("arbitrary", "arbitrary")),
        name=name,
    )(*args)


def _mm_residual_kernel(x_ref, w_ref, h_ref, gate_ref, o_ref, *, tm):
    acc = jnp.dot(x_ref[...], w_ref[...], preferred_element_type=F32)
    _mm_finish(acc, pl.program_id(0), o_ref, h_ref, gate_ref, "residual", tm)


def matmul_residual(x, w, h, mods_l, which_gate, *, tm, tn, name):
    m, kdim = x.shape
    n = w.shape[1]
    return pl.pallas_call(
        functools.partial(_mm_residual_kernel, tm=tm),
        grid=(m // tm, n // tn),
        in_specs=[
            pl.BlockSpec((tm, kdim), lambda i, j: (i, 0)),
            pl.BlockSpec((kdim, tn), lambda i, j: (0, j)),
            pl.BlockSpec((tm, tn), lambda i, j: (i, j)),
            pl.BlockSpec((1, 8, tn), lambda i, j: (which_gate, 0, j)),
        ],
        out_specs=pl.BlockSpec((tm, tn), lambda i, j: (i, j)),
        out_shape=jax.ShapeDtypeStruct((m, n), F32),
        input_output_aliases={2: 0},
        compiler_params=_cparams(("parallel", "parallel")),
        name=name,
    )(x, w, h, mods_l)


RET_NCHUNK = TB // RET_CHUNK
RET_CTX_CHUNKS = CTX // RET_CHUNK


def _log_sigmoid(x):
    return jnp.minimum(x, 0.0) - jnp.log(1.0 + jnp.exp(-jnp.abs(x)))


RET_BATCH = 9


def _ret_kernel(dec_ref, q_ref, k_ref, v_ref, g_ref, o_ref, y_ref):
    hh = pl.program_id(1)
    c = RET_CHUNK
    ri = lax.broadcasted_iota(jnp.int32, (c, c), 0).astype(F32)
    ci = lax.broadcasted_iota(jnp.int32, (c, c), 1).astype(F32)
    kscale = HD ** -0.5
    nt = (((1,), (1,)), ((), ()))
    for direction in range(2):
        lg = _log_sigmoid(jnp.full((c, c), dec_ref[direction, hh], F32))
        diff = (ri - ci) if direction == 0 else (ci - ri)
        intra = jnp.where(diff >= 0, jnp.exp(lg * jnp.maximum(diff, 0.0)), 0.0)
        pos = ri if direction == 0 else (c - 1.0) - ri
        q_decay = jnp.exp(lg * (pos + 1.0))
        k_decay = jnp.exp(lg * ((c - 1.0) - pos))
        chunk_decay = jnp.exp(lg * float(c))
        if direction == 0:
            order = list(range(RET_NCHUNK))
        else:
            order = list(range(RET_CTX_CHUNKS - 1, -1, -1)) + list(range(RET_NCHUNK - 1, RET_CTX_CHUNKS - 1, -1))
        state = jnp.zeros((c, c), F32)
        for b0 in range(0, RET_NCHUNK, RET_BATCH):
            rows = [slice(ch * c, (ch + 1) * c) for ch in order[b0:b0 + RET_BATCH]]
            n = range(len(rows))
            q = [q_ref[r, :] for r in rows]
            k = [k_ref[r, :] * kscale for r in rows]
            v = [v_ref[r, :].astype(BF16) for r in rows]
            sc = [lax.dot_general(q[i].astype(BF16), k[i].astype(BF16), nt, preferred_element_type=F32) * intra
                  for i in n]
            y_in = [jnp.dot(sc[i].astype(BF16), v[i], preferred_element_type=F32) for i in n]
            kv = [jnp.dot((k[i] * k_decay).T.astype(BF16), v[i], preferred_element_type=F32) for i in n]
            states = []
            for i in n:
                states.append(state)
                state = state * chunk_decay + kv[i]
            y_x = [jnp.dot((q[i] * q_decay).astype(BF16), states[i].astype(BF16), preferred_element_type=F32)
                   for i in n]
            for i in n:
                y = y_in[i] + y_x[i]
                if direction == 0:
                    y_ref[rows[i], :] = y
                else:
                    y = y + y_ref[rows[i], :]
                    mean = jnp.mean(y, axis=-1, keepdims=True)
                    yc = y - mean
                    var = jnp.mean(yc * yc, axis=-1, keepdims=True)
                    g = g_ref[rows[i], :]
                    o_ref[rows[i], :] = (g * jax.nn.sigmoid(g) * (yc * lax.rsqrt(var + RET_GN_EPS))).astype(o_ref.dtype)


def retention_mixer(p_ret, decay_l):
    nh = RET_HEADS
    blk = lambda off: pl.BlockSpec((TB, HD), lambda b, h, off=off: (b, off + h))
    rows = p_ret.shape[0]
    return pl.pallas_call(
        _ret_kernel,
        grid=(rows // TB, nh),
        in_specs=[pl.BlockSpec(memory_space=pltpu.SMEM), blk(0), blk(nh), blk(2 * nh), blk(3 * nh)],
        out_specs=pl.BlockSpec((TB, HD), lambda b, h: (b, h)),
        out_shape=jax.ShapeDtypeStruct((rows, W), BF16),
        scratch_shapes=[pltpu.VMEM((TB, HD), F32)],
        compiler_params=_cparams(("parallel", "parallel")),
        name="retention",
    )(decay_l, p_ret, p_ret, p_ret, p_ret)


SWA_NBLK = SEQ // SWA_BLOCK
SWA_WIN = 3 * SWA_BLOCK
SWA_BATCH = 4


def _rope(x, cos, sin_signed):
    lane = lax.broadcasted_iota(jnp.int32, x.shape, 1)
    partner = jnp.where(lane % 64 < 32, pltpu.roll(x, 96, 1), pltpu.roll(x, 32, 1))
    return x * cos + partner * sin_signed


def _swa_kernel(sink_ref, q_ref, k_ref, v_ref, cos_ref, sin_ref, o_ref, kt_ref, vb_ref):
    kvh = pl.program_id(1)
    scale = HD ** -0.5
    g = SWA_GROUP
    blk = SWA_BLOCK

    vb_ref[...] = v_ref[...].astype(BF16)
    for j in range(SWA_NBLK):
        rows = slice(j * blk, (j + 1) * blk)
        k_rot = _rope(k_ref[CTX + j * blk:CTX + (j + 1) * blk, :], cos_ref[rows, :], sin_ref[rows, :])
        kt_ref[j] = k_rot.T.astype(BF16)
    kc_t = jnp.concatenate([k_ref[j * blk:(j + 1) * blk, :].T for j in range(CTX // blk)], axis=1).astype(BF16)
    vc = vb_ref[:CTX, :]

    def sink_col(rows_per_head):
        parts = [jnp.full((rows_per_head, 1), sink_ref[kvh * g + gi], F32) for gi in range(g)]
        return jnp.concatenate(parts, axis=0)

    def lane_tiles(x):
        return [x[:, j * HD:(j + 1) * HD] for j in range(x.shape[1] // HD)]

    def row_max(*xs):
        tiles = [t for x in xs for t in lane_tiles(x)]
        return jnp.max(functools.reduce(jnp.maximum, tiles), axis=-1, keepdims=True)

    def row_sum(*xs):
        tiles = [t for x in xs for t in lane_tiles(x)]
        return jnp.sum(functools.reduce(jnp.add, tiles), axis=-1, keepdims=True)

    qc = jnp.concatenate([q_ref[:CTX, gi * HD:(gi + 1) * HD] for gi in range(g)], axis=0).astype(BF16)
    s = jnp.dot(qc, kc_t, preferred_element_type=F32) * scale
    sk = sink_col(CTX)
    m = jnp.maximum(row_max(s), sk)
    e = jnp.exp(s - m)
    den = row_sum(e) + jnp.exp(sk - m)
    oc = jnp.dot(e.astype(BF16), vc, preferred_element_type=F32) / den
    for gi in range(g):
        o_ref[:CTX, gi * HD:(gi + 1) * HD] = oc[gi * CTX:(gi + 1) * CTX].astype(o_ref.dtype)

    sk_b = sink_col(blk)
    delta = (lax.broadcasted_iota(jnp.int32, (g * blk, SWA_WIN), 1)
             - lax.broadcasted_iota(jnp.int32, (g * blk, SWA_WIN), 0) % blk)

    for n0 in range(0, SWA_NBLK, SWA_BATCH):
        ns = list(range(n0, n0 + SWA_BATCH))
        idx = range(len(ns))
        r0 = [n * blk for n in ns]
        j0 = [min(max(n - 1, 0), SWA_NBLK - SWA_WIN // blk) for n in ns]
        qn = [jnp.concatenate([_rope(q_ref[CTX + r0[i]:CTX + r0[i] + blk, gi * HD:(gi + 1) * HD],
                                     cos_ref[r0[i]:r0[i] + blk, :], sin_ref[r0[i]:r0[i] + blk, :])
                               for gi in range(g)], axis=0).astype(BF16) for i in idx]
        kw_t = [jnp.concatenate([kt_ref[j0[i] + w] for w in range(SWA_WIN // blk)], axis=1) for i in idx]
        s_win = [jnp.dot(qn[i], kw_t[i], preferred_element_type=F32) * scale for i in idx]
        s_win = [jnp.where(jnp.abs(delta + (j0[i] * blk - r0[i])) <= SWA_WINDOW, s_win[i], NEG_INF) for i in idx]
        s_ctx = [jnp.dot(qn[i], kc_t, preferred_element_type=F32) * scale for i in idx]
        m = [jnp.maximum(row_max(s_win[i], s_ctx[i]), sk_b) for i in idx]
        e_win = [jnp.exp(s_win[i] - m[i]) for i in idx]
        e_ctx = [jnp.exp(s_ctx[i] - m[i]) for i in idx]
        den = [row_sum(e_win[i], e_ctx[i]) + jnp.exp(sk_b - m[i]) for i in idx]
        o = [(jnp.dot(e_win[i].astype(BF16), vb_ref[CTX + j0[i] * blk:CTX + j0[i] * blk + SWA_WIN, :],
                      preferred_element_type=F32)
              + jnp.dot(e_ctx[i].astype(BF16), vc, preferred_element_type=F32)) / den[i] for i in idx]
        for i in idx:
            for gi in range(g):
                o_ref[CTX + r0[i]:CTX + r0[i] + blk, gi * HD:(gi + 1) * HD] = (
                    o[i][gi * blk:(gi + 1) * blk].astype(o_ref.dtype))


def swa_mixer(p_swa, sink_l, cos_full, sin_signed):
    gw = SWA_GROUP * HD
    rows = p_swa.shape[0]
    return pl.pallas_call(
        _swa_kernel,
        grid=(rows // TB, SWA_KV_HEADS),
        in_specs=[
            pl.BlockSpec(memory_space=pltpu.SMEM),
            pl.BlockSpec((TB, gw), lambda b, kv: (b, kv)),
            pl.BlockSpec((TB, HD), lambda b, kv: (b, SWA_Q_HEADS + kv)),
            pl.BlockSpec((TB, HD), lambda b, kv: (b, SWA_Q_HEADS + SWA_KV_HEADS + kv)),
            pl.BlockSpec((SEQ, HD), lambda b, kv: (0, 0)),
            pl.BlockSpec((SEQ, HD), lambda b, kv: (0, 0)),
        ],
        out_specs=pl.BlockSpec((TB, gw), lambda b, kv: (b, kv)),
        out_shape=jax.ShapeDtypeStruct((rows, W), BF16),
        scratch_shapes=[pltpu.VMEM((SWA_NBLK, HD, SWA_BLOCK), BF16), pltpu.VMEM((TB, HD), BF16)],
        compiler_params=_cparams(("parallel", "parallel")),
        name="swa",
    )(sink_l, p_swa, p_swa, p_swa, cos_full, sin_signed)


def rope_tables():
    rows = SEQ // GRID_W
    row = jnp.repeat(jnp.arange(rows), GRID_W).astype(F32)
    col = jnp.tile(jnp.arange(GRID_W), rows).astype(F32)
    n_freq = HD // 4
    inv_freq = ROPE_BASE ** (-jnp.arange(n_freq, dtype=F32) / n_freq)
    ang_r = row[:, None] * inv_freq
    ang_c = col[:, None] * inv_freq
    cr, sr, cc, sc = jnp.cos(ang_r), jnp.sin(ang_r), jnp.cos(ang_c), jnp.sin(ang_c)
    cos_full = jnp.concatenate([cr, cr, cc, cc], axis=-1)
    sin_signed = jnp.concatenate([-sr, sr, -sc, sc], axis=-1)
    return cos_full, sin_signed


RW_NCHUNK = TB // RW_C
RW_CTX_CHUNKS = CTX // RW_C
RW_GPC = RW_GW // RW_C


def _rw_chunk(direction, s):
    back = jnp.where(s < RW_CTX_CHUNKS, RW_CTX_CHUNKS - 1 - s, RW_NCHUNK - 1 + RW_CTX_CHUNKS - s)
    return jnp.where(direction == 0, s, back)


def _head_ones():
    r = lax.broadcasted_iota(jnp.int32, (RW_GW, RW_GW), 0) // RW_N
    c = lax.broadcasted_iota(jnp.int32, (RW_GW, RW_GW), 1) // RW_N
    return r == c


def _dot16(a, b):
    return jnp.dot(a.astype(BF16), b.astype(BF16), preferred_element_type=F32)


def _dot16_nt(a, b):
    return lax.dot_general(a.astype(BF16), b.astype(BF16), (((1,), (1,)), ((), ())), preferred_element_type=F32)


def _split_bf16(x, terms):
    parts = []
    for _ in range(terms):
        p = x.astype(BF16)
        parts.append(p)
        x = x - p.astype(F32)
    return parts


def _dot_exact_lhs(x, b16, terms):
    return jnp.dot(jnp.concatenate(_split_bf16(x, terms), axis=1), jnp.concatenate([b16] * terms, axis=0),
                   preferred_element_type=F32)


def _rwkv_kernel(r_ref, k_ref, v_ref, sm_ref, wup_ref, aup_ref, w0_ref, a0_ref, kk_ref, ka_ref, rk_ref, gup_ref,
                 lng_ref, o_ref, q_s, y1_s, m_s, z_s, bonus_s, gate_s, st_ref, ybuf):
    direction = pl.program_id(1)
    step = pl.program_id(2)
    cur = step % 2
    prev = 1 - cur
    c = RW_C
    same_head = _head_ones()
    mask16 = jnp.where(same_head, 1.0, 0.0).astype(BF16)
    nbat = r_ref.shape[0]
    chains = [(bb, g) for bb in range(nbat) for g in range(RW_GROUPS)]
    groups = range(len(chains))
    bbs = [bb for bb, _ in chains]
    sls = [slice(g * RW_GW, (g + 1) * RW_GW) for _, g in chains]

    @pl.when(step == 0)
    def _():
        st_ref[...] = jnp.zeros_like(st_ref)
        for ref in (q_s, y1_s, m_s, z_s, bonus_s, gate_s):
            ref[1] = jnp.zeros(ref.shape[1:], ref.dtype)

    ys = []
    for g in groups:
        m_bd = jnp.concatenate([m_s[prev, bbs[g], :, sls[g]]] * RW_GPC, axis=0) * mask16
        z_bd = jnp.where(same_head, jnp.concatenate([z_s[prev, bbs[g], :, sls[g]]] * RW_GPC, axis=0), 0.0)
        res = _dot16(jnp.concatenate([q_s[prev, bbs[g], :, sls[g]], m_bd], axis=0), st_ref[g])
        ys.append(res[:c] + y1_s[prev, bbs[g], :, sls[g]])
        st_ref[g] = res[c:] + z_bd

    def bd(x):
        return jnp.concatenate([x.astype(BF16)] * RW_GPC, axis=0) * mask16

    sgn = 1 - 2 * direction
    cum_terms = 2
    tt = lax.broadcasted_iota(jnp.int32, (c, cum_terms * c), 0)
    ss = lax.broadcasted_iota(jnp.int32, (c, cum_terms * c), 1) % c
    tri16 = jnp.where((tt - ss) * sgn >= 0, 1.0, 0.0).astype(BF16)
    t4 = lax.broadcasted_iota(jnp.int32, (c, RW_GW), 0)
    s4 = lax.broadcasted_iota(jnp.int32, (c, RW_GW), 1) % c
    d4 = (t4 - s4) * sgn
    strict = d4 > 0
    incl = d4 >= 0
    eye = jnp.where(d4 == 0, 1.0, 0.0)

    sm = [sm_ref[bb] for bb in range(nbat)]
    tanh_wd = [jnp.tanh(x[:, :2 * RW_RANK]).astype(BF16) for x in sm]
    ad = [x[:, 2 * RW_RANK:4 * RW_RANK].astype(BF16) for x in sm]

    r = [r_ref[bbs[g], :, sls[g]] for g in groups]
    k = [k_ref[bbs[g], :, sls[g]] for g in groups]
    v = [v_ref[bbs[g], :, sls[g]] for g in groups]
    logw = [-jax.nn.sigmoid(w0_ref[0, :, sls[g]] + jnp.dot(tanh_wd[bbs[g]], wup_ref[0, :, sls[g]],
                                                          preferred_element_type=F32)) * math.exp(-0.5)
            for g in groups]

    def alpha_of(d, g):
        return jax.nn.sigmoid(a0_ref[d, :, sls[g]] + jnp.dot(ad[bbs[g]], aup_ref[d, :, sls[g]],
                                                             preferred_element_type=F32))

    alpha = [alpha_of(direction, g) for g in groups]
    alpha_both = [alpha[g] + alpha_of(1 - direction, g) for g in groups]
    kkp = [k[g] * kk_ref[:, sls[g]] for g in groups]
    kdir = [k[g] * (1.0 + (alpha[g] - 1.0) * ka_ref[:, sls[g]]) for g in groups]
    kboth = [k[g] * (2.0 + (alpha_both[g] - 2.0) * ka_ref[:, sls[g]]) for g in groups]
    nch = len(chains)
    sums = _dot_exact_lhs(jnp.concatenate([kkp[g] * kkp[g] for g in groups]
                                          + [r[g] * kboth[g] * rk_ref[:, sls[g]] for g in groups], axis=0), mask16, 2)
    kk = [kkp[g] / jnp.maximum(jnp.sqrt(sums[g * c:(g + 1) * c]), 1e-12) for g in groups]
    for g in groups:
        bonus_s[cur, bbs[g], :, sls[g]] = sums[(nch + g) * c:(nch + g + 1) * c] * v[g]
    for bb in range(nbat):
        gate_s[cur, bb] = jnp.dot(jax.nn.sigmoid(sm[bb][:, 4 * RW_RANK:]).astype(BF16), gup_ref[...],
                                  preferred_element_type=F32)
    b_vec = [kk[g] * alpha[g] for g in groups]

    cum = [jnp.dot(tri16, jnp.concatenate(_split_bf16(logw[g], cum_terms), axis=0), preferred_element_type=F32)
           for g in groups]
    total = [jnp.sum(logw[g], axis=0, keepdims=True) for g in groups]
    e_out = [jnp.exp(-cum[g]) for g in groups]
    e_left = [jnp.exp(total[g] - cum[g]) for g in groups]
    a_t = [-kk[g] * jnp.exp(cum[g] - logw[g]) for g in groups]
    r_t = [r[g] * jnp.exp(cum[g]) for g in groups]
    lhs = [jnp.concatenate([a_t[g], r_t[g]], axis=0) for g in groups]
    gb = [_dot16_nt(lhs[g], bd(b_vec[g] * e_out[g])) for g in groups]
    gk = [_dot16_nt(lhs[g], bd(kdir[g] * e_out[g])) for g in groups]
    a_ab = [jnp.where(strict, gb[g][:c], 0.0) for g in groups]
    a_ak = [jnp.where(strict, gk[g][:c], 0.0) for g in groups]
    a_rb = [jnp.where(incl, gb[g][c:], 0.0) for g in groups]
    a_rk = [jnp.where(incl, gk[g][c:], 0.0) for g in groups]

    pt = jnp.where(direction == 0, t4, c - 1 - t4)
    ps = jnp.where(direction == 0, s4, c - 1 - s4)
    px = pt ^ ps

    def level_mask(lv):
        return (px >> lv) * 2 + ((pt >> lv) & 1) == 3

    tinv = [eye + jnp.where(level_mask(0), a_ab[g], 0.0) for g in groups]
    for lv in range(1, int(math.log2(c))):
        lm = level_mask(lv)
        cross = [_dot16(jnp.where(lm, a_ab[g], 0.0), bd(tinv[g])) for g in groups]
        tinv = [tinv[g] + _dot16(tinv[g], bd(cross[g])) for g in groups]

    eye16 = eye.astype(BF16)
    blt = [_dot16_nt(eye16, bd(b_vec[g] * e_left[g])) for g in groups]
    klt = [_dot16_nt(eye16, bd(kdir[g] * e_left[g])) for g in groups]
    vprod = [_dot16(jnp.concatenate([a_ak[g], a_rk[g], klt[g]], axis=0), bd(v[g])) for g in groups]
    pw = [_dot16(tinv[g], jnp.concatenate([bd(a_t[g]), bd(vprod[g][:c])], axis=1)) for g in groups]
    top = [_dot16(jnp.concatenate([a_rb[g], blt[g]], axis=0),
                  jnp.concatenate([bd(pw[g][:, :RW_GW]), bd(pw[g][:, RW_GW:])], axis=1)) for g in groups]
    low = [vprod[g][c:] for g in groups]
    for g in groups:
        sl = sls[g]
        q_s[cur, bbs[g], :, sl] = (r_t[g] + top[g][:c, :RW_GW]).astype(q_s.dtype)
        y1_s[cur, bbs[g], :, sl] = top[g][:c, RW_GW:] + low[g][:c]
        m_s[cur, bbs[g], :, sl] = (eye * jnp.exp(total[g]) + top[g][c:, :RW_GW]).astype(m_s.dtype)
        z_s[cur, bbs[g], :, sl] = top[g][c:, RW_GW:] + low[g][c:]

    chunk_prev = _rw_chunk(direction, jnp.maximum(step - 1, 0))

    @pl.when(direction == 0)
    def _():
        for g in groups:
            ybuf[bbs[g], chunk_prev, :, sls[g]] = ys[g].astype(ybuf.dtype)

    @pl.when(direction == 1)
    def _():
        y4 = jnp.concatenate([ys[g] + ybuf[bbs[g], chunk_prev, :, sls[g]] for g in groups], axis=0)
        mean = _dot_exact_lhs(y4, mask16, 2) * (1.0 / RW_N)
        yc = y4 - mean
        var = _dot_exact_lhs(yc * yc, mask16, 2) * (1.0 / RW_N)
        yn = yc * lax.rsqrt(var + RW_GN_EPS)
        for g in groups:
            sl = sls[g]
            o_ref[bbs[g], :, sl] = ((yn[g * c:(g + 1) * c] * lng_ref[:, sl] + bonus_s[prev, bbs[g], :, sl])
                                    * gate_s[prev, bbs[g], :, sl]).astype(o_ref.dtype)


RW_BPS = 4


def rwkv_mixer(ps, wup2, aup2, w0, a0, k_k, k_a, r_k, g_up, ln_g):
    rows = ps.shape[0]
    nb = rows // TB
    nsteps = RW_NCHUNK + 1

    def in_blk(d, s):
        return _rw_chunk(d, jnp.minimum(s, RW_NCHUNK - 1))

    def out_blk(d, s):
        return _rw_chunk(1, jnp.where(d == 0, 0, jnp.maximum(s - 1, 0)))

    feat = lambda off: pl.BlockSpec((RW_BPS, RW_C, W), lambda b, d, s, off=off: (b, in_blk(d, s), off))
    vec = pl.BlockSpec((1, W), lambda b, d, s: (0, 0))
    whole = lambda shape: pl.BlockSpec(shape, lambda b, d, s: (0,) * len(shape))
    slot = lambda dt: pltpu.VMEM((2, RW_BPS, RW_C, W), dt)
    out = pl.pallas_call(
        _rwkv_kernel,
        grid=(nb // RW_BPS, 2, nsteps),
        in_specs=[feat(0), feat(1), feat(2),
                  pl.BlockSpec((RW_BPS, RW_C, RW_SMALL), lambda b, d, s: (b, in_blk(d, s), 3 * W // RW_SMALL)),
                  pl.BlockSpec((1, 2 * RW_RANK, W), lambda b, d, s: (d, 0, 0)), whole((2, 2 * RW_RANK, W)),
                  pl.BlockSpec((1, 1, W), lambda b, d, s: (d, 0, 0)), whole((2, 1, W)), vec, vec, vec,
                  whole((RW_G_RANK, W)), vec],
        out_specs=pl.BlockSpec((RW_BPS, RW_C, W), lambda b, d, s: (b, out_blk(d, s), 0)),
        out_shape=jax.ShapeDtypeStruct((nb, TB, W), BF16),
        scratch_shapes=[slot(BF16), slot(F32), slot(BF16), slot(F32), slot(F32), slot(F32),
                        pltpu.VMEM((RW_BPS * RW_GROUPS, RW_GW, RW_GW), F32),
                        pltpu.VMEM((RW_BPS, RW_NCHUNK, RW_C, W), BF16)],
        compiler_params=_cparams(("arbitrary", "arbitrary", "arbitrary")),
        name="rwkv_mixer",
    )(*[ps.reshape(nb, TB, RWKV_IN)] * 4, wup2, aup2, w0, a0, k_k, k_a, r_k,
      g_up, ln_g)
    return out.reshape(rows, W)


MG_TM = 1152
MG_TN = 256


def _merge_kernel(u_ref, y0_ref, y1_ref, y2_ref, wg0_ref, wg1_ref, wg2_ref, wb_ref, o_ref, wgc_ref, wbc_ref):
    @pl.when(pl.program_id(1) == 0)
    def _():
        for n, wg_ref in enumerate((wg0_ref, wg1_ref, wg2_ref)):
            wgc_ref[n] = wg_ref[...].astype(BF16)
        wbc_ref[...] = wb_ref[...].astype(BF16)

    u = u_ref[...]
    acc = None
    for n, y_ref in enumerate((y0_ref, y1_ref, y2_ref)):
        gate = jax.nn.sigmoid(jnp.dot(u, wgc_ref[n], preferred_element_type=F32))
        term = gate * jnp.dot(y_ref[...], wbc_ref[n], preferred_element_type=F32)
        acc = term if acc is None else acc + term
    o_ref[...] = acc.astype(o_ref.dtype)


def merge_branches(u, y_ret, y_swa, y_rwkv, w_in, gate_col0, w_branch, layer):
    nj = D // MG_TN
    ysp = pl.BlockSpec((MG_TM, W), lambda j, i: (i, 0))
    wgsp = lambda n: pl.BlockSpec(
        (pl.Element(D), pl.Element(MG_TN)),
        lambda j, i, n=n: (layer * D, pl.multiple_of(gate_col0 + n * D + j * MG_TN, 128)))
    return pl.pallas_call(
        _merge_kernel,
        grid=(nj, R // MG_TM),
        in_specs=[pl.BlockSpec((MG_TM, D), lambda j, i: (i, 0)), ysp, ysp, ysp, wgsp(0), wgsp(1), wgsp(2),
                  pl.BlockSpec((None, 3, W, MG_TN), lambda j, i: (layer, 0, 0, j))],
        out_specs=pl.BlockSpec((MG_TM, MG_TN), lambda j, i: (i, j)),
        out_shape=jax.ShapeDtypeStruct((R, D), BF16),
        scratch_shapes=[pltpu.VMEM((3, D, MG_TN), BF16), pltpu.VMEM((3, W, MG_TN), BF16)],
        compiler_params=_cparams(("arbitrary", "arbitrary")),
        name="merge",
    )(u, y_ret, y_swa, y_rwkv, *[w_in.reshape(-1, w_in.shape[-1])] * 3, w_branch)


def kernel(x, c, ctx, c_ctx, norm1_g, norm2_g, w_mod, b_mod, w_in, ret_decay, swa_sink, rwkv_mu, rwkv_w0,
           rwkv_w_up, rwkv_a0, rwkv_a_up, rwkv_g_up, rwkv_k_k, rwkv_k_a, rwkv_r_k, rwkv_ln_g, w_branch,
           w_out, w_ff1, w_ff2, final_g):
    h = jnp.concatenate([ctx, x], axis=1).reshape(R, D)
    cond8 = jnp.concatenate([c, c_ctx[None, :], jnp.zeros((8 - B - 1, D), F32)], axis=0)
    mods = mod_vectors(cond8, w_mod, b_mod)
    mods = mods.reshape(L, 8, 6, D).transpose(0, 2, 1, 3)
    cos_full, sin_signed = rope_tables()

    o_swa = RET_IN
    o_rw = RET_IN + SWA_IN
    o_small = o_rw + 3 * W
    o_gate = o_rw + RWKV_IN
    zeros_lora = jnp.zeros((L, RW_RANK, W), BF16)

    for l in range(L):
        m_l = mods[l]
        u = norm_modulate(h, norm1_g[l], m_l, 0, 1)
        p_ret = matmul_f32w(u, w_in, l, 0, RET_IN, tn=1024, name="in_ret")
        p_swa = matmul_f32w(u, w_in, l, o_swa, SWA_IN, tn=768, name="in_swa")
        ps_rwkv = matmul_f32w(u, w_in, l, o_rw, RWKV_IN, tn=512, tm=TB, epilogue="token_shift", mu=rwkv_mu[l],
                              name="in_rwkv")

        y_ret = retention_mixer(p_ret, ret_decay[l])
        y_swa = swa_mixer(p_swa, swa_sink[l], cos_full, sin_signed)

        wup = rwkv_w_up[l].astype(BF16)
        aup = rwkv_a_up[l].astype(BF16)
        z = zeros_lora[l]
        wup2 = jnp.stack([jnp.concatenate([wup[0], z], axis=0), jnp.concatenate([z, wup[1]], axis=0)])
        aup2 = jnp.stack([jnp.concatenate([aup[0], z], axis=0), jnp.concatenate([z, aup[1]], axis=0)])
        y_rwkv = rwkv_mixer(
            ps_rwkv, wup2, aup2, rwkv_w0[l].reshape(2, 1, W), rwkv_a0[l].reshape(2, 1, W),
            rwkv_k_k[l].reshape(1, W), rwkv_k_a[l].reshape(1, W), rwkv_r_k[l].reshape(1, W),
            rwkv_g_up[l].astype(BF16), rwkv_ln_g[l].reshape(1, W))

        merged = merge_branches(u, y_ret, y_swa, y_rwkv, w_in, o_gate, w_branch, l)
        h = matmul_f32w(merged, w_out, l, 0, D, tn=1024, epilogue="residual", h=h, mods_l=m_l,
                        which_gate=2, name="out_proj")
        u2 = norm_modulate(h, norm2_g[l], m_l, 3, 4)
        f = matmul_f32w(u2, w_ff1, l, 0, D_FF, tn=1024, out_dtype=BF16, epilogue="relu2", name="ff1")
        h = matmul_residual(f, w_ff2[l].astype(BF16), h, m_l, 5, tm=576, tn=512, name="ff2")

    return final_norm(h, final_g).reshape(B, SEQ, D)
```

```python
import functools
import math

import jax
import jax.numpy as jnp
from jax import lax
from jax.experimental import pallas as pl
from jax.experimental.pallas import tpu as pltpu

F32 = jnp.float32
BF16 = jnp.bfloat16

D = 2048
B = 4
SEQ = 2048
CTX = 256
TB = CTX + SEQ
R = B * TB
L = 4
GRID_W = 64
EPS = 1e-6
ROPE_BASE = 10000.0
NEG_INF = -1e30
W = D // 2
HD = 128
RET_HEADS = W // HD
RET_CHUNK = 128
RET_GN_EPS = 1e-5
SWA_Q_HEADS = W // HD
SWA_KV_HEADS = SWA_Q_HEADS // 4
SWA_GROUP = SWA_Q_HEADS // SWA_KV_HEADS
SWA_WINDOW = 128
SWA_BLOCK = 128
RW_N = 64
RW_RANK = 64
RW_G_RANK = 128
RW_GN_EPS = 64e-5
RW_C = 64
RW_GW = 256
RW_GROUPS = W // RW_GW
RW_SMALL = 4 * RW_RANK + RW_G_RANK
D_FF = 4 * D
RET_IN = 4 * W
SWA_IN = (SWA_Q_HEADS + 2 * SWA_KV_HEADS) * HD
RWKV_IN = 3 * W + RW_SMALL
GATE_IN = 3 * D
N_IN = RET_IN + SWA_IN + RWKV_IN + GATE_IN

VMEM_LIMIT = 56 * 1024 * 1024


def _cparams(sem):
    return pltpu.CompilerParams(dimension_semantics=sem, vmem_limit_bytes=VMEM_LIMIT)


def _mod_kernel(x_ref, w_ref, b_ref, o_ref):
    x = x_ref[...]
    s = x * jax.nn.sigmoid(x)
    acc = jnp.dot(s.astype(BF16), w_ref[0].astype(BF16), preferred_element_type=F32)
    o_ref[0] = acc + b_ref[0]


def mod_vectors(cond8, w_mod, b_mod):
    tn = 1024
    n = w_mod.shape[-1]
    return pl.pallas_call(
        _mod_kernel,
        grid=(L, n // tn),
        in_specs=[
            pl.BlockSpec((8, D), lambda l, j: (0, 0)),
            pl.BlockSpec((1, D, tn), lambda l, j: (l, 0, j)),
            pl.BlockSpec((1, 1, tn), lambda l, j: (l, 0, j)),
        ],
        out_specs=pl.BlockSpec((1, 8, tn), lambda l, j: (l, 0, j)),
        out_shape=jax.ShapeDtypeStruct((L, 8, n), F32),
        compiler_params=_cparams(("parallel", "parallel")),
        name="mod_vectors",
    )(cond8, w_mod, b_mod.reshape(L, 1, n))


NORM_TM = 256
TILES_PER_BATCH = TB // NORM_TM


NORM_MOD_TM = 768
NORM_SUB = 64


def _norm_mod_kernel(x_ref, g_ref, sh_ref, sc_ref, o_ref):
    i = pl.program_id(0)
    tm = NORM_MOD_TM
    bidx = i // (TB // tm)
    first_tile = i % (TB // tm) == 0
    g = g_ref[...]
    sub = NORM_SUB

    def body(r, carry):
        row = jnp.where(jnp.logical_and(first_tile, r * sub < CTX), B, bidx)
        rows = pl.ds(pl.multiple_of(r * sub, sub), sub)
        x = x_ref[rows, :]
        ms = jnp.mean(x * x, axis=-1, keepdims=True)
        y = x * lax.rsqrt(ms + EPS) * g
        o_ref[rows, :] = (y * (1.0 + sc_ref[0, pl.ds(row, 1), :]) + sh_ref[0, pl.ds(row, 1), :]).astype(o_ref.dtype)
        return carry

    lax.fori_loop(0, tm // sub, body, 0, unroll=4)


def norm_modulate(h, g, mods_l, which_shift, which_scale):
    return pl.pallas_call(
        _norm_mod_kernel,
        grid=(R // NORM_MOD_TM,),
        in_specs=[
            pl.BlockSpec((NORM_MOD_TM, D), lambda i: (i, 0)),
            pl.BlockSpec((1, D), lambda i: (0, 0)),
            pl.BlockSpec((1, 8, D), lambda i: (which_shift, 0, 0)),
            pl.BlockSpec((1, 8, D), lambda i: (which_scale, 0, 0)),
        ],
        out_specs=pl.BlockSpec((NORM_MOD_TM, D), lambda i: (i, 0)),
        out_shape=jax.ShapeDtypeStruct((R, D), BF16),
        compiler_params=_cparams(("parallel",)),
        name="norm_modulate",
    )(h, g.reshape(1, D), mods_l, mods_l)


def _final_norm_kernel(x_ref, g_ref, o_ref):
    x = x_ref[...]
    ms = jnp.mean(x * x, axis=-1, keepdims=True)
    o_ref[...] = x * lax.rsqrt(ms + EPS) * g_ref[...]


def final_norm(h, g):
    per = SEQ // NORM_TM
    return pl.pallas_call(
        _final_norm_kernel,
        grid=(B, per),
        in_specs=[
            pl.BlockSpec((NORM_TM, D), lambda b, j: (b * TILES_PER_BATCH + CTX // NORM_TM + j, 0)),
            pl.BlockSpec((1, D), lambda b, j: (0, 0)),
        ],
        out_specs=pl.BlockSpec((NORM_TM, D), lambda b, j: (b * per + j, 0)),
        out_shape=jax.ShapeDtypeStruct((B * SEQ, D), F32),
        compiler_params=_cparams(("parallel", "parallel")),
        name="final_norm",
    )(h, g.reshape(1, D))


MM_TM = 1152


def _row_is_ctx(i, tm):
    rows = lax.broadcasted_iota(jnp.int32, (tm, 1), 0)
    return jnp.logical_and(i % (TB // tm) == 0, rows < CTX)


def _mm_finish(acc, i, o_ref, h_ref, gate_ref, epilogue, tm):
    if epilogue == "relu2":
        a = jnp.maximum(acc, 0.0)
        o_ref[...] = (a * a).astype(o_ref.dtype)
    elif epilogue == "residual":
        bidx = i // (TB // tm)
        g_b = gate_ref[0, pl.ds(bidx, 1), :]
        g_c = gate_ref[0, pl.ds(B, 1), :]
        gate = jnp.where(_row_is_ctx(i, tm), g_c, g_b)
        o_ref[...] = h_ref[...] + gate * acc
    elif epilogue == "token_shift":
        o_ref[...] = _token_shift(acc, gate_ref)
    else:
        o_ref[...] = acc.astype(o_ref.dtype)


def _token_shift(x, mu_ref):
    row = lax.broadcasted_iota(jnp.int32, (TB, 1), 0)
    prev = pltpu.roll(x, 1, 0)
    prev = jnp.where(jnp.logical_or(row == 0, row == CTX), 0.0, prev)
    nxt = pltpu.roll(x, TB - 1, 0)
    nxt = jnp.where(jnp.logical_or(row == CTX - 1, row == TB - 1), 0.0, nxt)
    return x + mu_ref[0:1, :] * (prev - x) + mu_ref[1:2, :] * (nxt - x)


def _mm_cached_kernel(*refs, epilogue, tm):
    if epilogue == "residual":
        x_ref, w_ref, h_ref, gate_ref, o_ref, wc_ref = refs
    elif epilogue == "token_shift":
        x_ref, w_ref, gate_ref, o_ref, wc_ref = refs
        h_ref = None
    else:
        x_ref, w_ref, o_ref, wc_ref = refs
        h_ref = gate_ref = None
    i = pl.program_id(1)

    @pl.when(i == 0)
    def _():
        wc_ref[...] = w_ref[...].astype(BF16)

    acc = jnp.dot(x_ref[...], wc_ref[...], preferred_element_type=F32)
    _mm_finish(acc, i, o_ref, h_ref, gate_ref, epilogue, tm)


def matmul_f32w(x, w_stack, layer, col0, n, *, tn, tm=MM_TM, out_dtype=F32, epilogue="plain", h=None, mods_l=None,
                which_gate=None, mu=None, name="matmul"):
    m, kdim = x.shape
    in_specs = [
        pl.BlockSpec((tm, kdim), lambda j, i: (i, 0)),
        pl.BlockSpec((pl.Element(kdim), pl.Element(tn)), lambda j, i: (layer * kdim, pl.multiple_of(col0 + j * tn, 128))),
    ]
    args = [x, w_stack.reshape(-1, w_stack.shape[-1])]
    aliases = {}
    if epilogue == "residual":
        in_specs += [
            pl.BlockSpec((tm, tn), lambda j, i: (i, j)),
            pl.BlockSpec((1, 8, tn), lambda j, i: (which_gate, 0, j)),
        ]
        args += [h, mods_l]
        aliases = {2: 0}
    elif epilogue == "token_shift":
        assert tm == TB
        in_specs.append(pl.BlockSpec((2, tn), lambda j, i: (0, j)))
        args.append(mu)
    return pl.pallas_call(
        functools.partial(_mm_cached_kernel, epilogue=epilogue, tm=tm),
        grid=(pl.cdiv(n, tn), m // tm),
        in_specs=in_specs,
        out_specs=pl.BlockSpec((tm, tn), lambda j, i: (i, j)),
        out_shape=jax.ShapeDtypeStruct((m, n), out_dtype),
        scratch_shapes=[pltpu.VMEM((kdim, tn), BF16)],
        input_output_aliases=aliases,
        compiler_params=_cparams(("arbitrary", "arbitrary")),
        name=name,
    )(*args)


def _mm_residual_kernel(x_ref, w_ref, h_ref, gate_ref, o_ref, *, tm):
    acc = jnp.dot(x_ref[...], w_ref[...], preferred_element_type=F32)
    _mm_finish(acc, pl.program_id(0), o_ref, h_ref, gate_ref, "residual", tm)


def matmul_residual(x, w, h, mods_l, which_gate, *, tm, tn, name):
    m, kdim = x.shape
    n = w.shape[1]
    return pl.pallas_call(
        functools.partial(_mm_residual_kernel, tm=tm),
        grid=(m // tm, n // tn),
        in_specs=[
            pl.BlockSpec((tm, kdim), lambda i, j: (i, 0)),
            pl.BlockSpec((kdim, tn), lambda i, j: (0, j)),
            pl.BlockSpec((tm, tn), lambda i, j: (i, j)),
            pl.BlockSpec((1, 8, tn), lambda i, j: (which_gate, 0, j)),
        ],
        out_specs=pl.BlockSpec((tm, tn), lambda i, j: (i, j)),
        out_shape=jax.ShapeDtypeStruct((m, n), F32),
        input_output_aliases={2: 0},
        compiler_params=_cparams(("parallel", "parallel")),
        name=name,
    )(x, w, h, mods_l)


RET_NCHUNK = TB // RET_CHUNK
RET_CTX_CHUNKS = CTX // RET_CHUNK


def _log_sigmoid(x):
    return jnp.minimum(x, 0.0) - jnp.log(1.0 + jnp.exp(-jnp.abs(x)))


RET_BATCH = 9


def _ret_kernel(dec_ref, q_ref, k_ref, v_ref, g_ref, o_ref, y_ref):
    hh = pl.program_id(1)
    c = RET_CHUNK
    ri = lax.broadcasted_iota(jnp.int32, (c, c), 0).astype(F32)
    ci = lax.broadcasted_iota(jnp.int32, (c, c), 1).astype(F32)
    kscale = HD ** -0.5
    nt = (((1,), (1,)), ((), ()))
    for direction in range(2):
        lg = _log_sigmoid(jnp.full((c, c), dec_ref[direction, hh], F32))
        diff = (ri - ci) if direction == 0 else (ci - ri)
        intra = jnp.where(diff >= 0, jnp.exp(lg * jnp.maximum(diff, 0.0)), 0.0)
        pos = ri if direction == 0 else (c - 1.0) - ri
        q_decay = jnp.exp(lg * (pos + 1.0))
        k_decay = jnp.exp(lg * ((c - 1.0) - pos))
        chunk_decay = jnp.exp(lg * float(c))
        if direction == 0:
            order = list(range(RET_NCHUNK))
        else:
            order = list(range(RET_CTX_CHUNKS - 1, -1, -1)) + list(range(RET_NCHUNK - 1, RET_CTX_CHUNKS - 1, -1))
        state = jnp.zeros((c, c), F32)
        for b0 in range(0, RET_NCHUNK, RET_BATCH):
            rows = [slice(ch * c, (ch + 1) * c) for ch in order[b0:b0 + RET_BATCH]]
            n = range(len(rows))
            q = [q_ref[r, :] for r in rows]
            k = [k_ref[r, :] * kscale for r in rows]
            v = [v_ref[r, :].astype(BF16) for r in rows]
            sc = [lax.dot_general(q[i].astype(BF16), k[i].astype(BF16), nt, preferred_element_type=F32) * intra
                  for i in n]
            y_in = [jnp.dot(sc[i].astype(BF16), v[i], preferred_element_type=F32) for i in n]
            kv = [jnp.dot((k[i] * k_decay).T.astype(BF16), v[i], preferred_element_type=F32) for i in n]
            states = []
            for i in n:
                states.append(state)
                state = state * chunk_decay + kv[i]
            y_x = [jnp.dot((q[i] * q_decay).astype(BF16), states[i].astype(BF16), preferred_element_type=F32)
                   for i in n]
            for i in n:
                y = y_in[i] + y_x[i]
                if direction == 0:
                    y_ref[rows[i], :] = y
                else:
                    y = y + y_ref[rows[i], :]
                    mean = jnp.mean(y, axis=-1, keepdims=True)
                    yc = y - mean
                    var = jnp.mean(yc * yc, axis=-1, keepdims=True)
                    g = g_ref[rows[i], :]
                    o_ref[rows[i], :] = (g * jax.nn.sigmoid(g) * (yc * lax.rsqrt(var + RET_GN_EPS))).astype(o_ref.dtype)


def retention_mixer(p_ret, decay_l):
    nh = RET_HEADS
    blk = lambda off: pl.BlockSpec((TB, HD), lambda b, h, off=off: (b, off + h))
    rows = p_ret.shape[0]
    return pl.pallas_call(
        _ret_kernel,
        grid=(rows // TB, nh),
        in_specs=[pl.BlockSpec(memory_space=pltpu.SMEM), blk(0), blk(nh), blk(2 * nh), blk(3 * nh)],
        out_specs=pl.BlockSpec((TB, HD), lambda b, h: (b, h)),
        out_shape=jax.ShapeDtypeStruct((rows, W), BF16),
        scratch_shapes=[pltpu.VMEM((TB, HD), F32)],
        compiler_params=_cparams(("parallel", "parallel")),
        name="retention",
    )(decay_l, p_ret, p_ret, p_ret, p_ret)


SWA_NBLK = SEQ // SWA_BLOCK
SWA_WIN = 3 * SWA_BLOCK
SWA_BATCH = 4


def _rope(x, cos, sin_signed):
    lane = lax.broadcasted_iota(jnp.int32, x.shape, 1)
    partner = jnp.where(lane % 64 < 32, pltpu.roll(x, 96, 1), pltpu.roll(x, 32, 1))
    return x * cos + partner * sin_signed


def _swa_kernel(sink_ref, q_ref, k_ref, v_ref, cos_ref, sin_ref, o_ref, kt_ref, vb_ref):
    kvh = pl.program_id(1)
    scale = HD ** -0.5
    g = SWA_GROUP
    blk = SWA_BLOCK

    vb_ref[...] = v_ref[...].astype(BF16)
    for j in range(SWA_NBLK):
        rows = slice(j * blk, (j + 1) * blk)
        k_rot = _rope(k_ref[CTX + j * blk:CTX + (j + 1) * blk, :], cos_ref[rows, :], sin_ref[rows, :])
        kt_ref[j] = k_rot.T.astype(BF16)
    kc_t = jnp.concatenate([k_ref[j * blk:(j + 1) * blk, :].T for j in range(CTX // blk)], axis=1).astype(BF16)
    vc = vb_ref[:CTX, :]

    def sink_col(rows_per_head):
        parts = [jnp.full((rows_per_head, 1), sink_ref[kvh * g + gi], F32) for gi in range(g)]
        return jnp.concatenate(parts, axis=0)

    def lane_tiles(x):
        return [x[:, j * HD:(j + 1) * HD] for j in range(x.shape[1] // HD)]

    def row_max(*xs):
        tiles = [t for x in xs for t in lane_tiles(x)]
        return jnp.max(functools.reduce(jnp.maximum, tiles), axis=-1, keepdims=True)

    def row_sum(*xs):
        tiles = [t for x in xs for t in lane_tiles(x)]
        return jnp.sum(functools.reduce(jnp.add, tiles), axis=-1, keepdims=True)

    qc = jnp.concatenate([q_ref[:CTX, gi * HD:(gi + 1) * HD] for gi in range(g)], axis=0).astype(BF16)
    s = jnp.dot(qc, kc_t, preferred_element_type=F32) * scale
    sk = sink_col(CTX)
    m = jnp.maximum(row_max(s), sk)
    e = jnp.exp(s - m)
    den = row_sum(e) + jnp.exp(sk - m)
    oc = jnp.dot(e.astype(BF16), vc, preferred_element_type=F32) / den
    for gi in range(g):
        o_ref[:CTX, gi * HD:(gi + 1) * HD] = oc[gi * CTX:(gi + 1) * CTX].astype(o_ref.dtype)

    sk_b = sink_col(blk)
    delta = (lax.broadcasted_iota(jnp.int32, (g * blk, SWA_WIN), 1)
             - lax.broadcasted_iota(jnp.int32, (g * blk, SWA_WIN), 0) % blk)

    for n0 in range(0, SWA_NBLK, SWA_BATCH):
        ns = list(range(n0, n0 + SWA_BATCH))
        idx = range(len(ns))
        r0 = [n * blk for n in ns]
        j0 = [min(max(n - 1, 0), SWA_NBLK - SWA_WIN // blk) for n in ns]
        qn = [jnp.concatenate([_rope(q_ref[CTX + r0[i]:CTX + r0[i] + blk, gi * HD:(gi + 1) * HD],
                                     cos_ref[r0[i]:r0[i] + blk, :], sin_ref[r0[i]:r0[i] + blk, :])
                               for gi in range(g)], axis=0).astype(BF16) for i in idx]
        kw_t = [jnp.concatenate([kt_ref[j0[i] + w] for w in range(SWA_WIN // blk)], axis=1) for i in idx]
        s_win = [jnp.dot(qn[i], kw_t[i], preferred_element_type=F32) * scale for i in idx]
        s_win = [jnp.where(jnp.abs(delta + (j0[i] * blk - r0[i])) <= SWA_WINDOW, s_win[i], NEG_INF) for i in idx]
        s_ctx = [jnp.dot(qn[i], kc_t, preferred_element_type=F32) * scale for i in idx]
        m = [jnp.maximum(row_max(s_win[i], s_ctx[i]), sk_b) for i in idx]
        e_win = [jnp.exp(s_win[i] - m[i]) for i in idx]
        e_ctx = [jnp.exp(s_ctx[i] - m[i]) for i in idx]
        den = [row_sum(e_win[i], e_ctx[i]) + jnp.exp(sk_b - m[i]) for i in idx]
        o = [(jnp.dot(e_win[i].astype(BF16), vb_ref[CTX + j0[i] * blk:CTX + j0[i] * blk + SWA_WIN, :],
                      preferred_element_type=F32)
              + jnp.dot(e_ctx[i].astype(BF16), vc, preferred_element_type=F32)) / den[i] for i in idx]
        for i in idx:
            for gi in range(g):
                o_ref[CTX + r0[i]:CTX + r0[i] + blk, gi * HD:(gi + 1) * HD] = (
                    o[i][gi * blk:(gi + 1) * blk].astype(o_ref.dtype))


def swa_mixer(p_swa, sink_l, cos_full, sin_signed):
    gw = SWA_GROUP * HD
    rows = p_swa.shape[0]
    return pl.pallas_call(
        _swa_kernel,
        grid=(rows // TB, SWA_KV_HEADS),
        in_specs=[
            pl.BlockSpec(memory_space=pltpu.SMEM),
            pl.BlockSpec((TB, gw), lambda b, kv: (b, kv)),
            pl.BlockSpec((TB, HD), lambda b, kv: (b, SWA_Q_HEADS + kv)),
            pl.BlockSpec((TB, HD), lambda b, kv: (b, SWA_Q_HEADS + SWA_KV_HEADS + kv)),
            pl.BlockSpec((SEQ, HD), lambda b, kv: (0, 0)),
            pl.BlockSpec((SEQ, HD), lambda b, kv: (0, 0)),
        ],
        out_specs=pl.BlockSpec((TB, gw), lambda b, kv: (b, kv)),
        out_shape=jax.ShapeDtypeStruct((rows, W), BF16),
        scratch_shapes=[pltpu.VMEM((SWA_NBLK, HD, SWA_BLOCK), BF16), pltpu.VMEM((TB, HD), BF16)],
        compiler_params=_cparams(("parallel", "parallel")),
        name="swa",
    )(sink_l, p_swa, p_swa, p_swa, cos_full, sin_signed)


def rope_tables():
    rows = SEQ // GRID_W
    row = jnp.repeat(jnp.arange(rows), GRID_W).astype(F32)
    col = jnp.tile(jnp.arange(GRID_W), rows).astype(F32)
    n_freq = HD // 4
    inv_freq = ROPE_BASE ** (-jnp.arange(n_freq, dtype=F32) / n_freq)
    ang_r = row[:, None] * inv_freq
    ang_c = col[:, None] * inv_freq
    cr, sr, cc, sc = jnp.cos(ang_r), jnp.sin(ang_r), jnp.cos(ang_c), jnp.sin(ang_c)
    cos_full = jnp.concatenate([cr, cr, cc, cc], axis=-1)
    sin_signed = jnp.concatenate([-sr, sr, -sc, sc], axis=-1)
    return cos_full, sin_signed


RW_NCHUNK = TB // RW_C
RW_CTX_CHUNKS = CTX // RW_C
RW_GPC = RW_GW // RW_C


def _rw_chunk(direction, s):
    back = jnp.where(s < RW_CTX_CHUNKS, RW_CTX_CHUNKS - 1 - s, RW_NCHUNK - 1 + RW_CTX_CHUNKS - s)
    return jnp.where(direction == 0, s, back)


def _head_ones():
    r = lax.broadcasted_iota(jnp.int32, (RW_GW, RW_GW), 0) // RW_N
    c = lax.broadcasted_iota(jnp.int32, (RW_GW, RW_GW), 1) // RW_N
    return r == c


def _dot16(a, b):
    return jnp.dot(a.astype(BF16), b.astype(BF16), preferred_element_type=F32)


def _dot16_nt(a, b):
    return lax.dot_general(a.astype(BF16), b.astype(BF16), (((1,), (1,)), ((), ())), preferred_element_type=F32)


def _split_bf16(x, terms):
    parts = []
    for _ in range(terms):
        p = x.astype(BF16)
        parts.append(p)
        x = x - p.astype(F32)
    return parts


def _dot_exact_lhs(x, b16, terms):
    return jnp.dot(jnp.concatenate(_split_bf16(x, terms), axis=1), jnp.concatenate([b16] * terms, axis=0),
                   preferred_element_type=F32)


def _rwkv_kernel(r_ref, k_ref, v_ref, sm_ref, wup_ref, aup_ref, w0_ref, a0_ref, kk_ref, ka_ref, rk_ref, gup_ref,
                 lng_ref, o_ref, q_s, y1_s, m_s, z_s, bonus_s, gate_s, st_ref, ybuf):
    direction = pl.program_id(1)
    step = pl.program_id(2)
    cur = step % 2
    prev = 1 - cur
    c = RW_C
    same_head = _head_ones()
    mask16 = jnp.where(same_head, 1.0, 0.0).astype(BF16)
    nbat = r_ref.shape[0]
    chains = [(bb, g) for bb in range(nbat) for g in range(RW_GROUPS)]
    groups = range(len(chains))
    bbs = [bb for bb, _ in chains]
    sls = [slice(g * RW_GW, (g + 1) * RW_GW) for _, g in chains]

    @pl.when(step == 0)
    def _():
        st_ref[...] = jnp.zeros_like(st_ref)
        for ref in (q_s, y1_s, m_s, z_s, bonus_s, gate_s):
            ref[1] = jnp.zeros(ref.shape[1:], ref.dtype)

    ys = []
    for g in groups:
        m_bd = jnp.concatenate([m_s[prev, bbs[g], :, sls[g]]] * RW_GPC, axis=0) * mask16
        z_bd = jnp.where(same_head, jnp.concatenate([z_s[prev, bbs[g], :, sls[g]]] * RW_GPC, axis=0), 0.0)
        res = _dot16(jnp.concatenate([q_s[prev, bbs[g], :, sls[g]], m_bd], axis=0), st_ref[g])
        ys.append(res[:c] + y1_s[prev, bbs[g], :, sls[g]])
        st_ref[g] = res[c:] + z_bd

    def bd(x):
        return jnp.concatenate([x.astype(BF16)] * RW_GPC, axis=0) * mask16

    sgn = 1 - 2 * direction
    cum_terms = 2
    tt = lax.broadcasted_iota(jnp.int32, (c, cum_terms * c), 0)
    ss = lax.broadcasted_iota(jnp.int32, (c, cum_terms * c), 1) % c
    tri16 = jnp.where((tt - ss) * sgn >= 0, 1.0, 0.0).astype(BF16)
    t4 = lax.broadcasted_iota(jnp.int32, (c, RW_GW), 0)
    s4 = lax.broadcasted_iota(jnp.int32, (c, RW_GW), 1) % c
    d4 = (t4 - s4) * sgn
    strict = d4 > 0
    incl = d4 >= 0
    eye = jnp.where(d4 == 0, 1.0, 0.0)

    sm = [sm_ref[bb] for bb in range(nbat)]
    tanh_wd = [jnp.tanh(x[:, :2 * RW_RANK]).astype(BF16) for x in sm]
    ad = [x[:, 2 * RW_RANK:4 * RW_RANK].astype(BF16) for x in sm]

    r = [r_ref[bbs[g], :, sls[g]] for g in groups]
    k = [k_ref[bbs[g], :, sls[g]] for g in groups]
    v = [v_ref[bbs[g], :, sls[g]] for g in groups]
    logw = [-jax.nn.sigmoid(w0_ref[0, :, sls[g]] + jnp.dot(tanh_wd[bbs[g]], wup_ref[0, :, sls[g]],
                                                          preferred_element_type=F32)) * math.exp(-0.5)
            for g in groups]

    def alpha_of(d, g):
        return jax.nn.sigmoid(a0_ref[d, :, sls[g]] + jnp.dot(ad[bbs[g]], aup_ref[d, :, sls[g]],
                                                             preferred_element_type=F32))

    alpha = [alpha_of(direction, g) for g in groups]
    alpha_both = [alpha[g] + alpha_of(1 - direction, g) for g in groups]
    kkp = [k[g] * kk_ref[:, sls[g]] for g in groups]
    kdir = [k[g] * (1.0 + (alpha[g] - 1.0) * ka_ref[:, sls[g]]) for g in groups]
    kboth = [k[g] * (2.0 + (alpha_both[g] - 2.0) * ka_ref[:, sls[g]]) for g in groups]
    nch = len(chains)
    sums = _dot_exact_lhs(jnp.concatenate([kkp[g] * kkp[g] for g in groups]
                                          + [r[g] * kboth[g] * rk_ref[:, sls[g]] for g in groups], axis=0), mask16, 2)
    kk = [kkp[g] / jnp.maximum(jnp.sqrt(sums[g * c:(g + 1) * c]), 1e-12) for g in groups]
    for g in groups:
        bonus_s[cur, bbs[g], :, sls[g]] = sums[(nch + g) * c:(nch + g + 1) * c] * v[g]
    for bb in range(nbat):
        gate_s[cur, bb] = jnp.dot(jax.nn.sigmoid(sm[bb][:, 4 * RW_RANK:]).astype(BF16), gup_ref[...],
                                  preferred_element_type=F32)
    b_vec = [kk[g] * alpha[g] for g in groups]

    cum = [jnp.dot(tri16, jnp.concatenate(_split_bf16(logw[g], cum_terms), axis=0), preferred_element_type=F32)
           for g in groups]
    total = [jnp.sum(logw[g], axis=0, keepdims=True) for g in groups]
    e_out = [jnp.exp(-cum[g]) for g in groups]
    e_left = [jnp.exp(total[g] - cum[g]) for g in groups]
    a_t = [-kk[g] * jnp.exp(cum[g] - logw[g]) for g in groups]
    r_t = [r[g] * jnp.exp(cum[g]) for g in groups]
    lhs = [jnp.concatenate([a_t[g], r_t[g]], axis=0) for g in groups]
    gb = [_dot16_nt(lhs[g], bd(b_vec[g] * e_out[g])) for g in groups]
    gk = [_dot16_nt(lhs[g], bd(kdir[g] * e_out[g])) for g in groups]
    a_ab = [jnp.where(strict, gb[g][:c], 0.0) for g in groups]
    a_ak = [jnp.where(strict, gk[g][:c], 0.0) for g in groups]
    a_rb = [jnp.where(incl, gb[g][c:], 0.0) for g in groups]
    a_rk = [jnp.where(incl, gk[g][c:], 0.0) for g in groups]

    pt = jnp.where(direction == 0, t4, c - 1 - t4)
    ps = jnp.where(direction == 0, s4, c - 1 - s4)
    px = pt ^ ps

    def level_mask(lv):
        return (px >> lv) * 2 + ((pt >> lv) & 1) == 3

    tinv = [eye + jnp.where(level_mask(0), a_ab[g], 0.0) for g in groups]
    for lv in range(1, int(math.log2(c))):
        lm = level_mask(lv)
        cross = [_dot16(jnp.where(lm, a_ab[g], 0.0), bd(tinv[g])) for g in groups]
        tinv = [tinv[g] + _dot16(tinv[g], bd(cross[g])) for g in groups]

    eye16 = eye.astype(BF16)
    blt = [_dot16_nt(eye16, bd(b_vec[g] * e_left[g])) for g in groups]
    klt = [_dot16_nt(eye16, bd(kdir[g] * e_left[g])) for g in groups]
    vprod = [_dot16(jnp.concatenate([a_ak[g], a_rk[g], klt[g]], axis=0), bd(v[g])) for g in groups]
    pw = [_dot16(tinv[g], jnp.concatenate([bd(a_t[g]), bd(vprod[g][:c])], axis=1)) for g in groups]
    top = [_dot16(jnp.concatenate([a_rb[g], blt[g]], axis=0),
                  jnp.concatenate([bd(pw[g][:, :RW_GW]), bd(pw[g][:, RW_GW:])], axis=1)) for g in groups]
    low = [vprod[g][c:] for g in groups]
    for g in groups:
        sl = sls[g]
        q_s[cur, bbs[g], :, sl] = (r_t[g] + top[g][:c, :RW_GW]).astype(q_s.dtype)
        y1_s[cur, bbs[g], :, sl] = top[g][:c, RW_GW:] + low[g][:c]
        m_s[cur, bbs[g], :, sl] = (eye * jnp.exp(total[g]) + top[g][c:, :RW_GW]).astype(m_s.dtype)
        z_s[cur, bbs[g], :, sl] = top[g][c:, RW_GW:] + low[g][c:]

    chunk_prev = _rw_chunk(direction, jnp.maximum(step - 1, 0))

    @pl.when(direction == 0)
    def _():
        for g in groups:
            ybuf[bbs[g], chunk_prev, :, sls[g]] = ys[g].astype(ybuf.dtype)

    @pl.when(direction == 1)
    def _():
        y4 = jnp.concatenate([ys[g] + ybuf[bbs[g], chunk_prev, :, sls[g]] for g in groups], axis=0)
        mean = _dot_exact_lhs(y4, mask16, 2) * (1.0 / RW_N)
        yc = y4 - mean
        var = _dot_exact_lhs(yc * yc, mask16, 2) * (1.0 / RW_N)
        yn = yc * lax.rsqrt(var + RW_GN_EPS)
        for g in groups:
            sl = sls[g]
            o_ref[bbs[g], :, sl] = ((yn[g * c:(g + 1) * c] * lng_ref[:, sl] + bonus_s[prev, bbs[g], :, sl])
                                    * gate_s[prev, bbs[g], :, sl]).astype(o_ref.dtype)


RW_BPS = 4


def rwkv_mixer(ps, layer, wup2, aup2, w0, a0, k_k, k_a, r_k, g_up, ln_g):
    rows = ps.shape[0]
    nb = rows // TB
    nsteps = RW_NCHUNK + 1

    def in_blk(d, s):
        return _rw_chunk(d, jnp.minimum(s, RW_NCHUNK - 1))

    def out_blk(d, s):
        return _rw_chunk(1, jnp.where(d == 0, 0, jnp.maximum(s - 1, 0)))

    feat = lambda off: pl.BlockSpec((RW_BPS, RW_C, W), lambda b, d, s, off=off: (b, in_blk(d, s), off))
    vec = pl.BlockSpec((None, 1, W), lambda b, d, s: (layer, 0, 0))
    per_dir = lambda r: pl.BlockSpec((None, 1, r, W), lambda b, d, s: (layer, d, 0, 0))
    both_dirs = lambda r: pl.BlockSpec((None, 2, r, W), lambda b, d, s: (layer, 0, 0, 0))
    slot = lambda dt: pltpu.VMEM((2, RW_BPS, RW_C, W), dt)
    out = pl.pallas_call(
        _rwkv_kernel,
        grid=(nb // RW_BPS, 2, nsteps),
        in_specs=[feat(0), feat(1), feat(2),
                  pl.BlockSpec((RW_BPS, RW_C, RW_SMALL), lambda b, d, s: (b, in_blk(d, s), 3 * W // RW_SMALL)),
                  per_dir(2 * RW_RANK), both_dirs(2 * RW_RANK), per_dir(1), both_dirs(1), vec, vec, vec,
                  pl.BlockSpec((None, RW_G_RANK, W), lambda b, d, s: (layer, 0, 0)), vec],
        out_specs=pl.BlockSpec((RW_BPS, RW_C, W), lambda b, d, s: (b, out_blk(d, s), 0)),
        out_shape=jax.ShapeDtypeStruct((nb, TB, W), BF16),
        scratch_shapes=[slot(BF16), slot(F32), slot(BF16), slot(F32), slot(F32), slot(F32),
                        pltpu.VMEM((RW_BPS * RW_GROUPS, RW_GW, RW_GW), F32),
                        pltpu.VMEM((RW_BPS, RW_NCHUNK, RW_C, W), BF16)],
        compiler_params=_cparams(("arbitrary", "arbitrary", "arbitrary")),
        name="rwkv_mixer",
    )(*[ps.reshape(nb, TB, RWKV_IN)] * 4, wup2, aup2, w0, a0, k_k, k_a, r_k,
      g_up, ln_g)
    return out.reshape(rows, W)


MG_TM = 1152
MG_TN = 256


def _merge_kernel(u_ref, y0_ref, y1_ref, y2_ref, wg0_ref, wg1_ref, wg2_ref, wb_ref, o_ref, wgc_ref, wbc_ref):
    @pl.when(pl.program_id(1) == 0)
    def _():
        for n, wg_ref in enumerate((wg0_ref, wg1_ref, wg2_ref)):
            wgc_ref[n] = wg_ref[...].astype(BF16)
        wbc_ref[...] = wb_ref[...].astype(BF16)

    u = u_ref[...]
    acc = None
    for n, y_ref in enumerate((y0_ref, y1_ref, y2_ref)):
        gate = jax.nn.sigmoid(jnp.dot(u, wgc_ref[n], preferred_element_type=F32))
        term = gate * jnp.dot(y_ref[...], wbc_ref[n], preferred_element_type=F32)
        acc = term if acc is None else acc + term
    o_ref[...] = acc.astype(o_ref.dtype)


def merge_branches(u, y_ret, y_swa, y_rwkv, w_in, gate_col0, w_branch, layer):
    nj = D // MG_TN
    ysp = pl.BlockSpec((MG_TM, W), lambda j, i: (i, 0))
    wgsp = lambda n: pl.BlockSpec(
        (pl.Element(D), pl.Element(MG_TN)),
        lambda j, i, n=n: (layer * D, pl.multiple_of(gate_col0 + n * D + j * MG_TN, 128)))
    return pl.pallas_call(
        _merge_kernel,
        grid=(nj, R // MG_TM),
        in_specs=[pl.BlockSpec((MG_TM, D), lambda j, i: (i, 0)), ysp, ysp, ysp, wgsp(0), wgsp(1), wgsp(2),
                  pl.BlockSpec((None, 3, W, MG_TN), lambda j, i: (layer, 0, 0, j))],
        out_specs=pl.BlockSpec((MG_TM, MG_TN), lambda j, i: (i, j)),
        out_shape=jax.ShapeDtypeStruct((R, D), BF16),
        scratch_shapes=[pltpu.VMEM((3, D, MG_TN), BF16), pltpu.VMEM((3, W, MG_TN), BF16)],
        compiler_params=_cparams(("arbitrary", "arbitrary")),
        name="merge",
    )(u, y_ret, y_swa, y_rwkv, *[w_in.reshape(-1, w_in.shape[-1])] * 3, w_branch)


def kernel(x, c, ctx, c_ctx, norm1_g, norm2_g, w_mod, b_mod, w_in, ret_decay, swa_sink, rwkv_mu, rwkv_w0,
           rwkv_w_up, rwkv_a0, rwkv_a_up, rwkv_g_up, rwkv_k_k, rwkv_k_a, rwkv_r_k, rwkv_ln_g, w_branch,
           w_out, w_ff1, w_ff2, final_g):
    h = jnp.concatenate([ctx, x], axis=1).reshape(R, D)
    cond8 = jnp.concatenate([c, c_ctx[None, :], jnp.zeros((8 - B - 1, D), F32)], axis=0)
    mods = mod_vectors(cond8, w_mod, b_mod)
    mods = mods.reshape(L, 8, 6, D).transpose(0, 2, 1, 3)
    cos_full, sin_signed = rope_tables()

    o_swa = RET_IN
    o_rw = RET_IN + SWA_IN
    o_small = o_rw + 3 * W
    o_gate = o_rw + RWKV_IN
    zl = jnp.zeros((L, RW_RANK, W), BF16)
    both = lambda a: jnp.stack([jnp.concatenate([a[:, 0], zl], axis=1), jnp.concatenate([zl, a[:, 1]], axis=1)], axis=1)
    rw_params = (both(rwkv_w_up.astype(BF16)), both(rwkv_a_up.astype(BF16)), rwkv_w0.reshape(L, 2, 1, W),
                 rwkv_a0.reshape(L, 2, 1, W), rwkv_k_k.reshape(L, 1, W), rwkv_k_a.reshape(L, 1, W),
                 rwkv_r_k.reshape(L, 1, W), rwkv_g_up.astype(BF16), rwkv_ln_g.reshape(L, 1, W))

    for l in range(L):
        m_l = mods[l]
        u = norm_modulate(h, norm1_g[l], m_l, 0, 1)
        p_ret = matmul_f32w(u, w_in, l, 0, RET_IN, tn=1024, name="in_ret")
        p_swa = matmul_f32w(u, w_in, l, o_swa, SWA_IN, tn=768, name="in_swa")
        ps_rwkv = matmul_f32w(u, w_in, l, o_rw, RWKV_IN, tn=512, tm=TB, epilogue="token_shift", mu=rwkv_mu[l],
                              name="in_rwkv")

        y_ret = retention_mixer(p_ret, ret_decay[l])
        y_swa = swa_mixer(p_swa, swa_sink[l], cos_full, sin_signed)

        y_rwkv = rwkv_mixer(ps_rwkv, l, *rw_params)

        merged = merge_branches(u, y_ret, y_swa, y_rwkv, w_in, o_gate, w_branch, l)
        h = matmul_f32w(merged, w_out, l, 0, D, tn=1024, epilogue="residual", h=h, mods_l=m_l,
                        which_gate=2, name="out_proj")
        u2 = norm_modulate(h, norm2_g[l], m_l, 3, 4)
        f = matmul_f32w(u2, w_ff1, l, 0, D_FF, tn=1024, out_dtype=BF16, epilogue="relu2", name="ff1")
        h = matmul_residual(f, w_ff2[l].astype(BF16), h, m_l, 5, tm=576, tn=512, name="ff2")

    return final_norm(h, final_g).reshape(B, SEQ, D)
```

```python
import functools
import math

import jax
import jax.numpy as jnp
from jax import lax
from jax.experimental import pallas as pl
from jax.experimental.pallas import tpu as pltpu

F32 = jnp.float32
BF16 = jnp.bfloat16

D = 2048
B = 4
SEQ = 2048
CTX = 256
TB = CTX + SEQ
R = B * TB
L = 4
GRID_W = 64
EPS = 1e-6
ROPE_BASE = 10000.0
NEG_INF = -1e30
W = D // 2
HD = 128
RET_HEADS = W // HD
RET_CHUNK = 128
RET_GN_EPS = 1e-5
SWA_Q_HEADS = W // HD
SWA_KV_HEADS = SWA_Q_HEADS // 4
SWA_GROUP = SWA_Q_HEADS // SWA_KV_HEADS
SWA_WINDOW = 128
SWA_BLOCK = 128
RW_N = 64
RW_RANK = 64
RW_G_RANK = 128
RW_GN_EPS = 64e-5
RW_C = 64
RW_GW = 256
RW_GROUPS = W // RW_GW
RW_SMALL = 4 * RW_RANK + RW_G_RANK
D_FF = 4 * D
RET_IN = 4 * W
SWA_IN = (SWA_Q_HEADS + 2 * SWA_KV_HEADS) * HD
RWKV_IN = 3 * W + RW_SMALL
GATE_IN = 3 * D
N_IN = RET_IN + SWA_IN + RWKV_IN + GATE_IN

VMEM_LIMIT = 56 * 1024 * 1024


def _cparams(sem):
    return pltpu.CompilerParams(dimension_semantics=sem, vmem_limit_bytes=VMEM_LIMIT)


def _mod_kernel(x_ref, w_ref, b_ref, o_ref):
    x = x_ref[...]
    s = x * jax.nn.sigmoid(x)
    acc = jnp.dot(s.astype(BF16), w_ref[0].astype(BF16), preferred_element_type=F32)
    o_ref[0] = acc + b_ref[0]


def mod_vectors(cond8, w_mod, b_mod):
    tn = 1024
    n = w_mod.shape[-1]
    return pl.pallas_call(
        _mod_kernel,
        grid=(L, n // tn),
        in_specs=[
            pl.BlockSpec((8, D), lambda l, j: (0, 0)),
            pl.BlockSpec((1, D, tn), lambda l, j: (l, 0, j)),
            pl.BlockSpec((1, 1, tn), lambda l, j: (l, 0, j)),
        ],
        out_specs=pl.BlockSpec((1, 8, tn), lambda l, j: (l, 0, j)),
        out_shape=jax.ShapeDtypeStruct((L, 8, n), F32),
        compiler_params=_cparams(("parallel", "parallel")),
        name="mod_vectors",
    )(cond8, w_mod, b_mod.reshape(L, 1, n))


NORM_TM = 256
TILES_PER_BATCH = TB // NORM_TM


NORM_MOD_TM = 768
NORM_SUB = 64


def _norm_mod_kernel(x_ref, g_ref, sh_ref, sc_ref, o_ref):
    i = pl.program_id(0)
    tm = NORM_MOD_TM
    bidx = i // (TB // tm)
    first_tile = i % (TB // tm) == 0
    g = g_ref[...]
    sub = NORM_SUB

    def body(r, carry):
        row = jnp.where(jnp.logical_and(first_tile, r * sub < CTX), B, bidx)
        rows = pl.ds(pl.multiple_of(r * sub, sub), sub)
        x = x_ref[rows, :]
        ms = jnp.mean(x * x, axis=-1, keepdims=True)
        y = x * lax.rsqrt(ms + EPS) * g
        o_ref[rows, :] = (y * (1.0 + sc_ref[0, pl.ds(row, 1), :]) + sh_ref[0, pl.ds(row, 1), :]).astype(o_ref.dtype)
        return carry

    lax.fori_loop(0, tm // sub, body, 0, unroll=4)


def norm_modulate(h, g, mods_l, which_shift, which_scale):
    return pl.pallas_call(
        _norm_mod_kernel,
        grid=(R // NORM_MOD_TM,),
        in_specs=[
            pl.BlockSpec((NORM_MOD_TM, D), lambda i: (i, 0)),
            pl.BlockSpec((1, D), lambda i: (0, 0)),
            pl.BlockSpec((1, 8, D), lambda i: (which_shift, 0, 0)),
            pl.BlockSpec((1, 8, D), lambda i: (which_scale, 0, 0)),
        ],
        out_specs=pl.BlockSpec((NORM_MOD_TM, D), lambda i: (i, 0)),
        out_shape=jax.ShapeDtypeStruct((R, D), BF16),
        compiler_params=_cparams(("parallel",)),
        name="norm_modulate",
    )(h, g.reshape(1, D), mods_l, mods_l)


def _final_norm_kernel(x_ref, g_ref, o_ref):
    x = x_ref[...]
    ms = jnp.mean(x * x, axis=-1, keepdims=True)
    o_ref[...] = x * lax.rsqrt(ms + EPS) * g_ref[...]


def final_norm(h, g):
    per = SEQ // NORM_TM
    return pl.pallas_call(
        _final_norm_kernel,
        grid=(B, per),
        in_specs=[
            pl.BlockSpec((NORM_TM, D), lambda b, j: (b * TILES_PER_BATCH + CTX // NORM_TM + j, 0)),
            pl.BlockSpec((1, D), lambda b, j: (0, 0)),
        ],
        out_specs=pl.BlockSpec((NORM_TM, D), lambda b, j: (b * per + j, 0)),
        out_shape=jax.ShapeDtypeStruct((B * SEQ, D), F32),
        compiler_params=_cparams(("parallel", "parallel")),
        name="final_norm",
    )(h, g.reshape(1, D))


MM_TM = 1152


def _row_is_ctx(i, tm):
    rows = lax.broadcasted_iota(jnp.int32, (tm, 1), 0)
    return jnp.logical_and(i % (TB // tm) == 0, rows < CTX)


def _mm_finish(acc, i, o_ref, h_ref, gate_ref, epilogue, tm):
    if epilogue == "relu2":
        a = jnp.maximum(acc, 0.0)
        o_ref[...] = (a * a).astype(o_ref.dtype)
    elif epilogue == "residual":
        bidx = i // (TB // tm)
        g_b = gate_ref[0, pl.ds(bidx, 1), :]
        g_c = gate_ref[0, pl.ds(B, 1), :]
        gate = jnp.where(_row_is_ctx(i, tm), g_c, g_b)
        o_ref[...] = h_ref[...] + gate * acc
    elif epilogue == "token_shift":
        o_ref[...] = _token_shift(acc, gate_ref)
    else:
        o_ref[...] = acc.astype(o_ref.dtype)


def _token_shift(x, mu_ref):
    row = lax.broadcasted_iota(jnp.int32, (TB, 1), 0)
    prev = pltpu.roll(x, 1, 0)
    prev = jnp.where(jnp.logical_or(row == 0, row == CTX), 0.0, prev)
    nxt = pltpu.roll(x, TB - 1, 0)
    nxt = jnp.where(jnp.logical_or(row == CTX - 1, row == TB - 1), 0.0, nxt)
    return x + mu_ref[0:1, :] * (prev - x) + mu_ref[1:2, :] * (nxt - x)


def _mm_cached_kernel(*refs, epilogue, tm):
    if epilogue == "residual":
        x_ref, w_ref, h_ref, gate_ref, o_ref, wc_ref = refs
    elif epilogue == "token_shift":
        x_ref, w_ref, gate_ref, o_ref, wc_ref = refs
        h_ref = None
    else:
        x_ref, w_ref, o_ref, wc_ref = refs
        h_ref = gate_ref = None
    i = pl.program_id(1)

    @pl.when(i == 0)
    def _():
        wc_ref[...] = w_ref[...].astype(BF16)

    acc = jnp.dot(x_ref[...], wc_ref[...], preferred_element_type=F32)
    _mm_finish(acc, i, o_ref, h_ref, gate_ref, epilogue, tm)


def matmul_f32w(x, w_stack, layer, col0, n, *, tn, tm=MM_TM, out_dtype=F32, epilogue="plain", h=None, mods_l=None,
                which_gate=None, mu=None, name="matmul"):
    m, kdim = x.shape
    in_specs = [
        pl.BlockSpec((tm, kdim), lambda j, i: (i, 0)),
        pl.BlockSpec((pl.Element(kdim), pl.Element(tn)), lambda j, i: (layer * kdim, pl.multiple_of(col0 + j * tn, 128))),
    ]
    args = [x, w_stack.reshape(-1, w_stack.shape[-1])]
    aliases = {}
    if epilogue == "residual":
        in_specs += [
            pl.BlockSpec((tm, tn), lambda j, i: (i, j)),
            pl.BlockSpec((1, 8, tn), lambda j, i: (which_gate, 0, j)),
        ]
        args += [h, mods_l]
        aliases = {2: 0}
    elif epilogue == "token_shift":
        assert tm == TB
        in_specs.append(pl.BlockSpec((2, tn), lambda j, i: (0, j)))
        args.append(mu)
    return pl.pallas_call(
        functools.partial(_mm_cached_kernel, epilogue=epilogue, tm=tm),
        grid=(pl.cdiv(n, tn), m // tm),
        in_specs=in_specs,
        out_specs=pl.BlockSpec((tm, tn), lambda j, i: (i, j)),
        out_shape=jax.ShapeDtypeStruct((m, n), out_dtype),
        scratch_shapes=[pltpu.VMEM((kdim, tn), BF16)],
        input_output_aliases=aliases,
        compiler_params=_cparams(("arbitrary", "arbitrary")),
        name=name,
    )(*args)


def _mm_residual_kernel(x_ref, w_ref, h_ref, gate_ref, o_ref, *, tm):
    acc = jnp.dot(x_ref[...], w_ref[...], preferred_element_type=F32)
    _mm_finish(acc, pl.program_id(0), o_ref, h_ref, gate_ref, "residual", tm)


def matmul_residual(x, w_stack, layer, h, mods_l, which_gate, *, tm, tn, name):
    m, kdim = x.shape
    n = w_stack.shape[2]
    return pl.pallas_call(
        functools.partial(_mm_residual_kernel, tm=tm),
        grid=(m // tm, n // tn),
        in_specs=[
            pl.BlockSpec((tm, kdim), lambda i, j: (i, 0)),
            pl.BlockSpec((None, kdim, tn), lambda i, j: (layer, 0, j)),
            pl.BlockSpec((tm, tn), lambda i, j: (i, j)),
            pl.BlockSpec((1, 8, tn), lambda i, j: (which_gate, 0, j)),
        ],
        out_specs=pl.BlockSpec((tm, tn), lambda i, j: (i, j)),
        out_shape=jax.ShapeDtypeStruct((m, n), F32),
        input_output_aliases={2: 0},
        compiler_params=_cparams(("parallel", "parallel")),
        name=name,
    )(x, w_stack, h, mods_l)


RET_NCHUNK = TB // RET_CHUNK
RET_CTX_CHUNKS = CTX // RET_CHUNK


def _log_sigmoid(x):
    return jnp.minimum(x, 0.0) - jnp.log(1.0 + jnp.exp(-jnp.abs(x)))


RET_BATCH = 9


def _ret_kernel(dec_ref, q_ref, k_ref, v_ref, g_ref, o_ref, y_ref):
    hh = pl.program_id(1)
    c = RET_CHUNK
    ri = lax.broadcasted_iota(jnp.int32, (c, c), 0).astype(F32)
    ci = lax.broadcasted_iota(jnp.int32, (c, c), 1).astype(F32)
    kscale = HD ** -0.5
    nt = (((1,), (1,)), ((), ()))
    for direction in range(2):
        lg = _log_sigmoid(jnp.full((c, c), dec_ref[direction, hh], F32))
        diff = (ri - ci) if direction == 0 else (ci - ri)
        intra = jnp.where(diff >= 0, jnp.exp(lg * jnp.maximum(diff, 0.0)), 0.0)
        pos = ri if direction == 0 else (c - 1.0) - ri
        q_decay = jnp.exp(lg * (pos + 1.0))
        k_decay = jnp.exp(lg * ((c - 1.0) - pos))
        chunk_decay = jnp.exp(lg * float(c))
        if direction == 0:
            order = list(range(RET_NCHUNK))
        else:
            order = list(range(RET_CTX_CHUNKS - 1, -1, -1)) + list(range(RET_NCHUNK - 1, RET_CTX_CHUNKS - 1, -1))
        state = jnp.zeros((c, c), F32)
        for b0 in range(0, RET_NCHUNK, RET_BATCH):
            rows = [slice(ch * c, (ch + 1) * c) for ch in order[b0:b0 + RET_BATCH]]
            n = range(len(rows))
            q = [q_ref[r, :] for r in rows]
            k = [k_ref[r, :] * kscale for r in rows]
            v = [v_ref[r, :].astype(BF16) for r in rows]
            sc = [lax.dot_general(q[i].astype(BF16), k[i].astype(BF16), nt, preferred_element_type=F32) * intra
                  for i in n]
            y_in = [jnp.dot(sc[i].astype(BF16), v[i], preferred_element_type=F32) for i in n]
            kv = [jnp.dot((k[i] * k_decay).T.astype(BF16), v[i], preferred_element_type=F32) for i in n]
            states = []
            for i in n:
                states.append(state)
                state = state * chunk_decay + kv[i]
            y_x = [jnp.dot((q[i] * q_decay).astype(BF16), states[i].astype(BF16), preferred_element_type=F32)
                   for i in n]
            for i in n:
                y = y_in[i] + y_x[i]
                if direction == 0:
                    y_ref[rows[i], :] = y
                else:
                    y = y + y_ref[rows[i], :]
                    mean = jnp.mean(y, axis=-1, keepdims=True)
                    yc = y - mean
                    var = jnp.mean(yc * yc, axis=-1, keepdims=True)
                    g = g_ref[rows[i], :]
                    o_ref[rows[i], :] = (g * jax.nn.sigmoid(g) * (yc * lax.rsqrt(var + RET_GN_EPS))).astype(o_ref.dtype)


def retention_mixer(p_ret, decay_l):
    nh = RET_HEADS
    blk = lambda off: pl.BlockSpec((TB, HD), lambda b, h, off=off: (b, off + h))
    rows = p_ret.shape[0]
    return pl.pallas_call(
        _ret_kernel,
        grid=(rows // TB, nh),
        in_specs=[pl.BlockSpec(memory_space=pltpu.SMEM), blk(0), blk(nh), blk(2 * nh), blk(3 * nh)],
        out_specs=pl.BlockSpec((TB, HD), lambda b, h: (b, h)),
        out_shape=jax.ShapeDtypeStruct((rows, W), BF16),
        scratch_shapes=[pltpu.VMEM((TB, HD), F32)],
        compiler_params=_cparams(("parallel", "parallel")),
        name="retention",
    )(decay_l, p_ret, p_ret, p_ret, p_ret)


SWA_NBLK = SEQ // SWA_BLOCK
SWA_WIN = 3 * SWA_BLOCK
SWA_BATCH = 4


def _rope(x, cos, sin_signed):
    lane = lax.broadcasted_iota(jnp.int32, x.shape, 1)
    partner = jnp.where(lane % 64 < 32, pltpu.roll(x, 96, 1), pltpu.roll(x, 32, 1))
    return x * cos + partner * sin_signed


def _swa_kernel(sink_ref, q_ref, k_ref, v_ref, cos_ref, sin_ref, o_ref, kt_ref, vb_ref):
    kvh = pl.program_id(1)
    scale = HD ** -0.5
    g = SWA_GROUP
    blk = SWA_BLOCK

    vb_ref[...] = v_ref[...].astype(BF16)
    for j in range(SWA_NBLK):
        rows = slice(j * blk, (j + 1) * blk)
        k_rot = _rope(k_ref[CTX + j * blk:CTX + (j + 1) * blk, :], cos_ref[rows, :], sin_ref[rows, :])
        kt_ref[j] = k_rot.T.astype(BF16)
    kc_t = jnp.concatenate([k_ref[j * blk:(j + 1) * blk, :].T for j in range(CTX // blk)], axis=1).astype(BF16)
    vc = vb_ref[:CTX, :]

    def sink_col(rows_per_head):
        parts = [jnp.full((rows_per_head, 1), sink_ref[kvh * g + gi], F32) for gi in range(g)]
        return jnp.concatenate(parts, axis=0)

    def lane_tiles(x):
        return [x[:, j * HD:(j + 1) * HD] for j in range(x.shape[1] // HD)]

    def row_max(*xs):
        tiles = [t for x in xs for t in lane_tiles(x)]
        return jnp.max(functools.reduce(jnp.maximum, tiles), axis=-1, keepdims=True)

    def row_sum(*xs):
        tiles = [t for x in xs for t in lane_tiles(x)]
        return jnp.sum(functools.reduce(jnp.add, tiles), axis=-1, keepdims=True)

    qc = jnp.concatenate([q_ref[:CTX, gi * HD:(gi + 1) * HD] for gi in range(g)], axis=0).astype(BF16)
    s = jnp.dot(qc, kc_t, preferred_element_type=F32) * scale
    sk = sink_col(CTX)
    m = jnp.maximum(row_max(s), sk)
    e = jnp.exp(s - m)
    den = row_sum(e) + jnp.exp(sk - m)
    oc = jnp.dot(e.astype(BF16), vc, preferred_element_type=F32) / den
    for gi in range(g):
        o_ref[:CTX, gi * HD:(gi + 1) * HD] = oc[gi * CTX:(gi + 1) * CTX].astype(o_ref.dtype)

    sk_b = sink_col(blk)
    delta = (lax.broadcasted_iota(jnp.int32, (g * blk, SWA_WIN), 1)
             - lax.broadcasted_iota(jnp.int32, (g * blk, SWA_WIN), 0) % blk)

    for n0 in range(0, SWA_NBLK, SWA_BATCH):
        ns = list(range(n0, n0 + SWA_BATCH))
        idx = range(len(ns))
        r0 = [n * blk for n in ns]
        j0 = [min(max(n - 1, 0), SWA_NBLK - SWA_WIN // blk) for n in ns]
        qn = [jnp.concatenate([_rope(q_ref[CTX + r0[i]:CTX + r0[i] + blk, gi * HD:(gi + 1) * HD],
                                     cos_ref[r0[i]:r0[i] + blk, :], sin_ref[r0[i]:r0[i] + blk, :])
                               for gi in range(g)], axis=0).astype(BF16) for i in idx]
        kw_t = [jnp.concatenate([kt_ref[j0[i] + w] for w in range(SWA_WIN // blk)], axis=1) for i in idx]
        s_win = [jnp.dot(qn[i], kw_t[i], preferred_element_type=F32) * scale for i in idx]
        s_win = [jnp.where(jnp.abs(delta + (j0[i] * blk - r0[i])) <= SWA_WINDOW, s_win[i], NEG_INF) for i in idx]
        s_ctx = [jnp.dot(qn[i], kc_t, preferred_element_type=F32) * scale for i in idx]
        m = [jnp.maximum(row_max(s_win[i], s_ctx[i]), sk_b) for i in idx]
        e_win = [jnp.exp(s_win[i] - m[i]) for i in idx]
        e_ctx = [jnp.exp(s_ctx[i] - m[i]) for i in idx]
        den = [row_sum(e_win[i], e_ctx[i]) + jnp.exp(sk_b - m[i]) for i in idx]
        o = [(jnp.dot(e_win[i].astype(BF16), vb_ref[CTX + j0[i] * blk:CTX + j0[i] * blk + SWA_WIN, :],
                      preferred_element_type=F32)
              + jnp.dot(e_ctx[i].astype(BF16), vc, preferred_element_type=F32)) / den[i] for i in idx]
        for i in idx:
            for gi in range(g):
                o_ref[CTX + r0[i]:CTX + r0[i] + blk, gi * HD:(gi + 1) * HD] = (
                    o[i][gi * blk:(gi + 1) * blk].astype(o_ref.dtype))


def swa_mixer(p_swa, sink_l, cos_full, sin_signed):
    gw = SWA_GROUP * HD
    rows = p_swa.shape[0]
    return pl.pallas_call(
        _swa_kernel,
        grid=(rows // TB, SWA_KV_HEADS),
        in_specs=[
            pl.BlockSpec(memory_space=pltpu.SMEM),
            pl.BlockSpec((TB, gw), lambda b, kv: (b, kv)),
            pl.BlockSpec((TB, HD), lambda b, kv: (b, SWA_Q_HEADS + kv)),
            pl.BlockSpec((TB, HD), lambda b, kv: (b, SWA_Q_HEADS + SWA_KV_HEADS + kv)),
            pl.BlockSpec((SEQ, HD), lambda b, kv: (0, 0)),
            pl.BlockSpec((SEQ, HD), lambda b, kv: (0, 0)),
        ],
        out_specs=pl.BlockSpec((TB, gw), lambda b, kv: (b, kv)),
        out_shape=jax.ShapeDtypeStruct((rows, W), BF16),
        scratch_shapes=[pltpu.VMEM((SWA_NBLK, HD, SWA_BLOCK), BF16), pltpu.VMEM((TB, HD), BF16)],
        compiler_params=_cparams(("parallel", "parallel")),
        name="swa",
    )(sink_l, p_swa, p_swa, p_swa, cos_full, sin_signed)


def rope_tables():
    rows = SEQ // GRID_W
    row = jnp.repeat(jnp.arange(rows), GRID_W).astype(F32)
    col = jnp.tile(jnp.arange(GRID_W), rows).astype(F32)
    n_freq = HD // 4
    inv_freq = ROPE_BASE ** (-jnp.arange(n_freq, dtype=F32) / n_freq)
    ang_r = row[:, None] * inv_freq
    ang_c = col[:, None] * inv_freq
    cr, sr, cc, sc = jnp.cos(ang_r), jnp.sin(ang_r), jnp.cos(ang_c), jnp.sin(ang_c)
    cos_full = jnp.concatenate([cr, cr, cc, cc], axis=-1)
    sin_signed = jnp.concatenate([-sr, sr, -sc, sc], axis=-1)
    return cos_full, sin_signed


RW_NCHUNK = TB // RW_C
RW_CTX_CHUNKS = CTX // RW_C
RW_GPC = RW_GW // RW_C


def _rw_chunk(direction, s):
    back = jnp.where(s < RW_CTX_CHUNKS, RW_CTX_CHUNKS - 1 - s, RW_NCHUNK - 1 + RW_CTX_CHUNKS - s)
    return jnp.where(direction == 0, s, back)


def _head_ones():
    r = lax.broadcasted_iota(jnp.int32, (RW_GW, RW_GW), 0) // RW_N
    c = lax.broadcasted_iota(jnp.int32, (RW_GW, RW_GW), 1) // RW_N
    return r == c


def _dot16(a, b):
    return jnp.dot(a.astype(BF16), b.astype(BF16), preferred_element_type=F32)


def _dot16_nt(a, b):
    return lax.dot_general(a.astype(BF16), b.astype(BF16), (((1,), (1,)), ((), ())), preferred_element_type=F32)


def _split_bf16(x, terms):
    parts = []
    for _ in range(terms):
        p = x.astype(BF16)
        parts.append(p)
        x = x - p.astype(F32)
    return parts


def _dot_exact_lhs(x, b16, terms):
    return jnp.dot(jnp.concatenate(_split_bf16(x, terms), axis=1), jnp.concatenate([b16] * terms, axis=0),
                   preferred_element_type=F32)


def _rwkv_kernel(r_ref, k_ref, v_ref, sm_ref, wup_ref, aup_ref, w0_ref, a0_ref, kk_ref, ka_ref, rk_ref, gup_ref,
                 lng_ref, o_ref, q_s, y1_s, m_s, z_s, bonus_s, gate_s, st_ref, ybuf):
    direction = pl.program_id(1)
    step = pl.program_id(2)
    cur = step % 2
    prev = 1 - cur
    c = RW_C
    same_head = _head_ones()
    mask16 = jnp.where(same_head, 1.0, 0.0).astype(BF16)
    nbat = r_ref.shape[0]
    chains = [(bb, g) for bb in range(nbat) for g in range(RW_GROUPS)]
    groups = range(len(chains))
    bbs = [bb for bb, _ in chains]
    sls = [slice(g * RW_GW, (g + 1) * RW_GW) for _, g in chains]

    @pl.when(step == 0)
    def _():
        st_ref[...] = jnp.zeros_like(st_ref)
        for ref in (q_s, y1_s, m_s, z_s, bonus_s, gate_s):
            ref[1] = jnp.zeros(ref.shape[1:], ref.dtype)

    ys = []
    for g in groups:
        m_bd = jnp.concatenate([m_s[prev, bbs[g], :, sls[g]]] * RW_GPC, axis=0) * mask16
        z_bd = jnp.where(same_head, jnp.concatenate([z_s[prev, bbs[g], :, sls[g]]] * RW_GPC, axis=0), 0.0)
        res = _dot16(jnp.concatenate([q_s[prev, bbs[g], :, sls[g]], m_bd], axis=0), st_ref[g])
        ys.append(res[:c] + y1_s[prev, bbs[g], :, sls[g]])
        st_ref[g] = res[c:] + z_bd

    def bd(x):
        return jnp.concatenate([x.astype(BF16)] * RW_GPC, axis=0) * mask16

    sgn = 1 - 2 * direction
    cum_terms = 2
    tt = lax.broadcasted_iota(jnp.int32, (c, cum_terms * c), 0)
    ss = lax.broadcasted_iota(jnp.int32, (c, cum_terms * c), 1) % c
    tri16 = jnp.where((tt - ss) * sgn >= 0, 1.0, 0.0).astype(BF16)
    t4 = lax.broadcasted_iota(jnp.int32, (c, RW_GW), 0)
    s4 = lax.broadcasted_iota(jnp.int32, (c, RW_GW), 1) % c
    d4 = (t4 - s4) * sgn
    strict = d4 > 0
    incl = d4 >= 0
    eye = jnp.where(d4 == 0, 1.0, 0.0)

    sm = [sm_ref[bb] for bb in range(nbat)]
    tanh_wd = [jnp.tanh(x[:, :2 * RW_RANK]).astype(BF16) for x in sm]
    ad = [x[:, 2 * RW_RANK:4 * RW_RANK].astype(BF16) for x in sm]

    r = [r_ref[bbs[g], :, sls[g]] for g in groups]
    k = [k_ref[bbs[g], :, sls[g]] for g in groups]
    v = [v_ref[bbs[g], :, sls[g]] for g in groups]
    logw = [-jax.nn.sigmoid(w0_ref[0, :, sls[g]] + jnp.dot(tanh_wd[bbs[g]], wup_ref[0, :, sls[g]],
                                                          preferred_element_type=F32)) * math.exp(-0.5)
            for g in groups]

    def alpha_of(d, g):
        return jax.nn.sigmoid(a0_ref[d, :, sls[g]] + jnp.dot(ad[bbs[g]], aup_ref[d, :, sls[g]],
                                                             preferred_element_type=F32))

    alpha = [alpha_of(direction, g) for g in groups]
    alpha_both = [alpha[g] + alpha_of(1 - direction, g) for g in groups]
    kkp = [k[g] * kk_ref[:, sls[g]] for g in groups]
    kdir = [k[g] * (1.0 + (alpha[g] - 1.0) * ka_ref[:, sls[g]]) for g in groups]
    kboth = [k[g] * (2.0 + (alpha_both[g] - 2.0) * ka_ref[:, sls[g]]) for g in groups]
    nch = len(chains)
    sums = _dot_exact_lhs(jnp.concatenate([kkp[g] * kkp[g] for g in groups]
                                          + [r[g] * kboth[g] * rk_ref[:, sls[g]] for g in groups], axis=0), mask16, 2)
    kk = [kkp[g] / jnp.maximum(jnp.sqrt(sums[g * c:(g + 1) * c]), 1e-12) for g in groups]
    for g in groups:
        bonus_s[cur, bbs[g], :, sls[g]] = sums[(nch + g) * c:(nch + g + 1) * c] * v[g]
    for bb in range(nbat):
        gate_s[cur, bb] = jnp.dot(jax.nn.sigmoid(sm[bb][:, 4 * RW_RANK:]).astype(BF16), gup_ref[...],
                                  preferred_element_type=F32)
    b_vec = [kk[g] * alpha[g] for g in groups]

    cum = [jnp.dot(tri16, jnp.concatenate(_split_bf16(logw[g], cum_terms), axis=0), preferred_element_type=F32)
           for g in groups]
    total = [jnp.sum(logw[g], axis=0, keepdims=True) for g in groups]
    e_out = [jnp.exp(-cum[g]) for g in groups]
    e_left = [jnp.exp(total[g] - cum[g]) for g in groups]
    a_t = [-kk[g] * jnp.exp(cum[g] - logw[g]) for g in groups]
    r_t = [r[g] * jnp.exp(cum[g]) for g in groups]
    lhs = [jnp.concatenate([a_t[g], r_t[g]], axis=0) for g in groups]
    gb = [_dot16_nt(lhs[g], bd(b_vec[g] * e_out[g])) for g in groups]
    gk = [_dot16_nt(lhs[g], bd(kdir[g] * e_out[g])) for g in groups]
    a_ab = [jnp.where(strict, gb[g][:c], 0.0) for g in groups]
    a_ak = [jnp.where(strict, gk[g][:c], 0.0) for g in groups]
    a_rb = [jnp.where(incl, gb[g][c:], 0.0) for g in groups]
    a_rk = [jnp.where(incl, gk[g][c:], 0.0) for g in groups]

    pt = jnp.where(direction == 0, t4, c - 1 - t4)
    ps = jnp.where(direction == 0, s4, c - 1 - s4)
    px = pt ^ ps

    def level_mask(lv):
        return (px >> lv) * 2 + ((pt >> lv) & 1) == 3

    tinv = [eye + jnp.where(level_mask(0), a_ab[g], 0.0) for g in groups]
    for lv in range(1, int(math.log2(c))):
        lm = level_mask(lv)
        cross = [_dot16(jnp.where(lm, a_ab[g], 0.0), bd(tinv[g])) for g in groups]
        tinv = [tinv[g] + _dot16(tinv[g], bd(cross[g])) for g in groups]

    eye16 = eye.astype(BF16)
    blt = [_dot16_nt(eye16, bd(b_vec[g] * e_left[g])) for g in groups]
    klt = [_dot16_nt(eye16, bd(kdir[g] * e_left[g])) for g in groups]
    vprod = [_dot16(jnp.concatenate([a_ak[g], a_rk[g], klt[g]], axis=0), bd(v[g])) for g in groups]
    pw = [_dot16(tinv[g], jnp.concatenate([bd(a_t[g]), bd(vprod[g][:c])], axis=1)) for g in groups]
    top = [_dot16(jnp.concatenate([a_rb[g], blt[g]], axis=0),
                  jnp.concatenate([bd(pw[g][:, :RW_GW]), bd(pw[g][:, RW_GW:])], axis=1)) for g in groups]
    low = [vprod[g][c:] for g in groups]
    for g in groups:
        sl = sls[g]
        q_s[cur, bbs[g], :, sl] = (r_t[g] + top[g][:c, :RW_GW]).astype(q_s.dtype)
        y1_s[cur, bbs[g], :, sl] = top[g][:c, RW_GW:] + low[g][:c]
        m_s[cur, bbs[g], :, sl] = (eye * jnp.exp(total[g]) + top[g][c:, :RW_GW]).astype(m_s.dtype)
        z_s[cur, bbs[g], :, sl] = top[g][c:, RW_GW:] + low[g][c:]

    chunk_prev = _rw_chunk(direction, jnp.maximum(step - 1, 0))

    @pl.when(direction == 0)
    def _():
        for g in groups:
            ybuf[bbs[g], chunk_prev, :, sls[g]] = ys[g].astype(ybuf.dtype)

    @pl.when(direction == 1)
    def _():
        y4 = jnp.concatenate([ys[g] + ybuf[bbs[g], chunk_prev, :, sls[g]] for g in groups], axis=0)
        mean = _dot_exact_lhs(y4, mask16, 2) * (1.0 / RW_N)
        yc = y4 - mean
        var = _dot_exact_lhs(yc * yc, mask16, 2) * (1.0 / RW_N)
        yn = yc * lax.rsqrt(var + RW_GN_EPS)
        for g in groups:
            sl = sls[g]
            o_ref[bbs[g], :, sl] = ((yn[g * c:(g + 1) * c] * lng_ref[:, sl] + bonus_s[prev, bbs[g], :, sl])
                                    * gate_s[prev, bbs[g], :, sl]).astype(o_ref.dtype)


RW_BPS = 4


def rwkv_mixer(ps, layer, wup2, aup2, w0, a0, k_k, k_a, r_k, g_up, ln_g):
    rows = ps.shape[0]
    nb = rows // TB
    nsteps = RW_NCHUNK + 1

    def in_blk(d, s):
        return _rw_chunk(d, jnp.minimum(s, RW_NCHUNK - 1))

    def out_blk(d, s):
        return _rw_chunk(1, jnp.where(d == 0, 0, jnp.maximum(s - 1, 0)))

    feat = lambda off: pl.BlockSpec((RW_BPS, RW_C, W), lambda b, d, s, off=off: (b, in_blk(d, s), off))
    vec = pl.BlockSpec((None, 1, W), lambda b, d, s: (layer, 0, 0))
    per_dir = lambda r: pl.BlockSpec((None, 1, r, W), lambda b, d, s: (layer, d, 0, 0))
    both_dirs = lambda r: pl.BlockSpec((None, 2, r, W), lambda b, d, s: (layer, 0, 0, 0))
    slot = lambda dt: pltpu.VMEM((2, RW_BPS, RW_C, W), dt)
    out = pl.pallas_call(
        _rwkv_kernel,
        grid=(nb // RW_BPS, 2, nsteps),
        in_specs=[feat(0), feat(1), feat(2),
                  pl.BlockSpec((RW_BPS, RW_C, RW_SMALL), lambda b, d, s: (b, in_blk(d, s), 3 * W // RW_SMALL)),
                  per_dir(2 * RW_RANK), both_dirs(2 * RW_RANK), per_dir(1), both_dirs(1), vec, vec, vec,
                  pl.BlockSpec((None, RW_G_RANK, W), lambda b, d, s: (layer, 0, 0)), vec],
        out_specs=pl.BlockSpec((RW_BPS, RW_C, W), lambda b, d, s: (b, out_blk(d, s), 0)),
        out_shape=jax.ShapeDtypeStruct((nb, TB, W), BF16),
        scratch_shapes=[slot(BF16), slot(F32), slot(BF16), slot(F32), slot(F32), slot(F32),
                        pltpu.VMEM((RW_BPS * RW_GROUPS, RW_GW, RW_GW), F32),
                        pltpu.VMEM((RW_BPS, RW_NCHUNK, RW_C, W), BF16)],
        compiler_params=_cparams(("arbitrary", "arbitrary", "arbitrary")),
        name="rwkv_mixer",
    )(*[ps.reshape(nb, TB, RWKV_IN)] * 4, wup2, aup2, w0, a0, k_k, k_a, r_k,
      g_up, ln_g)
    return out.reshape(rows, W)


MG_TM = 1152
MG_TN = 256


def _merge_kernel(u_ref, y0_ref, y1_ref, y2_ref, wg0_ref, wg1_ref, wg2_ref, wb_ref, o_ref, wgc_ref, wbc_ref):
    @pl.when(pl.program_id(1) == 0)
    def _():
        for n, wg_ref in enumerate((wg0_ref, wg1_ref, wg2_ref)):
            wgc_ref[n] = wg_ref[...].astype(BF16)
        wbc_ref[...] = wb_ref[...].astype(BF16)

    u = u_ref[...]
    acc = None
    for n, y_ref in enumerate((y0_ref, y1_ref, y2_ref)):
        gate = jax.nn.sigmoid(jnp.dot(u, wgc_ref[n], preferred_element_type=F32))
        term = gate * jnp.dot(y_ref[...], wbc_ref[n], preferred_element_type=F32)
        acc = term if acc is None else acc + term
    o_ref[...] = acc.astype(o_ref.dtype)


def merge_branches(u, y_ret, y_swa, y_rwkv, w_in, gate_col0, w_branch, layer):
    nj = D // MG_TN
    ysp = pl.BlockSpec((MG_TM, W), lambda j, i: (i, 0))
    wgsp = lambda n: pl.BlockSpec(
        (pl.Element(D), pl.Element(MG_TN)),
        lambda j, i, n=n: (layer * D, pl.multiple_of(gate_col0 + n * D + j * MG_TN, 128)))
    return pl.pallas_call(
        _merge_kernel,
        grid=(nj, R // MG_TM),
        in_specs=[pl.BlockSpec((MG_TM, D), lambda j, i: (i, 0)), ysp, ysp, ysp, wgsp(0), wgsp(1), wgsp(2),
                  pl.BlockSpec((None, 3, W, MG_TN), lambda j, i: (layer, 0, 0, j))],
        out_specs=pl.BlockSpec((MG_TM, MG_TN), lambda j, i: (i, j)),
        out_shape=jax.ShapeDtypeStruct((R, D), BF16),
        scratch_shapes=[pltpu.VMEM((3, D, MG_TN), BF16), pltpu.VMEM((3, W, MG_TN), BF16)],
        compiler_params=_cparams(("arbitrary", "arbitrary")),
        name="merge",
    )(u, y_ret, y_swa, y_rwkv, *[w_in.reshape(-1, w_in.shape[-1])] * 3, w_branch)


def kernel(x, c, ctx, c_ctx, norm1_g, norm2_g, w_mod, b_mod, w_in, ret_decay, swa_sink, rwkv_mu, rwkv_w0,
           rwkv_w_up, rwkv_a0, rwkv_a_up, rwkv_g_up, rwkv_k_k, rwkv_k_a, rwkv_r_k, rwkv_ln_g, w_branch,
           w_out, w_ff1, w_ff2, final_g):
    h = jnp.concatenate([ctx, x], axis=1).reshape(R, D)
    cond8 = jnp.concatenate([c, c_ctx[None, :], jnp.zeros((8 - B - 1, D), F32)], axis=0)
    mods = mod_vectors(cond8, w_mod, b_mod)
    mods = mods.reshape(L, 8, 6, D).transpose(0, 2, 1, 3)
    cos_full, sin_signed = rope_tables()

    o_swa = RET_IN
    o_rw = RET_IN + SWA_IN
    o_small = o_rw + 3 * W
    o_gate = o_rw + RWKV_IN
    w_ff2_bf16 = w_ff2.astype(BF16)
    zl = jnp.zeros((L, RW_RANK, W), BF16)
    both = lambda a: jnp.stack([jnp.concatenate([a[:, 0], zl], axis=1), jnp.concatenate([zl, a[:, 1]], axis=1)], axis=1)
    rw_params = (both(rwkv_w_up.astype(BF16)), both(rwkv_a_up.astype(BF16)), rwkv_w0.reshape(L, 2, 1, W),
                 rwkv_a0.reshape(L, 2, 1, W), rwkv_k_k.reshape(L, 1, W), rwkv_k_a.reshape(L, 1, W),
                 rwkv_r_k.reshape(L, 1, W), rwkv_g_up.astype(BF16), rwkv_ln_g.reshape(L, 1, W))

    for l in range(L):
        m_l = mods[l]
        u = norm_modulate(h, norm1_g[l], m_l, 0, 1)
        p_ret = matmul_f32w(u, w_in, l, 0, RET_IN, tn=1024, name="in_ret")
        p_swa = matmul_f32w(u, w_in, l, o_swa, SWA_IN, tn=768, name="in_swa")
        ps_rwkv = matmul_f32w(u, w_in, l, o_rw, RWKV_IN, tn=512, tm=TB, epilogue="token_shift", mu=rwkv_mu[l],
                              name="in_rwkv")

        y_ret = retention_mixer(p_ret, ret_decay[l])
        y_swa = swa_mixer(p_swa, swa_sink[l], cos_full, sin_signed)

        y_rwkv = rwkv_mixer(ps_rwkv, l, *rw_params)

        merged = merge_branches(u, y_ret, y_swa, y_rwkv, w_in, o_gate, w_branch, l)
        h = matmul_f32w(merged, w_out, l, 0, D, tn=1024, epilogue="residual", h=h, mods_l=m_l,
                        which_gate=2, name="out_proj")
        u2 = norm_modulate(h, norm2_g[l], m_l, 3, 4)
        f = matmul_f32w(u2, w_ff1, l, 0, D_FF, tn=1024, out_dtype=BF16, epilogue="relu2", name="ff1")
        h = matmul_residual(f, w_ff2_bf16, l, h, m_l, 5, tm=576, tn=512, name="ff2")

    return final_norm(h, final_g).reshape(B, SEQ, D)
```

```python
import functools
import math

import jax
import jax.numpy as jnp
from jax import lax
from jax.experimental import pallas as pl
from jax.experimental.pallas import tpu as pltpu

F32 = jnp.float32
BF16 = jnp.bfloat16

D = 2048
B = 4
SEQ = 2048
CTX = 256
TB = CTX + SEQ
R = B * TB
L = 4
GRID_W = 64
EPS = 1e-6
ROPE_BASE = 10000.0
NEG_INF = -1e30
W = D // 2
HD = 128
RET_HEADS = W // HD
RET_CHUNK = 128
RET_GN_EPS = 1e-5
SWA_Q_HEADS = W // HD
SWA_KV_HEADS = SWA_Q_HEADS // 4
SWA_GROUP = SWA_Q_HEADS // SWA_KV_HEADS
SWA_WINDOW = 128
SWA_BLOCK = 128
RW_N = 64
RW_RANK = 64
RW_G_RANK = 128
RW_GN_EPS = 64e-5
RW_C = 64
RW_GW = 256
RW_GROUPS = W // RW_GW
RW_SMALL = 4 * RW_RANK + RW_G_RANK
D_FF = 4 * D
RET_IN = 4 * W
SWA_IN = (SWA_Q_HEADS + 2 * SWA_KV_HEADS) * HD
RWKV_IN = 3 * W + RW_SMALL
GATE_IN = 3 * D
N_IN = RET_IN + SWA_IN + RWKV_IN + GATE_IN

VMEM_LIMIT = 56 * 1024 * 1024


def _cparams(sem):
    return pltpu.CompilerParams(dimension_semantics=sem, vmem_limit_bytes=VMEM_LIMIT)


def _mod_kernel(x_ref, w_ref, b_ref, o_ref):
    x = x_ref[...]
    s = x * jax.nn.sigmoid(x)
    acc = jnp.dot(s.astype(BF16), w_ref[0].astype(BF16), preferred_element_type=F32)
    o_ref[0] = acc + b_ref[0]


def mod_vectors(cond8, w_mod, b_mod):
    tn = 1024
    n = w_mod.shape[-1]
    return pl.pallas_call(
        _mod_kernel,
        grid=(L, n // tn),
        in_specs=[
            pl.BlockSpec((8, D), lambda l, j: (0, 0)),
            pl.BlockSpec((1, D, tn), lambda l, j: (l, 0, j)),
            pl.BlockSpec((1, 1, tn), lambda l, j: (l, 0, j)),
        ],
        out_specs=pl.BlockSpec((1, 8, tn), lambda l, j: (l, 0, j)),
        out_shape=jax.ShapeDtypeStruct((L, 8, n), F32),
        compiler_params=_cparams(("parallel", "parallel")),
        name="mod_vectors",
    )(cond8, w_mod, b_mod.reshape(L, 1, n))


NORM_TM = 256
TILES_PER_BATCH = TB // NORM_TM


NORM_MOD_TM = 768
NORM_SUB = 64


def _norm_mod_kernel(x_ref, g_ref, sh_ref, sc_ref, o_ref):
    i = pl.program_id(0)
    tm = NORM_MOD_TM
    bidx = i // (TB // tm)
    first_tile = i % (TB // tm) == 0
    g = g_ref[...]
    sub = NORM_SUB

    def body(r, carry):
        row = jnp.where(jnp.logical_and(first_tile, r * sub < CTX), B, bidx)
        rows = pl.ds(pl.multiple_of(r * sub, sub), sub)
        x = x_ref[rows, :]
        ms = jnp.mean(x * x, axis=-1, keepdims=True)
        y = x * lax.rsqrt(ms + EPS) * g
        o_ref[rows, :] = (y * (1.0 + sc_ref[0, pl.ds(row, 1), :]) + sh_ref[0, pl.ds(row, 1), :]).astype(o_ref.dtype)
        return carry

    lax.fori_loop(0, tm // sub, body, 0, unroll=4)


def norm_modulate(h, g, mods_l, which_shift, which_scale):
    return pl.pallas_call(
        _norm_mod_kernel,
        grid=(R // NORM_MOD_TM,),
        in_specs=[
            pl.BlockSpec((NORM_MOD_TM, D), lambda i: (i, 0)),
            pl.BlockSpec((1, D), lambda i: (0, 0)),
            pl.BlockSpec((1, 8, D), lambda i: (which_shift, 0, 0)),
            pl.BlockSpec((1, 8, D), lambda i: (which_scale, 0, 0)),
        ],
        out_specs=pl.BlockSpec((NORM_MOD_TM, D), lambda i: (i, 0)),
        out_shape=jax.ShapeDtypeStruct((R, D), BF16),
        compiler_params=_cparams(("parallel",)),
        name="norm_modulate",
    )(h, g.reshape(1, D), mods_l, mods_l)


def _final_norm_kernel(x_ref, g_ref, o_ref):
    x = x_ref[...]
    ms = jnp.mean(x * x, axis=-1, keepdims=True)
    o_ref[...] = x * lax.rsqrt(ms + EPS) * g_ref[...]


def final_norm(h, g):
    per = SEQ // NORM_TM
    return pl.pallas_call(
        _final_norm_kernel,
        grid=(B, per),
        in_specs=[
            pl.BlockSpec((NORM_TM, D), lambda b, j: (b * TILES_PER_BATCH + CTX // NORM_TM + j, 0)),
            pl.BlockSpec((1, D), lambda b, j: (0, 0)),
        ],
        out_specs=pl.BlockSpec((NORM_TM, D), lambda b, j: (b * per + j, 0)),
        out_shape=jax.ShapeDtypeStruct((B * SEQ, D), F32),
        compiler_params=_cparams(("parallel", "parallel")),
        name="final_norm",
    )(h, g.reshape(1, D))


MM_TM = 1152


def _row_is_ctx(i, tm):
    rows = lax.broadcasted_iota(jnp.int32, (tm, 1), 0)
    return jnp.logical_and(i % (TB // tm) == 0, rows < CTX)


def _mm_finish(acc, i, o_ref, h_ref, gate_ref, epilogue, tm):
    if epilogue == "relu2":
        a = jnp.maximum(acc, 0.0)
        o_ref[...] = (a * a).astype(o_ref.dtype)
    elif epilogue == "residual":
        bidx = i // (TB // tm)
        g_b = gate_ref[0, pl.ds(bidx, 1), :]
        g_c = gate_ref[0, pl.ds(B, 1), :]
        gate = jnp.where(_row_is_ctx(i, tm), g_c, g_b)
        o_ref[...] = h_ref[...] + gate * acc
    elif epilogue == "token_shift":
        o_ref[...] = _token_shift(acc, gate_ref)
    else:
        o_ref[...] = acc.astype(o_ref.dtype)


def _token_shift(x, mu_ref):
    row = lax.broadcasted_iota(jnp.int32, (TB, 1), 0)
    prev = pltpu.roll(x, 1, 0)
    prev = jnp.where(jnp.logical_or(row == 0, row == CTX), 0.0, prev)
    nxt = pltpu.roll(x, TB - 1, 0)
    nxt = jnp.where(jnp.logical_or(row == CTX - 1, row == TB - 1), 0.0, nxt)
    return x + mu_ref[0:1, :] * (prev - x) + mu_ref[1:2, :] * (nxt - x)


def _mm_cached_kernel(*refs, epilogue, tm):
    if epilogue == "residual":
        x_ref, w_ref, h_ref, gate_ref, o_ref, wc_ref = refs
    elif epilogue == "token_shift":
        x_ref, w_ref, gate_ref, o_ref, wc_ref = refs
        h_ref = None
    else:
        x_ref, w_ref, o_ref, wc_ref = refs
        h_ref = gate_ref = None
    i = pl.program_id(1)

    @pl.when(i == 0)
    def _():
        wc_ref[...] = w_ref[...].astype(BF16)

    acc = jnp.dot(x_ref[...], wc_ref[...], preferred_element_type=F32)
    _mm_finish(acc, i, o_ref, h_ref, gate_ref, epilogue, tm)


def matmul_f32w(x, w_stack, layer, col0, n, *, tn, tm=MM_TM, out_dtype=F32, epilogue="plain", h=None, mods_l=None,
                which_gate=None, mu=None, name="matmul"):
    m, kdim = x.shape
    in_specs = [
        pl.BlockSpec((tm, kdim), lambda j, i: (i, 0)),
        pl.BlockSpec((pl.Element(kdim), pl.Element(tn)), lambda j, i: (layer * kdim, pl.multiple_of(col0 + j * tn, 128))),
    ]
    args = [x, w_stack.reshape(-1, w_stack.shape[-1])]
    aliases = {}
    if epilogue == "residual":
        in_specs += [
            pl.BlockSpec((tm, tn), lambda j, i: (i, j)),
            pl.BlockSpec((1, 8, tn), lambda j, i: (which_gate, 0, j)),
        ]
        args += [h, mods_l]
        aliases = {2: 0}
    elif epilogue == "token_shift":
        assert tm == TB
        in_specs.append(pl.BlockSpec((2, tn), lambda j, i: (0, j)))
        args.append(mu)
    return pl.pallas_call(
        functools.partial(_mm_cached_kernel, epilogue=epilogue, tm=tm),
        grid=(pl.cdiv(n, tn), m // tm),
        in_specs=in_specs,
        out_specs=pl.BlockSpec((tm, tn), lambda j, i: (i, j)),
        out_shape=jax.ShapeDtypeStruct((m, n), out_dtype),
        scratch_shapes=[pltpu.VMEM((kdim, tn), BF16)],
        input_output_aliases=aliases,
        compiler_params=_cparams(("arbitrary", "arbitrary")),
        name=name,
    )(*args)


def _mm_residual_kernel(x_ref, w_ref, h_ref, gate_ref, o_ref, *, tm):
    acc = jnp.dot(x_ref[...], w_ref[...], preferred_element_type=F32)
    _mm_finish(acc, pl.program_id(0), o_ref, h_ref, gate_ref, "residual", tm)


def matmul_residual(x, w_stack, layer, h, mods_l, which_gate, *, tm, tn, name):
    m, kdim = x.shape
    n = w_stack.shape[2]
    return pl.pallas_call(
        functools.partial(_mm_residual_kernel, tm=tm),
        grid=(m // tm, n // tn),
        in_specs=[
            pl.BlockSpec((tm, kdim), lambda i, j: (i, 0)),
            pl.BlockSpec((None, kdim, tn), lambda i, j: (layer, 0, j)),
            pl.BlockSpec((tm, tn), lambda i, j: (i, j)),
            pl.BlockSpec((1, 8, tn), lambda i, j: (which_gate, 0, j)),
        ],
        out_specs=pl.BlockSpec((tm, tn), lambda i, j: (i, j)),
        out_shape=jax.ShapeDtypeStruct((m, n), F32),
        input_output_aliases={2: 0},
        compiler_params=_cparams(("parallel", "parallel")),
        name=name,
    )(x, w_stack, h, mods_l)


RET_NCHUNK = TB // RET_CHUNK
RET_CTX_CHUNKS = CTX // RET_CHUNK


def _log_sigmoid(x):
    return jnp.minimum(x, 0.0) - jnp.log(1.0 + jnp.exp(-jnp.abs(x)))


RET_BATCH = 9


def _ret_kernel(dec_ref, q_ref, k_ref, v_ref, g_ref, o_ref, y_ref):
    hh = pl.program_id(1)
    c = RET_CHUNK
    ri = lax.broadcasted_iota(jnp.int32, (c, c), 0).astype(F32)
    ci = lax.broadcasted_iota(jnp.int32, (c, c), 1).astype(F32)
    kscale = HD ** -0.5
    nt = (((1,), (1,)), ((), ()))
    for direction in range(2):
        lg = _log_sigmoid(jnp.full((c, c), dec_ref[direction, hh], F32))
        diff = (ri - ci) if direction == 0 else (ci - ri)
        intra = jnp.where(diff >= 0, jnp.exp(lg * jnp.maximum(diff, 0.0)), 0.0)
        pos = ri if direction == 0 else (c - 1.0) - ri
        q_decay = jnp.exp(lg * (pos + 1.0))
        k_decay = jnp.exp(lg * ((c - 1.0) - pos))
        chunk_decay = jnp.exp(lg * float(c))
        if direction == 0:
            order = list(range(RET_NCHUNK))
        else:
            order = list(range(RET_CTX_CHUNKS - 1, -1, -1)) + list(range(RET_NCHUNK - 1, RET_CTX_CHUNKS - 1, -1))
        state = jnp.zeros((c, c), F32)
        for b0 in range(0, RET_NCHUNK, RET_BATCH):
            rows = [slice(ch * c, (ch + 1) * c) for ch in order[b0:b0 + RET_BATCH]]
            n = range(len(rows))
            q = [q_ref[r, :] for r in rows]
            k = [k_ref[r, :] * kscale for r in rows]
            v = [v_ref[r, :].astype(BF16) for r in rows]
            sc = [lax.dot_general(q[i].astype(BF16), k[i].astype(BF16), nt, preferred_element_type=F32) * intra
                  for i in n]
            y_in = [jnp.dot(sc[i].astype(BF16), v[i], preferred_element_type=F32) for i in n]
            kv = [jnp.dot((k[i] * k_decay).T.astype(BF16), v[i], preferred_element_type=F32) for i in n]
            states = []
            for i in n:
                states.append(state)
                state = state * chunk_decay + kv[i]
            y_x = [jnp.dot((q[i] * q_decay).astype(BF16), states[i].astype(BF16), preferred_element_type=F32)
                   for i in n]
            for i in n:
                y = y_in[i] + y_x[i]
                if direction == 0:
                    y_ref[rows[i], :] = y
                else:
                    y = y + y_ref[rows[i], :]
                    mean = jnp.mean(y, axis=-1, keepdims=True)
                    yc = y - mean
                    var = jnp.mean(yc * yc, axis=-1, keepdims=True)
                    g = g_ref[rows[i], :]
                    o_ref[rows[i], :] = (g * jax.nn.sigmoid(g) * (yc * lax.rsqrt(var + RET_GN_EPS))).astype(o_ref.dtype)


def retention_mixer(p_ret, decay_l):
    nh = RET_HEADS
    blk = lambda off: pl.BlockSpec((TB, HD), lambda b, h, off=off: (b, off + h))
    rows = p_ret.shape[0]
    return pl.pallas_call(
        _ret_kernel,
        grid=(rows // TB, nh),
        in_specs=[pl.BlockSpec(memory_space=pltpu.SMEM), blk(0), blk(nh), blk(2 * nh), blk(3 * nh)],
        out_specs=pl.BlockSpec((TB, HD), lambda b, h: (b, h)),
        out_shape=jax.ShapeDtypeStruct((rows, W), BF16),
        scratch_shapes=[pltpu.VMEM((TB, HD), F32)],
        compiler_params=_cparams(("parallel", "parallel")),
        name="retention",
    )(decay_l, p_ret, p_ret, p_ret, p_ret)


SWA_NBLK = SEQ // SWA_BLOCK
SWA_WIN = 3 * SWA_BLOCK
SWA_BATCH = 4


def _rope(x, cos, sin_signed):
    lane = lax.broadcasted_iota(jnp.int32, x.shape, 1)
    partner = jnp.where(lane % 64 < 32, pltpu.roll(x, 96, 1), pltpu.roll(x, 32, 1))
    return x * cos + partner * sin_signed


def _swa_kernel(sink_ref, q_ref, k_ref, v_ref, cos_ref, sin_ref, o_ref, kt_ref, vb_ref):
    kvh = pl.program_id(1)
    scale = HD ** -0.5
    g = SWA_GROUP
    blk = SWA_BLOCK

    vb_ref[...] = v_ref[...].astype(BF16)
    for j in range(SWA_NBLK):
        rows = slice(j * blk, (j + 1) * blk)
        k_rot = _rope(k_ref[CTX + j * blk:CTX + (j + 1) * blk, :], cos_ref[rows, :], sin_ref[rows, :])
        kt_ref[j] = k_rot.T.astype(BF16)
    kc_t = jnp.concatenate([k_ref[j * blk:(j + 1) * blk, :].T for j in range(CTX // blk)], axis=1).astype(BF16)
    vc = vb_ref[:CTX, :]

    def sink_col(rows_per_head):
        parts = [jnp.full((rows_per_head, 1), sink_ref[kvh * g + gi], F32) for gi in range(g)]
        return jnp.concatenate(parts, axis=0)

    def lane_tiles(x):
        return [x[:, j * HD:(j + 1) * HD] for j in range(x.shape[1] // HD)]

    def row_max(*xs):
        tiles = [t for x in xs for t in lane_tiles(x)]
        return jnp.max(functools.reduce(jnp.maximum, tiles), axis=-1, keepdims=True)

    def row_sum(*xs):
        tiles = [t for x in xs for t in lane_tiles(x)]
        return jnp.sum(functools.reduce(jnp.add, tiles), axis=-1, keepdims=True)

    qc = jnp.concatenate([q_ref[:CTX, gi * HD:(gi + 1) * HD] for gi in range(g)], axis=0).astype(BF16)
    s = jnp.dot(qc, kc_t, preferred_element_type=F32) * scale
    sk = sink_col(CTX)
    m = jnp.maximum(row_max(s), sk)
    e = jnp.exp(s - m)
    den = row_sum(e) + jnp.exp(sk - m)
    oc = jnp.dot(e.astype(BF16), vc, preferred_element_type=F32) / den
    for gi in range(g):
        o_ref[:CTX, gi * HD:(gi + 1) * HD] = oc[gi * CTX:(gi + 1) * CTX].astype(o_ref.dtype)

    sk_b = sink_col(blk)
    ones_w = jnp.ones((SWA_WIN, HD), BF16)
    vc_ones = jnp.concatenate([vc, jnp.ones((CTX, HD), BF16)], axis=1)
    delta = (lax.broadcasted_iota(jnp.int32, (g * blk, SWA_WIN), 1)
             - lax.broadcasted_iota(jnp.int32, (g * blk, SWA_WIN), 0) % blk)

    for n0 in range(0, SWA_NBLK, SWA_BATCH):
        ns = list(range(n0, n0 + SWA_BATCH))
        idx = range(len(ns))
        r0 = [n * blk for n in ns]
        j0 = [min(max(n - 1, 0), SWA_NBLK - SWA_WIN // blk) for n in ns]
        qn = [jnp.concatenate([_rope(q_ref[CTX + r0[i]:CTX + r0[i] + blk, gi * HD:(gi + 1) * HD],
                                     cos_ref[r0[i]:r0[i] + blk, :], sin_ref[r0[i]:r0[i] + blk, :])
                               for gi in range(g)], axis=0).astype(BF16) for i in idx]
        kw_t = [jnp.concatenate([kt_ref[j0[i] + w] for w in range(SWA_WIN // blk)], axis=1) for i in idx]
        s_win = [jnp.dot(qn[i], kw_t[i], preferred_element_type=F32) * scale for i in idx]
        s_win = [jnp.where(jnp.abs(delta + (j0[i] * blk - r0[i])) <= SWA_WINDOW, s_win[i], NEG_INF) for i in idx]
        s_ctx = [jnp.dot(qn[i], kc_t, preferred_element_type=F32) * scale for i in idx]
        m = [jnp.maximum(row_max(s_win[i], s_ctx[i]), sk_b) for i in idx]
        e_win = [jnp.exp(s_win[i] - m[i]) for i in idx]
        e_ctx = [jnp.exp(s_ctx[i] - m[i]) for i in idx]
        acc = [jnp.dot(e_win[i].astype(BF16),
                       jnp.concatenate([vb_ref[CTX + j0[i] * blk:CTX + j0[i] * blk + SWA_WIN, :], ones_w], axis=1),
                       preferred_element_type=F32)
               + jnp.dot(e_ctx[i].astype(BF16), vc_ones, preferred_element_type=F32) for i in idx]
        o = [acc[i][:, :HD] / (acc[i][:, HD:] + jnp.exp(sk_b - m[i])) for i in idx]
        for i in idx:
            for gi in range(g):
                o_ref[CTX + r0[i]:CTX + r0[i] + blk, gi * HD:(gi + 1) * HD] = (
                    o[i][gi * blk:(gi + 1) * blk].astype(o_ref.dtype))


def swa_mixer(p_swa, sink_l, cos_full, sin_signed):
    gw = SWA_GROUP * HD
    rows = p_swa.shape[0]
    return pl.pallas_call(
        _swa_kernel,
        grid=(rows // TB, SWA_KV_HEADS),
        in_specs=[
            pl.BlockSpec(memory_space=pltpu.SMEM),
            pl.BlockSpec((TB, gw), lambda b, kv: (b, kv)),
            pl.BlockSpec((TB, HD), lambda b, kv: (b, SWA_Q_HEADS + kv)),
            pl.BlockSpec((TB, HD), lambda b, kv: (b, SWA_Q_HEADS + SWA_KV_HEADS + kv)),
            pl.BlockSpec((SEQ, HD), lambda b, kv: (0, 0)),
            pl.BlockSpec((SEQ, HD), lambda b, kv: (0, 0)),
        ],
        out_specs=pl.BlockSpec((TB, gw), lambda b, kv: (b, kv)),
        out_shape=jax.ShapeDtypeStruct((rows, W), BF16),
        scratch_shapes=[pltpu.VMEM((SWA_NBLK, HD, SWA_BLOCK), BF16), pltpu.VMEM((TB, HD), BF16)],
        compiler_params=_cparams(("parallel", "parallel")),
        name="swa",
    )(sink_l, p_swa, p_swa, p_swa, cos_full, sin_signed)


def rope_tables():
    rows = SEQ // GRID_W
    row = jnp.repeat(jnp.arange(rows), GRID_W).astype(F32)
    col = jnp.tile(jnp.arange(GRID_W), rows).astype(F32)
    n_freq = HD // 4
    inv_freq = ROPE_BASE ** (-jnp.arange(n_freq, dtype=F32) / n_freq)
    ang_r = row[:, None] * inv_freq
    ang_c = col[:, None] * inv_freq
    cr, sr, cc, sc = jnp.cos(ang_r), jnp.sin(ang_r), jnp.cos(ang_c), jnp.sin(ang_c)
    cos_full = jnp.concatenate([cr, cr, cc, cc], axis=-1)
    sin_signed = jnp.concatenate([-sr, sr, -sc, sc], axis=-1)
    return cos_full, sin_signed


RW_NCHUNK = TB // RW_C
RW_CTX_CHUNKS = CTX // RW_C
RW_GPC = RW_GW // RW_C


def _rw_chunk(direction, s):
    back = jnp.where(s < RW_CTX_CHUNKS, RW_CTX_CHUNKS - 1 - s, RW_NCHUNK - 1 + RW_CTX_CHUNKS - s)
    return jnp.where(direction == 0, s, back)


def _head_ones():
    r = lax.broadcasted_iota(jnp.int32, (RW_GW, RW_GW), 0) // RW_N
    c = lax.broadcasted_iota(jnp.int32, (RW_GW, RW_GW), 1) // RW_N
    return r == c


def _dot16(a, b):
    return jnp.dot(a.astype(BF16), b.astype(BF16), preferred_element_type=F32)


def _dot16_nt(a, b):
    return lax.dot_general(a.astype(BF16), b.astype(BF16), (((1,), (1,)), ((), ())), preferred_element_type=F32)


def _split_bf16(x, terms):
    parts = []
    for _ in range(terms):
        p = x.astype(BF16)
        parts.append(p)
        x = x - p.astype(F32)
    return parts


def _dot_exact_lhs(x, b16, terms):
    return jnp.dot(jnp.concatenate(_split_bf16(x, terms), axis=1), jnp.concatenate([b16] * terms, axis=0),
                   preferred_element_type=F32)


def _rwkv_kernel(r_ref, k_ref, v_ref, sm_ref, wup_ref, aup_ref, w0_ref, a0_ref, kk_ref, ka_ref, rk_ref, gup_ref,
                 lng_ref, o_ref, q_s, y1_s, m_s, z_s, bonus_s, gate_s, st_ref, ybuf):
    direction = pl.program_id(1)
    step = pl.program_id(2)
    cur = step % 2
    prev = 1 - cur
    c = RW_C
    same_head = _head_ones()
    mask16 = jnp.where(same_head, 1.0, 0.0).astype(BF16)
    nbat = r_ref.shape[0]
    chains = [(bb, g) for bb in range(nbat) for g in range(RW_GROUPS)]
    groups = range(len(chains))
    bbs = [bb for bb, _ in chains]
    sls = [slice(g * RW_GW, (g + 1) * RW_GW) for _, g in chains]

    @pl.when(step == 0)
    def _():
        st_ref[...] = jnp.zeros_like(st_ref)
        for ref in (q_s, y1_s, m_s, z_s, bonus_s, gate_s):
            ref[1] = jnp.zeros(ref.shape[1:], ref.dtype)

    ys = []
    for g in groups:
        m_bd = jnp.concatenate([m_s[prev, bbs[g], :, sls[g]]] * RW_GPC, axis=0) * mask16
        z_bd = jnp.where(same_head, jnp.concatenate([z_s[prev, bbs[g], :, sls[g]]] * RW_GPC, axis=0), 0.0)
        res = _dot16(jnp.concatenate([q_s[prev, bbs[g], :, sls[g]], m_bd], axis=0), st_ref[g])
        ys.append(res[:c] + y1_s[prev, bbs[g], :, sls[g]])
        st_ref[g] = res[c:] + z_bd

    def bd(x):
        return jnp.concatenate([x.astype(BF16)] * RW_GPC, axis=0) * mask16

    sgn = 1 - 2 * direction
    cum_terms = 2
    tt = lax.broadcasted_iota(jnp.int32, (c, cum_terms * c), 0)
    ss = lax.broadcasted_iota(jnp.int32, (c, cum_terms * c), 1) % c
    tri16 = jnp.where((tt - ss) * sgn >= 0, 1.0, 0.0).astype(BF16)
    t4 = lax.broadcasted_iota(jnp.int32, (c, RW_GW), 0)
    s4 = lax.broadcasted_iota(jnp.int32, (c, RW_GW), 1) % c
    d4 = (t4 - s4) * sgn
    strict = d4 > 0
    incl = d4 >= 0
    eye = jnp.where(d4 == 0, 1.0, 0.0)

    sm = [sm_ref[bb] for bb in range(nbat)]
    tanh_wd = [jnp.tanh(x[:, :2 * RW_RANK]).astype(BF16) for x in sm]
    ad = [x[:, 2 * RW_RANK:4 * RW_RANK].astype(BF16) for x in sm]

    r = [r_ref[bbs[g], :, sls[g]] for g in groups]
    k = [k_ref[bbs[g], :, sls[g]] for g in groups]
    v = [v_ref[bbs[g], :, sls[g]] for g in groups]
    logw = [-jax.nn.sigmoid(w0_ref[0, :, sls[g]] + jnp.dot(tanh_wd[bbs[g]], wup_ref[0, :, sls[g]],
                                                          preferred_element_type=F32)) * math.exp(-0.5)
            for g in groups]

    def alpha_of(d, g):
        return jax.nn.sigmoid(a0_ref[d, :, sls[g]] + jnp.dot(ad[bbs[g]], aup_ref[d, :, sls[g]],
                                                             preferred_element_type=F32))

    alpha = [alpha_of(direction, g) for g in groups]
    alpha_both = [alpha[g] + alpha_of(1 - direction, g) for g in groups]
    kkp = [k[g] * kk_ref[:, sls[g]] for g in groups]
    kdir = [k[g] * (1.0 + (alpha[g] - 1.0) * ka_ref[:, sls[g]]) for g in groups]
    kboth = [k[g] * (2.0 + (alpha_both[g] - 2.0) * ka_ref[:, sls[g]]) for g in groups]
    nch = len(chains)
    sums = _dot_exact_lhs(jnp.concatenate([kkp[g] * kkp[g] for g in groups]
                                          + [r[g] * kboth[g] * rk_ref[:, sls[g]] for g in groups], axis=0), mask16, 2)
    kk = [kkp[g] / jnp.maximum(jnp.sqrt(sums[g * c:(g + 1) * c]), 1e-12) for g in groups]
    for g in groups:
        bonus_s[cur, bbs[g], :, sls[g]] = sums[(nch + g) * c:(nch + g + 1) * c] * v[g]
    for bb in range(nbat):
        gate_s[cur, bb] = jnp.dot(jax.nn.sigmoid(sm[bb][:, 4 * RW_RANK:]).astype(BF16), gup_ref[...],
                                  preferred_element_type=F32)
    b_vec = [kk[g] * alpha[g] for g in groups]

    cum = [jnp.dot(tri16, jnp.concatenate(_split_bf16(logw[g], cum_terms), axis=0), preferred_element_type=F32)
           for g in groups]
    total = [jnp.sum(logw[g], axis=0, keepdims=True) for g in groups]
    e_out = [jnp.exp(-cum[g]) for g in groups]
    e_left = [jnp.exp(total[g] - cum[g]) for g in groups]
    a_t = [-kk[g] * jnp.exp(cum[g] - logw[g]) for g in groups]
    r_t = [r[g] * jnp.exp(cum[g]) for g in groups]
    lhs = [jnp.concatenate([a_t[g], r_t[g]], axis=0) for g in groups]
    gb = [_dot16_nt(lhs[g], bd(b_vec[g] * e_out[g])) for g in groups]
    gk = [_dot16_nt(lhs[g], bd(kdir[g] * e_out[g])) for g in groups]
    a_ab = [jnp.where(strict, gb[g][:c], 0.0) for g in groups]
    a_ak = [jnp.where(strict, gk[g][:c], 0.0) for g in groups]
    a_rb = [jnp.where(incl, gb[g][c:], 0.0) for g in groups]
    a_rk = [jnp.where(incl, gk[g][c:], 0.0) for g in groups]

    pt = jnp.where(direction == 0, t4, c - 1 - t4)
    ps = jnp.where(direction == 0, s4, c - 1 - s4)
    px = pt ^ ps

    def level_mask(lv):
        return (px >> lv) * 2 + ((pt >> lv) & 1) == 3

    tinv = [eye + jnp.where(level_mask(0), a_ab[g], 0.0) for g in groups]
    for lv in range(1, int(math.log2(c))):
        lm = level_mask(lv)
        cross = [_dot16(jnp.where(lm, a_ab[g], 0.0), bd(tinv[g])) for g in groups]
        tinv = [tinv[g] + _dot16(tinv[g], bd(cross[g])) for g in groups]

    eye16 = eye.astype(BF16)
    blt = [_dot16_nt(eye16, bd(b_vec[g] * e_left[g])) for g in groups]
    klt = [_dot16_nt(eye16, bd(kdir[g] * e_left[g])) for g in groups]
    vprod = [_dot16(jnp.concatenate([a_ak[g], a_rk[g], klt[g]], axis=0), bd(v[g])) for g in groups]
    pw = [_dot16(tinv[g], jnp.concatenate([bd(a_t[g]), bd(vprod[g][:c])], axis=1)) for g in groups]
    top = [_dot16(jnp.concatenate([a_rb[g], blt[g]], axis=0),
                  jnp.concatenate([bd(pw[g][:, :RW_GW]), bd(pw[g][:, RW_GW:])], axis=1)) for g in groups]
    low = [vprod[g][c:] for g in groups]
    for g in groups:
        sl = sls[g]
        q_s[cur, bbs[g], :, sl] = (r_t[g] + top[g][:c, :RW_GW]).astype(q_s.dtype)
        y1_s[cur, bbs[g], :, sl] = top[g][:c, RW_GW:] + low[g][:c]
        m_s[cur, bbs[g], :, sl] = (eye * jnp.exp(total[g]) + top[g][c:, :RW_GW]).astype(m_s.dtype)
        z_s[cur, bbs[g], :, sl] = top[g][c:, RW_GW:] + low[g][c:]

    chunk_prev = _rw_chunk(direction, jnp.maximum(step - 1, 0))

    @pl.when(direction == 0)
    def _():
        for g in groups:
            ybuf[bbs[g], chunk_prev, :, sls[g]] = ys[g].astype(ybuf.dtype)

    @pl.when(direction == 1)
    def _():
        y4 = jnp.concatenate([ys[g] + ybuf[bbs[g], chunk_prev, :, sls[g]] for g in groups], axis=0)
        mean = _dot_exact_lhs(y4, mask16, 2) * (1.0 / RW_N)
        yc = y4 - mean
        var = _dot_exact_lhs(yc * yc, mask16, 2) * (1.0 / RW_N)
        yn = yc * lax.rsqrt(var + RW_GN_EPS)
        for g in groups:
            sl = sls[g]
            o_ref[bbs[g], :, sl] = ((yn[g * c:(g + 1) * c] * lng_ref[:, sl] + bonus_s[prev, bbs[g], :, sl])
                                    * gate_s[prev, bbs[g], :, sl]).astype(o_ref.dtype)


RW_BPS = 4


def rwkv_mixer(ps, layer, wup2, aup2, w0, a0, k_k, k_a, r_k, g_up, ln_g):
    rows = ps.shape[0]
    nb = rows // TB
    nsteps = RW_NCHUNK + 1

    def in_blk(d, s):
        return _rw_chunk(d, jnp.minimum(s, RW_NCHUNK - 1))

    def out_blk(d, s):
        return _rw_chunk(1, jnp.where(d == 0, 0, jnp.maximum(s - 1, 0)))

    feat = lambda off: pl.BlockSpec((RW_BPS, RW_C, W), lambda b, d, s, off=off: (b, in_blk(d, s), off))
    vec = pl.BlockSpec((None, 1, W), lambda b, d, s: (layer, 0, 0))
    per_dir = lambda r: pl.BlockSpec((None, 1, r, W), lambda b, d, s: (layer, d, 0, 0))
    both_dirs = lambda r: pl.BlockSpec((None, 2, r, W), lambda b, d, s: (layer, 0, 0, 0))
    slot = lambda dt: pltpu.VMEM((2, RW_BPS, RW_C, W), dt)
    out = pl.pallas_call(
        _rwkv_kernel,
        grid=(nb // RW_BPS, 2, nsteps),
        in_specs=[feat(0), feat(1), feat(2),
                  pl.BlockSpec((RW_BPS, RW_C, RW_SMALL), lambda b, d, s: (b, in_blk(d, s), 3 * W // RW_SMALL)),
                  per_dir(2 * RW_RANK), both_dirs(2 * RW_RANK), per_dir(1), both_dirs(1), vec, vec, vec,
                  pl.BlockSpec((None, RW_G_RANK, W), lambda b, d, s: (layer, 0, 0)), vec],
        out_specs=pl.BlockSpec((RW_BPS, RW_C, W), lambda b, d, s: (b, out_blk(d, s), 0)),
        out_shape=jax.ShapeDtypeStruct((nb, TB, W), BF16),
        scratch_shapes=[slot(BF16), slot(F32), slot(BF16), slot(F32), slot(F32), slot(F32),
                        pltpu.VMEM((RW_BPS * RW_GROUPS, RW_GW, RW_GW), F32),
                        pltpu.VMEM((RW_BPS, RW_NCHUNK, RW_C, W), BF16)],
        compiler_params=_cparams(("arbitrary", "arbitrary", "arbitrary")),
        name="rwkv_mixer",
    )(*[ps.reshape(nb, TB, RWKV_IN)] * 4, wup2, aup2, w0, a0, k_k, k_a, r_k,
      g_up, ln_g)
    return out.reshape(rows, W)


MG_TM = 1152
MG_TN = 256


def _merge_kernel(u_ref, y0_ref, y1_ref, y2_ref, wg0_ref, wg1_ref, wg2_ref, wb_ref, o_ref, wgc_ref, wbc_ref):
    @pl.when(pl.program_id(1) == 0)
    def _():
        for n, wg_ref in enumerate((wg0_ref, wg1_ref, wg2_ref)):
            wgc_ref[n] = wg_ref[...].astype(BF16)
        wbc_ref[...] = wb_ref[...].astype(BF16)

    u = u_ref[...]
    acc = None
    for n, y_ref in enumerate((y0_ref, y1_ref, y2_ref)):
        gate = jax.nn.sigmoid(jnp.dot(u, wgc_ref[n], preferred_element_type=F32))
        term = gate * jnp.dot(y_ref[...], wbc_ref[n], preferred_element_type=F32)
        acc = term if acc is None else acc + term
    o_ref[...] = acc.astype(o_ref.dtype)


def merge_branches(u, y_ret, y_swa, y_rwkv, w_in, gate_col0, w_branch, layer):
    nj = D // MG_TN
    ysp = pl.BlockSpec((MG_TM, W), lambda j, i: (i, 0))
    wgsp = lambda n: pl.BlockSpec(
        (pl.Element(D), pl.Element(MG_TN)),
        lambda j, i, n=n: (layer * D, pl.multiple_of(gate_col0 + n * D + j * MG_TN, 128)))
    return pl.pallas_call(
        _merge_kernel,
        grid=(nj, R // MG_TM),
        in_specs=[pl.BlockSpec((MG_TM, D), lambda j, i: (i, 0)), ysp, ysp, ysp, wgsp(0), wgsp(1), wgsp(2),
                  pl.BlockSpec((None, 3, W, MG_TN), lambda j, i: (layer, 0, 0, j))],
        out_specs=pl.BlockSpec((MG_TM, MG_TN), lambda j, i: (i, j)),
        out_shape=jax.ShapeDtypeStruct((R, D), BF16),
        scratch_shapes=[pltpu.VMEM((3, D, MG_TN), BF16), pltpu.VMEM((3, W, MG_TN), BF16)],
        compiler_params=_cparams(("arbitrary", "arbitrary")),
        name="merge",
    )(u, y_ret, y_swa, y_rwkv, *[w_in.reshape(-1, w_in.shape[-1])] * 3, w_branch)


def kernel(x, c, ctx, c_ctx, norm1_g, norm2_g, w_mod, b_mod, w_in, ret_decay, swa_sink, rwkv_mu, rwkv_w0,
           rwkv_w_up, rwkv_a0, rwkv_a_up, rwkv_g_up, rwkv_k_k, rwkv_k_a, rwkv_r_k, rwkv_ln_g, w_branch,
           w_out, w_ff1, w_ff2, final_g):
    h = jnp.concatenate([ctx, x], axis=1).reshape(R, D)
    cond8 = jnp.concatenate([c, c_ctx[None, :], jnp.zeros((8 - B - 1, D), F32)], axis=0)
    mods = mod_vectors(cond8, w_mod, b_mod)
    mods = mods.reshape(L, 8, 6, D).transpose(0, 2, 1, 3)
    cos_full, sin_signed = rope_tables()

    o_swa = RET_IN
    o_rw = RET_IN + SWA_IN
    o_small = o_rw + 3 * W
    o_gate = o_rw + RWKV_IN
    w_ff2_bf16 = w_ff2.astype(BF16)
    zl = jnp.zeros((L, RW_RANK, W), BF16)
    both = lambda a: jnp.stack([jnp.concatenate([a[:, 0], zl], axis=1), jnp.concatenate([zl, a[:, 1]], axis=1)], axis=1)
    rw_params = (both(rwkv_w_up.astype(BF16)), both(rwkv_a_up.astype(BF16)), rwkv_w0.reshape(L, 2, 1, W),
                 rwkv_a0.reshape(L, 2, 1, W), rwkv_k_k.reshape(L, 1, W), rwkv_k_a.reshape(L, 1, W),
                 rwkv_r_k.reshape(L, 1, W), rwkv_g_up.astype(BF16), rwkv_ln_g.reshape(L, 1, W))

    for l in range(L):
        m_l = mods[l]
        u = norm_modulate(h, norm1_g[l], m_l, 0, 1)
        p_ret = matmul_f32w(u, w_in, l, 0, RET_IN, tn=1024, name="in_ret")
        p_swa = matmul_f32w(u, w_in, l, o_swa, SWA_IN, tn=768, name="in_swa")
        ps_rwkv = matmul_f32w(u, w_in, l, o_rw, RWKV_IN, tn=512, tm=TB, epilogue="token_shift", mu=rwkv_mu[l],
                              name="in_rwkv")

        y_ret = retention_mixer(p_ret, ret_decay[l])
        y_swa = swa_mixer(p_swa, swa_sink[l], cos_full, sin_signed)

        y_rwkv = rwkv_mixer(ps_rwkv, l, *rw_params)

        merged = merge_branches(u, y_ret, y_swa, y_rwkv, w_in, o_gate, w_branch, l)
        h = matmul_f32w(merged, w_out, l, 0, D, tn=1024, epilogue="residual", h=h, mods_l=m_l,
                        which_gate=2, name="out_proj")
        u2 = norm_modulate(h, norm2_g[l], m_l, 3, 4)
        f = matmul_f32w(u2, w_ff1, l, 0, D_FF, tn=1024, out_dtype=BF16, epilogue="relu2", name="ff1")
        h = matmul_residual(f, w_ff2_bf16, l, h, m_l, 5, tm=576, tn=512, name="ff2")

    return final_norm(h, final_g).reshape(B, SEQ, D)
```
